```python
import math, functools
import jax, jax.numpy as jnp
from jax import lax
import numpy as np

D_MODEL = 1024
BATCH = 1
SEQ = 16384
DEPTH = 2
DEC_BATCH = 32
DEC_SEQ = 64
PAST_LEN = 4096

CHUNK = 64
HEAD_DIM = 64
A_HEADS = 4
A_DK = 64
A_DV = 128
A_GATE_RANK = 16
A_GATE_TAU = 16.0
B_HEADS = 4
B_KV_HEADS = 2
B_WINDOW = 128
B_PREV_CHUNKS = B_WINDOW // CHUNK
C_HEADS = 4
C_PREV_CHUNKS = 8
C_BAND = C_PREV_CHUNKS * CHUNK
C_CLIP = 256
T5_BUCKETS = 32
T5_MAX_DIST = 128
D_FF = 4 * D_MODEL
D_MIX = A_HEADS * A_DV + B_HEADS * HEAD_DIM + C_HEADS * HEAD_DIM
SPLIT_SIZES = (A_HEADS * A_DK, A_HEADS * A_DK, A_HEADS * A_DV, A_HEADS * A_DV, A_GATE_RANK,
               B_HEADS * HEAD_DIM, B_KV_HEADS * HEAD_DIM, B_KV_HEADS * HEAD_DIM,
               C_HEADS * HEAD_DIM, C_HEADS * HEAD_DIM, C_HEADS * HEAD_DIM)
D_IN = sum(SPLIT_SIZES)
NORM_EPS = 1e-6
NEG_INF = -1e30

kernel_name = 'hybrid_streaming_encoder_step'


def rms_norm(x, g):
    x32 = x.astype(jnp.float32)
    y = x32 * lax.rsqrt(jnp.mean(x32 * x32, axis=-1, keepdims=True) + NORM_EPS)
    return (y * g.astype(jnp.float32)).astype(x.dtype)


def modulate(x, g, shift, scale):
    return rms_norm(x, g) * (1 + scale[:, None, :]) + shift[:, None, :]


def split_offsets():
    offs, t = [], 0
    for s in SPLIT_SIZES[:-1]:
        t += s
        offs.append(t)
    return offs


def project_heads(h, w_in, w_a2, b_a2):
    B_, T, _ = h.shape
    qa, ka, va, ga, ra, qb, kb, vb, qc, kc, vc = jnp.split(h @ w_in, split_offsets(), axis=-1)
    la = jax.nn.log_sigmoid((ra @ w_a2 + b_a2).astype(jnp.float32)) / A_GATE_TAU
    heads = lambda t, n: t.reshape(B_, T, n, -1)
    return (heads(qa, A_HEADS) * (A_DK ** -0.5), heads(ka, A_HEADS), heads(va, A_HEADS), ga,
            heads(la, A_HEADS),
            heads(qb, B_HEADS), heads(kb, B_KV_HEADS), heads(vb, B_KV_HEADS),
            heads(qc, C_HEADS), heads(kc, C_HEADS), heads(vc, C_HEADS))


def gla_chunk(S, q, k, v, la):
    L = q.shape[2]
    b = jnp.cumsum(la, axis=2)
    causal = jnp.tril(jnp.ones((L, L), dtype=bool))
    diff = b[:, :, :, None, :] - b[:, :, None, :, :]
    decay = jnp.exp(jnp.where(causal[None, None, :, :, None], diff, -jnp.inf))
    attn = jnp.einsum('bhtd,bhsd,bhtsd->bhts', q, k, decay)
    o = jnp.einsum('bhts,bhsv->bhtv', attn, v) + jnp.einsum('bhtd,bhdv->bhtv', q * jnp.exp(b), S)
    b_last = b[:, :, -1:, :]
    S_new = jnp.exp(b_last[:, :, 0, :])[..., None] * S + jnp.einsum('bhsd,bhsv->bhdv', k * jnp.exp(b_last - b), v)
    return S_new, o


def gla_prompt(q, k, v, la):
    B_, T, H, dk = q.shape
    dv = v.shape[-1]
    nc = T // CHUNK
    to_blocks = lambda t: t.astype(jnp.float32).reshape(B_, nc, CHUNK, H, t.shape[-1]).transpose(1, 0, 3, 2, 4)
    S0 = jnp.zeros((B_, H, dk, dv), jnp.float32)
    S, o = lax.scan(lambda S, inp: gla_chunk(S, *inp), S0, (to_blocks(q), to_blocks(k), to_blocks(v), to_blocks(la)))
    return S, o.transpose(1, 0, 3, 2, 4).reshape(B_, T, H, dv)


def gla_sample(S, q, k, v, la):
    tr = lambda t: t.astype(jnp.float32).transpose(0, 2, 1, 3)
    S_new, o = gla_chunk(S.astype(jnp.float32), tr(q), tr(k), tr(v), tr(la))
    return S_new, o.transpose(0, 2, 1, 3)


def gla_output(o, g, norm_g):
    B_, T, H, dv = o.shape
    on = rms_norm(o, norm_g)
    return (on * jax.nn.silu(g.astype(jnp.float32)).reshape(B_, T, H, dv)).reshape(B_, T, H * dv)


def t5_bucket(rel):
    half = T5_BUCKETS // 2
    max_exact = half // 2
    n = jnp.abs(rel)
    log_ratio = jnp.log(jnp.maximum(n, 1).astype(jnp.float32) / max_exact) / math.log(T5_MAX_DIST / max_exact)
    large = jnp.minimum(max_exact + (log_ratio * (half - max_exact)).astype(jnp.int32), half - 1)
    return jnp.where(rel > 0, half, 0) + jnp.where(n < max_exact, n, large)


def t5_logits(table, rel):
    return jnp.moveaxis(table[t5_bucket(rel)], -1, 1).astype(jnp.float32)


def clipped_logits(table, rel):
    return jnp.moveaxis(table[jnp.clip(rel, -C_CLIP, C_CLIP) + C_CLIP], -1, 1).astype(jnp.float32)


def band_attention(q, k, v, bias, valid, sink):
    B_, N, Lq, H, hd = q.shape
    Lk, G = k.shape[2], k.shape[3]
    R = H // G
    qg = q.reshape(B_, N, Lq, G, R, hd)
    s = jnp.einsum('bnqgrd,bnkgd->bngrqk', qg, k).astype(jnp.float32) * (hd ** -0.5)
    s = s + bias.reshape(bias.shape[0], G, R, Lq, Lk)[None]
    s = jnp.where(valid[None, :, None, None, None, :], s, NEG_INF)
    if sink is None:
        p = jax.nn.softmax(s, axis=-1)
    else:
        sk = sink.astype(jnp.float32).reshape(G, R)[None, None, :, :, None, None]
        m = jnp.maximum(jnp.max(s, axis=-1, keepdims=True), sk)
        e = jnp.exp(s - m)
        p = e / (jnp.sum(e, axis=-1, keepdims=True) + jnp.exp(sk - m))
    o = jnp.einsum('bngrqk,bnkgd->bnqgrd', p.astype(v.dtype), v)
    return o.reshape(B_, N, Lq, H * hd)


def chunk_band(x, n_prev):
    B_, T = x.shape[:2]
    nc = T // CHUNK
    xc = x.reshape(B_, nc, CHUNK, x.shape[2], x.shape[3])
    xp = jnp.pad(xc, ((0, 0), (n_prev, 0), (0, 0), (0, 0), (0, 0)))
    return jnp.concatenate([xp[:, i:i + nc] for i in range(n_prev + 1)], axis=2)


def attend_prompt(q, k, v, n_prev, bias_fn, sink):
    B_, T, H, hd = q.shape
    nc = T // CHUNK
    lk = (n_prev + 1) * CHUNK
    kpos = jnp.arange(lk) - n_prev * CHUNK
    rel = (kpos[None, :] - jnp.arange(CHUNK)[:, None])[None]
    valid = (jnp.arange(nc)[:, None] * CHUNK + kpos[None, :]) >= 0
    o = band_attention(q.reshape(B_, nc, CHUNK, H, hd), chunk_band(k, n_prev), chunk_band(v, n_prev),
                       bias_fn(rel), valid, sink)
    return o.reshape(B_, T, H * hd)


def attend_sample(q, k_new, v_new, k_cache, v_cache, bias_fn, sink):
    B_, S = q.shape[:2]
    Lc = k_cache.shape[1]
    k = jnp.concatenate([k_cache.astype(k_new.dtype), k_new], axis=1)
    v = jnp.concatenate([v_cache.astype(v_new.dtype), v_new], axis=1)
    lk = Lc + S
    rel = ((jnp.arange(lk) - Lc)[None, :] - jnp.arange(S)[:, None])[None]
    valid = jnp.ones((1, lk), dtype=bool)
    o = band_attention(q[:, None], k[:, None], v[:, None], bias_fn(rel), valid, sink)
    return o.reshape(B_, S, -1), k[:, -Lc:], v[:, -Lc:]


def run_trunk(x, c, state_gla, cache_b_k, cache_b_v, cache_c_k, cache_c_v,
              w_ada, b_ada, norm_mix_g, norm_mlp_g, w_in, w_a2, b_a2, a_norm_g, b_sink, t5_bias,
              c_rel_bias, w_out, w_up, w_down, final_norm_g):
    is_prompt = state_gla is None
    T = x.shape[1]
    new_gla, new_kb, new_vb, new_kc, new_vc = [], [], [], [], []
    t5_fn = functools.partial(t5_logits, t5_bias)
    for l in range(DEPTH):
        sh_m, sc_m, gt_m, sh_f, sc_f, gt_f = jnp.split(jax.nn.silu(c) @ w_ada[l] + b_ada[l], 6, axis=-1)
        h = modulate(x, norm_mix_g[l], sh_m, sc_m)
        qa, ka, va, ga, la, qb, kb, vb, qc, kc, vc = project_heads(h, w_in[l], w_a2[l], b_a2[l])
        c_fn = functools.partial(clipped_logits, c_rel_bias[l])
        if is_prompt:
            S, oa = gla_prompt(qa, ka, va, la)
            ob = attend_prompt(qb, kb, vb, B_PREV_CHUNKS, t5_fn, b_sink[l])
            oc = attend_prompt(qc, kc, vc, C_PREV_CHUNKS, c_fn, None)
            lc = min(C_BAND, T)
            kb_buf, vb_buf = kb[:, -B_WINDOW:], vb[:, -B_WINDOW:]
            kc_buf, vc_buf = kc[:, -lc:], vc[:, -lc:]
        else:
            S, oa = gla_sample(state_gla[l], qa, ka, va, la)
            ob, kb_buf, vb_buf = attend_sample(qb, kb, vb, cache_b_k[l], cache_b_v[l], t5_fn, b_sink[l])
            oc, kc_buf, vc_buf = attend_sample(qc, kc, vc, cache_c_k[l], cache_c_v[l], c_fn, None)
        oa = gla_output(oa, ga, a_norm_g[l]).astype(x.dtype)
        mixed = jnp.concatenate([oa, ob, oc], axis=-1) @ w_out[l]
        x = x + gt_m[:, None, :] * mixed
        h = modulate(x, norm_mlp_g[l], sh_f, sc_f)
        x = x + gt_f[:, None, :] * (jnp.square(jax.nn.relu(h @ w_up[l])) @ w_down[l])
        new_gla.append(S.astype(x.dtype))
        new_kb.append(kb_buf)
        new_vb.append(vb_buf)
        new_kc.append(kc_buf)
        new_vc.append(vc_buf)
    y = rms_norm(x, final_norm_g)
    return y, (jnp.stack(new_gla), jnp.stack(new_kb), jnp.stack(new_vb), jnp.stack(new_kc), jnp.stack(new_vc))


def setup_inputs(seed: int = 0) -> dict:
    key = jax.random.key(seed)
    ks = jax.random.split(key, 32)
    nrm = lambda k, shape, scale: jax.random.normal(k, shape, jnp.float32) * scale
    lc = min(C_BAND, PAST_LEN)
    return {
        'x_prompt': nrm(ks[0], (BATCH, SEQ, D_MODEL), 1.0),
        'x_sample': nrm(ks[1], (DEC_BATCH, DEC_SEQ, D_MODEL), 1.0),
        'c_prompt': nrm(ks[2], (BATCH, D_MODEL), 1.0),
        'c_sample': nrm(ks[3], (DEC_BATCH, D_MODEL), 1.0),
        'state_gla': nrm(ks[4], (DEPTH, DEC_BATCH, A_HEADS, A_DK, A_DV), 1.0),
        'cache_b_k': nrm(ks[5], (DEPTH, DEC_BATCH, B_WINDOW, B_KV_HEADS, HEAD_DIM), 1.0),
        'cache_b_v': nrm(ks[6], (DEPTH, DEC_BATCH, B_WINDOW, B_KV_HEADS, HEAD_DIM), 1.0),
        'cache_c_k': nrm(ks[7], (DEPTH, DEC_BATCH, lc, C_HEADS, HEAD_DIM), 1.0),
        'cache_c_v': nrm(ks[8], (DEPTH, DEC_BATCH, lc, C_HEADS, HEAD_DIM), 1.0),
        'w_ada': nrm(ks[9], (DEPTH, D_MODEL, 6 * D_MODEL), 0.5 * D_MODEL ** -0.5),
        'b_ada': nrm(ks[10], (DEPTH, 6 * D_MODEL), 0.01),
        'norm_mix_g': 1.0 + nrm(ks[11], (DEPTH, D_MODEL), 0.05),
        'norm_mlp_g': 1.0 + nrm(ks[12], (DEPTH, D_MODEL), 0.05),
        'w_in': nrm(ks[13], (DEPTH, D_MODEL, D_IN), D_MODEL ** -0.5),
        'w_a2': nrm(ks[14], (DEPTH, A_GATE_RANK, A_HEADS * A_DK), A_GATE_RANK ** -0.5),
        'b_a2': nrm(ks[15], (DEPTH, A_HEADS * A_DK), 0.1),
        'a_norm_g': 1.0 + nrm(ks[16], (DEPTH, A_DV), 0.05),
        'b_sink': nrm(ks[17], (DEPTH, B_HEADS), 1.0),
        't5_bias': nrm(ks[18], (T5_BUCKETS, B_HEADS), 0.5),
        'c_rel_bias': nrm(ks[19], (DEPTH, 2 * C_CLIP + 1, C_HEADS), 0.5),
        'w_out': nrm(ks[20], (DEPTH, D_MIX, D_MODEL), D_MIX ** -0.5),
        'w_up': nrm(ks[21], (DEPTH, D_MODEL, D_FF), D_MODEL ** -0.5),
        'w_down': nrm(ks[22], (DEPTH, D_FF, D_MODEL), D_FF ** -0.5),
        'final_norm_g': 1.0 + nrm(ks[23], (D_MODEL,), 0.05),
    }


def reference(x_prompt, x_sample, c_prompt, c_sample, state_gla, cache_b_k, cache_b_v, cache_c_k, cache_c_v,
              w_ada, b_ada, norm_mix_g, norm_mlp_g, w_in, w_a2, b_a2, a_norm_g, b_sink, t5_bias, c_rel_bias,
              w_out, w_up, w_down, final_norm_g):
    weights = (w_ada, b_ada, norm_mix_g, norm_mlp_g, w_in, w_a2, b_a2, a_norm_g, b_sink, t5_bias,
               c_rel_bias, w_out, w_up, w_down, final_norm_g)
    y_prompt, (sg_p, kb_p, vb_p, kc_p, vc_p) = run_trunk(x_prompt, c_prompt, None, None, None, None, None, *weights)
    y_sample, (sg_s, kb_s, vb_s, kc_s, vc_s) = run_trunk(x_sample, c_sample, state_gla, cache_b_k, cache_b_v,
                                                         cache_c_k, cache_c_v, *weights)
    return (y_prompt, y_sample, sg_p, kb_p, vb_p, kc_p, vc_p, sg_s, kb_s, vb_s, kc_s, vc_s)
```

```python
import functools

import numpy as np
import jax
import jax.numpy as jnp
from jax import lax
from jax.experimental import pallas as pl
from jax.experimental.pallas import tpu as pltpu

D_MODEL = 1024
DEPTH = 2
CHUNK = 64
HEAD_DIM = 64
A_HEADS = 4
A_DK = 64
A_DV = 128
A_GATE_RANK = 16
A_GATE_TAU = 16.0
B_HEADS = 4
B_KV_HEADS = 2
B_WINDOW = 128
C_HEADS = 4
C_BAND = 512
C_CLIP = 256
T5_BUCKETS = 32
T5_MAX_DIST = 128
D_FF = 4 * D_MODEL
NORM_EPS = 1e-6
NEG_INF = -1e30

LANES = 128
SUBLANES = 8
VMEM_LIMIT_BYTES = 60000 * 1024

OFF_QA = 0
OFF_KA = OFF_QA + A_HEADS * A_DK
OFF_VA = OFF_KA + A_HEADS * A_DK
OFF_GA = OFF_VA + A_HEADS * A_DV
OFF_QB = OFF_GA + A_HEADS * A_DV
OFF_KB = OFF_QB + B_HEADS * HEAD_DIM
OFF_VB = OFF_KB + B_KV_HEADS * HEAD_DIM
OFF_QC = OFF_VB + B_KV_HEADS * HEAD_DIM
OFF_KC = OFF_QC + C_HEADS * HEAD_DIM
OFF_VC = OFF_KC + C_HEADS * HEAD_DIM
OFF_RA = OFF_VC + C_HEADS * HEAD_DIM
P_IN = OFF_RA + LANES
D_AQK = A_HEADS * A_DK
D_AV = A_HEADS * A_DV
D_BKV = B_KV_HEADS * HEAD_DIM
D_C = C_HEADS * HEAD_DIM
D_MIX = D_AV + B_HEADS * HEAD_DIM + D_C
LK_B = B_WINDOW + CHUNK
LK_C = C_BAND + CHUNK
STACK = 4 * CHUNK
GLA_LEVELS = (1, 2, 4, 8, 16, 32)
GLA_SCANS = (2, 4, 8, 16, 32, 64)
FF_BLOCK = 1024

BF16 = jnp.bfloat16
F32 = jnp.float32


def _dot(a, b):
    return jnp.dot(a, b, preferred_element_type=F32)


def _dot_nt(a, b):
    return lax.dot_general(a, b, (((1,), (1,)), ((), ())), preferred_element_type=F32)


def _dot_tn(a, b):
    return lax.dot_general(a, b, (((0,), (0,)), ((), ())), preferred_element_type=F32)


def _rms(x, g):
    return x * lax.rsqrt(jnp.mean(x * x, axis=-1, keepdims=True) + NORM_EPS) * g


def _head_masks(width, per_head):
    lane = lax.broadcasted_iota(jnp.int32, (1, width), 1)
    return [(lane >= h * per_head) & (lane < (h + 1) * per_head) for h in range(width // per_head)]


def _stack_masked(x, masks):
    return jnp.concatenate([jnp.where(m, x, 0.0) for m in masks], axis=0)


def _tile_rows(x, n):
    return jnp.concatenate([x] * n, axis=0)


def _attend(q_st, kwin, vwin, bias_st, maskrow, sink):
    s = _dot_nt(q_st, kwin) + bias_st
    if maskrow is not None:
        s = s + maskrow
    m = jnp.max(s, axis=1, keepdims=True)
    if sink is not None:
        m = jnp.maximum(m, sink)
    e = jnp.exp(s - m)
    l = jnp.sum(e, axis=1, keepdims=True)
    if sink is not None:
        l = l + jnp.exp(sink - m)
    return _dot(e.astype(BF16), vwin) / l


def _gla_chunk(q, k, v, la, s_t, wall, lvl, hm_a):
    hi = la.astype(BF16)
    r1 = la - hi.astype(F32)
    mid = r1.astype(BF16)
    lo = (r1 - mid.astype(F32)).astype(BF16)
    scans = _dot(wall, hi) + _dot(wall, mid) + _dot(wall, lo)
    nl = len(GLA_SCANS)
    eq = {m: scans[i * CHUNK:(i + 1) * CHUNK] for i, m in enumerate(GLA_SCANS)}
    ek = {m: scans[(nl + i) * CHUNK:(nl + i + 1) * CHUNK] for i, m in enumerate(GLA_SCANS)}
    eq[1] = la

    q_st = _stack_masked(q, hm_a)
    k16 = k.astype(BF16)
    attn = jnp.where(lvl == 0, _dot_nt(q_st.astype(BF16), k16), 0.0)
    for li, m in enumerate(GLA_LEVELS):
        qt = (q_st * _tile_rows(jnp.exp(eq[m]), A_HEADS)).astype(BF16)
        kt = k16 if m == 1 else (k * jnp.exp(ek[m])).astype(BF16)
        attn = jnp.where(lvl == li + 1, _dot_nt(qt, kt), attn)

    b = eq[CHUNK]
    qhat = (q_st * _tile_rows(jnp.exp(b), A_HEADS)).astype(BF16)
    o_inter = _dot_nt(qhat, s_t.astype(BF16))
    v16 = v.astype(BF16)
    o_intra = jnp.concatenate(
        [_dot(attn[h * CHUNK:(h + 1) * CHUNK].astype(BF16), v16[:, h * A_DV:(h + 1) * A_DV])
         for h in range(A_HEADS)], axis=0)
    khat = (k * jnp.exp(ek[CHUNK])).astype(BF16)
    upd = _dot_tn(v16, khat)
    s_new = s_t * jnp.exp(b[CHUNK - 1:CHUNK, :])
    for h in range(A_HEADS):
        s_new = s_new + jnp.where(hm_a[h], upd[h * A_DV:(h + 1) * A_DV], 0.0)
    return o_intra + o_inter, s_new


def _layer_kernel(*refs, prompt, final, nch, tq):
    (x_ref, mods_ref, gmix_ref, gmlp_ref, gfin_ref, anorm_ref, win_ref, wa2_ref, ba2_ref, wout_ref,
     wup_ref, wdown_ref, biasb_ref, biasc_ref, sink_ref, wall_ref, lvl_ref) = refs[:17]
    pos = 17
    if not prompt:
        st_ref, cbk_ref, cbv_ref, cck_ref, ccv_ref = refs[pos:pos + 5]
        pos += 5
    y_ref, so_ref, kbo_ref, vbo_ref, kco_ref, vco_ref = refs[pos:pos + 6]
    pos += 6
    h_scr, proj_scr, la_scr, mix_scr, x1_scr, st_scr, kbw, vbw, kcw, vcw = refs[pos:]

    step = pl.program_id(0)

    def mod(c, j):
        return mods_ref[0 if prompt else c, j:j + 1, :]

    if prompt:
        @pl.when(step == 0)
        def _():
            st_scr[...] = jnp.zeros_like(st_scr)
            kbw[...] = jnp.zeros_like(kbw)
            vbw[...] = jnp.zeros_like(vbw)
            kcw[...] = jnp.zeros_like(kcw)
            vcw[...] = jnp.zeros_like(vcw)

    gmix = gmix_ref[...]
    for c in range(nch):
        r = slice(c * CHUNK, (c + 1) * CHUNK)
        h = _rms(x_ref[r, :], gmix) * (1.0 + mod(c, 1)) + mod(c, 0)
        h_scr[r, :] = h.astype(BF16)
    proj_scr[...] = _dot(h_scr[...], win_ref[...])
    z = _dot(proj_scr[:, OFF_RA:OFF_RA + LANES].astype(BF16), wa2_ref[...]) + ba2_ref[...]
    la_scr[...] = (jnp.minimum(z, 0.0) - jnp.log1p(jnp.exp(-jnp.abs(z)))) * (1.0 / A_GATE_TAU)

    if prompt:
        kbw[B_WINDOW:B_WINDOW + tq, :] = proj_scr[:, OFF_KB:OFF_KB + D_BKV].astype(BF16)
        vbw[B_WINDOW:B_WINDOW + tq, :] = proj_scr[:, OFF_VB:OFF_VB + D_BKV].astype(BF16)
        kcw[C_BAND:C_BAND + tq, :] = proj_scr[:, OFF_KC:OFF_KC + D_C].astype(BF16)
        vcw[C_BAND:C_BAND + tq, :] = proj_scr[:, OFF_VC:OFF_VC + D_C].astype(BF16)

    hm_a = _head_masks(D_AQK, A_DK)
    hm_c = _head_masks(D_C, HEAD_DIM)
    hm_b = _head_masks(D_BKV, HEAD_DIM)
    anorm = anorm_ref[...]

    def chunk_body(c, carry):
        r0 = pl.multiple_of(c * CHUNK, CHUNK)
        rows = pl.ds(r0, CHUNK)

        q = proj_scr[rows, OFF_QA:OFF_QA + D_AQK] * (A_DK ** -0.5)
        k = proj_scr[rows, OFF_KA:OFF_KA + D_AQK]
        v = proj_scr[rows, OFF_VA:OFF_VA + D_AV]
        la = la_scr[rows, :]
        s_t = st_scr[...] if prompt else st_ref[c]
        o_st, s_new = _gla_chunk(q, k, v, la, s_t, wall_ref[...], lvl_ref[...], hm_a)
        if prompt:
            st_scr[...] = s_new
        else:
            so_ref[c] = s_new
        g = proj_scr[rows, OFF_GA:OFF_GA + D_AV]
        for h in range(A_HEADS):
            on = _rms(o_st[h * CHUNK:(h + 1) * CHUNK], anorm)
            gh = g[:, h * A_DV:(h + 1) * A_DV]
            mix_scr[rows, h * A_DV:(h + 1) * A_DV] = (on * (gh * jax.nn.sigmoid(gh))).astype(BF16)

        if prompt:
            kwb, vwb = kbw[pl.ds(r0, LK_B), :], vbw[pl.ds(r0, LK_B), :]
            kwc, vwc = kcw[pl.ds(r0, LK_C), :], vcw[pl.ds(r0, LK_C), :]
            gidx = step * nch + c
            lane_b = lax.broadcasted_iota(jnp.int32, (1, LK_B), 1)
            lane_c = lax.broadcasted_iota(jnp.int32, (1, LK_C), 1)
            mask_b = jnp.where(lane_b >= B_WINDOW - gidx * CHUNK, 0.0, NEG_INF)
            mask_c = jnp.where(lane_c >= C_BAND - gidx * CHUNK, 0.0, NEG_INF)
        else:
            kb_new = proj_scr[rows, OFF_KB:OFF_KB + D_BKV]
            vb_new = proj_scr[rows, OFF_VB:OFF_VB + D_BKV]
            kc_new = proj_scr[rows, OFF_KC:OFF_KC + D_C]
            vc_new = proj_scr[rows, OFF_VC:OFF_VC + D_C]
            kwb = jnp.concatenate([cbk_ref[c].astype(BF16), kb_new.astype(BF16)], axis=0)
            vwb = jnp.concatenate([cbv_ref[c].astype(BF16), vb_new.astype(BF16)], axis=0)
            kwc = jnp.concatenate([cck_ref[c].astype(BF16), kc_new.astype(BF16)], axis=0)
            vwc = jnp.concatenate([ccv_ref[c].astype(BF16), vc_new.astype(BF16)], axis=0)
            mask_b = mask_c = None
            kbo_ref[c, 0:B_WINDOW - CHUNK, :] = cbk_ref[c, CHUNK:B_WINDOW, :]
            vbo_ref[c, 0:B_WINDOW - CHUNK, :] = cbv_ref[c, CHUNK:B_WINDOW, :]
            kbo_ref[c, B_WINDOW - CHUNK:B_WINDOW, :] = kb_new
            vbo_ref[c, B_WINDOW - CHUNK:B_WINDOW, :] = vb_new
            kco_ref[c, 0:C_BAND - CHUNK, :] = cck_ref[c, CHUNK:C_BAND, :]
            vco_ref[c, 0:C_BAND - CHUNK, :] = ccv_ref[c, CHUNK:C_BAND, :]
            kco_ref[c, C_BAND - CHUNK:C_BAND, :] = kc_new
            vco_ref[c, C_BAND - CHUNK:C_BAND, :] = vc_new

        qb = proj_scr[rows, OFF_QB:OFF_QB + B_HEADS * HEAD_DIM] * (HEAD_DIM ** -0.5)
        qb_a, qb_b = qb[:, 0:D_BKV], qb[:, D_BKV:2 * D_BKV]
        qb_st = jnp.concatenate([jnp.where(hm_b[0], qb_a, 0.0), jnp.where(hm_b[0], qb_b, 0.0),
                                 jnp.where(hm_b[1], qb_a, 0.0), jnp.where(hm_b[1], qb_b, 0.0)], axis=0)
        ob_st = _attend(qb_st.astype(BF16), kwb, vwb, biasb_ref[...], mask_b, sink_ref[...])
        ob_a = jnp.where(hm_b[0], ob_st[0:CHUNK], ob_st[2 * CHUNK:3 * CHUNK])
        ob_b = jnp.where(hm_b[0], ob_st[CHUNK:2 * CHUNK], ob_st[3 * CHUNK:4 * CHUNK])
        mix_scr[rows, D_AV:D_AV + D_BKV] = ob_a.astype(BF16)
        mix_scr[rows, D_AV + D_BKV:D_AV + 2 * D_BKV] = ob_b.astype(BF16)

        qc = proj_scr[rows, OFF_QC:OFF_QC + D_C] * (HEAD_DIM ** -0.5)
        oc_st = _attend(_stack_masked(qc, hm_c).astype(BF16), kwc, vwc, biasc_ref[...], mask_c, None)
        oc = jnp.where(hm_c[0], oc_st[0:CHUNK], 0.0)
        for h in range(1, C_HEADS):
            oc = jnp.where(hm_c[h], oc_st[h * CHUNK:(h + 1) * CHUNK], oc)
        mix_scr[rows, D_AV + 2 * D_BKV:D_MIX] = oc.astype(BF16)
        return carry

    lax.fori_loop(0, nch, chunk_body, 0)

    if prompt:
        kbw[0:B_WINDOW, :] = kbw[tq:tq + B_WINDOW, :]
        vbw[0:B_WINDOW, :] = vbw[tq:tq + B_WINDOW, :]
        kcw[0:C_BAND, :] = kcw[tq:tq + C_BAND, :]
        vcw[0:C_BAND, :] = vcw[tq:tq + C_BAND, :]
        so_ref[...] = st_scr[...]
        kbo_ref[...] = proj_scr[tq - B_WINDOW:tq, OFF_KB:OFF_KB + D_BKV]
        vbo_ref[...] = proj_scr[tq - B_WINDOW:tq, OFF_VB:OFF_VB + D_BKV]
        rb = kco_ref.shape[0]
        kco_ref[...] = proj_scr[tq - rb:tq, OFF_KC:OFF_KC + D_C]
        vco_ref[...] = proj_scr[tq - rb:tq, OFF_VC:OFF_VC + D_C]

    mixed = _dot(mix_scr[...], wout_ref[...])
    gmlp = gmlp_ref[...]
    for c in range(nch):
        r = slice(c * CHUNK, (c + 1) * CHUNK)
        x1 = x_ref[r, :] + mod(c, 2) * mixed[r, :]
        x1_scr[r, :] = x1
        h2 = _rms(x1, gmlp) * (1.0 + mod(c, 4)) + mod(c, 3)
        h_scr[r, :] = h2.astype(BF16)

    h2 = h_scr[...]
    acc = None
    for j in range(D_FF // FF_BLOCK):
        up = _dot(h2, wup_ref[:, j * FF_BLOCK:(j + 1) * FF_BLOCK])
        act = jnp.square(jnp.maximum(up, 0.0)).astype(BF16)
        part = _dot(act, wdown_ref[j * FF_BLOCK:(j + 1) * FF_BLOCK, :])
        acc = part if acc is None else acc + part

    gfin = gfin_ref[...]
    for c in range(nch):
        r = slice(c * CHUNK, (c + 1) * CHUNK)
        x2 = x1_scr[r, :] + mod(c, 5) * acc[r, :]
        y_ref[r, :] = _rms(x2, gfin) if final else x2


def _const_spec(shape):
    nd = len(shape)
    return pl.BlockSpec(shape, lambda i, _nd=nd: (0,) * _nd, pipeline_mode=pl.Buffered(1))


def _run_layer(x, mods, consts, caches, *, prompt, final, tq, nb):
    rows = x.shape[0]
    nch = tq // CHUNK
    nt = rows // tq
    assert rows % tq == 0 and tq % CHUNK == 0
    (gmix, gmlp, gfin, anorm, win, wa2, ba2, wout, wup, wdown, biasb, biasc, sink, wall, lvl) = consts

    in_specs = [pl.BlockSpec((tq, D_MODEL), lambda i: (i, 0))]
    if prompt:
        in_specs.append(pl.BlockSpec((1, SUBLANES, D_MODEL), lambda i: (0, 0, 0)))
    else:
        assert nb == nch
        in_specs.append(pl.BlockSpec((nb, SUBLANES, D_MODEL), lambda i: (i, 0, 0)))
    in_specs += [_const_spec(c.shape) for c in consts]
    args = [x, mods, *consts]

    y_spec = pl.BlockSpec((tq, D_MODEL), lambda i: (i, 0))
    if prompt:
        assert tq >= B_WINDOW and (tq % C_BAND == 0 or C_BAND % tq == 0)
        rb = min(tq, C_BAND)
        first = nt - C_BAND // rb
        out_shape = [jax.ShapeDtypeStruct((rows, D_MODEL), F32),
                     jax.ShapeDtypeStruct((A_DV, D_AQK), F32),
                     jax.ShapeDtypeStruct((B_WINDOW, D_BKV), F32),
                     jax.ShapeDtypeStruct((B_WINDOW, D_BKV), F32),
                     jax.ShapeDtypeStruct((C_BAND, D_C), F32),
                     jax.ShapeDtypeStruct((C_BAND, D_C), F32)]
        band = pl.BlockSpec((rb, D_C), lambda i: (jnp.maximum(i - first, 0), 0))
        out_specs = [y_spec,
                     pl.BlockSpec((A_DV, D_AQK), lambda i: (0, 0)),
                     pl.BlockSpec((B_WINDOW, D_BKV), lambda i: (0, 0)),
                     pl.BlockSpec((B_WINDOW, D_BKV), lambda i: (0, 0)),
                     band, band]
        kbw_rows, kcw_rows = B_WINDOW + tq, C_BAND + tq
    else:
        nseq = rows // CHUNK
        st, cbk, cbv, cck, ccv = caches
        in_specs += [pl.BlockSpec((nb, A_DV, D_AQK), lambda i: (i, 0, 0)),
                     pl.BlockSpec((nb, B_WINDOW, D_BKV), lambda i: (i, 0, 0)),
                     pl.BlockSpec((nb, B_WINDOW, D_BKV), lambda i: (i, 0, 0)),
                     pl.BlockSpec((nb, C_BAND, D_C), lambda i: (i, 0, 0)),
                     pl.BlockSpec((nb, C_BAND, D_C), lambda i: (i, 0, 0))]
        args += [st, cbk, cbv, cck, ccv]
        out_shape = [jax.ShapeDtypeStruct((rows, D_MODEL), F32),
                     jax.ShapeDtypeStruct((nseq, A_DV, D_AQK), F32),
                     jax.ShapeDtypeStruct((nseq, B_WINDOW, D_BKV), F32),
                     jax.ShapeDtypeStruct((nseq, B_WINDOW, D_BKV), F32),
                     jax.ShapeDtypeStruct((nseq, C_BAND, D_C), F32),
                     jax.ShapeDtypeStruct((nseq, C_BAND, D_C), F32)]
        out_specs = [y_spec,
                     pl.BlockSpec((nb, A_DV, D_AQK), lambda i: (i, 0, 0)),
                     pl.BlockSpec((nb, B_WINDOW, D_BKV), lambda i: (i, 0, 0)),
                     pl.BlockSpec((nb, B_WINDOW, D_BKV), lambda i: (i, 0, 0)),
                     pl.BlockSpec((nb, C_BAND, D_C), lambda i: (i, 0, 0)),
                     pl.BlockSpec((nb, C_BAND, D_C), lambda i: (i, 0, 0))]
        kbw_rows, kcw_rows = SUBLANES * 2, SUBLANES * 2

    scratch = [pltpu.VMEM((tq, D_MODEL), BF16),
               pltpu.VMEM((tq, P_IN), F32),
               pltpu.VMEM((tq, D_AQK), F32),
               pltpu.VMEM((tq, D_MIX), BF16),
               pltpu.VMEM((tq, D_MODEL), F32),
               pltpu.VMEM((A_DV, D_AQK), F32),
               pltpu.VMEM((kbw_rows, D_BKV), BF16), pltpu.VMEM((kbw_rows, D_BKV), BF16),
               pltpu.VMEM((kcw_rows, D_C), BF16), pltpu.VMEM((kcw_rows, D_C), BF16)]

    kern = functools.partial(_layer_kernel, prompt=prompt, final=final, nch=nch, tq=tq)
    return pl.pallas_call(
        kern,
        grid=(nt,),
        in_specs=in_specs,
        out_specs=out_specs,
        out_shape=out_shape,
        scratch_shapes=scratch,
        compiler_params=pltpu.CompilerParams(dimension_semantics=("arbitrary",),
                                             vmem_limit_bytes=VMEM_LIMIT_BYTES),
        name=("layer_prompt" if prompt else "layer_sample") + ("_final" if final else ""),
    )(*args)


ADA_BLOCK = 1536


def _ada_kernel(c_ref, w_ref, b_ref, o_ref):
    c = c_ref[...]
    sc = c * jax.nn.sigmoid(c)
    o_ref[0] = jnp.dot(sc, w_ref[0], preferred_element_type=F32,
                       precision=lax.Precision.HIGHEST) + b_ref[0]


def _ada(c_all, w_ada, b_ada):
    rows = c_all.shape[0]
    n = 6 * D_MODEL
    return pl.pallas_call(
        _ada_kernel,
        grid=(DEPTH, n // ADA_BLOCK),
        in_specs=[pl.BlockSpec((rows, D_MODEL), lambda l, j: (0, 0)),
                  pl.BlockSpec((1, D_MODEL, ADA_BLOCK), lambda l, j: (l, 0, j)),
                  pl.BlockSpec((1, 1, ADA_BLOCK), lambda l, j: (l, 0, j))],
        out_specs=pl.BlockSpec((1, rows, ADA_BLOCK), lambda l, j: (l, 0, j)),
        out_shape=jax.ShapeDtypeStruct((DEPTH, rows, n), F32),
        compiler_params=pltpu.CompilerParams(dimension_semantics=("arbitrary", "arbitrary"),
                                             vmem_limit_bytes=VMEM_LIMIT_BYTES),
        name="adaln",
    )(c_all, w_ada, b_ada.reshape(DEPTH, 1, n))


C_TABLE_USED = CHUNK + C_CLIP


def _bias_kernel(t5_ref, crel_ref, bucket_ref, cidx_ref, ob_ref, oc_ref):
    bucket = bucket_ref[...]
    for h in range(B_HEADS):
        def b_body(b, acc, h=h):
            return jnp.where(bucket == b, t5_ref[h, b], acc)
        ob_ref[h] = lax.fori_loop(0, T5_BUCKETS, b_body, jnp.zeros(bucket.shape, F32))
    cidx = cidx_ref[...]
    for l in range(DEPTH):
        for h in range(C_HEADS):
            def c_body(j, acc, l=l, h=h):
                return jnp.where(cidx == j, crel_ref[h, l * (2 * C_CLIP + 1) + j], acc)
            oc_ref[l, h] = lax.fori_loop(0, C_TABLE_USED, c_body, jnp.zeros(cidx.shape, F32))


def _bias_tables(t5_bias, c_rel_bias, bucket, cidx):
    smem = pl.BlockSpec(memory_space=pltpu.SMEM)
    vmem = pl.BlockSpec(memory_space=pltpu.VMEM)
    crel = c_rel_bias.reshape(DEPTH * (2 * C_CLIP + 1), C_HEADS)
    return pl.pallas_call(
        _bias_kernel,
        in_specs=[smem, smem, vmem, vmem],
        out_specs=[vmem, vmem],
        out_shape=[jax.ShapeDtypeStruct((B_HEADS, CHUNK, LK_B), F32),
                   jax.ShapeDtypeStruct((DEPTH, C_HEADS, CHUNK, LK_C), F32)],
        name="bias_tables",
    )(t5_bias.T, crel.T, bucket, cidx)


def _t5_bucket(rel):
    half = T5_BUCKETS // 2
    max_exact = half // 2
    n = jnp.abs(rel)
    log_ratio = jnp.log(jnp.maximum(n, 1).astype(jnp.float32) / max_exact) / np.log(T5_MAX_DIST / max_exact)
    large = jnp.minimum(max_exact + (log_ratio * (half - max_exact)).astype(jnp.int32), half - 1)
    return jnp.where(rel > 0, half, 0) + jnp.where(n < max_exact, n, large)


def _gla_constants():
    t = np.arange(CHUNK)[:, None]
    j = np.arange(CHUNK)[None, :]
    mats = []
    for m in GLA_SCANS:
        mats.append(((j >= (t // m) * m) & (j <= t)).astype(np.float32))
    for m in GLA_SCANS:
        mats.append(((j > t) & (j <= (t // m) * m + m - 1)).astype(np.float32))
    wall = np.concatenate(mats, axis=0)
    s = j
    lvl = np.full((CHUNK, CHUNK), -1, np.int32)
    lvl[t == s] = 0
    for li, m in enumerate(GLA_LEVELS):
        sel = ((t // m) % 2 == 1) & ((s // m) == (t // m) - 1)
        lvl[np.broadcast_to(sel, lvl.shape)] = li + 1
    return jnp.asarray(wall, BF16), jnp.asarray(np.tile(lvl, (A_HEADS, 1)))


TQ_PROMPT = 256
NB_SAMPLE = 4


def kernel(x_prompt, x_sample, c_prompt, c_sample, state_gla, cache_b_k, cache_b_v, cache_c_k, cache_c_v,
           w_ada, b_ada, norm_mix_g, norm_mlp_g, w_in, w_a2, b_a2, a_norm_g, b_sink, t5_bias, c_rel_bias,
           w_out, w_up, w_down, final_norm_g):
    bsz, seq, _ = x_prompt.shape
    dec_b, dec_s, _ = x_sample.shape
    assert bsz == 1 and dec_s == CHUNK

    n_c = bsz + dec_b
    c_rows = -(-n_c // SUBLANES) * SUBLANES
    c_all = jnp.concatenate([c_prompt, c_sample, jnp.zeros((c_rows - n_c, D_MODEL), F32)], axis=0)
    mods = _ada(c_all, w_ada, b_ada).reshape(DEPTH, c_rows, 6, D_MODEL)
    mods = jnp.pad(mods, ((0, 0), (0, 0), (0, SUBLANES - 6), (0, 0)))

    q_pos = jnp.arange(CHUNK)[:, None]
    rel_b = (jnp.arange(LK_B) - B_WINDOW)[None, :] - q_pos
    rel_c = (jnp.arange(LK_C) - C_BAND)[None, :] - q_pos
    bias_b, bias_c = _bias_tables(t5_bias, c_rel_bias, _t5_bucket(rel_b).astype(jnp.int32),
                                  (jnp.clip(rel_c, -C_CLIP, C_CLIP) + C_CLIP).astype(jnp.int32))
    bias_b = bias_b.reshape(STACK, LK_B)
    bias_c = bias_c.reshape(DEPTH, STACK, LK_C)

    wall, lvl = _gla_constants()
    qb0 = 1552
    head = lambda h: slice(qb0 + h * HEAD_DIM, qb0 + (h + 1) * HEAD_DIM)
    ob0 = D_AV
    orow = lambda h: slice(ob0 + h * HEAD_DIM, ob0 + (h + 1) * HEAD_DIM)

    x_p = x_prompt.reshape(seq, D_MODEL)
    x_s = x_sample.reshape(dec_b * dec_s, D_MODEL)
    outs_p, outs_s = [], []
    for l in range(DEPTH):
        w = w_in[l]
        win = jnp.concatenate([w[:, 0:1536], w[:, head(0)], w[:, head(2)], w[:, head(1)], w[:, head(3)],
                               w[:, 1808:2832], w[:, 1536:1552],
                               jnp.zeros((D_MODEL, LANES - A_GATE_RANK), F32)], axis=1).astype(BF16)
        wa2 = jnp.concatenate([w_a2[l], jnp.zeros((LANES - A_GATE_RANK, D_AQK), F32)], axis=0).astype(BF16)
        wo = w_out[l]
        wout = jnp.concatenate([wo[0:D_AV], wo[orow(0)], wo[orow(2)], wo[orow(1)], wo[orow(3)],
                                wo[D_AV + B_HEADS * HEAD_DIM:]], axis=0).astype(BF16)
        sink = jnp.repeat(b_sink[l], CHUNK)[:, None]
        consts = (norm_mix_g[l][None, :], norm_mlp_g[l][None, :], final_norm_g[None, :], a_norm_g[l][None, :],
                  win, wa2, b_a2[l][None, :], wout, w_up[l].astype(BF16), w_down[l].astype(BF16),
                  bias_b, bias_c[l], sink, wall, lvl)
        final = l == DEPTH - 1

        res_p = _run_layer(x_p, mods[l, 0:bsz], consts, None, prompt=True, final=final,
                           tq=TQ_PROMPT, nb=None)
        x_p = res_p[0]
        outs_p.append(res_p[1:])

        st = state_gla[l].reshape(dec_b, D_AQK, A_DV).transpose(0, 2, 1)
        caches = (st,
                  cache_b_k[l].reshape(dec_b, B_WINDOW, D_BKV), cache_b_v[l].reshape(dec_b, B_WINDOW, D_BKV),
                  cache_c_k[l].reshape(dec_b, C_BAND, D_C), cache_c_v[l].reshape(dec_b, C_BAND, D_C))
        res_s = _run_layer(x_s, mods[l, bsz:bsz + dec_b], consts, caches, prompt=False, final=final,
                           tq=NB_SAMPLE * CHUNK, nb=NB_SAMPLE)
        x_s = res_s[0]
        outs_s.append(res_s[1:])

    def to_state(s_t, nbatch):
        return s_t.reshape(nbatch, A_DV, A_HEADS, A_DK).transpose(0, 2, 3, 1)

    sg_p = jnp.stack([to_state(o[0], bsz) for o in outs_p])
    kb_p = jnp.stack([o[1].reshape(bsz, B_WINDOW, B_KV_HEADS, HEAD_DIM) for o in outs_p])
    vb_p = jnp.stack([o[2].reshape(bsz, B_WINDOW, B_KV_HEADS, HEAD_DIM) for o in outs_p])
    kc_p = jnp.stack([o[3].reshape(bsz, C_BAND, C_HEADS, HEAD_DIM) for o in outs_p])
    vc_p = jnp.stack([o[4].reshape(bsz, C_BAND, C_HEADS, HEAD_DIM) for o in outs_p])
    sg_s = jnp.stack([to_state(o[0], dec_b) for o in outs_s])
    kb_s = jnp.stack([o[1].reshape(dec_b, B_WINDOW, B_KV_HEADS, HEAD_DIM) for o in outs_s])
    vb_s = jnp.stack([o[2].reshape(dec_b, B_WINDOW, B_KV_HEADS, HEAD_DIM) for o in outs_s])
    kc_s = jnp.stack([o[3].reshape(dec_b, C_BAND, C_HEADS, HEAD_DIM) for o in outs_s])
    vc_s = jnp.stack([o[4].reshape(dec_b, C_BAND, C_HEADS, HEAD_DIM) for o in outs_s])
    return (x_p.reshape(bsz, seq, D_MODEL), x_s.reshape(dec_b, dec_s, D_MODEL),
            sg_p, kb_p, vb_p, kc_p, vc_p, sg_s, kb_s, vb_s, kc_s, vc_s)
```

```python
import functools

import numpy as np
import jax
import jax.numpy as jnp
from jax import lax
from jax.experimental import pallas as pl
from jax.experimental.pallas import tpu as pltpu

D_MODEL = 1024
DEPTH = 2
CHUNK = 64
HEAD_DIM = 64
A_HEADS = 4
A_DK = 64
A_DV = 128
A_GATE_RANK = 16
A_GATE_TAU = 16.0
B_HEADS = 4
B_KV_HEADS = 2
B_WINDOW = 128
C_HEADS = 4
C_BAND = 512
C_CLIP = 256
T5_BUCKETS = 32
T5_MAX_DIST = 128
D_FF = 4 * D_MODEL
NORM_EPS = 1e-6
NEG_INF = -1e30

LANES = 128
SUBLANES = 8
VMEM_LIMIT_BYTES = 60000 * 1024

OFF_QA = 0
OFF_KA = OFF_QA + A_HEADS * A_DK
OFF_VA = OFF_KA + A_HEADS * A_DK
OFF_GA = OFF_VA + A_HEADS * A_DV
OFF_QB = OFF_GA + A_HEADS * A_DV
OFF_KB = OFF_QB + B_HEADS * HEAD_DIM
OFF_VB = OFF_KB + B_KV_HEADS * HEAD_DIM
OFF_QC = OFF_VB + B_KV_HEADS * HEAD_DIM
OFF_KC = OFF_QC + C_HEADS * HEAD_DIM
OFF_VC = OFF_KC + C_HEADS * HEAD_DIM
OFF_RA = OFF_VC + C_HEADS * HEAD_DIM
P_IN = OFF_RA + LANES
D_AQK = A_HEADS * A_DK
D_AV = A_HEADS * A_DV
D_BKV = B_KV_HEADS * HEAD_DIM
D_C = C_HEADS * HEAD_DIM
D_MIX = D_AV + B_HEADS * HEAD_DIM + D_C
LK_B = B_WINDOW + CHUNK
LK_C = C_BAND + CHUNK
STACK = 4 * CHUNK
GLA_LEVELS = (1, 2, 4, 8, 16, 32)
FF_BLOCK = 1024

BF16 = jnp.bfloat16
F32 = jnp.float32


def _dot(a, b):
    return jnp.dot(a, b, preferred_element_type=F32)


def _dot_nt(a, b):
    return lax.dot_general(a, b, (((1,), (1,)), ((), ())), preferred_element_type=F32)


def _dot_tn(a, b):
    return lax.dot_general(a, b, (((0,), (0,)), ((), ())), preferred_element_type=F32)


def _rms(x, g):
    return x * lax.rsqrt(jnp.mean(x * x, axis=-1, keepdims=True) + NORM_EPS) * g


def _split3(x):
    hi = x.astype(BF16)
    r1 = x - hi.astype(F32)
    mid = r1.astype(BF16)
    lo = (r1 - mid.astype(F32)).astype(BF16)
    return hi, mid, lo


def _head_masks(width, per_head):
    lane = lax.broadcasted_iota(jnp.int32, (1, width), 1)
    return [(lane >= h * per_head) & (lane < (h + 1) * per_head) for h in range(width // per_head)]


def _stack_masked(x, masks):
    return jnp.concatenate([jnp.where(m, x, 0.0) for m in masks], axis=0)


def _tile_rows(x, n):
    return jnp.concatenate([x] * n, axis=0)


def _attend(q_st, kwin, vwin, bias_st, maskrow, sink):
    s = _dot_nt(q_st, kwin) + bias_st
    if maskrow is not None:
        s = s + maskrow
    m = jnp.max(s, axis=1, keepdims=True)
    if sink is not None:
        m = jnp.maximum(m, sink)
    e = jnp.exp(s - m)
    l = jnp.sum(e, axis=1, keepdims=True)
    if sink is not None:
        l = l + jnp.exp(sink - m)
    return _dot(e.astype(BF16), vwin) / l


def _level_exponent(b, m, t):
    bcast = lambda i: jnp.broadcast_to(b[i:i + 1], (SUBLANES, b.shape[1]))
    low_half = lax.broadcasted_iota(jnp.int32, (SUBLANES, b.shape[1]), 0) < SUBLANES // 2
    parts = []
    for g in range(CHUNK // SUBLANES):
        r = g * SUBLANES
        rows = b[r:r + SUBLANES]
        if m >= SUBLANES:
            ref = b[(r // (2 * m)) * 2 * m + m - 1:(r // (2 * m)) * 2 * m + m]
            parts.append(rows - ref if (r // m) % 2 == 1 else ref - rows)
        elif m == 4:
            parts.append(rows - bcast(r + 3))
        else:
            parts.append(rows - jnp.where(low_half, bcast(r + 1), bcast(r + 5)))
    d = jnp.concatenate(parts, axis=0)
    return d if m >= SUBLANES else jnp.where((t & m) != 0, d, -d)


def _gla_chunk(q, k, v, la, b, s, lvl, hm_a):
    t = lax.broadcasted_iota(jnp.int32, (CHUNK, D_AQK), 0)
    expo = {1: la}
    for m in GLA_LEVELS[1:]:
        expo[m] = _level_exponent(b, m, t)

    q_st = _stack_masked(q, hm_a)
    k16 = k.astype(BF16)
    attn = jnp.where(lvl == 0, _dot_nt(q_st.astype(BF16), k16), 0.0)
    for li, m in enumerate(GLA_LEVELS):
        e = jnp.exp(expo[m])
        qt = (q_st * _tile_rows(e, A_HEADS)).astype(BF16)
        kt = k16 if m == 1 else (k * e).astype(BF16)
        attn = jnp.where(lvl == li + 1, _dot_nt(qt, kt), attn)

    qhat = (q_st * _tile_rows(jnp.exp(b), A_HEADS)).astype(BF16)
    o_inter = _dot(qhat, s.astype(BF16))
    v16 = v.astype(BF16)
    o_intra = jnp.concatenate(
        [_dot(attn[h * CHUNK:(h + 1) * CHUNK].astype(BF16), v16[:, h * A_DV:(h + 1) * A_DV])
         for h in range(A_HEADS)], axis=0)
    b_last = b[CHUNK - 1:CHUNK]
    khat = (k * jnp.exp(b_last - b)).astype(BF16)
    upd = _dot_tn(khat, v16)
    decay = jnp.broadcast_to(jnp.exp(b_last), (A_DV, D_AQK)).T
    s_new = s * decay + jnp.concatenate(
        [upd[h * A_DK:(h + 1) * A_DK, h * A_DV:(h + 1) * A_DV] for h in range(A_HEADS)], axis=0)
    return o_intra + o_inter, s_new


def _layer_kernel(*refs, prompt, final, nch, tq):
    (x_ref, mods_ref, gmix_ref, gmlp_ref, gfin_ref, anorm_ref, win_ref, wa2_ref, ba2_ref, wout_ref,
     wup_ref, wdown_ref, biasb_ref, biasc_ref, sink_ref, tri_ref, lvl_ref) = refs[:17]
    pos = 17
    if not prompt:
        st_ref, cbk_ref, cbv_ref, cck_ref, ccv_ref = refs[pos:pos + 5]
        pos += 5
    y_ref, so_ref, kbo_ref, vbo_ref, kco_ref, vco_ref = refs[pos:pos + 6]
    pos += 6
    h_scr, proj_scr, la_scr, b_scr, mix_scr, x1_scr, st_scr, kbw, vbw, kcw, vcw = refs[pos:]

    step = pl.program_id(0)

    def mod(c, j):
        return mods_ref[0 if prompt else c, j:j + 1, :]

    if prompt:
        @pl.when(step == 0)
        def _():
            st_scr[...] = jnp.zeros_like(st_scr)
            kbw[...] = jnp.zeros_like(kbw)
            vbw[...] = jnp.zeros_like(vbw)
            kcw[...] = jnp.zeros_like(kcw)
            vcw[...] = jnp.zeros_like(vcw)

    gmix = gmix_ref[...]
    for c in range(nch):
        r = slice(c * CHUNK, (c + 1) * CHUNK)
        h = _rms(x_ref[r, :], gmix) * (1.0 + mod(c, 1)) + mod(c, 0)
        h_scr[r, :] = h.astype(BF16)
    proj_scr[...] = _dot(h_scr[...], win_ref[...])
    z = _dot(proj_scr[:, OFF_RA:OFF_RA + LANES].astype(BF16), wa2_ref[...]) + ba2_ref[...]
    la = (jnp.minimum(z, 0.0) - jnp.log1p(jnp.exp(-jnp.abs(z)))) * (1.0 / A_GATE_TAU)
    la_scr[...] = la
    tri = tri_ref[...]
    hi, mid, lo = _split3(la)
    b_scr[...] = _dot(tri, hi) + _dot(tri, mid) + _dot(tri, lo)

    if prompt:
        kbw[B_WINDOW:B_WINDOW + tq, :] = proj_scr[:, OFF_KB:OFF_KB + D_BKV].astype(BF16)
        vbw[B_WINDOW:B_WINDOW + tq, :] = proj_scr[:, OFF_VB:OFF_VB + D_BKV].astype(BF16)
        kcw[C_BAND:C_BAND + tq, :] = proj_scr[:, OFF_KC:OFF_KC + D_C].astype(BF16)
        vcw[C_BAND:C_BAND + tq, :] = proj_scr[:, OFF_VC:OFF_VC + D_C].astype(BF16)

    hm_a = _head_masks(D_AQK, A_DK)
    hm_c = _head_masks(D_C, HEAD_DIM)
    hm_b = _head_masks(D_BKV, HEAD_DIM)
    anorm = anorm_ref[...]

    def chunk_body(c, carry):
        r0 = pl.multiple_of(c * CHUNK, CHUNK)
        rows = pl.ds(r0, CHUNK)

        q = proj_scr[rows, OFF_QA:OFF_QA + D_AQK] * (A_DK ** -0.5)
        k = proj_scr[rows, OFF_KA:OFF_KA + D_AQK]
        v = proj_scr[rows, OFF_VA:OFF_VA + D_AV]
        s_old = st_scr[...] if prompt else st_ref[c]
        o_st, s_new = _gla_chunk(q, k, v, la_scr[rows, :], b_scr[rows, :], s_old, lvl_ref[...], hm_a)
        if prompt:
            st_scr[...] = s_new
        else:
            so_ref[c] = s_new
        g = proj_scr[rows, OFF_GA:OFF_GA + D_AV]
        for h in range(A_HEADS):
            on = _rms(o_st[h * CHUNK:(h + 1) * CHUNK], anorm)
            gh = g[:, h * A_DV:(h + 1) * A_DV]
            mix_scr[rows, h * A_DV:(h + 1) * A_DV] = (on * (gh * jax.nn.sigmoid(gh))).astype(BF16)

        if prompt:
            kwb, vwb = kbw[pl.ds(r0, LK_B), :], vbw[pl.ds(r0, LK_B), :]
            kwc, vwc = kcw[pl.ds(r0, LK_C), :], vcw[pl.ds(r0, LK_C), :]
            gidx = step * nch + c
            lane_b = lax.broadcasted_iota(jnp.int32, (1, LK_B), 1)
            lane_c = lax.broadcasted_iota(jnp.int32, (1, LK_C), 1)
            mask_b = jnp.where(lane_b >= B_WINDOW - gidx * CHUNK, 0.0, NEG_INF)
            mask_c = jnp.where(lane_c >= C_BAND - gidx * CHUNK, 0.0, NEG_INF)
        else:
            kb_new = proj_scr[rows, OFF_KB:OFF_KB + D_BKV]
            vb_new = proj_scr[rows, OFF_VB:OFF_VB + D_BKV]
            kc_new = proj_scr[rows, OFF_KC:OFF_KC + D_C]
            vc_new = proj_scr[rows, OFF_VC:OFF_VC + D_C]
            kwb = jnp.concatenate([cbk_ref[c].astype(BF16), kb_new.astype(BF16)], axis=0)
            vwb = jnp.concatenate([cbv_ref[c].astype(BF16), vb_new.astype(BF16)], axis=0)
            kwc = jnp.concatenate([cck_ref[c].astype(BF16), kc_new.astype(BF16)], axis=0)
            vwc = jnp.concatenate([ccv_ref[c].astype(BF16), vc_new.astype(BF16)], axis=0)
            mask_b = mask_c = None
            kbo_ref[c, 0:B_WINDOW - CHUNK, :] = cbk_ref[c, CHUNK:B_WINDOW, :]
            vbo_ref[c, 0:B_WINDOW - CHUNK, :] = cbv_ref[c, CHUNK:B_WINDOW, :]
            kbo_ref[c, B_WINDOW - CHUNK:B_WINDOW, :] = kb_new
            vbo_ref[c, B_WINDOW - CHUNK:B_WINDOW, :] = vb_new
            kco_ref[c, 0:C_BAND - CHUNK, :] = cck_ref[c, CHUNK:C_BAND, :]
            vco_ref[c, 0:C_BAND - CHUNK, :] = ccv_ref[c, CHUNK:C_BAND, :]
            kco_ref[c, C_BAND - CHUNK:C_BAND, :] = kc_new
            vco_ref[c, C_BAND - CHUNK:C_BAND, :] = vc_new

        qb = proj_scr[rows, OFF_QB:OFF_QB + B_HEADS * HEAD_DIM] * (HEAD_DIM ** -0.5)
        qb_a, qb_b = qb[:, 0:D_BKV], qb[:, D_BKV:2 * D_BKV]
        qb_st = jnp.concatenate([jnp.where(hm_b[0], qb_a, 0.0), jnp.where(hm_b[0], qb_b, 0.0),
                                 jnp.where(hm_b[1], qb_a, 0.0), jnp.where(hm_b[1], qb_b, 0.0)], axis=0)
        ob_st = _attend(qb_st.astype(BF16), kwb, vwb, biasb_ref[...], mask_b, sink_ref[...])
        ob_a = jnp.where(hm_b[0], ob_st[0:CHUNK], ob_st[2 * CHUNK:3 * CHUNK])
        ob_b = jnp.where(hm_b[0], ob_st[CHUNK:2 * CHUNK], ob_st[3 * CHUNK:4 * CHUNK])
        mix_scr[rows, D_AV:D_AV + D_BKV] = ob_a.astype(BF16)
        mix_scr[rows, D_AV + D_BKV:D_AV + 2 * D_BKV] = ob_b.astype(BF16)

        qc = proj_scr[rows, OFF_QC:OFF_QC + D_C] * (HEAD_DIM ** -0.5)
        oc_st = _attend(_stack_masked(qc, hm_c).astype(BF16), kwc, vwc, biasc_ref[...], mask_c, None)
        oc = jnp.where(hm_c[0], oc_st[0:CHUNK], 0.0)
        for h in range(1, C_HEADS):
            oc = jnp.where(hm_c[h], oc_st[h * CHUNK:(h + 1) * CHUNK], oc)
        mix_scr[rows, D_AV + 2 * D_BKV:D_MIX] = oc.astype(BF16)
        return carry

    lax.fori_loop(0, nch, chunk_body, 0)

    if prompt:
        kbw[0:B_WINDOW, :] = kbw[tq:tq + B_WINDOW, :]
        vbw[0:B_WINDOW, :] = vbw[tq:tq + B_WINDOW, :]
        kcw[0:C_BAND, :] = kcw[tq:tq + C_BAND, :]
        vcw[0:C_BAND, :] = vcw[tq:tq + C_BAND, :]
        so_ref[...] = st_scr[...]
        kbo_ref[...] = proj_scr[tq - B_WINDOW:tq, OFF_KB:OFF_KB + D_BKV]
        vbo_ref[...] = proj_scr[tq - B_WINDOW:tq, OFF_VB:OFF_VB + D_BKV]
        rb = kco_ref.shape[0]
        kco_ref[...] = proj_scr[tq - rb:tq, OFF_KC:OFF_KC + D_C]
        vco_ref[...] = proj_scr[tq - rb:tq, OFF_VC:OFF_VC + D_C]

    mixed = _dot(mix_scr[...], wout_ref[...])
    gmlp = gmlp_ref[...]
    for c in range(nch):
        r = slice(c * CHUNK, (c + 1) * CHUNK)
        x1 = x_ref[r, :] + mod(c, 2) * mixed[r, :]
        x1_scr[r, :] = x1
        h2 = _rms(x1, gmlp) * (1.0 + mod(c, 4)) + mod(c, 3)
        h_scr[r, :] = h2.astype(BF16)

    h2 = h_scr[...]
    acc = None
    for j in range(D_FF // FF_BLOCK):
        up = _dot(h2, wup_ref[:, j * FF_BLOCK:(j + 1) * FF_BLOCK])
        act = jnp.square(jnp.maximum(up, 0.0)).astype(BF16)
        part = _dot(act, wdown_ref[j * FF_BLOCK:(j + 1) * FF_BLOCK, :])
        acc = part if acc is None else acc + part

    gfin = gfin_ref[...]
    for c in range(nch):
        r = slice(c * CHUNK, (c + 1) * CHUNK)
        x2 = x1_scr[r, :] + mod(c, 5) * acc[r, :]
        y_ref[r, :] = _rms(x2, gfin) if final else x2


def _layer_spec(arr, layer):
    nd = arr.ndim - 1
    return pl.BlockSpec((None,) + arr.shape[1:], lambda i, _l=layer, _nd=nd: (_l,) + (0,) * _nd,
                        pipeline_mode=pl.Buffered(1))


def _const_spec(arr):
    nd = arr.ndim
    return pl.BlockSpec(arr.shape, lambda i, _nd=nd: (0,) * _nd, pipeline_mode=pl.Buffered(1))


def _run_layer(x, mods, layer, stacked, shared, caches, *, prompt, final, tq, nb):
    rows = x.shape[0]
    nch = tq // CHUNK
    nt = rows // tq
    assert rows % tq == 0 and tq % CHUNK == 0
    (gmix, gmlp, anorm, win, wa2, ba2, wout, wup, wdown, biasc, sink) = stacked
    (gfin, biasb, tri, lvl) = shared

    in_specs = [pl.BlockSpec((tq, D_MODEL), lambda i: (i, 0))]
    if prompt:
        in_specs.append(pl.BlockSpec((1, SUBLANES, D_MODEL), lambda i: (0, 0, 0)))
    else:
        assert nb == nch
        in_specs.append(pl.BlockSpec((nb, SUBLANES, D_MODEL), lambda i: (i, 0, 0)))
    ls = functools.partial(_layer_spec, layer=layer)
    in_specs += [ls(gmix), ls(gmlp), _const_spec(gfin), ls(anorm), ls(win), ls(wa2), ls(ba2), ls(wout),
                 ls(wup), ls(wdown), _const_spec(biasb), ls(biasc), ls(sink), _const_spec(tri), _const_spec(lvl)]
    args = [x, mods, gmix, gmlp, gfin, anorm, win, wa2, ba2, wout, wup, wdown, biasb, biasc, sink, tri, lvl]

    y_spec = pl.BlockSpec((tq, D_MODEL), lambda i: (i, 0))
    if prompt:
        assert tq >= B_WINDOW and (tq % C_BAND == 0 or C_BAND % tq == 0)
        rb = min(tq, C_BAND)
        first = nt - C_BAND // rb
        out_shape = [jax.ShapeDtypeStruct((rows, D_MODEL), F32),
                     jax.ShapeDtypeStruct((D_AQK, A_DV), F32),
                     jax.ShapeDtypeStruct((B_WINDOW, D_BKV), F32),
                     jax.ShapeDtypeStruct((B_WINDOW, D_BKV), F32),
                     jax.ShapeDtypeStruct((C_BAND, D_C), F32),
                     jax.ShapeDtypeStruct((C_BAND, D_C), F32)]
        band = pl.BlockSpec((rb, D_C), lambda i: (jnp.maximum(i - first, 0), 0))
        out_specs = [y_spec,
                     pl.BlockSpec((D_AQK, A_DV), lambda i: (0, 0)),
                     pl.BlockSpec((B_WINDOW, D_BKV), lambda i: (0, 0)),
                     pl.BlockSpec((B_WINDOW, D_BKV), lambda i: (0, 0)),
                     band, band]
        kbw_rows, kcw_rows = B_WINDOW + tq, C_BAND + tq
    else:
        nseq = rows // CHUNK
        st, cbk, cbv, cck, ccv = caches
        seq_spec = lambda r, w: pl.BlockSpec((nb, r, w), lambda i: (i, 0, 0))
        in_specs += [seq_spec(D_AQK, A_DV), seq_spec(B_WINDOW, D_BKV), seq_spec(B_WINDOW, D_BKV),
                     seq_spec(C_BAND, D_C), seq_spec(C_BAND, D_C)]
        args += [st, cbk, cbv, cck, ccv]
        out_shape = [jax.ShapeDtypeStruct((rows, D_MODEL), F32),
                     jax.ShapeDtypeStruct((nseq, D_AQK, A_DV), F32),
                     jax.ShapeDtypeStruct((nseq, B_WINDOW, D_BKV), F32),
                     jax.ShapeDtypeStruct((nseq, B_WINDOW, D_BKV), F32),
                     jax.ShapeDtypeStruct((nseq, C_BAND, D_C), F32),
                     jax.ShapeDtypeStruct((nseq, C_BAND, D_C), F32)]
        out_specs = [y_spec, seq_spec(D_AQK, A_DV), seq_spec(B_WINDOW, D_BKV), seq_spec(B_WINDOW, D_BKV),
                     seq_spec(C_BAND, D_C), seq_spec(C_BAND, D_C)]
        kbw_rows, kcw_rows = SUBLANES * 2, SUBLANES * 2

    scratch = [pltpu.VMEM((tq, D_MODEL), BF16),
               pltpu.VMEM((tq, P_IN), F32),
               pltpu.VMEM((tq, D_AQK), F32),
               pltpu.VMEM((tq, D_AQK), F32),
               pltpu.VMEM((tq, D_MIX), BF16),
               pltpu.VMEM((tq, D_MODEL), F32),
               pltpu.VMEM((D_AQK, A_DV), F32),
               pltpu.VMEM((kbw_rows, D_BKV), BF16), pltpu.VMEM((kbw_rows, D_BKV), BF16),
               pltpu.VMEM((kcw_rows, D_C), BF16), pltpu.VMEM((kcw_rows, D_C), BF16)]

    kern = functools.partial(_layer_kernel, prompt=prompt, final=final, nch=nch, tq=tq)
    return pl.pallas_call(
        kern,
        grid=(nt,),
        in_specs=in_specs,
        out_specs=out_specs,
        out_shape=out_shape,
        scratch_shapes=scratch,
        compiler_params=pltpu.CompilerParams(dimension_semantics=("arbitrary",),
                                             vmem_limit_bytes=VMEM_LIMIT_BYTES),
        name=("layer_prompt" if prompt else "layer_sample") + ("_final" if final else ""),
    )(*args)


ADA_BLOCK = 1536


def _ada_kernel(c_ref, w_ref, b_ref, o_ref):
    c = c_ref[...]
    sc = c * jax.nn.sigmoid(c)
    o_ref[0] = jnp.dot(sc, w_ref[0], preferred_element_type=F32,
                       precision=lax.Precision.HIGHEST) + b_ref[0]


def _ada(c_all, w_ada, b_ada):
    rows = c_all.shape[0]
    n = 6 * D_MODEL
    return pl.pallas_call(
        _ada_kernel,
        grid=(DEPTH, n // ADA_BLOCK),
        in_specs=[pl.BlockSpec((rows, D_MODEL), lambda l, j: (0, 0)),
                  pl.BlockSpec((1, D_MODEL, ADA_BLOCK), lambda l, j: (l, 0, j)),
                  pl.BlockSpec((1, 1, ADA_BLOCK), lambda l, j: (l, 0, j))],
        out_specs=pl.BlockSpec((1, rows, ADA_BLOCK), lambda l, j: (l, 0, j)),
        out_shape=jax.ShapeDtypeStruct((DEPTH, rows, n), F32),
        compiler_params=pltpu.CompilerParams(dimension_semantics=("arbitrary", "arbitrary"),
                                             vmem_limit_bytes=VMEM_LIMIT_BYTES),
        name="adaln",
    )(c_all, w_ada, b_ada.reshape(DEPTH, 1, n))


TB_W = 2 * LANES
TC_W = 5 * LANES
TC_FLAT = TC_W - (CHUNK + C_CLIP + 1)


def _shear(f, width):
    tiled = jnp.broadcast_to(f, (CHUNK, width))
    return pltpu.roll(tiled, width - (CHUNK - 1), 1, stride=1, stride_axis=0)


def _bias_kernel(t5_ref, crel_ref, bucket_ref, ob_ref, oc_ref):
    bucket = bucket_ref[...]
    for h in range(B_HEADS):
        def b_body(i, acc, h=h):
            return jnp.where(bucket == i, t5_ref[h, i], acc)
        f = lax.fori_loop(0, T5_BUCKETS, b_body, jnp.zeros(bucket.shape, F32))
        ob_ref[h] = _shear(f, TB_W)[:, 0:LK_B]
    lane = lax.broadcasted_iota(jnp.int32, (1, TC_W), 1)
    for l in range(DEPTH):
        for h in range(C_HEADS):
            row = crel_ref[l, h:h + 1, :]
            f = jnp.where(lane < TC_FLAT, row[:, 0:1], pltpu.roll(row, TC_FLAT, 1))
            oc_ref[l, h] = _shear(f, TC_W)[:, 0:LK_C]


def _bias_tables(t5_bias, c_rel_bias, bucket):
    smem = pl.BlockSpec(memory_space=pltpu.SMEM)
    vmem = pl.BlockSpec(memory_space=pltpu.VMEM)
    crel = jnp.pad(jnp.swapaxes(c_rel_bias, 1, 2),
                   ((0, 0), (0, SUBLANES - C_HEADS), (0, TC_W - (2 * C_CLIP + 1))))
    return pl.pallas_call(
        _bias_kernel,
        in_specs=[smem, vmem, vmem],
        out_specs=[vmem, vmem],
        out_shape=[jax.ShapeDtypeStruct((B_HEADS, CHUNK, LK_B), F32),
                   jax.ShapeDtypeStruct((DEPTH, C_HEADS, CHUNK, LK_C), F32)],
        name="bias_tables",
    )(t5_bias.T, crel, bucket)


def _t5_bucket(rel):
    half = T5_BUCKETS // 2
    max_exact = half // 2
    n = jnp.abs(rel)
    log_ratio = jnp.log(jnp.maximum(n, 1).astype(jnp.float32) / max_exact) / np.log(T5_MAX_DIST / max_exact)
    large = jnp.minimum(max_exact + (log_ratio * (half - max_exact)).astype(jnp.int32), half - 1)
    return jnp.where(rel > 0, half, 0) + jnp.where(n < max_exact, n, large)


def _gla_constants(tq):
    r = np.arange(tq)
    tri = ((r[:, None] // CHUNK == r[None, :] // CHUNK) & (r[None, :] <= r[:, None])).astype(np.float32)
    t = np.arange(CHUNK)[:, None]
    s = np.arange(CHUNK)[None, :]
    lvl = np.full((CHUNK, CHUNK), -1, np.int32)
    lvl[t == s] = 0
    for li, m in enumerate(GLA_LEVELS):
        sel = ((t // m) % 2 == 1) & ((s // m) == (t // m) - 1)
        lvl[np.broadcast_to(sel, lvl.shape)] = li + 1
    return jnp.asarray(tri, BF16), jnp.asarray(np.tile(lvl, (A_HEADS, 1)))


TQ_PROMPT = 256
NB_SAMPLE = 4


def kernel(x_prompt, x_sample, c_prompt, c_sample, state_gla, cache_b_k, cache_b_v, cache_c_k, cache_c_v,
           w_ada, b_ada, norm_mix_g, norm_mlp_g, w_in, w_a2, b_a2, a_norm_g, b_sink, t5_bias, c_rel_bias,
           w_out, w_up, w_down, final_norm_g):
    bsz, seq, _ = x_prompt.shape
    dec_b, dec_s, _ = x_sample.shape
    assert bsz == 1 and dec_s == CHUNK and TQ_PROMPT == NB_SAMPLE * CHUNK

    n_c = bsz + dec_b
    c_rows = -(-n_c // SUBLANES) * SUBLANES
    c_all = jnp.concatenate([c_prompt, c_sample, jnp.zeros((c_rows - n_c, D_MODEL), F32)], axis=0)
    mods = _ada(c_all, w_ada, b_ada).reshape(DEPTH, c_rows, 6, D_MODEL)
    mods = jnp.pad(mods, ((0, 0), (0, 0), (0, SUBLANES - 6), (0, 0)))

    rel_b = jnp.arange(TB_W) - (CHUNK - 1) - B_WINDOW
    bias_b, bias_c = _bias_tables(t5_bias, c_rel_bias, _t5_bucket(rel_b).astype(jnp.int32)[None, :])
    bias_b = bias_b.reshape(STACK, LK_B)
    bias_c = bias_c.reshape(DEPTH, STACK, LK_C)

    tri, lvl = _gla_constants(TQ_PROMPT)
    qb0 = 1552
    head = lambda h: slice(qb0 + h * HEAD_DIM, qb0 + (h + 1) * HEAD_DIM)
    ob0 = D_AV
    orow = lambda h: slice(ob0 + h * HEAD_DIM, ob0 + (h + 1) * HEAD_DIM)

    win = jnp.concatenate([w_in[:, :, 0:1536], w_in[:, :, head(0)], w_in[:, :, head(2)], w_in[:, :, head(1)],
                           w_in[:, :, head(3)], w_in[:, :, 1808:2832], w_in[:, :, 1536:1552],
                           jnp.zeros((DEPTH, D_MODEL, LANES - A_GATE_RANK), F32)], axis=2).astype(BF16)
    wa2 = jnp.concatenate([w_a2, jnp.zeros((DEPTH, LANES - A_GATE_RANK, D_AQK), F32)], axis=1).astype(BF16)
    wout = jnp.concatenate([w_out[:, 0:D_AV], w_out[:, orow(0)], w_out[:, orow(2)], w_out[:, orow(1)],
                            w_out[:, orow(3)], w_out[:, D_AV + B_HEADS * HEAD_DIM:]], axis=1).astype(BF16)
    sink = jnp.repeat(b_sink, CHUNK, axis=1)[:, :, None]
    stacked = (norm_mix_g[:, None, :], norm_mlp_g[:, None, :], a_norm_g[:, None, :], win, wa2,
               b_a2[:, None, :], wout, w_up.astype(BF16), w_down.astype(BF16), bias_c, sink)
    shared = (final_norm_g[None, :], bias_b, tri, lvl)

    x_p = x_prompt.reshape(seq, D_MODEL)
    x_s = x_sample.reshape(dec_b * dec_s, D_MODEL)
    outs_p, outs_s = [], []
    for l in range(DEPTH):
        final = l == DEPTH - 1
        res_p = _run_layer(x_p, mods[l, 0:bsz], l, stacked, shared, None, prompt=True, final=final,
                           tq=TQ_PROMPT, nb=None)
        x_p = res_p[0]
        outs_p.append(res_p[1:])

        caches = (state_gla[l].reshape(dec_b, D_AQK, A_DV),
                  cache_b_k[l].reshape(dec_b, B_WINDOW, D_BKV), cache_b_v[l].reshape(dec_b, B_WINDOW, D_BKV),
                  cache_c_k[l].reshape(dec_b, C_BAND, D_C), cache_c_v[l].reshape(dec_b, C_BAND, D_C))
        res_s = _run_layer(x_s, mods[l, bsz:bsz + dec_b], l, stacked, shared, caches, prompt=False,
                           final=final, tq=NB_SAMPLE * CHUNK, nb=NB_SAMPLE)
        x_s = res_s[0]
        outs_s.append(res_s[1:])

    sg_p = jnp.stack([o[0].reshape(bsz, A_HEADS, A_DK, A_DV) for o in outs_p])
    kb_p = jnp.stack([o[1].reshape(bsz, B_WINDOW, B_KV_HEADS, HEAD_DIM) for o in outs_p])
    vb_p = jnp.stack([o[2].reshape(bsz, B_WINDOW, B_KV_HEADS, HEAD_DIM) for o in outs_p])
    kc_p = jnp.stack([o[3].reshape(bsz, C_BAND, C_HEADS, HEAD_DIM) for o in outs_p])
    vc_p = jnp.stack([o[4].reshape(bsz, C_BAND, C_HEADS, HEAD_DIM) for o in outs_p])
    sg_s = jnp.stack([o[0].reshape(dec_b, A_HEADS, A_DK, A_DV) for o in outs_s])
    kb_s = jnp.stack([o[1].reshape(dec_b, B_WINDOW, B_KV_HEADS, HEAD_DIM) for o in outs_s])
    vb_s = jnp.stack([o[2].reshape(dec_b, B_WINDOW, B_KV_HEADS, HEAD_DIM) for o in outs_s])
    kc_s = jnp.stack([o[3].reshape(dec_b, C_BAND, C_HEADS, HEAD_DIM) for o in outs_s])
    vc_s = jnp.stack([o[4].reshape(dec_b, C_BAND, C_HEADS, HEAD_DIM) for o in outs_s])
    return (x_p.reshape(bsz, seq, D_MODEL), x_s.reshape(dec_b, dec_s, D_MODEL),
            sg_p, kb_p, vb_p, kc_p, vc_p, sg_s, kb_s, vb_s, kc_s, vc_s)
```

```python
import functools

import numpy as np
import jax
import jax.numpy as jnp
from jax import lax
from jax.experimental import pallas as pl
from jax.experimental.pallas import tpu as pltpu

D_MODEL = 1024
DEPTH = 2
CHUNK = 64
HEAD_DIM = 64
A_HEADS = 4
A_DK = 64
A_DV = 128
A_GATE_RANK = 16
A_GATE_TAU = 16.0
B_HEADS = 4
B_KV_HEADS = 2
B_WINDOW = 128
C_HEADS = 4
C_BAND = 512
C_CLIP = 256
T5_BUCKETS = 32
T5_MAX_DIST = 128
D_FF = 4 * D_MODEL
NORM_EPS = 1e-6
NEG_INF = -1e30

LANES = 128
SUBLANES = 8
VMEM_LIMIT_BYTES = 60000 * 1024

OFF_QA = 0
OFF_KA = OFF_QA + A_HEADS * A_DK
OFF_VA = OFF_KA + A_HEADS * A_DK
OFF_GA = OFF_VA + A_HEADS * A_DV
OFF_QB = OFF_GA + A_HEADS * A_DV
OFF_KB = OFF_QB + B_HEADS * HEAD_DIM
OFF_VB = OFF_KB + B_KV_HEADS * HEAD_DIM
OFF_QC = OFF_VB + B_KV_HEADS * HEAD_DIM
OFF_KC = OFF_QC + C_HEADS * HEAD_DIM
OFF_VC = OFF_KC + C_HEADS * HEAD_DIM
OFF_RA = OFF_VC + C_HEADS * HEAD_DIM
P_IN = OFF_RA + LANES
D_AQK = A_HEADS * A_DK
D_AV = A_HEADS * A_DV
D_BKV = B_KV_HEADS * HEAD_DIM
D_C = C_HEADS * HEAD_DIM
D_MIX = D_AV + B_HEADS * HEAD_DIM + D_C
LK_B = B_WINDOW + CHUNK
LK_C = C_BAND + CHUNK
STACK = 4 * CHUNK
GLA_LEVELS = (1, 2, 4, 8, 16, 32)
FF_BLOCK = 1024

BF16 = jnp.bfloat16
F32 = jnp.float32


def _dot(a, b):
    return jnp.dot(a, b, preferred_element_type=F32)


def _dot_nt(a, b):
    return lax.dot_general(a, b, (((1,), (1,)), ((), ())), preferred_element_type=F32)


def _dot_tn(a, b):
    return lax.dot_general(a, b, (((0,), (0,)), ((), ())), preferred_element_type=F32)


def _rms(x, g):
    return x * lax.rsqrt(jnp.mean(x * x, axis=-1, keepdims=True) + NORM_EPS) * g


def _split3(x):
    hi = x.astype(BF16)
    r1 = x - hi.astype(F32)
    mid = r1.astype(BF16)
    lo = (r1 - mid.astype(F32)).astype(BF16)
    return hi, mid, lo


def _head_masks(width, per_head):
    lane = lax.broadcasted_iota(jnp.int32, (1, width), 1)
    return [(lane >= h * per_head) & (lane < (h + 1) * per_head) for h in range(width // per_head)]


def _stack_masked(x, masks):
    return jnp.concatenate([jnp.where(m, x, 0.0) for m in masks], axis=0)


def _tile_rows(x, n):
    return jnp.concatenate([x] * n, axis=0)


def _attend(q_st, segments, bias_st, maskrow, sink):
    scores, off = [], 0
    for k, v, transposed in segments:
        n = k.shape[1] if transposed else k.shape[0]
        s = (_dot(q_st, k) if transposed else _dot_nt(q_st, k)) + bias_st[:, off:off + n]
        if maskrow is not None:
            s = s + maskrow[:, off:off + n]
        scores.append(s)
        off += n
    m = functools.reduce(jnp.maximum, [jnp.max(s, axis=1, keepdims=True) for s in scores])
    if sink is not None:
        m = jnp.maximum(m, sink)
    es = [jnp.exp(s - m) for s in scores]
    l = functools.reduce(jnp.add, [jnp.sum(e, axis=1, keepdims=True) for e in es])
    if sink is not None:
        l = l + jnp.exp(sink - m)
    o = None
    for e, (k, v, transposed) in zip(es, segments):
        part = _dot_nt(e.astype(BF16), v) if transposed else _dot(e.astype(BF16), v)
        o = part if o is None else o + part
    return o / l


def _level_exponent(b, m, t):
    bcast = lambda i: jnp.broadcast_to(b[i:i + 1], (SUBLANES, b.shape[1]))
    low_half = lax.broadcasted_iota(jnp.int32, (SUBLANES, b.shape[1]), 0) < SUBLANES // 2
    parts = []
    for g in range(CHUNK // SUBLANES):
        r = g * SUBLANES
        rows = b[r:r + SUBLANES]
        if m >= SUBLANES:
            ref = b[(r // (2 * m)) * 2 * m + m - 1:(r // (2 * m)) * 2 * m + m]
            parts.append(rows - ref if (r // m) % 2 == 1 else ref - rows)
        elif m == 4:
            parts.append(rows - bcast(r + 3))
        else:
            parts.append(rows - jnp.where(low_half, bcast(r + 1), bcast(r + 5)))
    d = jnp.concatenate(parts, axis=0)
    return d if m >= SUBLANES else jnp.where((t & m) != 0, d, -d)


def _gla_chunk(q, k, v, la, b, s, lvl, hm_a):
    t = lax.broadcasted_iota(jnp.int32, (CHUNK, D_AQK), 0)
    expo = {1: la}
    for m in GLA_LEVELS[1:]:
        expo[m] = _level_exponent(b, m, t)

    q_st = _stack_masked(q, hm_a)
    k16 = k.astype(BF16)
    attn = jnp.where(lvl == 0, _dot_nt(q_st.astype(BF16), k16), 0.0)
    for li, m in enumerate(GLA_LEVELS):
        e = jnp.exp(expo[m])
        qt = (q_st * _tile_rows(e, A_HEADS)).astype(BF16)
        kt = k16 if m == 1 else (k * e).astype(BF16)
        attn = jnp.where(lvl == li + 1, _dot_nt(qt, kt), attn)

    qhat = (q_st * _tile_rows(jnp.exp(b), A_HEADS)).astype(BF16)
    o_inter = _dot(qhat, s.astype(BF16))
    v16 = v.astype(BF16)
    o_intra = jnp.concatenate(
        [_dot(attn[h * CHUNK:(h + 1) * CHUNK].astype(BF16), v16[:, h * A_DV:(h + 1) * A_DV])
         for h in range(A_HEADS)], axis=0)
    b_last = b[CHUNK - 1:CHUNK]
    khat = (k * jnp.exp(b_last - b)).astype(BF16)
    upd = _dot_tn(khat, v16)
    decay = jnp.broadcast_to(jnp.exp(b_last), (A_DV, D_AQK)).T
    s_new = s * decay + jnp.concatenate(
        [upd[h * A_DK:(h + 1) * A_DK, h * A_DV:(h + 1) * A_DV] for h in range(A_HEADS)], axis=0)
    return o_intra + o_inter, s_new


def _layer_kernel(*refs, prompt, final, nch, tq):
    (x_ref, mods_ref, gmix_ref, gmlp_ref, gfin_ref, anorm_ref, win_ref, wa2_ref, ba2_ref, wout_ref,
     wup_ref, wdown_ref, biasb_ref, biasc_ref, sink_ref, tri_ref, lvl_ref) = refs[:17]
    pos = 17
    if not prompt:
        st_ref, cbk_ref, cbv_ref, cck_ref, ccv_ref = refs[pos:pos + 5]
        pos += 5
    y_ref, so_ref, kbo_ref, vbo_ref, kco_ref, vco_ref = refs[pos:pos + 6]
    pos += 6
    h_scr, h2_scr, proj_scr, la_scr, b_scr, mix_scr, x1_scr, st_scr, kbw, vbw, kcw, vcw = refs[pos:]

    step = pl.program_id(0)

    def mod(c, j):
        return mods_ref[0 if prompt else c, j:j + 1, :]

    if prompt:
        @pl.when(step == 0)
        def _():
            st_scr[...] = jnp.zeros_like(st_scr)
            kbw[...] = jnp.zeros_like(kbw)
            vbw[...] = jnp.zeros_like(vbw)
            kcw[...] = jnp.zeros_like(kcw)
            vcw[...] = jnp.zeros_like(vcw)

    gmix = gmix_ref[...]
    for c in range(nch):
        r = slice(c * CHUNK, (c + 1) * CHUNK)
        h = _rms(x_ref[r, :], gmix) * (1.0 + mod(c, 1)) + mod(c, 0)
        h_scr[r, :] = h.astype(BF16)
    proj_scr[...] = _dot(h_scr[...], win_ref[...])
    z = _dot(proj_scr[:, OFF_RA:OFF_RA + LANES].astype(BF16), wa2_ref[...]) + ba2_ref[...]
    la = (jnp.minimum(z, 0.0) - jnp.log1p(jnp.exp(-jnp.abs(z)))) * (1.0 / A_GATE_TAU)
    la_scr[...] = la
    tri = tri_ref[...]
    hi, mid, lo = _split3(la)
    b_scr[...] = _dot(tri, hi) + _dot(tri, mid) + _dot(tri, lo)

    if prompt:
        kbw[B_WINDOW:B_WINDOW + tq, :] = proj_scr[:, OFF_KB:OFF_KB + D_BKV].astype(BF16)
        vbw[B_WINDOW:B_WINDOW + tq, :] = proj_scr[:, OFF_VB:OFF_VB + D_BKV].astype(BF16)
        kcw[C_BAND:C_BAND + tq, :] = proj_scr[:, OFF_KC:OFF_KC + D_C].astype(BF16)
        vcw[C_BAND:C_BAND + tq, :] = proj_scr[:, OFF_VC:OFF_VC + D_C].astype(BF16)

    hm_a = _head_masks(D_AQK, A_DK)
    hm_c = _head_masks(D_C, HEAD_DIM)
    hm_b = _head_masks(D_BKV, HEAD_DIM)
    anorm = anorm_ref[...]

    def chunk_body(c, carry):
        r0 = pl.multiple_of(c * CHUNK, CHUNK)
        rows = pl.ds(r0, CHUNK)

        q = proj_scr[rows, OFF_QA:OFF_QA + D_AQK] * (A_DK ** -0.5)
        k = proj_scr[rows, OFF_KA:OFF_KA + D_AQK]
        v = proj_scr[rows, OFF_VA:OFF_VA + D_AV]
        s_old = st_scr[...] if prompt else st_ref[c]
        o_st, s_new = _gla_chunk(q, k, v, la_scr[rows, :], b_scr[rows, :], s_old, lvl_ref[...], hm_a)
        if prompt:
            st_scr[...] = s_new
        else:
            so_ref[c] = s_new
        g = proj_scr[rows, OFF_GA:OFF_GA + D_AV]
        for h in range(A_HEADS):
            on = _rms(o_st[h * CHUNK:(h + 1) * CHUNK], anorm)
            gh = g[:, h * A_DV:(h + 1) * A_DV]
            mix_scr[rows, h * A_DV:(h + 1) * A_DV] = (on * (gh * jax.nn.sigmoid(gh))).astype(BF16)

        if prompt:
            seg_b = [(kbw[pl.ds(r0, LK_B), :], vbw[pl.ds(r0, LK_B), :], False)]
            seg_c = [(kcw[pl.ds(r0, LK_C), :], vcw[pl.ds(r0, LK_C), :], False)]
            gidx = step * nch + c
            lane_b = lax.broadcasted_iota(jnp.int32, (1, LK_B), 1)
            lane_c = lax.broadcasted_iota(jnp.int32, (1, LK_C), 1)
            mask_b = jnp.where(lane_b >= B_WINDOW - gidx * CHUNK, 0.0, NEG_INF)
            mask_c = jnp.where(lane_c >= C_BAND - gidx * CHUNK, 0.0, NEG_INF)
        else:
            kb_new = proj_scr[rows, OFF_KB:OFF_KB + D_BKV]
            vb_new = proj_scr[rows, OFF_VB:OFF_VB + D_BKV]
            kc_new = proj_scr[rows, OFF_KC:OFF_KC + D_C]
            vc_new = proj_scr[rows, OFF_VC:OFF_VC + D_C]
            seg_b = [(cbk_ref[c].astype(BF16), cbv_ref[c].astype(BF16), True),
                     (kb_new.astype(BF16), vb_new.astype(BF16), False)]
            seg_c = [(cck_ref[c].astype(BF16), ccv_ref[c].astype(BF16), True),
                     (kc_new.astype(BF16), vc_new.astype(BF16), False)]
            mask_b = mask_c = None
            for dst, src, new, width in ((kbo_ref, cbk_ref, kb_new, B_WINDOW), (vbo_ref, cbv_ref, vb_new, B_WINDOW),
                                         (kco_ref, cck_ref, kc_new, C_BAND), (vco_ref, ccv_ref, vc_new, C_BAND)):
                dst[c, :, 0:width - CHUNK] = src[c, :, CHUNK:width]
                dst[c, :, width - CHUNK:width] = new.T

        qb = proj_scr[rows, OFF_QB:OFF_QB + B_HEADS * HEAD_DIM] * (HEAD_DIM ** -0.5)
        qb_a, qb_b = qb[:, 0:D_BKV], qb[:, D_BKV:2 * D_BKV]
        qb_st = jnp.concatenate([jnp.where(hm_b[0], qb_a, 0.0), jnp.where(hm_b[0], qb_b, 0.0),
                                 jnp.where(hm_b[1], qb_a, 0.0), jnp.where(hm_b[1], qb_b, 0.0)], axis=0)
        ob_st = _attend(qb_st.astype(BF16), seg_b, biasb_ref[...], mask_b, sink_ref[...])
        ob_a = jnp.where(hm_b[0], ob_st[0:CHUNK], ob_st[2 * CHUNK:3 * CHUNK])
        ob_b = jnp.where(hm_b[0], ob_st[CHUNK:2 * CHUNK], ob_st[3 * CHUNK:4 * CHUNK])
        mix_scr[rows, D_AV:D_AV + D_BKV] = ob_a.astype(BF16)
        mix_scr[rows, D_AV + D_BKV:D_AV + 2 * D_BKV] = ob_b.astype(BF16)

        qc = proj_scr[rows, OFF_QC:OFF_QC + D_C] * (HEAD_DIM ** -0.5)
        oc_st = _attend(_stack_masked(qc, hm_c).astype(BF16), seg_c, biasc_ref[...], mask_c, None)
        oc = jnp.where(hm_c[0], oc_st[0:CHUNK], 0.0)
        for h in range(1, C_HEADS):
            oc = jnp.where(hm_c[h], oc_st[h * CHUNK:(h + 1) * CHUNK], oc)
        mix_scr[rows, D_AV + 2 * D_BKV:D_MIX] = oc.astype(BF16)
        return carry

    lax.fori_loop(0, nch, chunk_body, 0)

    if prompt:
        kbw[0:B_WINDOW, :] = kbw[tq:tq + B_WINDOW, :]
        vbw[0:B_WINDOW, :] = vbw[tq:tq + B_WINDOW, :]
        kcw[0:C_BAND, :] = kcw[tq:tq + C_BAND, :]
        vcw[0:C_BAND, :] = vcw[tq:tq + C_BAND, :]
        so_ref[...] = st_scr[...]
        kbo_ref[...] = proj_scr[tq - B_WINDOW:tq, OFF_KB:OFF_KB + D_BKV]
        vbo_ref[...] = proj_scr[tq - B_WINDOW:tq, OFF_VB:OFF_VB + D_BKV]
        rb = kco_ref.shape[0]
        kco_ref[...] = proj_scr[tq - rb:tq, OFF_KC:OFF_KC + D_C]
        vco_ref[...] = proj_scr[tq - rb:tq, OFF_VC:OFF_VC + D_C]

    mixed = _dot(mix_scr[...], wout_ref[...])
    gmlp = gmlp_ref[...]
    for c in range(nch):
        r = slice(c * CHUNK, (c + 1) * CHUNK)
        x1 = x_ref[r, :] + mod(c, 2) * mixed[r, :]
        x1_scr[r, :] = x1
        h2 = _rms(x1, gmlp) * (1.0 + mod(c, 4)) + mod(c, 3)
        h2_scr[r, :] = h2.astype(BF16)

    h2 = h2_scr[...]
    acc = None
    for j in range(D_FF // FF_BLOCK):
        up = _dot(h2, wup_ref[:, j * FF_BLOCK:(j + 1) * FF_BLOCK])
        act = jnp.square(jnp.maximum(up, 0.0)).astype(BF16)
        part = _dot(act, wdown_ref[j * FF_BLOCK:(j + 1) * FF_BLOCK, :])
        acc = part if acc is None else acc + part

    gfin = gfin_ref[...]
    for c in range(nch):
        r = slice(c * CHUNK, (c + 1) * CHUNK)
        x2 = x1_scr[r, :] + mod(c, 5) * acc[r, :]
        y_ref[r, :] = _rms(x2, gfin) if final else x2


def _layer_spec(arr, layer):
    nd = arr.ndim - 1
    return pl.BlockSpec((None,) + arr.shape[1:], lambda i, _l=layer, _nd=nd: (_l,) + (0,) * _nd,
                        pipeline_mode=pl.Buffered(1))


def _const_spec(arr):
    nd = arr.ndim
    return pl.BlockSpec(arr.shape, lambda i, _nd=nd: (0,) * _nd, pipeline_mode=pl.Buffered(1))


def _run_layer(x, mods, layer, stacked, shared, caches, *, prompt, final, tq, nb):
    rows = x.shape[0]
    nch = tq // CHUNK
    nt = rows // tq
    assert rows % tq == 0 and tq % CHUNK == 0
    (gmix, gmlp, anorm, win, wa2, ba2, wout, wup, wdown, biasc, sink) = stacked
    (gfin, biasb, tri, lvl) = shared

    in_specs = [pl.BlockSpec((tq, D_MODEL), lambda i: (i, 0))]
    if prompt:
        in_specs.append(pl.BlockSpec((1, SUBLANES, D_MODEL), lambda i: (0, 0, 0)))
    else:
        assert nb == nch
        in_specs.append(pl.BlockSpec((nb, SUBLANES, D_MODEL), lambda i: (i, 0, 0)))
    ls = functools.partial(_layer_spec, layer=layer)
    in_specs += [ls(gmix), ls(gmlp), _const_spec(gfin), ls(anorm), ls(win), ls(wa2), ls(ba2), ls(wout),
                 ls(wup), ls(wdown), _const_spec(biasb), ls(biasc), ls(sink), _const_spec(tri), _const_spec(lvl)]
    args = [x, mods, gmix, gmlp, gfin, anorm, win, wa2, ba2, wout, wup, wdown, biasb, biasc, sink, tri, lvl]

    y_spec = pl.BlockSpec((tq, D_MODEL), lambda i: (i, 0))
    if prompt:
        assert tq >= B_WINDOW and (tq % C_BAND == 0 or C_BAND % tq == 0)
        rb = min(tq, C_BAND)
        first = nt - C_BAND // rb
        out_shape = [jax.ShapeDtypeStruct((rows, D_MODEL), F32),
                     jax.ShapeDtypeStruct((D_AQK, A_DV), F32),
                     jax.ShapeDtypeStruct((B_WINDOW, D_BKV), F32),
                     jax.ShapeDtypeStruct((B_WINDOW, D_BKV), F32),
                     jax.ShapeDtypeStruct((C_BAND, D_C), F32),
                     jax.ShapeDtypeStruct((C_BAND, D_C), F32)]
        band = pl.BlockSpec((rb, D_C), lambda i: (jnp.maximum(i - first, 0), 0))
        out_specs = [y_spec,
                     pl.BlockSpec((D_AQK, A_DV), lambda i: (0, 0)),
                     pl.BlockSpec((B_WINDOW, D_BKV), lambda i: (0, 0)),
                     pl.BlockSpec((B_WINDOW, D_BKV), lambda i: (0, 0)),
                     band, band]
        kbw_rows, kcw_rows = B_WINDOW + tq, C_BAND + tq
    else:
        nseq = rows // CHUNK
        st, cbk, cbv, cck, ccv = caches
        seq_in = lambda r, w: pl.BlockSpec((None, nb, r, w), lambda i, _l=layer: (_l, i, 0, 0))
        seq_out = lambda r, w: pl.BlockSpec((nb, r, w), lambda i: (i, 0, 0))
        in_specs += [seq_in(D_AQK, A_DV), seq_in(D_BKV, B_WINDOW), seq_in(D_BKV, B_WINDOW),
                     seq_in(D_C, C_BAND), seq_in(D_C, C_BAND)]
        args += [st, cbk, cbv, cck, ccv]
        out_shape = [jax.ShapeDtypeStruct((rows, D_MODEL), F32),
                     jax.ShapeDtypeStruct((nseq, D_AQK, A_DV), F32),
                     jax.ShapeDtypeStruct((nseq, D_BKV, B_WINDOW), F32),
                     jax.ShapeDtypeStruct((nseq, D_BKV, B_WINDOW), F32),
                     jax.ShapeDtypeStruct((nseq, D_C, C_BAND), F32),
                     jax.ShapeDtypeStruct((nseq, D_C, C_BAND), F32)]
        out_specs = [y_spec, seq_out(D_AQK, A_DV), seq_out(D_BKV, B_WINDOW), seq_out(D_BKV, B_WINDOW),
                     seq_out(D_C, C_BAND), seq_out(D_C, C_BAND)]
        kbw_rows, kcw_rows = SUBLANES * 2, SUBLANES * 2

    scratch = [pltpu.VMEM((tq, D_MODEL), BF16),
               pltpu.VMEM((tq, D_MODEL), BF16),
               pltpu.VMEM((tq, P_IN), F32),
               pltpu.VMEM((tq, D_AQK), F32),
               pltpu.VMEM((tq, D_AQK), F32),
               pltpu.VMEM((tq, D_MIX), BF16),
               pltpu.VMEM((tq, D_MODEL), F32),
               pltpu.VMEM((D_AQK, A_DV), F32),
               pltpu.VMEM((kbw_rows, D_BKV), BF16), pltpu.VMEM((kbw_rows, D_BKV), BF16),
               pltpu.VMEM((kcw_rows, D_C), BF16), pltpu.VMEM((kcw_rows, D_C), BF16)]

    kern = functools.partial(_layer_kernel, prompt=prompt, final=final, nch=nch, tq=tq)
    return pl.pallas_call(
        kern,
        grid=(nt,),
        in_specs=in_specs,
        out_specs=out_specs,
        out_shape=out_shape,
        scratch_shapes=scratch,
        compiler_params=pltpu.CompilerParams(dimension_semantics=("arbitrary",),
                                             vmem_limit_bytes=VMEM_LIMIT_BYTES),
        name=("layer_prompt" if prompt else "layer_sample") + ("_final" if final else ""),
    )(*args)


ADA_BLOCK = 1536


def _ada_kernel(c_ref, w_ref, b_ref, o_ref):
    c = c_ref[...]
    sc = c * jax.nn.sigmoid(c)
    o_ref[0] = jnp.dot(sc, w_ref[0], preferred_element_type=F32,
                       precision=lax.Precision.HIGHEST) + b_ref[0]


def _ada(c_all, w_ada, b_ada):
    rows = c_all.shape[0]
    n = 6 * D_MODEL
    return pl.pallas_call(
        _ada_kernel,
        grid=(DEPTH, n // ADA_BLOCK),
        in_specs=[pl.BlockSpec((rows, D_MODEL), lambda l, j: (0, 0)),
                  pl.BlockSpec((1, D_MODEL, ADA_BLOCK), lambda l, j: (l, 0, j)),
                  pl.BlockSpec((1, 1, ADA_BLOCK), lambda l, j: (l, 0, j))],
        out_specs=pl.BlockSpec((1, rows, ADA_BLOCK), lambda l, j: (l, 0, j)),
        out_shape=jax.ShapeDtypeStruct((DEPTH, rows, n), F32),
        compiler_params=pltpu.CompilerParams(dimension_semantics=("arbitrary", "arbitrary"),
                                             vmem_limit_bytes=VMEM_LIMIT_BYTES),
        name="adaln",
    )(c_all, w_ada, b_ada.reshape(DEPTH, 1, n))


TB_W = 2 * LANES
TC_W = 5 * LANES
TC_FLAT = TC_W - (CHUNK + C_CLIP + 1)


def _shear(f, width):
    tiled = jnp.broadcast_to(f, (CHUNK, width))
    return pltpu.roll(tiled, width - (CHUNK - 1), 1, stride=1, stride_axis=0)


def _bias_kernel(t5_ref, crel_ref, bucket_ref, ob_ref, oc_ref):
    bucket = bucket_ref[...]
    for h in range(B_HEADS):
        def b_body(i, acc, h=h):
            return jnp.where(bucket == i, t5_ref[h, i], acc)
        f = lax.fori_loop(0, T5_BUCKETS, b_body, jnp.zeros(bucket.shape, F32))
        ob_ref[h] = _shear(f, TB_W)[:, 0:LK_B]
    lane = lax.broadcasted_iota(jnp.int32, (1, TC_W), 1)
    for l in range(DEPTH):
        for h in range(C_HEADS):
            row = crel_ref[l, h:h + 1, :]
            f = jnp.where(lane < TC_FLAT, row[:, 0:1], pltpu.roll(row, TC_FLAT, 1))
            oc_ref[l, h] = _shear(f, TC_W)[:, 0:LK_C]


def _bias_tables(t5_bias, c_rel_bias, bucket):
    smem = pl.BlockSpec(memory_space=pltpu.SMEM)
    vmem = pl.BlockSpec(memory_space=pltpu.VMEM)
    crel = jnp.pad(jnp.swapaxes(c_rel_bias, 1, 2),
                   ((0, 0), (0, SUBLANES - C_HEADS), (0, TC_W - (2 * C_CLIP + 1))))
    return pl.pallas_call(
        _bias_kernel,
        in_specs=[smem, vmem, vmem],
        out_specs=[vmem, vmem],
        out_shape=[jax.ShapeDtypeStruct((B_HEADS, CHUNK, LK_B), F32),
                   jax.ShapeDtypeStruct((DEPTH, C_HEADS, CHUNK, LK_C), F32)],
        name="bias_tables",
    )(t5_bias.T, crel, bucket)


def _t5_bucket(rel):
    half = T5_BUCKETS // 2
    max_exact = half // 2
    n = jnp.abs(rel)
    log_ratio = jnp.log(jnp.maximum(n, 1).astype(jnp.float32) / max_exact) / np.log(T5_MAX_DIST / max_exact)
    large = jnp.minimum(max_exact + (log_ratio * (half - max_exact)).astype(jnp.int32), half - 1)
    return jnp.where(rel > 0, half, 0) + jnp.where(n < max_exact, n, large)


def _gla_constants(tq):
    r = np.arange(tq)
    tri = ((r[:, None] // CHUNK == r[None, :] // CHUNK) & (r[None, :] <= r[:, None])).astype(np.float32)
    t = np.arange(CHUNK)[:, None]
    s = np.arange(CHUNK)[None, :]
    lvl = np.full((CHUNK, CHUNK), -1, np.int32)
    lvl[t == s] = 0
    for li, m in enumerate(GLA_LEVELS):
        sel = ((t // m) % 2 == 1) & ((s // m) == (t // m) - 1)
        lvl[np.broadcast_to(sel, lvl.shape)] = li + 1
    return jnp.asarray(tri, BF16), jnp.asarray(np.tile(lvl, (A_HEADS, 1)))


TQ_PROMPT = 256
NB_SAMPLE = 4


def kernel(x_prompt, x_sample, c_prompt, c_sample, state_gla, cache_b_k, cache_b_v, cache_c_k, cache_c_v,
           w_ada, b_ada, norm_mix_g, norm_mlp_g, w_in, w_a2, b_a2, a_norm_g, b_sink, t5_bias, c_rel_bias,
           w_out, w_up, w_down, final_norm_g):
    bsz, seq, _ = x_prompt.shape
    dec_b, dec_s, _ = x_sample.shape
    assert bsz == 1 and dec_s == CHUNK and TQ_PROMPT == NB_SAMPLE * CHUNK

    n_c = bsz + dec_b
    c_rows = -(-n_c // SUBLANES) * SUBLANES
    c_all = jnp.concatenate([c_prompt, c_sample, jnp.zeros((c_rows - n_c, D_MODEL), F32)], axis=0)
    mods = _ada(c_all, w_ada, b_ada).reshape(DEPTH, c_rows, 6, D_MODEL)
    mods = jnp.pad(mods, ((0, 0), (0, 0), (0, SUBLANES - 6), (0, 0)))

    rel_b = jnp.arange(TB_W) - (CHUNK - 1) - B_WINDOW
    bias_b, bias_c = _bias_tables(t5_bias, c_rel_bias, _t5_bucket(rel_b).astype(jnp.int32)[None, :])
    bias_b = bias_b.reshape(STACK, LK_B)
    bias_c = bias_c.reshape(DEPTH, STACK, LK_C)

    tri, lvl = _gla_constants(TQ_PROMPT)
    qb0 = 1552
    head = lambda h: slice(qb0 + h * HEAD_DIM, qb0 + (h + 1) * HEAD_DIM)
    ob0 = D_AV
    orow = lambda h: slice(ob0 + h * HEAD_DIM, ob0 + (h + 1) * HEAD_DIM)

    win = jnp.concatenate([w_in[:, :, 0:1536], w_in[:, :, head(0)], w_in[:, :, head(2)], w_in[:, :, head(1)],
                           w_in[:, :, head(3)], w_in[:, :, 1808:2832], w_in[:, :, 1536:1552],
                           jnp.zeros((DEPTH, D_MODEL, LANES - A_GATE_RANK), F32)], axis=2).astype(BF16)
    wa2 = jnp.concatenate([w_a2, jnp.zeros((DEPTH, LANES - A_GATE_RANK, D_AQK), F32)], axis=1).astype(BF16)
    wout = jnp.concatenate([w_out[:, 0:D_AV], w_out[:, orow(0)], w_out[:, orow(2)], w_out[:, orow(1)],
                            w_out[:, orow(3)], w_out[:, D_AV + B_HEADS * HEAD_DIM:]], axis=1).astype(BF16)
    sink = jnp.repeat(b_sink, CHUNK, axis=1)[:, :, None]
    stacked = (norm_mix_g[:, None, :], norm_mlp_g[:, None, :], a_norm_g[:, None, :], win, wa2,
               b_a2[:, None, :], wout, w_up.astype(BF16), w_down.astype(BF16), bias_c, sink)
    shared = (final_norm_g[None, :], bias_b, tri, lvl)

    x_p = x_prompt.reshape(seq, D_MODEL)
    x_s = x_sample.reshape(dec_b * dec_s, D_MODEL)
    outs_p, outs_s = [], []
    fmajor = lambda cache: jnp.transpose(cache, (0, 1, 3, 4, 2)).reshape(
        DEPTH, dec_b, cache.shape[3] * cache.shape[4], cache.shape[2])
    caches = (state_gla.reshape(DEPTH, dec_b, D_AQK, A_DV), fmajor(cache_b_k), fmajor(cache_b_v),
              fmajor(cache_c_k), fmajor(cache_c_v))
    for l in range(DEPTH):
        final = l == DEPTH - 1
        res_p = _run_layer(x_p, mods[l, 0:bsz], l, stacked, shared, None, prompt=True, final=final,
                           tq=TQ_PROMPT, nb=None)
        x_p = res_p[0]
        outs_p.append(res_p[1:])

        res_s = _run_layer(x_s, mods[l, bsz:bsz + dec_b], l, stacked, shared, caches, prompt=False,
                           final=final, tq=NB_SAMPLE * CHUNK, nb=NB_SAMPLE)
        x_s = res_s[0]
        outs_s.append(res_s[1:])

    sg_p = jnp.stack([o[0].reshape(bsz, A_HEADS, A_DK, A_DV) for o in outs_p])
    kb_p = jnp.stack([o[1].reshape(bsz, B_WINDOW, B_KV_HEADS, HEAD_DIM) for o in outs_p])
    vb_p = jnp.stack([o[2].reshape(bsz, B_WINDOW, B_KV_HEADS, HEAD_DIM) for o in outs_p])
    kc_p = jnp.stack([o[3].reshape(bsz, C_BAND, C_HEADS, HEAD_DIM) for o in outs_p])
    vc_p = jnp.stack([o[4].reshape(bsz, C_BAND, C_HEADS, HEAD_DIM) for o in outs_p])
    sg_s = jnp.stack([o[0].reshape(dec_b, A_HEADS, A_DK, A_DV) for o in outs_s])
    tmajor = lambda arrs, heads: jnp.transpose(
        jnp.stack(arrs).reshape(DEPTH, dec_b, heads, HEAD_DIM, -1), (0, 1, 4, 2, 3))
    kb_s = tmajor([o[1] for o in outs_s], B_KV_HEADS)
    vb_s = tmajor([o[2] for o in outs_s], B_KV_HEADS)
    kc_s = tmajor([o[3] for o in outs_s], C_HEADS)
    vc_s = tmajor([o[4] for o in outs_s], C_HEADS)
    return (x_p.reshape(bsz, seq, D_MODEL), x_s.reshape(dec_b, dec_s, D_MODEL),
            sg_p, kb_p, vb_p, kc_p, vc_p, sg_s, kb_s, vb_s, kc_s, vc_s)
```

```python
import functools

import numpy as np
import jax
import jax.numpy as jnp
from jax import lax
from jax.experimental import pallas as pl
from jax.experimental.pallas import tpu as pltpu

D_MODEL = 1024
DEPTH = 2
CHUNK = 64
HEAD_DIM = 64
A_HEADS = 4
A_DK = 64
A_DV = 128
A_GATE_RANK = 16
A_GATE_TAU = 16.0
B_HEADS = 4
B_KV_HEADS = 2
B_WINDOW = 128
C_HEADS = 4
C_BAND = 512
C_CLIP = 256
T5_BUCKETS = 32
T5_MAX_DIST = 128
D_FF = 4 * D_MODEL
NORM_EPS = 1e-6
NEG_INF = -1e30

LANES = 128
SUBLANES = 8
VMEM_LIMIT_BYTES = 60000 * 1024

OFF_QA = 0
OFF_KA = OFF_QA + A_HEADS * A_DK
OFF_VA = OFF_KA + A_HEADS * A_DK
OFF_GA = OFF_VA + A_HEADS * A_DV
OFF_QB = OFF_GA + A_HEADS * A_DV
OFF_KB = OFF_QB + B_HEADS * HEAD_DIM
OFF_VB = OFF_KB + B_KV_HEADS * HEAD_DIM
OFF_QC = OFF_VB + B_KV_HEADS * HEAD_DIM
OFF_KC = OFF_QC + C_HEADS * HEAD_DIM
OFF_VC = OFF_KC + C_HEADS * HEAD_DIM
OFF_RA = OFF_VC + C_HEADS * HEAD_DIM
P_IN = OFF_RA + LANES
D_AQK = A_HEADS * A_DK
D_AV = A_HEADS * A_DV
D_BKV = B_KV_HEADS * HEAD_DIM
D_C = C_HEADS * HEAD_DIM
D_MIX = D_AV + B_HEADS * HEAD_DIM + D_C
LK_B = B_WINDOW + CHUNK
LK_C = C_BAND + CHUNK
STACK = 4 * CHUNK
GLA_LEVELS = (1, 2, 4, 8, 16, 32)
FF_BLOCK = 1024
N_FF = D_FF // FF_BLOCK

BF16 = jnp.bfloat16
F32 = jnp.float32


def _dot(a, b):
    return jnp.dot(a, b, preferred_element_type=F32)


def _dot_nt(a, b):
    return lax.dot_general(a, b, (((1,), (1,)), ((), ())), preferred_element_type=F32)


def _dot_tn(a, b):
    return lax.dot_general(a, b, (((0,), (0,)), ((), ())), preferred_element_type=F32)


def _rms(x, g):
    return x * lax.rsqrt(jnp.mean(x * x, axis=-1, keepdims=True) + NORM_EPS) * g


def _split3(x):
    hi = x.astype(BF16)
    r1 = x - hi.astype(F32)
    mid = r1.astype(BF16)
    lo = (r1 - mid.astype(F32)).astype(BF16)
    return hi, mid, lo


def _head_masks(width, per_head):
    lane = lax.broadcasted_iota(jnp.int32, (1, width), 1)
    return [(lane >= h * per_head) & (lane < (h + 1) * per_head) for h in range(width // per_head)]


def _stack_masked(x, masks):
    return jnp.concatenate([jnp.where(m, x, 0.0) for m in masks], axis=0)


def _tile_rows(x, n):
    return jnp.concatenate([x] * n, axis=0)


def _scores(q_st, segments, bias_st, maskrow):
    scores, off = [], 0
    for k, v, transposed in segments:
        n = k.shape[1] if transposed else k.shape[0]
        s = (_dot(q_st, k) if transposed else _dot_nt(q_st, k)) + bias_st[:, off:off + n]
        if maskrow is not None:
            s = s + maskrow[:, off:off + n]
        scores.append(s)
        off += n
    return scores


def _softmax_pv(scores, segments, sink):
    m = functools.reduce(jnp.maximum, [jnp.max(s, axis=1, keepdims=True) for s in scores])
    if sink is not None:
        m = jnp.maximum(m, sink)
    es = [jnp.exp(s - m) for s in scores]
    l = functools.reduce(jnp.add, [jnp.sum(e, axis=1, keepdims=True) for e in es])
    if sink is not None:
        l = l + jnp.exp(sink - m)
    o = None
    for e, (k, v, transposed) in zip(es, segments):
        part = _dot_nt(e.astype(BF16), v) if transposed else _dot(e.astype(BF16), v)
        o = part if o is None else o + part
    return o / l


def _level_exponent(b, m, t):
    bcast = lambda i: jnp.broadcast_to(b[i:i + 1], (SUBLANES, b.shape[1]))
    low_half = lax.broadcasted_iota(jnp.int32, (SUBLANES, b.shape[1]), 0) < SUBLANES // 2
    parts = []
    for g in range(CHUNK // SUBLANES):
        r = g * SUBLANES
        rows = b[r:r + SUBLANES]
        if m >= SUBLANES:
            ref = b[(r // (2 * m)) * 2 * m + m - 1:(r // (2 * m)) * 2 * m + m]
            parts.append(rows - ref if (r // m) % 2 == 1 else ref - rows)
        elif m == 4:
            parts.append(rows - bcast(r + 3))
        else:
            parts.append(rows - jnp.where(low_half, bcast(r + 1), bcast(r + 5)))
    d = jnp.concatenate(parts, axis=0)
    return d if m >= SUBLANES else jnp.where((t & m) != 0, d, -d)


def _gla_levels(q, k, la, b, hm_a):
    t = lax.broadcasted_iota(jnp.int32, (CHUNK, D_AQK), 0)
    q_st = _stack_masked(q, hm_a)
    k16 = k.astype(BF16)
    prods = [_dot_nt(q_st.astype(BF16), k16)]
    for m in GLA_LEVELS:
        e = jnp.exp(la if m == 1 else _level_exponent(b, m, t))
        qt = (q_st * _tile_rows(e, A_HEADS)).astype(BF16)
        kt = k16 if m == 1 else (k * e).astype(BF16)
        prods.append(_dot_nt(qt, kt))
    return q_st, prods


def _gla_finish(q_st, prods, k, v, b, s, lvl):
    attn = jnp.where(lvl == 0, prods[0], 0.0)
    for li in range(1, len(prods)):
        attn = jnp.where(lvl == li, prods[li], attn)
    qhat = (q_st * _tile_rows(jnp.exp(b), A_HEADS)).astype(BF16)
    o_inter = _dot(qhat, s.astype(BF16))
    v16 = v.astype(BF16)
    o_intra = jnp.concatenate(
        [_dot(attn[h * CHUNK:(h + 1) * CHUNK].astype(BF16), v16[:, h * A_DV:(h + 1) * A_DV])
         for h in range(A_HEADS)], axis=0)
    b_last = b[CHUNK - 1:CHUNK]
    khat = (k * jnp.exp(b_last - b)).astype(BF16)
    upd = _dot_tn(khat, v16)
    decay = jnp.broadcast_to(jnp.exp(b_last), (A_DV, D_AQK)).T
    s_new = s * decay + jnp.concatenate(
        [upd[h * A_DK:(h + 1) * A_DK, h * A_DV:(h + 1) * A_DV] for h in range(A_HEADS)], axis=0)
    return o_intra + o_inter, s_new


def _layer_kernel(*refs, prompt, final, nch, tq, nt):
    (x_ref, mods_ref, gmix_ref, gmlp_ref, gfin_ref, anorm_ref, win_ref, wa2_ref, ba2_ref, wout_ref,
     wup_ref, wdown_ref, biasb_ref, biasc_ref, sink_ref, tri_ref, lvl_ref) = refs[:17]
    pos = 17
    if not prompt:
        st_ref, cbk_ref, cbv_ref, cck_ref, ccv_ref = refs[pos:pos + 5]
        pos += 5
    y_ref, so_ref, kbo_ref, vbo_ref, kco_ref, vco_ref = refs[pos:pos + 6]
    pos += 6
    (h_scr, h2_scr, proj_scr, la_scr, b_scr, mix_scr, x1_scr, acc_scr, st_scr,
     kbw, vbw, kcw, vcw) = refs[pos:]

    step = pl.program_id(0)
    tile = jnp.minimum(step, nt - 1)
    prev_tile = jnp.maximum(step - 1, 0)

    def mod(t_idx, c, j):
        return mods_ref[0 if prompt else t_idx * nch + c, j:j + 1, :]

    @pl.when(step == 0)
    def _():
        h2_scr[...] = jnp.zeros_like(h2_scr)
        x1_scr[...] = jnp.zeros_like(x1_scr)
        acc_scr[...] = jnp.zeros_like(acc_scr)
        if prompt:
            st_scr[...] = jnp.zeros_like(st_scr)
            kbw[...] = jnp.zeros_like(kbw)
            vbw[...] = jnp.zeros_like(vbw)
            kcw[...] = jnp.zeros_like(kcw)
            vcw[...] = jnp.zeros_like(vcw)

    gmix = gmix_ref[...]
    for c in range(nch):
        r = slice(c * CHUNK, (c + 1) * CHUNK)
        h = _rms(x_ref[r, :], gmix) * (1.0 + mod(tile, c, 1)) + mod(tile, c, 0)
        h_scr[r, :] = h.astype(BF16)
    proj_scr[...] = _dot(h_scr[...], win_ref[...])
    z = _dot(proj_scr[:, OFF_RA:OFF_RA + LANES].astype(BF16), wa2_ref[...]) + ba2_ref[...]
    la = (jnp.minimum(z, 0.0) - jnp.log1p(jnp.exp(-jnp.abs(z)))) * (1.0 / A_GATE_TAU)
    la_scr[...] = la
    tri = tri_ref[...]
    hi, mid, lo = _split3(la)
    b_scr[...] = _dot(tri, hi) + _dot(tri, mid) + _dot(tri, lo)

    if prompt:
        kbw[B_WINDOW:B_WINDOW + tq, :] = proj_scr[:, OFF_KB:OFF_KB + D_BKV].astype(BF16)
        vbw[B_WINDOW:B_WINDOW + tq, :] = proj_scr[:, OFF_VB:OFF_VB + D_BKV].astype(BF16)
        kcw[C_BAND:C_BAND + tq, :] = proj_scr[:, OFF_KC:OFF_KC + D_C].astype(BF16)
        vcw[C_BAND:C_BAND + tq, :] = proj_scr[:, OFF_VC:OFF_VC + D_C].astype(BF16)

    hm_a = _head_masks(D_AQK, A_DK)
    hm_c = _head_masks(D_C, HEAD_DIM)
    hm_b = _head_masks(D_BKV, HEAD_DIM)
    anorm = anorm_ref[...]

    def chunk_body(c, carry):
        r0 = pl.multiple_of(c * CHUNK, CHUNK)
        rows = pl.ds(r0, CHUNK)

        if prompt:
            seg_b = [(kbw[pl.ds(r0, LK_B), :], vbw[pl.ds(r0, LK_B), :], False)]
            seg_c = [(kcw[pl.ds(r0, LK_C), :], vcw[pl.ds(r0, LK_C), :], False)]
            gidx = tile * nch + c
            lane_b = lax.broadcasted_iota(jnp.int32, (1, LK_B), 1)
            lane_c = lax.broadcasted_iota(jnp.int32, (1, LK_C), 1)
            mask_b = jnp.where(lane_b >= B_WINDOW - gidx * CHUNK, 0.0, NEG_INF)
            mask_c = jnp.where(lane_c >= C_BAND - gidx * CHUNK, 0.0, NEG_INF)
        else:
            kb_new = proj_scr[rows, OFF_KB:OFF_KB + D_BKV]
            vb_new = proj_scr[rows, OFF_VB:OFF_VB + D_BKV]
            kc_new = proj_scr[rows, OFF_KC:OFF_KC + D_C]
            vc_new = proj_scr[rows, OFF_VC:OFF_VC + D_C]
            seg_b = [(cbk_ref[c].astype(BF16), cbv_ref[c].astype(BF16), True),
                     (kb_new.astype(BF16), vb_new.astype(BF16), False)]
            seg_c = [(cck_ref[c].astype(BF16), ccv_ref[c].astype(BF16), True),
                     (kc_new.astype(BF16), vc_new.astype(BF16), False)]
            mask_b = mask_c = None

        q = proj_scr[rows, OFF_QA:OFF_QA + D_AQK] * (A_DK ** -0.5)
        k = proj_scr[rows, OFF_KA:OFF_KA + D_AQK]
        b = b_scr[rows, :]
        q_st, prods = _gla_levels(q, k, la_scr[rows, :], b, hm_a)
        qc = proj_scr[rows, OFF_QC:OFF_QC + D_C] * (HEAD_DIM ** -0.5)
        sc_c = _scores(_stack_masked(qc, hm_c).astype(BF16), seg_c, biasc_ref[...], mask_c)
        qb = proj_scr[rows, OFF_QB:OFF_QB + B_HEADS * HEAD_DIM] * (HEAD_DIM ** -0.5)
        qb_a, qb_b = qb[:, 0:D_BKV], qb[:, D_BKV:2 * D_BKV]
        qb_st = jnp.concatenate([jnp.where(hm_b[0], qb_a, 0.0), jnp.where(hm_b[0], qb_b, 0.0),
                                 jnp.where(hm_b[1], qb_a, 0.0), jnp.where(hm_b[1], qb_b, 0.0)], axis=0)
        sc_b = _scores(qb_st.astype(BF16), seg_b, biasb_ref[...], mask_b)

        up = _dot(h2_scr[...], wup_ref[c])
        act = jnp.square(jnp.maximum(up, 0.0)).astype(BF16)

        v = proj_scr[rows, OFF_VA:OFF_VA + D_AV]
        s_old = st_scr[...] if prompt else st_ref[c]
        o_st, s_new = _gla_finish(q_st, prods, k, v, b, s_old, lvl_ref[...])
        if prompt:
            st_scr[...] = s_new
        else:
            so_ref[c] = s_new
        g = proj_scr[rows, OFF_GA:OFF_GA + D_AV]
        for h in range(A_HEADS):
            on = _rms(o_st[h * CHUNK:(h + 1) * CHUNK], anorm)
            gh = g[:, h * A_DV:(h + 1) * A_DV]
            mix_scr[rows, h * A_DV:(h + 1) * A_DV] = (on * (gh * jax.nn.sigmoid(gh))).astype(BF16)

        oc_st = _softmax_pv(sc_c, seg_c, None)
        oc = jnp.where(hm_c[0], oc_st[0:CHUNK], 0.0)
        for h in range(1, C_HEADS):
            oc = jnp.where(hm_c[h], oc_st[h * CHUNK:(h + 1) * CHUNK], oc)
        mix_scr[rows, D_AV + 2 * D_BKV:D_MIX] = oc.astype(BF16)

        ob_st = _softmax_pv(sc_b, seg_b, sink_ref[...])
        ob_a = jnp.where(hm_b[0], ob_st[0:CHUNK], ob_st[2 * CHUNK:3 * CHUNK])
        ob_b = jnp.where(hm_b[0], ob_st[CHUNK:2 * CHUNK], ob_st[3 * CHUNK:4 * CHUNK])
        mix_scr[rows, D_AV:D_AV + D_BKV] = ob_a.astype(BF16)
        mix_scr[rows, D_AV + D_BKV:D_AV + 2 * D_BKV] = ob_b.astype(BF16)

        acc_scr[...] += _dot(act, wdown_ref[c])

        if not prompt:
            for dst, src, new, width in ((kbo_ref, cbk_ref, kb_new, B_WINDOW), (vbo_ref, cbv_ref, vb_new, B_WINDOW),
                                         (kco_ref, cck_ref, kc_new, C_BAND), (vco_ref, ccv_ref, vc_new, C_BAND)):
                dst[c, :, 0:width - CHUNK] = src[c, :, CHUNK:width]
                dst[c, :, width - CHUNK:width] = new.T
        return carry

    lax.fori_loop(0, nch, chunk_body, 0)

    if prompt:
        kbw[0:B_WINDOW, :] = kbw[tq:tq + B_WINDOW, :]
        vbw[0:B_WINDOW, :] = vbw[tq:tq + B_WINDOW, :]
        kcw[0:C_BAND, :] = kcw[tq:tq + C_BAND, :]
        vcw[0:C_BAND, :] = vcw[tq:tq + C_BAND, :]

        @pl.when(step < nt)
        def _():
            so_ref[...] = st_scr[...]
            kbo_ref[...] = proj_scr[tq - B_WINDOW:tq, OFF_KB:OFF_KB + D_BKV]
            vbo_ref[...] = proj_scr[tq - B_WINDOW:tq, OFF_VB:OFF_VB + D_BKV]
            rb = kco_ref.shape[0]
            kco_ref[...] = proj_scr[tq - rb:tq, OFF_KC:OFF_KC + D_C]
            vco_ref[...] = proj_scr[tq - rb:tq, OFF_VC:OFF_VC + D_C]

    gfin = gfin_ref[...]
    for c in range(nch):
        r = slice(c * CHUNK, (c + 1) * CHUNK)
        x2 = x1_scr[r, :] + mod(prev_tile, c, 5) * acc_scr[r, :]
        y_ref[r, :] = _rms(x2, gfin) if final else x2
    acc_scr[...] = jnp.zeros_like(acc_scr)

    mixed = _dot(mix_scr[...], wout_ref[...])
    gmlp = gmlp_ref[...]
    for c in range(nch):
        r = slice(c * CHUNK, (c + 1) * CHUNK)
        x1 = x_ref[r, :] + mod(tile, c, 2) * mixed[r, :]
        x1_scr[r, :] = x1
        h2 = _rms(x1, gmlp) * (1.0 + mod(tile, c, 4)) + mod(tile, c, 3)
        h2_scr[r, :] = h2.astype(BF16)


def _layer_spec(arr, layer):
    nd = arr.ndim - 1
    return pl.BlockSpec((None,) + arr.shape[1:], lambda i, _l=layer, _nd=nd: (_l,) + (0,) * _nd,
                        pipeline_mode=pl.Buffered(1))


def _const_spec(arr):
    nd = arr.ndim
    return pl.BlockSpec(arr.shape, lambda i, _nd=nd: (0,) * _nd, pipeline_mode=pl.Buffered(1))


def _run_layer(x, mods, layer, stacked, shared, caches, *, prompt, final, tq, nb):
    rows = x.shape[0]
    nch = tq // CHUNK
    nt = rows // tq
    assert rows % tq == 0 and nch == N_FF
    (gmix, gmlp, anorm, win, wa2, ba2, wout, wup, wdown, biasc, sink) = stacked
    (gfin, biasb, tri, lvl) = shared
    cur = lambda i: jnp.minimum(i, nt - 1)
    prv = lambda i: jnp.maximum(i - 1, 0)

    in_specs = [pl.BlockSpec((tq, D_MODEL), lambda i: (cur(i), 0)), _const_spec(mods)]
    ls = functools.partial(_layer_spec, layer=layer)
    in_specs += [ls(gmix), ls(gmlp), _const_spec(gfin), ls(anorm), ls(win), ls(wa2), ls(ba2), ls(wout),
                 ls(wup), ls(wdown), _const_spec(biasb), ls(biasc), ls(sink), _const_spec(tri), _const_spec(lvl)]
    args = [x, mods, gmix, gmlp, gfin, anorm, win, wa2, ba2, wout, wup, wdown, biasb, biasc, sink, tri, lvl]

    y_spec = pl.BlockSpec((tq, D_MODEL), lambda i: (prv(i), 0))
    if prompt:
        assert tq >= B_WINDOW and (tq % C_BAND == 0 or C_BAND % tq == 0)
        rb = min(tq, C_BAND)
        first = nt - C_BAND // rb
        out_shape = [jax.ShapeDtypeStruct((rows, D_MODEL), F32),
                     jax.ShapeDtypeStruct((D_AQK, A_DV), F32),
                     jax.ShapeDtypeStruct((B_WINDOW, D_BKV), F32),
                     jax.ShapeDtypeStruct((B_WINDOW, D_BKV), F32),
                     jax.ShapeDtypeStruct((C_BAND, D_C), F32),
                     jax.ShapeDtypeStruct((C_BAND, D_C), F32)]
        band = pl.BlockSpec((rb, D_C), lambda i: (jnp.maximum(cur(i) - first, 0), 0))
        out_specs = [y_spec,
                     pl.BlockSpec((D_AQK, A_DV), lambda i: (0, 0)),
                     pl.BlockSpec((B_WINDOW, D_BKV), lambda i: (0, 0)),
                     pl.BlockSpec((B_WINDOW, D_BKV), lambda i: (0, 0)),
                     band, band]
        kbw_rows, kcw_rows = B_WINDOW + tq, C_BAND + tq
    else:
        assert nb == nch
        nseq = rows // CHUNK
        st, cbk, cbv, cck, ccv = caches
        seq_in = lambda r, w: pl.BlockSpec((None, nb, r, w), lambda i, _l=layer: (_l, cur(i), 0, 0),
                                           pipeline_mode=pl.Buffered(1))
        seq_out = lambda r, w: pl.BlockSpec((nb, r, w), lambda i: (cur(i), 0, 0))
        in_specs += [seq_in(D_AQK, A_DV), seq_in(D_BKV, B_WINDOW), seq_in(D_BKV, B_WINDOW),
                     seq_in(D_C, C_BAND), seq_in(D_C, C_BAND)]
        args += [st, cbk, cbv, cck, ccv]
        out_shape = [jax.ShapeDtypeStruct((rows, D_MODEL), F32),
                     jax.ShapeDtypeStruct((nseq, D_AQK, A_DV), F32),
                     jax.ShapeDtypeStruct((nseq, D_BKV, B_WINDOW), F32),
                     jax.ShapeDtypeStruct((nseq, D_BKV, B_WINDOW), F32),
                     jax.ShapeDtypeStruct((nseq, D_C, C_BAND), F32),
                     jax.ShapeDtypeStruct((nseq, D_C, C_BAND), F32)]
        out_specs = [y_spec, seq_out(D_AQK, A_DV), seq_out(D_BKV, B_WINDOW), seq_out(D_BKV, B_WINDOW),
                     seq_out(D_C, C_BAND), seq_out(D_C, C_BAND)]
        kbw_rows, kcw_rows = SUBLANES * 2, SUBLANES * 2

    scratch = [pltpu.VMEM((tq, D_MODEL), BF16),
               pltpu.VMEM((tq, D_MODEL), BF16),
               pltpu.VMEM((tq, P_IN), F32),
               pltpu.VMEM((tq, D_AQK), F32),
               pltpu.VMEM((tq, D_AQK), F32),
               pltpu.VMEM((tq, D_MIX), BF16),
               pltpu.VMEM((tq, D_MODEL), F32),
               pltpu.VMEM((tq, D_MODEL), F32),
               pltpu.VMEM((D_AQK, A_DV), F32),
               pltpu.VMEM((kbw_rows, D_BKV), BF16), pltpu.VMEM((kbw_rows, D_BKV), BF16),
               pltpu.VMEM((kcw_rows, D_C), BF16), pltpu.VMEM((kcw_rows, D_C), BF16)]

    kern = functools.partial(_layer_kernel, prompt=prompt, final=final, nch=nch, tq=tq, nt=nt)
    return pl.pallas_call(
        kern,
        grid=(nt + 1,),
        in_specs=in_specs,
        out_specs=out_specs,
        out_shape=out_shape,
        scratch_shapes=scratch,
        compiler_params=pltpu.CompilerParams(dimension_semantics=("arbitrary",),
                                             vmem_limit_bytes=VMEM_LIMIT_BYTES),
        name=("layer_prompt" if prompt else "layer_sample") + ("_final" if final else ""),
    )(*args)


ADA_BLOCK = 1536


def _ada_kernel(c_ref, w_ref, b_ref, o_ref):
    c = c_ref[...]
    sc = c * jax.nn.sigmoid(c)
    o_ref[0] = jnp.dot(sc, w_ref[0], preferred_element_type=F32,
                       precision=lax.Precision.HIGHEST) + b_ref[0]


def _ada(c_all, w_ada, b_ada):
    rows = c_all.shape[0]
    n = 6 * D_MODEL
    return pl.pallas_call(
        _ada_kernel,
        grid=(DEPTH, n // ADA_BLOCK),
        in_specs=[pl.BlockSpec((rows, D_MODEL), lambda l, j: (0, 0)),
                  pl.BlockSpec((1, D_MODEL, ADA_BLOCK), lambda l, j: (l, 0, j)),
                  pl.BlockSpec((1, 1, ADA_BLOCK), lambda l, j: (l, 0, j))],
        out_specs=pl.BlockSpec((1, rows, ADA_BLOCK), lambda l, j: (l, 0, j)),
        out_shape=jax.ShapeDtypeStruct((DEPTH, rows, n), F32),
        compiler_params=pltpu.CompilerParams(dimension_semantics=("arbitrary", "arbitrary"),
                                             vmem_limit_bytes=VMEM_LIMIT_BYTES),
        name="adaln",
    )(c_all, w_ada, b_ada.reshape(DEPTH, 1, n))


TB_W = 2 * LANES
TC_W = 5 * LANES
TC_FLAT = TC_W - (CHUNK + C_CLIP + 1)


def _shear(f, width):
    tiled = jnp.broadcast_to(f, (CHUNK, width))
    return pltpu.roll(tiled, width - (CHUNK - 1), 1, stride=1, stride_axis=0)


def _bias_kernel(t5_ref, crel_ref, bucket_ref, ob_ref, oc_ref):
    bucket = bucket_ref[...]
    for h in range(B_HEADS):
        def b_body(i, acc, h=h):
            return jnp.where(bucket == i, t5_ref[h, i], acc)
        f = lax.fori_loop(0, T5_BUCKETS, b_body, jnp.zeros(bucket.shape, F32))
        ob_ref[h] = _shear(f, TB_W)[:, 0:LK_B]
    lane = lax.broadcasted_iota(jnp.int32, (1, TC_W), 1)
    for l in range(DEPTH):
        for h in range(C_HEADS):
            row = crel_ref[l, h:h + 1, :]
            f = jnp.where(lane < TC_FLAT, row[:, 0:1], pltpu.roll(row, TC_FLAT, 1))
            oc_ref[l, h] = _shear(f, TC_W)[:, 0:LK_C]


def _bias_tables(t5_bias, c_rel_bias, bucket):
    smem = pl.BlockSpec(memory_space=pltpu.SMEM)
    vmem = pl.BlockSpec(memory_space=pltpu.VMEM)
    crel = jnp.pad(jnp.swapaxes(c_rel_bias, 1, 2),
                   ((0, 0), (0, SUBLANES - C_HEADS), (0, TC_W - (2 * C_CLIP + 1))))
    return pl.pallas_call(
        _bias_kernel,
        in_specs=[smem, vmem, vmem],
        out_specs=[vmem, vmem],
        out_shape=[jax.ShapeDtypeStruct((B_HEADS, CHUNK, LK_B), F32),
                   jax.ShapeDtypeStruct((DEPTH, C_HEADS, CHUNK, LK_C), F32)],
        name="bias_tables",
    )(t5_bias.T, crel, bucket)


def _t5_bucket(rel):
    half = T5_BUCKETS // 2
    max_exact = half // 2
    n = jnp.abs(rel)
    log_ratio = jnp.log(jnp.maximum(n, 1).astype(jnp.float32) / max_exact) / np.log(T5_MAX_DIST / max_exact)
    large = jnp.minimum(max_exact + (log_ratio * (half - max_exact)).astype(jnp.int32), half - 1)
    return jnp.where(rel > 0, half, 0) + jnp.where(n < max_exact, n, large)


def _gla_constants(tq):
    r = np.arange(tq)
    tri = ((r[:, None] // CHUNK == r[None, :] // CHUNK) & (r[None, :] <= r[:, None])).astype(np.float32)
    t = np.arange(CHUNK)[:, None]
    s = np.arange(CHUNK)[None, :]
    lvl = np.full((CHUNK, CHUNK), -1, np.int32)
    lvl[t == s] = 0
    for li, m in enumerate(GLA_LEVELS):
        sel = ((t // m) % 2 == 1) & ((s // m) == (t // m) - 1)
        lvl[np.broadcast_to(sel, lvl.shape)] = li + 1
    return jnp.asarray(tri, BF16), jnp.asarray(np.tile(lvl, (A_HEADS, 1)))


TQ_PROMPT = 256
NB_SAMPLE = 4


def kernel(x_prompt, x_sample, c_prompt, c_sample, state_gla, cache_b_k, cache_b_v, cache_c_k, cache_c_v,
           w_ada, b_ada, norm_mix_g, norm_mlp_g, w_in, w_a2, b_a2, a_norm_g, b_sink, t5_bias, c_rel_bias,
           w_out, w_up, w_down, final_norm_g):
    bsz, seq, _ = x_prompt.shape
    dec_b, dec_s, _ = x_sample.shape
    assert bsz == 1 and dec_s == CHUNK and TQ_PROMPT == NB_SAMPLE * CHUNK

    n_c = bsz + dec_b
    c_rows = -(-n_c // SUBLANES) * SUBLANES
    c_all = jnp.concatenate([c_prompt, c_sample, jnp.zeros((c_rows - n_c, D_MODEL), F32)], axis=0)
    mods = _ada(c_all, w_ada, b_ada).reshape(DEPTH, c_rows, 6, D_MODEL)
    mods = jnp.pad(mods, ((0, 0), (0, 0), (0, SUBLANES - 6), (0, 0)))

    rel_b = jnp.arange(TB_W) - (CHUNK - 1) - B_WINDOW
    bias_b, bias_c = _bias_tables(t5_bias, c_rel_bias, _t5_bucket(rel_b).astype(jnp.int32)[None, :])
    bias_b = bias_b.reshape(STACK, LK_B)
    bias_c = bias_c.reshape(DEPTH, STACK, LK_C)

    tri, lvl = _gla_constants(TQ_PROMPT)
    qb0 = 1552
    head = lambda h: slice(qb0 + h * HEAD_DIM, qb0 + (h + 1) * HEAD_DIM)
    ob0 = D_AV
    orow = lambda h: slice(ob0 + h * HEAD_DIM, ob0 + (h + 1) * HEAD_DIM)

    win = jnp.concatenate([w_in[:, :, 0:1536], w_in[:, :, head(0)], w_in[:, :, head(2)], w_in[:, :, head(1)],
                           w_in[:, :, head(3)], w_in[:, :, 1808:2832], w_in[:, :, 1536:1552],
                           jnp.zeros((DEPTH, D_MODEL, LANES - A_GATE_RANK), F32)], axis=2).astype(BF16)
    wa2 = jnp.concatenate([w_a2, jnp.zeros((DEPTH, LANES - A_GATE_RANK, D_AQK), F32)], axis=1).astype(BF16)
    wout = jnp.concatenate([w_out[:, 0:D_AV], w_out[:, orow(0)], w_out[:, orow(2)], w_out[:, orow(1)],
                            w_out[:, orow(3)], w_out[:, D_AV + B_HEADS * HEAD_DIM:]], axis=1).astype(BF16)
    wup = jnp.transpose(w_up.astype(BF16).reshape(DEPTH, D_MODEL, N_FF, FF_BLOCK), (0, 2, 1, 3))
    wdown = w_down.astype(BF16).reshape(DEPTH, N_FF, FF_BLOCK, D_MODEL)
    sink = jnp.repeat(b_sink, CHUNK, axis=1)[:, :, None]
    stacked = (norm_mix_g[:, None, :], norm_mlp_g[:, None, :], a_norm_g[:, None, :], win, wa2,
               b_a2[:, None, :], wout, wup, wdown, bias_c, sink)
    shared = (final_norm_g[None, :], bias_b, tri, lvl)

    x_p = x_prompt.reshape(seq, D_MODEL)
    x_s = x_sample.reshape(dec_b * dec_s, D_MODEL)
    outs_p, outs_s = [], []
    fmajor = lambda cache: jnp.transpose(cache, (0, 1, 3, 4, 2)).reshape(
        DEPTH, dec_b, cache.shape[3] * cache.shape[4], cache.shape[2])
    caches = (state_gla.reshape(DEPTH, dec_b, D_AQK, A_DV), fmajor(cache_b_k), fmajor(cache_b_v),
              fmajor(cache_c_k), fmajor(cache_c_v))
    for l in range(DEPTH):
        final = l == DEPTH - 1
        res_p = _run_layer(x_p, mods[l, 0:bsz], l, stacked, shared, None, prompt=True, final=final,
                           tq=TQ_PROMPT, nb=None)
        x_p = res_p[0]
        outs_p.append(res_p[1:])

        res_s = _run_layer(x_s, mods[l, bsz:bsz + dec_b], l, stacked, shared, caches, prompt=False,
                           final=final, tq=NB_SAMPLE * CHUNK, nb=NB_SAMPLE)
        x_s = res_s[0]
        outs_s.append(res_s[1:])

    sg_p = jnp.stack([o[0].reshape(bsz, A_HEADS, A_DK, A_DV) for o in outs_p])
    kb_p = jnp.stack([o[1].reshape(bsz, B_WINDOW, B_KV_HEADS, HEAD_DIM) for o in outs_p])
    vb_p = jnp.stack([o[2].reshape(bsz, B_WINDOW, B_KV_HEADS, HEAD_DIM) for o in outs_p])
    kc_p = jnp.stack([o[3].reshape(bsz, C_BAND, C_HEADS, HEAD_DIM) for o in outs_p])
    vc_p = jnp.stack([o[4].reshape(bsz, C_BAND, C_HEADS, HEAD_DIM) for o in outs_p])
    sg_s = jnp.stack([o[0].reshape(dec_b, A_HEADS, A_DK, A_DV) for o in outs_s])
    tmajor = lambda arrs, heads: jnp.transpose(
        jnp.stack(arrs).reshape(DEPTH, dec_b, heads, HEAD_DIM, -1), (0, 1, 4, 2, 3))
    kb_s = tmajor([o[1] for o in outs_s], B_KV_HEADS)
    vb_s = tmajor([o[2] for o in outs_s], B_KV_HEADS)
    kc_s = tmajor([o[3] for o in outs_s], C_HEADS)
    vc_s = tmajor([o[4] for o in outs_s], C_HEADS)
    return (x_p.reshape(bsz, seq, D_MODEL), x_s.reshape(dec_b, dec_s, D_MODEL),
            sg_p, kb_p, vb_p, kc_p, vc_p, sg_s, kb_s, vb_s, kc_s, vc_s)
```

```python
import functools

import numpy as np
import jax
import jax.numpy as jnp
from jax import lax
from jax.experimental import pallas as pl
from jax.experimental.pallas import tpu as pltpu

D_MODEL = 1024
DEPTH = 2
CHUNK = 64
HEAD_DIM = 64
A_HEADS = 4
A_DK = 64
A_DV = 128
A_GATE_RANK = 16
A_GATE_TAU = 16.0
B_HEADS = 4
B_KV_HEADS = 2
B_WINDOW = 128
C_HEADS = 4
C_BAND = 512
C_CLIP = 256
T5_BUCKETS = 32
T5_MAX_DIST = 128
D_FF = 4 * D_MODEL
NORM_EPS = 1e-6
NEG_INF = -1e30

LANES = 128
SUBLANES = 8
VMEM_LIMIT_BYTES = 60000 * 1024

OFF_QA = 0
OFF_KA = OFF_QA + A_HEADS * A_DK
OFF_VA = OFF_KA + A_HEADS * A_DK
OFF_GA = OFF_VA + A_HEADS * A_DV
OFF_QB = OFF_GA + A_HEADS * A_DV
OFF_KB = OFF_QB + B_HEADS * HEAD_DIM
OFF_VB = OFF_KB + B_KV_HEADS * HEAD_DIM
OFF_QC = OFF_VB + B_KV_HEADS * HEAD_DIM
OFF_KC = OFF_QC + C_HEADS * HEAD_DIM
OFF_VC = OFF_KC + C_HEADS * HEAD_DIM
OFF_RA = OFF_VC + C_HEADS * HEAD_DIM
P_IN = OFF_RA + LANES
D_AQK = A_HEADS * A_DK
D_AV = A_HEADS * A_DV
D_BKV = B_KV_HEADS * HEAD_DIM
D_C = C_HEADS * HEAD_DIM
D_MIX = D_AV + B_HEADS * HEAD_DIM + D_C
LK_B = B_WINDOW + CHUNK
LK_C = C_BAND + CHUNK
STACK = 4 * CHUNK
GLA_LEVELS = (1, 2, 4, 8, 16, 32)
FF_BLOCK = 1024
N_FF = D_FF // FF_BLOCK

BF16 = jnp.bfloat16
F32 = jnp.float32


def _dot(a, b):
    return jnp.dot(a, b, preferred_element_type=F32)


def _dot_nt(a, b):
    return lax.dot_general(a, b, (((1,), (1,)), ((), ())), preferred_element_type=F32)


def _dot_tn(a, b):
    return lax.dot_general(a, b, (((0,), (0,)), ((), ())), preferred_element_type=F32)


def _rms(x, g):
    return x * lax.rsqrt(jnp.mean(x * x, axis=-1, keepdims=True) + NORM_EPS) * g


def _split3(x):
    hi = x.astype(BF16)
    r1 = x - hi.astype(F32)
    mid = r1.astype(BF16)
    lo = (r1 - mid.astype(F32)).astype(BF16)
    return hi, mid, lo


def _head_masks(width, per_head):
    lane = lax.broadcasted_iota(jnp.int32, (1, width), 1)
    return [(lane >= h * per_head) & (lane < (h + 1) * per_head) for h in range(width // per_head)]


def _stack_masked(x, masks):
    return jnp.concatenate([jnp.where(m, x, 0.0) for m in masks], axis=0)


def _tile_rows(x, n):
    return jnp.concatenate([x] * n, axis=0)


def _scores(q_st, segments, bias_st, maskrow):
    scores, off = [], 0
    for k, v, transposed in segments:
        n = k.shape[1] if transposed else k.shape[0]
        s = (_dot(q_st, k) if transposed else _dot_nt(q_st, k)) + bias_st[:, off:off + n]
        if maskrow is not None:
            s = s + maskrow[:, off:off + n]
        scores.append(s)
        off += n
    return scores


def _softmax_pv(scores, segments, sink):
    m = functools.reduce(jnp.maximum, [jnp.max(s, axis=1, keepdims=True) for s in scores])
    if sink is not None:
        m = jnp.maximum(m, sink)
    es = [jnp.exp(s - m) for s in scores]
    l = functools.reduce(jnp.add, [jnp.sum(e, axis=1, keepdims=True) for e in es])
    if sink is not None:
        l = l + jnp.exp(sink - m)
    o = None
    for e, (k, v, transposed) in zip(es, segments):
        part = _dot_nt(e.astype(BF16), v) if transposed else _dot(e.astype(BF16), v)
        o = part if o is None else o + part
    return o / l


def _level_exponent(b, m, t):
    bcast = lambda i: jnp.broadcast_to(b[i:i + 1], (SUBLANES, b.shape[1]))
    low_half = lax.broadcasted_iota(jnp.int32, (SUBLANES, b.shape[1]), 0) < SUBLANES // 2
    parts = []
    for g in range(CHUNK // SUBLANES):
        r = g * SUBLANES
        rows = b[r:r + SUBLANES]
        if m >= SUBLANES:
            ref = b[(r // (2 * m)) * 2 * m + m - 1:(r // (2 * m)) * 2 * m + m]
            parts.append(rows - ref if (r // m) % 2 == 1 else ref - rows)
        elif m == 4:
            parts.append(rows - bcast(r + 3))
        else:
            parts.append(rows - jnp.where(low_half, bcast(r + 1), bcast(r + 5)))
    d = jnp.concatenate(parts, axis=0)
    return d if m >= SUBLANES else jnp.where((t & m) != 0, d, -d)


def _gla_levels(q, k, la, b, hm_a):
    t = lax.broadcasted_iota(jnp.int32, (CHUNK, D_AQK), 0)
    q_st = _stack_masked(q, hm_a)
    k16 = k.astype(BF16)
    prods = [_dot_nt(q_st.astype(BF16), k16)]
    for m in GLA_LEVELS:
        e = jnp.exp(la if m == 1 else _level_exponent(b, m, t))
        qt = (q_st * _tile_rows(e, A_HEADS)).astype(BF16)
        kt = k16 if m == 1 else (k * e).astype(BF16)
        prods.append(_dot_nt(qt, kt))
    return q_st, prods


def _gla_finish(q_st, prods, k, v, b, s, lvl):
    attn = jnp.where(lvl == 0, prods[0], 0.0)
    for li in range(1, len(prods)):
        attn = jnp.where(lvl == li, prods[li], attn)
    qhat = (q_st * _tile_rows(jnp.exp(b), A_HEADS)).astype(BF16)
    o_inter = _dot(qhat, s.astype(BF16))
    v16 = v.astype(BF16)
    o_intra = jnp.concatenate(
        [_dot(attn[h * CHUNK:(h + 1) * CHUNK].astype(BF16), v16[:, h * A_DV:(h + 1) * A_DV])
         for h in range(A_HEADS)], axis=0)
    b_last = b[CHUNK - 1:CHUNK]
    khat = (k * jnp.exp(b_last - b)).astype(BF16)
    upd = _dot_tn(khat, v16)
    decay = jnp.broadcast_to(jnp.exp(b_last), (A_DV, D_AQK)).T
    s_new = s * decay + jnp.concatenate(
        [upd[h * A_DK:(h + 1) * A_DK, h * A_DV:(h + 1) * A_DV] for h in range(A_HEADS)], axis=0)
    return o_intra + o_inter, s_new


def _layer_kernel(*refs, prompt, final, nch, tq, nt):
    (x_ref, mods_ref, modp_ref, gmix_ref, gmlp_ref, gfin_ref, anorm_ref, win_ref, wa2_ref, ba2_ref, wout_ref,
     wup_ref, wdown_ref, biasb_ref, biasc_ref, sink_ref, tri_ref, lvl_ref) = refs[:18]
    pos = 18
    if not prompt:
        st_ref, cbk_ref, cbv_ref, cck_ref, ccv_ref = refs[pos:pos + 5]
        pos += 5
    y_ref, so_ref, kbo_ref, vbo_ref, kco_ref, vco_ref = refs[pos:pos + 6]
    pos += 6
    (h_scr, h2_scr, proj_scr, la_scr, b_scr, mix_scr, x1_scr, acc_scr, st_scr,
     kbw, vbw, kcw, vcw) = refs[pos:]

    step = pl.program_id(0)
    tile = jnp.minimum(step, nt - 1)

    def mod(ref, c, j):
        return ref[0 if prompt else c, j:j + 1, :]

    @pl.when(step == 0)
    def _():
        h2_scr[...] = jnp.zeros_like(h2_scr)
        x1_scr[...] = jnp.zeros_like(x1_scr)
        acc_scr[...] = jnp.zeros_like(acc_scr)
        if prompt:
            st_scr[...] = jnp.zeros_like(st_scr)
            kbw[...] = jnp.zeros_like(kbw)
            vbw[...] = jnp.zeros_like(vbw)
            kcw[...] = jnp.zeros_like(kcw)
            vcw[...] = jnp.zeros_like(vcw)

    gmix = gmix_ref[...]
    for c in range(nch):
        r = slice(c * CHUNK, (c + 1) * CHUNK)
        h = _rms(x_ref[r, :], gmix) * (1.0 + mod(mods_ref, c, 1)) + mod(mods_ref, c, 0)
        h_scr[r, :] = h.astype(BF16)
    hmix = h_scr[...]
    cut1, cut2 = OFF_GA, OFF_QC
    ra = _dot(hmix, win_ref[:, OFF_RA:OFF_RA + LANES])
    proj_scr[:, 0:cut1] = _dot(hmix, win_ref[:, 0:cut1])
    z = _dot(ra.astype(BF16), wa2_ref[...]) + ba2_ref[...]
    la = (jnp.minimum(z, 0.0) - jnp.log1p(jnp.exp(-jnp.abs(z)))) * (1.0 / A_GATE_TAU)
    la_scr[...] = la
    proj_scr[:, cut1:cut2] = _dot(hmix, win_ref[:, cut1:cut2])
    tri = tri_ref[...]
    hi, mid, lo = _split3(la)
    b_scr[...] = _dot(tri, hi) + _dot(tri, mid) + _dot(tri, lo)
    proj_scr[:, cut2:OFF_RA] = _dot(hmix, win_ref[:, cut2:OFF_RA])

    if prompt:
        kbw[B_WINDOW:B_WINDOW + tq, :] = proj_scr[:, OFF_KB:OFF_KB + D_BKV].astype(BF16)
        vbw[B_WINDOW:B_WINDOW + tq, :] = proj_scr[:, OFF_VB:OFF_VB + D_BKV].astype(BF16)
        kcw[C_BAND:C_BAND + tq, :] = proj_scr[:, OFF_KC:OFF_KC + D_C].astype(BF16)
        vcw[C_BAND:C_BAND + tq, :] = proj_scr[:, OFF_VC:OFF_VC + D_C].astype(BF16)

    hm_a = _head_masks(D_AQK, A_DK)
    hm_c = _head_masks(D_C, HEAD_DIM)
    hm_b = _head_masks(D_BKV, HEAD_DIM)
    anorm = anorm_ref[...]

    def chunk_body(c, carry):
        r0 = pl.multiple_of(c * CHUNK, CHUNK)
        rows = pl.ds(r0, CHUNK)

        if prompt:
            seg_b = [(kbw[pl.ds(r0, LK_B), :], vbw[pl.ds(r0, LK_B), :], False)]
            seg_c = [(kcw[pl.ds(r0, LK_C), :], vcw[pl.ds(r0, LK_C), :], False)]
            gidx = tile * nch + c
            lane_b = lax.broadcasted_iota(jnp.int32, (1, LK_B), 1)
            lane_c = lax.broadcasted_iota(jnp.int32, (1, LK_C), 1)
            mask_b = jnp.where(lane_b >= B_WINDOW - gidx * CHUNK, 0.0, NEG_INF)
            mask_c = jnp.where(lane_c >= C_BAND - gidx * CHUNK, 0.0, NEG_INF)
        else:
            kb_new = proj_scr[rows, OFF_KB:OFF_KB + D_BKV]
            vb_new = proj_scr[rows, OFF_VB:OFF_VB + D_BKV]
            kc_new = proj_scr[rows, OFF_KC:OFF_KC + D_C]
            vc_new = proj_scr[rows, OFF_VC:OFF_VC + D_C]
            seg_b = [(cbk_ref[c].astype(BF16), cbv_ref[c].astype(BF16), True),
                     (kb_new.astype(BF16), vb_new.astype(BF16), False)]
            seg_c = [(cck_ref[c].astype(BF16), ccv_ref[c].astype(BF16), True),
                     (kc_new.astype(BF16), vc_new.astype(BF16), False)]
            mask_b = mask_c = None

        q = proj_scr[rows, OFF_QA:OFF_QA + D_AQK] * (A_DK ** -0.5)
        k = proj_scr[rows, OFF_KA:OFF_KA + D_AQK]
        b = b_scr[rows, :]
        q_st, prods = _gla_levels(q, k, la_scr[rows, :], b, hm_a)
        qc = proj_scr[rows, OFF_QC:OFF_QC + D_C] * (HEAD_DIM ** -0.5)
        sc_c = _scores(_stack_masked(qc, hm_c).astype(BF16), seg_c, biasc_ref[...], mask_c)
        qb = proj_scr[rows, OFF_QB:OFF_QB + B_HEADS * HEAD_DIM] * (HEAD_DIM ** -0.5)
        qb_a, qb_b = qb[:, 0:D_BKV], qb[:, D_BKV:2 * D_BKV]
        qb_st = jnp.concatenate([jnp.where(hm_b[0], qb_a, 0.0), jnp.where(hm_b[0], qb_b, 0.0),
                                 jnp.where(hm_b[1], qb_a, 0.0), jnp.where(hm_b[1], qb_b, 0.0)], axis=0)
        sc_b = _scores(qb_st.astype(BF16), seg_b, biasb_ref[...], mask_b)

        up = _dot(h2_scr[...], wup_ref[c])
        act = jnp.square(jnp.maximum(up, 0.0)).astype(BF16)

        v = proj_scr[rows, OFF_VA:OFF_VA + D_AV]
        s_old = st_scr[...] if prompt else st_ref[c]
        o_st, s_new = _gla_finish(q_st, prods, k, v, b, s_old, lvl_ref[...])
        if prompt:
            st_scr[...] = s_new
        else:
            so_ref[c] = s_new
        g = proj_scr[rows, OFF_GA:OFF_GA + D_AV]
        for h in range(A_HEADS):
            on = _rms(o_st[h * CHUNK:(h + 1) * CHUNK], anorm)
            gh = g[:, h * A_DV:(h + 1) * A_DV]
            mix_scr[rows, h * A_DV:(h + 1) * A_DV] = (on * (gh * jax.nn.sigmoid(gh))).astype(BF16)

        oc_st = _softmax_pv(sc_c, seg_c, None)
        oc = jnp.where(hm_c[0], oc_st[0:CHUNK], 0.0)
        for h in range(1, C_HEADS):
            oc = jnp.where(hm_c[h], oc_st[h * CHUNK:(h + 1) * CHUNK], oc)
        mix_scr[rows, D_AV + 2 * D_BKV:D_MIX] = oc.astype(BF16)

        ob_st = _softmax_pv(sc_b, seg_b, sink_ref[...])
        ob_a = jnp.where(hm_b[0], ob_st[0:CHUNK], ob_st[2 * CHUNK:3 * CHUNK])
        ob_b = jnp.where(hm_b[0], ob_st[CHUNK:2 * CHUNK], ob_st[3 * CHUNK:4 * CHUNK])
        mix_scr[rows, D_AV:D_AV + D_BKV] = ob_a.astype(BF16)
        mix_scr[rows, D_AV + D_BKV:D_AV + 2 * D_BKV] = ob_b.astype(BF16)

        acc_scr[...] += _dot(act, wdown_ref[c])

        if not prompt:
            for dst, src, new, width in ((kbo_ref, cbk_ref, kb_new, B_WINDOW), (vbo_ref, cbv_ref, vb_new, B_WINDOW),
                                         (kco_ref, cck_ref, kc_new, C_BAND), (vco_ref, ccv_ref, vc_new, C_BAND)):
                dst[c, :, 0:width - CHUNK] = src[c, :, CHUNK:width]
                dst[c, :, width - CHUNK:width] = new.T
        return carry

    lax.fori_loop(0, nch, chunk_body, 0)

    if prompt:
        kbw[0:B_WINDOW, :] = kbw[tq:tq + B_WINDOW, :]
        vbw[0:B_WINDOW, :] = vbw[tq:tq + B_WINDOW, :]
        kcw[0:C_BAND, :] = kcw[tq:tq + C_BAND, :]
        vcw[0:C_BAND, :] = vcw[tq:tq + C_BAND, :]

        @pl.when(step < nt)
        def _():
            so_ref[...] = st_scr[...]
            kbo_ref[...] = proj_scr[tq - B_WINDOW:tq, OFF_KB:OFF_KB + D_BKV]
            vbo_ref[...] = proj_scr[tq - B_WINDOW:tq, OFF_VB:OFF_VB + D_BKV]
            rb = kco_ref.shape[0]
            kco_ref[...] = proj_scr[tq - rb:tq, OFF_KC:OFF_KC + D_C]
            vco_ref[...] = proj_scr[tq - rb:tq, OFF_VC:OFF_VC + D_C]

    mixed = _dot(mix_scr[...], wout_ref[...])

    gfin = gfin_ref[...]
    for c in range(nch):
        r = slice(c * CHUNK, (c + 1) * CHUNK)
        x2 = x1_scr[r, :] + mod(modp_ref, c, 5) * acc_scr[r, :]
        y_ref[r, :] = _rms(x2, gfin) if final else x2
    acc_scr[...] = jnp.zeros_like(acc_scr)

    gmlp = gmlp_ref[...]
    for c in range(nch):
        r = slice(c * CHUNK, (c + 1) * CHUNK)
        x1 = x_ref[r, :] + mod(mods_ref, c, 2) * mixed[r, :]
        x1_scr[r, :] = x1
        h2 = _rms(x1, gmlp) * (1.0 + mod(mods_ref, c, 4)) + mod(mods_ref, c, 3)
        h2_scr[r, :] = h2.astype(BF16)


def _layer_spec(arr, layer):
    nd = arr.ndim - 1
    return pl.BlockSpec((None,) + arr.shape[1:], lambda i, _l=layer, _nd=nd: (_l,) + (0,) * _nd,
                        pipeline_mode=pl.Buffered(1))


def _const_spec(arr):
    nd = arr.ndim
    return pl.BlockSpec(arr.shape, lambda i, _nd=nd: (0,) * _nd, pipeline_mode=pl.Buffered(1))


def _run_layer(x, mods, layer, stacked, shared, caches, *, prompt, final, tq, nb):
    rows = x.shape[0]
    nch = tq // CHUNK
    nt = rows // tq
    assert rows % tq == 0 and nch == N_FF
    (gmix, gmlp, anorm, win, wa2, ba2, wout, wup, wdown, biasc, sink) = stacked
    (gfin, biasb, tri, lvl) = shared
    cur = lambda i: jnp.minimum(i, nt - 1)
    prv = lambda i: jnp.maximum(i - 1, 0)

    nmod = 1 if prompt else nb
    in_specs = [pl.BlockSpec((tq, D_MODEL), lambda i: (cur(i), 0),
                             pipeline_mode=pl.Buffered(2 if prompt else 1)),
                pl.BlockSpec((nmod, SUBLANES, D_MODEL), lambda i: (0 if prompt else cur(i), 0, 0)),
                pl.BlockSpec((nmod, SUBLANES, D_MODEL), lambda i: (0 if prompt else prv(i), 0, 0))]
    ls = functools.partial(_layer_spec, layer=layer)
    in_specs += [ls(gmix), ls(gmlp), _const_spec(gfin), ls(anorm), ls(win), ls(wa2), ls(ba2), ls(wout),
                 ls(wup), ls(wdown), _const_spec(biasb), ls(biasc), ls(sink), _const_spec(tri), _const_spec(lvl)]
    args = [x, mods, mods, gmix, gmlp, gfin, anorm, win, wa2, ba2, wout, wup, wdown, biasb, biasc, sink, tri, lvl]

    y_spec = pl.BlockSpec((tq, D_MODEL), lambda i: (prv(i), 0))
    if prompt:
        assert tq >= B_WINDOW and (tq % C_BAND == 0 or C_BAND % tq == 0)
        rb = min(tq, C_BAND)
        first = nt - C_BAND // rb
        out_shape = [jax.ShapeDtypeStruct((rows, D_MODEL), F32),
                     jax.ShapeDtypeStruct((D_AQK, A_DV), F32),
                     jax.ShapeDtypeStruct((B_WINDOW, D_BKV), F32),
                     jax.ShapeDtypeStruct((B_WINDOW, D_BKV), F32),
                     jax.ShapeDtypeStruct((C_BAND, D_C), F32),
                     jax.ShapeDtypeStruct((C_BAND, D_C), F32)]
        band = pl.BlockSpec((rb, D_C), lambda i: (jnp.maximum(cur(i) - first, 0), 0))
        out_specs = [y_spec,
                     pl.BlockSpec((D_AQK, A_DV), lambda i: (0, 0)),
                     pl.BlockSpec((B_WINDOW, D_BKV), lambda i: (0, 0)),
                     pl.BlockSpec((B_WINDOW, D_BKV), lambda i: (0, 0)),
                     band, band]
        kbw_rows, kcw_rows = B_WINDOW + tq, C_BAND + tq
    else:
        assert nb == nch
        nseq = rows // CHUNK
        st, cbk, cbv, cck, ccv = caches
        seq_in = lambda r, w, bufs: pl.BlockSpec((None, nb, r, w), lambda i, _l=layer: (_l, cur(i), 0, 0),
                                                 pipeline_mode=pl.Buffered(bufs))
        seq_out = lambda r, w: pl.BlockSpec((nb, r, w), lambda i: (cur(i), 0, 0))
        in_specs += [seq_in(D_AQK, A_DV, 1), seq_in(D_BKV, B_WINDOW, 1), seq_in(D_BKV, B_WINDOW, 1),
                     seq_in(D_C, C_BAND, 2), seq_in(D_C, C_BAND, 2)]
        args += [st, cbk, cbv, cck, ccv]
        out_shape = [jax.ShapeDtypeStruct((rows, D_MODEL), F32),
                     jax.ShapeDtypeStruct((nseq, D_AQK, A_DV), F32),
                     jax.ShapeDtypeStruct((nseq, D_BKV, B_WINDOW), F32),
                     jax.ShapeDtypeStruct((nseq, D_BKV, B_WINDOW), F32),
                     jax.ShapeDtypeStruct((nseq, D_C, C_BAND), F32),
                     jax.ShapeDtypeStruct((nseq, D_C, C_BAND), F32)]
        out_specs = [y_spec, seq_out(D_AQK, A_DV), seq_out(D_BKV, B_WINDOW), seq_out(D_BKV, B_WINDOW),
                     seq_out(D_C, C_BAND), seq_out(D_C, C_BAND)]
        kbw_rows, kcw_rows = SUBLANES * 2, SUBLANES * 2

    scratch = [pltpu.VMEM((tq, D_MODEL), BF16),
               pltpu.VMEM((tq, D_MODEL), BF16),
               pltpu.VMEM((tq, OFF_RA), F32),
               pltpu.VMEM((tq, D_AQK), F32),
               pltpu.VMEM((tq, D_AQK), F32),
               pltpu.VMEM((tq, D_MIX), BF16),
               pltpu.VMEM((tq, D_MODEL), F32),
               pltpu.VMEM((tq, D_MODEL), F32),
               pltpu.VMEM((D_AQK, A_DV), F32),
               pltpu.VMEM((kbw_rows, D_BKV), BF16), pltpu.VMEM((kbw_rows, D_BKV), BF16),
               pltpu.VMEM((kcw_rows, D_C), BF16), pltpu.VMEM((kcw_rows, D_C), BF16)]

    kern = functools.partial(_layer_kernel, prompt=prompt, final=final, nch=nch, tq=tq, nt=nt)
    return pl.pallas_call(
        kern,
        grid=(nt + 1,),
        in_specs=in_specs,
        out_specs=out_specs,
        out_shape=out_shape,
        scratch_shapes=scratch,
        compiler_params=pltpu.CompilerParams(dimension_semantics=("arbitrary",),
                                             vmem_limit_bytes=VMEM_LIMIT_BYTES),
        name=("layer_prompt" if prompt else "layer_sample") + ("_final" if final else ""),
    )(*args)


ADA_BLOCK = 1536


def _ada_kernel(c_ref, w_ref, b_ref, o_ref):
    c = c_ref[...]
    sc = c * jax.nn.sigmoid(c)
    o_ref[0] = jnp.dot(sc, w_ref[0], preferred_element_type=F32,
                       precision=lax.Precision.HIGHEST) + b_ref[0]


def _ada(c_all, w_ada, b_ada):
    rows = c_all.shape[0]
    n = 6 * D_MODEL
    return pl.pallas_call(
        _ada_kernel,
        grid=(DEPTH, n // ADA_BLOCK),
        in_specs=[pl.BlockSpec((rows, D_MODEL), lambda l, j: (0, 0)),
                  pl.BlockSpec((1, D_MODEL, ADA_BLOCK), lambda l, j: (l, 0, j)),
                  pl.BlockSpec((1, 1, ADA_BLOCK), lambda l, j: (l, 0, j))],
        out_specs=pl.BlockSpec((1, rows, ADA_BLOCK), lambda l, j: (l, 0, j)),
        out_shape=jax.ShapeDtypeStruct((DEPTH, rows, n), F32),
        compiler_params=pltpu.CompilerParams(dimension_semantics=("arbitrary", "arbitrary"),
                                             vmem_limit_bytes=VMEM_LIMIT_BYTES),
        name="adaln",
    )(c_all, w_ada, b_ada.reshape(DEPTH, 1, n))


TB_W = 2 * LANES
TC_W = 5 * LANES
TC_FLAT = TC_W - (CHUNK + C_CLIP + 1)


def _shear(f, width):
    tiled = jnp.broadcast_to(f, (CHUNK, width))
    return pltpu.roll(tiled, width - (CHUNK - 1), 1, stride=1, stride_axis=0)


def _bias_kernel(t5_ref, crel_ref, bucket_ref, ob_ref, oc_ref):
    bucket = bucket_ref[...]
    for h in range(B_HEADS):
        def b_body(i, acc, h=h):
            return jnp.where(bucket == i, t5_ref[h, i], acc)
        f = lax.fori_loop(0, T5_BUCKETS, b_body, jnp.zeros(bucket.shape, F32))
        ob_ref[h] = _shear(f, TB_W)[:, 0:LK_B]
    lane = lax.broadcasted_iota(jnp.int32, (1, TC_W), 1)
    for l in range(DEPTH):
        for h in range(C_HEADS):
            row = crel_ref[l, h:h + 1, :]
            f = jnp.where(lane < TC_FLAT, row[:, 0:1], pltpu.roll(row, TC_FLAT, 1))
            oc_ref[l, h] = _shear(f, TC_W)[:, 0:LK_C]


def _bias_tables(t5_bias, c_rel_bias, bucket):
    smem = pl.BlockSpec(memory_space=pltpu.SMEM)
    vmem = pl.BlockSpec(memory_space=pltpu.VMEM)
    crel = jnp.pad(jnp.swapaxes(c_rel_bias, 1, 2),
                   ((0, 0), (0, SUBLANES - C_HEADS), (0, TC_W - (2 * C_CLIP + 1))))
    return pl.pallas_call(
        _bias_kernel,
        in_specs=[smem, vmem, vmem],
        out_specs=[vmem, vmem],
        out_shape=[jax.ShapeDtypeStruct((B_HEADS, CHUNK, LK_B), F32),
                   jax.ShapeDtypeStruct((DEPTH, C_HEADS, CHUNK, LK_C), F32)],
        name="bias_tables",
    )(t5_bias.T, crel, bucket)


def _t5_bucket(rel):
    half = T5_BUCKETS // 2
    max_exact = half // 2
    n = jnp.abs(rel)
    log_ratio = jnp.log(jnp.maximum(n, 1).astype(jnp.float32) / max_exact) / np.log(T5_MAX_DIST / max_exact)
    large = jnp.minimum(max_exact + (log_ratio * (half - max_exact)).astype(jnp.int32), half - 1)
    return jnp.where(rel > 0, half, 0) + jnp.where(n < max_exact, n, large)


def _gla_constants(tq):
    r = np.arange(tq)
    tri = ((r[:, None] // CHUNK == r[None, :] // CHUNK) & (r[None, :] <= r[:, None])).astype(np.float32)
    t = np.arange(CHUNK)[:, None]
    s = np.arange(CHUNK)[None, :]
    lvl = np.full((CHUNK, CHUNK), -1, np.int32)
    lvl[t == s] = 0
    for li, m in enumerate(GLA_LEVELS):
        sel = ((t // m) % 2 == 1) & ((s // m) == (t // m) - 1)
        lvl[np.broadcast_to(sel, lvl.shape)] = li + 1
    return jnp.asarray(tri, BF16), jnp.asarray(np.tile(lvl, (A_HEADS, 1)))


TQ_PROMPT = 256
NB_SAMPLE = 4


def kernel(x_prompt, x_sample, c_prompt, c_sample, state_gla, cache_b_k, cache_b_v, cache_c_k, cache_c_v,
           w_ada, b_ada, norm_mix_g, norm_mlp_g, w_in, w_a2, b_a2, a_norm_g, b_sink, t5_bias, c_rel_bias,
           w_out, w_up, w_down, final_norm_g):
    bsz, seq, _ = x_prompt.shape
    dec_b, dec_s, _ = x_sample.shape
    assert bsz == 1 and dec_s == CHUNK and TQ_PROMPT == NB_SAMPLE * CHUNK

    n_c = bsz + dec_b
    c_rows = -(-n_c // SUBLANES) * SUBLANES
    c_all = jnp.concatenate([c_prompt, c_sample, jnp.zeros((c_rows - n_c, D_MODEL), F32)], axis=0)
    mods = _ada(c_all, w_ada, b_ada).reshape(DEPTH, c_rows, 6, D_MODEL)
    mods = jnp.pad(mods, ((0, 0), (0, 0), (0, SUBLANES - 6), (0, 0)))

    rel_b = jnp.arange(TB_W) - (CHUNK - 1) - B_WINDOW
    bias_b, bias_c = _bias_tables(t5_bias, c_rel_bias, _t5_bucket(rel_b).astype(jnp.int32)[None, :])
    bias_b = bias_b.reshape(STACK, LK_B)
    bias_c = bias_c.reshape(DEPTH, STACK, LK_C)

    tri, lvl = _gla_constants(TQ_PROMPT)
    qb0 = 1552
    head = lambda h: slice(qb0 + h * HEAD_DIM, qb0 + (h + 1) * HEAD_DIM)
    ob0 = D_AV
    orow = lambda h: slice(ob0 + h * HEAD_DIM, ob0 + (h + 1) * HEAD_DIM)

    win = jnp.concatenate([w_in[:, :, 0:1536], w_in[:, :, head(0)], w_in[:, :, head(2)], w_in[:, :, head(1)],
                           w_in[:, :, head(3)], w_in[:, :, 1808:2832], w_in[:, :, 1536:1552],
                           jnp.zeros((DEPTH, D_MODEL, LANES - A_GATE_RANK), F32)], axis=2).astype(BF16)
    wa2 = jnp.concatenate([w_a2, jnp.zeros((DEPTH, LANES - A_GATE_RANK, D_AQK), F32)], axis=1).astype(BF16)
    wout = jnp.concatenate([w_out[:, 0:D_AV], w_out[:, orow(0)], w_out[:, orow(2)], w_out[:, orow(1)],
                            w_out[:, orow(3)], w_out[:, D_AV + B_HEADS * HEAD_DIM:]], axis=1).astype(BF16)
    wup = jnp.transpose(w_up.astype(BF16).reshape(DEPTH, D_MODEL, N_FF, FF_BLOCK), (0, 2, 1, 3))
    wdown = w_down.astype(BF16).reshape(DEPTH, N_FF, FF_BLOCK, D_MODEL)
    sink = jnp.repeat(b_sink, CHUNK, axis=1)[:, :, None]
    stacked = (norm_mix_g[:, None, :], norm_mlp_g[:, None, :], a_norm_g[:, None, :], win, wa2,
               b_a2[:, None, :], wout, wup, wdown, bias_c, sink)
    shared = (final_norm_g[None, :], bias_b, tri, lvl)

    x_p = x_prompt.reshape(seq, D_MODEL)
    x_s = x_sample.reshape(dec_b * dec_s, D_MODEL)
    outs_p, outs_s = [], []
    fmajor = lambda cache: jnp.transpose(cache, (0, 1, 3, 4, 2)).reshape(
        DEPTH, dec_b, cache.shape[3] * cache.shape[4], cache.shape[2])
    caches = (state_gla.reshape(DEPTH, dec_b, D_AQK, A_DV), fmajor(cache_b_k), fmajor(cache_b_v),
              fmajor(cache_c_k), fmajor(cache_c_v))
    for l in range(DEPTH):
        final = l == DEPTH - 1
        res_p = _run_layer(x_p, mods[l, 0:bsz], l, stacked, shared, None, prompt=True, final=final,
                           tq=TQ_PROMPT, nb=None)
        x_p = res_p[0]
        outs_p.append(res_p[1:])

        res_s = _run_layer(x_s, mods[l, bsz:bsz + dec_b], l, stacked, shared, caches, prompt=False,
                           final=final, tq=NB_SAMPLE * CHUNK, nb=NB_SAMPLE)
        x_s = res_s[0]
        outs_s.append(res_s[1:])

    sg_p = jnp.stack([o[0].reshape(bsz, A_HEADS, A_DK, A_DV) for o in outs_p])
    kb_p = jnp.stack([o[1].reshape(bsz, B_WINDOW, B_KV_HEADS, HEAD_DIM) for o in outs_p])
    vb_p = jnp.stack([o[2].reshape(bsz, B_WINDOW, B_KV_HEADS, HEAD_DIM) for o in outs_p])
    kc_p = jnp.stack([o[3].reshape(bsz, C_BAND, C_HEADS, HEAD_DIM) for o in outs_p])
    vc_p = jnp.stack([o[4].reshape(bsz, C_BAND, C_HEADS, HEAD_DIM) for o in outs_p])
    sg_s = jnp.stack([o[0].reshape(dec_b, A_HEADS, A_DK, A_DV) for o in outs_s])
    tmajor = lambda arrs, heads: jnp.transpose(
        jnp.stack(arrs).reshape(DEPTH, dec_b, heads, HEAD_DIM, -1), (0, 1, 4, 2, 3))
    kb_s = tmajor([o[1] for o in outs_s], B_KV_HEADS)
    vb_s = tmajor([o[2] for o in outs_s], B_KV_HEADS)
    kc_s = tmajor([o[3] for o in outs_s], C_HEADS)
    vc_s = tmajor([o[4] for o in outs_s], C_HEADS)
    return (x_p.reshape(bsz, seq, D_MODEL), x_s.reshape(dec_b, dec_s, D_MODEL),
            sg_p, kb_p, vb_p, kc_p, vc_p, sg_s, kb_s, vb_s, kc_s, vc_s)
```

```python
import functools

import numpy as np
import jax
import jax.numpy as jnp
from jax import lax
from jax.experimental import pallas as pl
from jax.experimental.pallas import tpu as pltpu

D_MODEL = 1024
DEPTH = 2
CHUNK = 64
HEAD_DIM = 64
A_HEADS = 4
A_DK = 64
A_DV = 128
A_GATE_RANK = 16
A_GATE_TAU = 16.0
B_HEADS = 4
B_KV_HEADS = 2
B_WINDOW = 128
C_HEADS = 4
C_BAND = 512
C_CLIP = 256
T5_BUCKETS = 32
T5_MAX_DIST = 128
D_FF = 4 * D_MODEL
NORM_EPS = 1e-6
NEG_INF = -1e30

LANES = 128
SUBLANES = 8
VMEM_LIMIT_BYTES = 60000 * 1024
VMEM_LIMIT_SAMPLE_BYTES = 62 * 1024 * 1024

OFF_QA = 0
OFF_KA = OFF_QA + A_HEADS * A_DK
OFF_VA = OFF_KA + A_HEADS * A_DK
OFF_GA = OFF_VA + A_HEADS * A_DV
OFF_QB = OFF_GA + A_HEADS * A_DV
OFF_KB = OFF_QB + B_HEADS * HEAD_DIM
OFF_VB = OFF_KB + B_KV_HEADS * HEAD_DIM
OFF_QC = OFF_VB + B_KV_HEADS * HEAD_DIM
OFF_KC = OFF_QC + C_HEADS * HEAD_DIM
OFF_VC = OFF_KC + C_HEADS * HEAD_DIM
OFF_RA = OFF_VC + C_HEADS * HEAD_DIM
P_IN = OFF_RA + LANES
D_AQK = A_HEADS * A_DK
D_AV = A_HEADS * A_DV
D_BKV = B_KV_HEADS * HEAD_DIM
D_C = C_HEADS * HEAD_DIM
D_MIX = D_AV + B_HEADS * HEAD_DIM + D_C
LK_B = B_WINDOW + CHUNK
LK_C = C_BAND + CHUNK
STACK = 4 * CHUNK
GLA_LEVELS = (1, 2, 4, 8, 16, 32)
FF_BLOCK = 1024
N_FF = D_FF // FF_BLOCK

BF16 = jnp.bfloat16
F32 = jnp.float32


def _dot(a, b):
    return jnp.dot(a, b, preferred_element_type=F32)


def _dot_nt(a, b):
    return lax.dot_general(a, b, (((1,), (1,)), ((), ())), preferred_element_type=F32)


def _dot_tn(a, b):
    return lax.dot_general(a, b, (((0,), (0,)), ((), ())), preferred_element_type=F32)


def _rms(x, g):
    return x * lax.rsqrt(jnp.mean(x * x, axis=-1, keepdims=True) + NORM_EPS) * g


def _split3(x):
    hi = x.astype(BF16)
    r1 = x - hi.astype(F32)
    mid = r1.astype(BF16)
    lo = (r1 - mid.astype(F32)).astype(BF16)
    return hi, mid, lo


def _head_masks(width, per_head):
    lane = lax.broadcasted_iota(jnp.int32, (1, width), 1)
    return [(lane >= h * per_head) & (lane < (h + 1) * per_head) for h in range(width // per_head)]


def _stack_masked(x, masks):
    return jnp.concatenate([jnp.where(m, x, 0.0) for m in masks], axis=0)


def _tile_rows(x, n):
    return jnp.concatenate([x] * n, axis=0)


def _scores(q_st, segments, bias_st, maskrow):
    scores, off = [], 0
    for k, v, transposed in segments:
        n = k.shape[1] if transposed else k.shape[0]
        s = (_dot(q_st, k) if transposed else _dot_nt(q_st, k)) + bias_st[:, off:off + n]
        if maskrow is not None:
            s = s + maskrow[:, off:off + n]
        scores.append(s)
        off += n
    return scores


def _softmax_pv(scores, segments, sink):
    m = functools.reduce(jnp.maximum, [jnp.max(s, axis=1, keepdims=True) for s in scores])
    if sink is not None:
        m = jnp.maximum(m, sink)
    es = [jnp.exp(s - m) for s in scores]
    l = functools.reduce(jnp.add, [jnp.sum(e, axis=1, keepdims=True) for e in es])
    if sink is not None:
        l = l + jnp.exp(sink - m)
    o = None
    for e, (k, v, transposed) in zip(es, segments):
        part = _dot_nt(e.astype(BF16), v) if transposed else _dot(e.astype(BF16), v)
        o = part if o is None else o + part
    return o / l


def _level_exponent(b, m, t):
    bcast = lambda i: jnp.broadcast_to(b[i:i + 1], (SUBLANES, b.shape[1]))
    low_half = lax.broadcasted_iota(jnp.int32, (SUBLANES, b.shape[1]), 0) < SUBLANES // 2
    parts = []
    for g in range(CHUNK // SUBLANES):
        r = g * SUBLANES
        rows = b[r:r + SUBLANES]
        if m >= SUBLANES:
            ref = b[(r // (2 * m)) * 2 * m + m - 1:(r // (2 * m)) * 2 * m + m]
            parts.append(rows - ref if (r // m) % 2 == 1 else ref - rows)
        elif m == 4:
            parts.append(rows - bcast(r + 3))
        else:
            parts.append(rows - jnp.where(low_half, bcast(r + 1), bcast(r + 5)))
    d = jnp.concatenate(parts, axis=0)
    return d if m >= SUBLANES else jnp.where((t & m) != 0, d, -d)


def _gla_levels(q, k, la, b, hm_a):
    t = lax.broadcasted_iota(jnp.int32, (CHUNK, D_AQK), 0)
    q_st = _stack_masked(q, hm_a)
    k16 = k.astype(BF16)
    prods = [_dot_nt(q_st.astype(BF16), k16)]
    for m in GLA_LEVELS:
        e = jnp.exp(la if m == 1 else _level_exponent(b, m, t))
        qt = (q_st * _tile_rows(e, A_HEADS)).astype(BF16)
        kt = k16 if m == 1 else (k * e).astype(BF16)
        prods.append(_dot_nt(qt, kt))
    return q_st, prods


def _gla_finish(q_st, prods, k, v, b, s, lvl):
    attn = jnp.where(lvl == 0, prods[0], 0.0)
    for li in range(1, len(prods)):
        attn = jnp.where(lvl == li, prods[li], attn)
    qhat = (q_st * _tile_rows(jnp.exp(b), A_HEADS)).astype(BF16)
    o_inter = _dot(qhat, s.astype(BF16))
    v16 = v.astype(BF16)
    o_intra = jnp.concatenate(
        [_dot(attn[h * CHUNK:(h + 1) * CHUNK].astype(BF16), v16[:, h * A_DV:(h + 1) * A_DV])
         for h in range(A_HEADS)], axis=0)
    b_last = b[CHUNK - 1:CHUNK]
    khat = (k * jnp.exp(b_last - b)).astype(BF16)
    upd = _dot_tn(khat, v16)
    decay = jnp.broadcast_to(jnp.exp(b_last), (A_DV, D_AQK)).T
    s_new = s * decay + jnp.concatenate(
        [upd[h * A_DK:(h + 1) * A_DK, h * A_DV:(h + 1) * A_DV] for h in range(A_HEADS)], axis=0)
    return o_intra + o_inter, s_new


def _layer_kernel(*refs, prompt, final, nch, tq, nt):
    (x_ref, mods_ref, modp_ref, gmix_ref, gmlp_ref, gfin_ref, anorm_ref, win_ref, wa2_ref, ba2_ref, wout_ref,
     wup_ref, wdown_ref, biasb_ref, biasc_ref, sink_ref, tri_ref, lvl_ref) = refs[:18]
    pos = 18
    if not prompt:
        st_ref, cbk_ref, cbv_ref, cck_ref, ccv_ref = refs[pos:pos + 5]
        pos += 5
    y_ref, so_ref, kbo_ref, vbo_ref, kco_ref, vco_ref = refs[pos:pos + 6]
    pos += 6
    (h_scr, h2_scr, proj_scr, la_scr, b_scr, mix_scr, x1_scr, acc_scr, st_scr,
     kbw, vbw, kcw, vcw) = refs[pos:]

    step = pl.program_id(0)
    tile = jnp.minimum(step, nt - 1)

    def mod(ref, c, j):
        return ref[0 if prompt else c, j:j + 1, :]

    @pl.when(step == 0)
    def _():
        h2_scr[...] = jnp.zeros_like(h2_scr)
        x1_scr[...] = jnp.zeros_like(x1_scr)
        acc_scr[...] = jnp.zeros_like(acc_scr)
        if prompt:
            st_scr[...] = jnp.zeros_like(st_scr)
            kbw[...] = jnp.zeros_like(kbw)
            vbw[...] = jnp.zeros_like(vbw)
            kcw[...] = jnp.zeros_like(kcw)
            vcw[...] = jnp.zeros_like(vcw)

    gmix = gmix_ref[...]
    for c in range(nch):
        r = slice(c * CHUNK, (c + 1) * CHUNK)
        h = _rms(x_ref[r, :], gmix) * (1.0 + mod(mods_ref, c, 1)) + mod(mods_ref, c, 0)
        h_scr[r, :] = h.astype(BF16)
    hmix = h_scr[...]
    cut1, cut2 = OFF_GA, OFF_QC
    ra = _dot(hmix, win_ref[:, OFF_RA:OFF_RA + LANES])
    proj_scr[:, 0:cut1] = _dot(hmix, win_ref[:, 0:cut1])
    z = _dot(ra.astype(BF16), wa2_ref[...]) + ba2_ref[...]
    la = (jnp.minimum(z, 0.0) - jnp.log1p(jnp.exp(-jnp.abs(z)))) * (1.0 / A_GATE_TAU)
    la_scr[...] = la
    proj_scr[:, cut1:cut2] = _dot(hmix, win_ref[:, cut1:cut2])
    tri = tri_ref[...]
    hi, mid, lo = _split3(la)
    b_scr[...] = _dot(tri, hi) + _dot(tri, mid) + _dot(tri, lo)
    proj_scr[:, cut2:OFF_RA] = _dot(hmix, win_ref[:, cut2:OFF_RA])

    if prompt:
        kbw[B_WINDOW:B_WINDOW + tq, :] = proj_scr[:, OFF_KB:OFF_KB + D_BKV].astype(BF16)
        vbw[B_WINDOW:B_WINDOW + tq, :] = proj_scr[:, OFF_VB:OFF_VB + D_BKV].astype(BF16)
        kcw[C_BAND:C_BAND + tq, :] = proj_scr[:, OFF_KC:OFF_KC + D_C].astype(BF16)
        vcw[C_BAND:C_BAND + tq, :] = proj_scr[:, OFF_VC:OFF_VC + D_C].astype(BF16)

    hm_a = _head_masks(D_AQK, A_DK)
    hm_c = _head_masks(D_C, HEAD_DIM)
    hm_b = _head_masks(D_BKV, HEAD_DIM)
    anorm = anorm_ref[...]

    def chunk_body(c, carry):
        r0 = pl.multiple_of(c * CHUNK, CHUNK)
        rows = pl.ds(r0, CHUNK)

        if prompt:
            seg_b = [(kbw[pl.ds(r0, LK_B), :], vbw[pl.ds(r0, LK_B), :], False)]
            seg_c = [(kcw[pl.ds(r0, LK_C), :], vcw[pl.ds(r0, LK_C), :], False)]
            gidx = tile * nch + c
            lane_b = lax.broadcasted_iota(jnp.int32, (1, LK_B), 1)
            lane_c = lax.broadcasted_iota(jnp.int32, (1, LK_C), 1)
            mask_b = jnp.where(lane_b >= B_WINDOW - gidx * CHUNK, 0.0, NEG_INF)
            mask_c = jnp.where(lane_c >= C_BAND - gidx * CHUNK, 0.0, NEG_INF)
        else:
            kb_new = proj_scr[rows, OFF_KB:OFF_KB + D_BKV]
            vb_new = proj_scr[rows, OFF_VB:OFF_VB + D_BKV]
            kc_new = proj_scr[rows, OFF_KC:OFF_KC + D_C]
            vc_new = proj_scr[rows, OFF_VC:OFF_VC + D_C]
            seg_b = [(cbk_ref[c].astype(BF16), cbv_ref[c].astype(BF16), True),
                     (kb_new.astype(BF16), vb_new.astype(BF16), False)]
            seg_c = [(cck_ref[c].astype(BF16), ccv_ref[c].astype(BF16), True),
                     (kc_new.astype(BF16), vc_new.astype(BF16), False)]
            mask_b = mask_c = None

        q = proj_scr[rows, OFF_QA:OFF_QA + D_AQK] * (A_DK ** -0.5)
        k = proj_scr[rows, OFF_KA:OFF_KA + D_AQK]
        b = b_scr[rows, :]
        q_st, prods = _gla_levels(q, k, la_scr[rows, :], b, hm_a)
        qc = proj_scr[rows, OFF_QC:OFF_QC + D_C] * (HEAD_DIM ** -0.5)
        sc_c = _scores(_stack_masked(qc, hm_c).astype(BF16), seg_c, biasc_ref[...], mask_c)
        qb = proj_scr[rows, OFF_QB:OFF_QB + B_HEADS * HEAD_DIM] * (HEAD_DIM ** -0.5)
        qb_a, qb_b = qb[:, 0:D_BKV], qb[:, D_BKV:2 * D_BKV]
        qb_st = jnp.concatenate([jnp.where(hm_b[0], qb_a, 0.0), jnp.where(hm_b[0], qb_b, 0.0),
                                 jnp.where(hm_b[1], qb_a, 0.0), jnp.where(hm_b[1], qb_b, 0.0)], axis=0)
        sc_b = _scores(qb_st.astype(BF16), seg_b, biasb_ref[...], mask_b)

        up = _dot(h2_scr[...], wup_ref[c])
        act = jnp.square(jnp.maximum(up, 0.0)).astype(BF16)

        v = proj_scr[rows, OFF_VA:OFF_VA + D_AV]
        s_old = st_scr[...] if prompt else st_ref[c]
        o_st, s_new = _gla_finish(q_st, prods, k, v, b, s_old, lvl_ref[...])
        if prompt:
            st_scr[...] = s_new
        else:
            so_ref[c] = s_new
        g = proj_scr[rows, OFF_GA:OFF_GA + D_AV]
        for h in range(A_HEADS):
            on = _rms(o_st[h * CHUNK:(h + 1) * CHUNK], anorm)
            gh = g[:, h * A_DV:(h + 1) * A_DV]
            mix_scr[rows, h * A_DV:(h + 1) * A_DV] = (on * (gh * jax.nn.sigmoid(gh))).astype(BF16)

        oc_st = _softmax_pv(sc_c, seg_c, None)
        oc = jnp.where(hm_c[0], oc_st[0:CHUNK], 0.0)
        for h in range(1, C_HEADS):
            oc = jnp.where(hm_c[h], oc_st[h * CHUNK:(h + 1) * CHUNK], oc)
        mix_scr[rows, D_AV + 2 * D_BKV:D_MIX] = oc.astype(BF16)

        ob_st = _softmax_pv(sc_b, seg_b, sink_ref[...])
        ob_a = jnp.where(hm_b[0], ob_st[0:CHUNK], ob_st[2 * CHUNK:3 * CHUNK])
        ob_b = jnp.where(hm_b[0], ob_st[CHUNK:2 * CHUNK], ob_st[3 * CHUNK:4 * CHUNK])
        mix_scr[rows, D_AV:D_AV + D_BKV] = ob_a.astype(BF16)
        mix_scr[rows, D_AV + D_BKV:D_AV + 2 * D_BKV] = ob_b.astype(BF16)

        acc_scr[...] += _dot(act, wdown_ref[c])

        if not prompt:
            for dst, src, new, width in ((kbo_ref, cbk_ref, kb_new, B_WINDOW), (vbo_ref, cbv_ref, vb_new, B_WINDOW),
                                         (kco_ref, cck_ref, kc_new, C_BAND), (vco_ref, ccv_ref, vc_new, C_BAND)):
                dst[c, :, 0:width - CHUNK] = src[c, :, CHUNK:width]
                dst[c, :, width - CHUNK:width] = new.T
        return carry

    lax.fori_loop(0, nch, chunk_body, 0)

    if prompt:
        kbw[0:B_WINDOW, :] = kbw[tq:tq + B_WINDOW, :]
        vbw[0:B_WINDOW, :] = vbw[tq:tq + B_WINDOW, :]
        kcw[0:C_BAND, :] = kcw[tq:tq + C_BAND, :]
        vcw[0:C_BAND, :] = vcw[tq:tq + C_BAND, :]

        @pl.when(step < nt)
        def _():
            so_ref[...] = st_scr[...]
            kbo_ref[...] = proj_scr[tq - B_WINDOW:tq, OFF_KB:OFF_KB + D_BKV]
            vbo_ref[...] = proj_scr[tq - B_WINDOW:tq, OFF_VB:OFF_VB + D_BKV]
            rb = kco_ref.shape[0]
            kco_ref[...] = proj_scr[tq - rb:tq, OFF_KC:OFF_KC + D_C]
            vco_ref[...] = proj_scr[tq - rb:tq, OFF_VC:OFF_VC + D_C]

    mixed = _dot(mix_scr[...], wout_ref[...])

    gfin = gfin_ref[...]
    for c in range(nch):
        r = slice(c * CHUNK, (c + 1) * CHUNK)
        x2 = x1_scr[r, :] + mod(modp_ref, c, 5) * acc_scr[r, :]
        y_ref[r, :] = _rms(x2, gfin) if final else x2
    acc_scr[...] = jnp.zeros_like(acc_scr)

    gmlp = gmlp_ref[...]
    for c in range(nch):
        r = slice(c * CHUNK, (c + 1) * CHUNK)
        x1 = x_ref[r, :] + mod(mods_ref, c, 2) * mixed[r, :]
        x1_scr[r, :] = x1
        h2 = _rms(x1, gmlp) * (1.0 + mod(mods_ref, c, 4)) + mod(mods_ref, c, 3)
        h2_scr[r, :] = h2.astype(BF16)


def _layer_spec(arr, layer):
    nd = arr.ndim - 1
    return pl.BlockSpec((None,) + arr.shape[1:], lambda i, _l=layer, _nd=nd: (_l,) + (0,) * _nd,
                        pipeline_mode=pl.Buffered(1))


def _const_spec(arr):
    nd = arr.ndim
    return pl.BlockSpec(arr.shape, lambda i, _nd=nd: (0,) * _nd, pipeline_mode=pl.Buffered(1))


def _run_layer(x, mods, layer, stacked, shared, caches, *, prompt, final, tq, nb):
    rows = x.shape[0]
    nch = tq // CHUNK
    nt = rows // tq
    assert rows % tq == 0 and nch == N_FF
    (gmix, gmlp, anorm, win, wa2, ba2, wout, wup, wdown, biasc, sink) = stacked
    (gfin, biasb, tri, lvl) = shared
    cur = lambda i: jnp.minimum(i, nt - 1)
    prv = lambda i: jnp.maximum(i - 1, 0)

    nmod = 1 if prompt else nb
    in_specs = [pl.BlockSpec((tq, D_MODEL), lambda i: (cur(i), 0)),
                pl.BlockSpec((nmod, SUBLANES, D_MODEL), lambda i: (0 if prompt else cur(i), 0, 0)),
                pl.BlockSpec((nmod, SUBLANES, D_MODEL), lambda i: (0 if prompt else prv(i), 0, 0))]
    ls = functools.partial(_layer_spec, layer=layer)
    in_specs += [ls(gmix), ls(gmlp), _const_spec(gfin), ls(anorm), ls(win), ls(wa2), ls(ba2), ls(wout),
                 ls(wup), ls(wdown), _const_spec(biasb), ls(biasc), ls(sink), _const_spec(tri), _const_spec(lvl)]
    args = [x, mods, mods, gmix, gmlp, gfin, anorm, win, wa2, ba2, wout, wup, wdown, biasb, biasc, sink, tri, lvl]

    y_spec = pl.BlockSpec((tq, D_MODEL), lambda i: (prv(i), 0))
    if prompt:
        assert tq >= B_WINDOW and (tq % C_BAND == 0 or C_BAND % tq == 0)
        rb = min(tq, C_BAND)
        first = nt - C_BAND // rb
        out_shape = [jax.ShapeDtypeStruct((rows, D_MODEL), F32),
                     jax.ShapeDtypeStruct((D_AQK, A_DV), F32),
                     jax.ShapeDtypeStruct((B_WINDOW, D_BKV), F32),
                     jax.ShapeDtypeStruct((B_WINDOW, D_BKV), F32),
                     jax.ShapeDtypeStruct((C_BAND, D_C), F32),
                     jax.ShapeDtypeStruct((C_BAND, D_C), F32)]
        band = pl.BlockSpec((rb, D_C), lambda i: (jnp.maximum(cur(i) - first, 0), 0))
        out_specs = [y_spec,
                     pl.BlockSpec((D_AQK, A_DV), lambda i: (0, 0)),
                     pl.BlockSpec((B_WINDOW, D_BKV), lambda i: (0, 0)),
                     pl.BlockSpec((B_WINDOW, D_BKV), lambda i: (0, 0)),
                     band, band]
        kbw_rows, kcw_rows = B_WINDOW + tq, C_BAND + tq
    else:
        assert nb == nch
        nseq = rows // CHUNK
        st, cbk, cbv, cck, ccv = caches
        seq_in = lambda r, w: pl.BlockSpec((None, nb, r, w), lambda i, _l=layer: (_l, cur(i), 0, 0))
        seq_out = lambda r, w: pl.BlockSpec((nb, r, w), lambda i: (cur(i), 0, 0))
        in_specs += [seq_in(D_AQK, A_DV), seq_in(D_BKV, B_WINDOW), seq_in(D_BKV, B_WINDOW),
                     seq_in(D_C, C_BAND), seq_in(D_C, C_BAND)]
        args += [st, cbk, cbv, cck, ccv]
        out_shape = [jax.ShapeDtypeStruct((rows, D_MODEL), F32),
                     jax.ShapeDtypeStruct((nseq, D_AQK, A_DV), F32),
                     jax.ShapeDtypeStruct((nseq, D_BKV, B_WINDOW), F32),
                     jax.ShapeDtypeStruct((nseq, D_BKV, B_WINDOW), F32),
                     jax.ShapeDtypeStruct((nseq, D_C, C_BAND), F32),
                     jax.ShapeDtypeStruct((nseq, D_C, C_BAND), F32)]
        out_specs = [y_spec, seq_out(D_AQK, A_DV), seq_out(D_BKV, B_WINDOW), seq_out(D_BKV, B_WINDOW),
                     seq_out(D_C, C_BAND), seq_out(D_C, C_BAND)]
        kbw_rows, kcw_rows = SUBLANES * 2, SUBLANES * 2

    scratch = [pltpu.VMEM((tq, D_MODEL), BF16),
               pltpu.VMEM((tq, D_MODEL), BF16),
               pltpu.VMEM((tq, OFF_RA), F32),
               pltpu.VMEM((tq, D_AQK), F32),
               pltpu.VMEM((tq, D_AQK), F32),
               pltpu.VMEM((tq, D_MIX), BF16),
               pltpu.VMEM((tq, D_MODEL), F32),
               pltpu.VMEM((tq, D_MODEL), F32),
               pltpu.VMEM((D_AQK, A_DV), F32),
               pltpu.VMEM((kbw_rows, D_BKV), BF16), pltpu.VMEM((kbw_rows, D_BKV), BF16),
               pltpu.VMEM((kcw_rows, D_C), BF16), pltpu.VMEM((kcw_rows, D_C), BF16)]

    kern = functools.partial(_layer_kernel, prompt=prompt, final=final, nch=nch, tq=tq, nt=nt)
    return pl.pallas_call(
        kern,
        grid=(nt + 1,),
        in_specs=in_specs,
        out_specs=out_specs,
        out_shape=out_shape,
        scratch_shapes=scratch,
        compiler_params=pltpu.CompilerParams(
            dimension_semantics=("arbitrary",),
            vmem_limit_bytes=VMEM_LIMIT_BYTES if prompt else VMEM_LIMIT_SAMPLE_BYTES),
        name=("layer_prompt" if prompt else "layer_sample") + ("_final" if final else ""),
    )(*args)


ADA_BLOCK = 1536


def _ada_kernel(c_ref, w_ref, b_ref, o_ref):
    c = c_ref[...]
    sc = c * jax.nn.sigmoid(c)
    o_ref[0] = jnp.dot(sc, w_ref[0], preferred_element_type=F32,
                       precision=lax.Precision.HIGHEST) + b_ref[0]


def _ada(c_all, w_ada, b_ada):
    rows = c_all.shape[0]
    n = 6 * D_MODEL
    return pl.pallas_call(
        _ada_kernel,
        grid=(DEPTH, n // ADA_BLOCK),
        in_specs=[pl.BlockSpec((rows, D_MODEL), lambda l, j: (0, 0)),
                  pl.BlockSpec((1, D_MODEL, ADA_BLOCK), lambda l, j: (l, 0, j)),
                  pl.BlockSpec((1, 1, ADA_BLOCK), lambda l, j: (l, 0, j))],
        out_specs=pl.BlockSpec((1, rows, ADA_BLOCK), lambda l, j: (l, 0, j)),
        out_shape=jax.ShapeDtypeStruct((DEPTH, rows, n), F32),
        compiler_params=pltpu.CompilerParams(dimension_semantics=("arbitrary", "arbitrary"),
                                             vmem_limit_bytes=VMEM_LIMIT_BYTES),
        name="adaln",
    )(c_all, w_ada, b_ada.reshape(DEPTH, 1, n))


TB_W = 2 * LANES
TC_W = 5 * LANES
TC_FLAT = TC_W - (CHUNK + C_CLIP + 1)


def _shear(f, width):
    tiled = jnp.broadcast_to(f, (CHUNK, width))
    return pltpu.roll(tiled, width - (CHUNK - 1), 1, stride=1, stride_axis=0)


def _bias_kernel(t5_ref, crel_ref, bucket_ref, ob_ref, oc_ref):
    bucket = bucket_ref[...]
    for h in range(B_HEADS):
        def b_body(i, acc, h=h):
            return jnp.where(bucket == i, t5_ref[h, i], acc)
        f = lax.fori_loop(0, T5_BUCKETS, b_body, jnp.zeros(bucket.shape, F32))
        ob_ref[h] = _shear(f, TB_W)[:, 0:LK_B]
    lane = lax.broadcasted_iota(jnp.int32, (1, TC_W), 1)
    for l in range(DEPTH):
        for h in range(C_HEADS):
            row = crel_ref[l, h:h + 1, :]
            f = jnp.where(lane < TC_FLAT, row[:, 0:1], pltpu.roll(row, TC_FLAT, 1))
            oc_ref[l, h] = _shear(f, TC_W)[:, 0:LK_C]


def _bias_tables(t5_bias, c_rel_bias, bucket):
    smem = pl.BlockSpec(memory_space=pltpu.SMEM)
    vmem = pl.BlockSpec(memory_space=pltpu.VMEM)
    crel = jnp.pad(jnp.swapaxes(c_rel_bias, 1, 2),
                   ((0, 0), (0, SUBLANES - C_HEADS), (0, TC_W - (2 * C_CLIP + 1))))
    return pl.pallas_call(
        _bias_kernel,
        in_specs=[smem, vmem, vmem],
        out_specs=[vmem, vmem],
        out_shape=[jax.ShapeDtypeStruct((B_HEADS, CHUNK, LK_B), F32),
                   jax.ShapeDtypeStruct((DEPTH, C_HEADS, CHUNK, LK_C), F32)],
        name="bias_tables",
    )(t5_bias.T, crel, bucket)


def _t5_bucket(rel):
    half = T5_BUCKETS // 2
    max_exact = half // 2
    n = jnp.abs(rel)
    log_ratio = jnp.log(jnp.maximum(n, 1).astype(jnp.float32) / max_exact) / np.log(T5_MAX_DIST / max_exact)
    large = jnp.minimum(max_exact + (log_ratio * (half - max_exact)).astype(jnp.int32), half - 1)
    return jnp.where(rel > 0, half, 0) + jnp.where(n < max_exact, n, large)


def _gla_constants(tq):
    r = np.arange(tq)
    tri = ((r[:, None] // CHUNK == r[None, :] // CHUNK) & (r[None, :] <= r[:, None])).astype(np.float32)
    t = np.arange(CHUNK)[:, None]
    s = np.arange(CHUNK)[None, :]
    lvl = np.full((CHUNK, CHUNK), -1, np.int32)
    lvl[t == s] = 0
    for li, m in enumerate(GLA_LEVELS):
        sel = ((t // m) % 2 == 1) & ((s // m) == (t // m) - 1)
        lvl[np.broadcast_to(sel, lvl.shape)] = li + 1
    return jnp.asarray(tri, BF16), jnp.asarray(np.tile(lvl, (A_HEADS, 1)))


TQ_PROMPT = 256
NB_SAMPLE = 4


def kernel(x_prompt, x_sample, c_prompt, c_sample, state_gla, cache_b_k, cache_b_v, cache_c_k, cache_c_v,
           w_ada, b_ada, norm_mix_g, norm_mlp_g, w_in, w_a2, b_a2, a_norm_g, b_sink, t5_bias, c_rel_bias,
           w_out, w_up, w_down, final_norm_g):
    bsz, seq, _ = x_prompt.shape
    dec_b, dec_s, _ = x_sample.shape
    assert bsz == 1 and dec_s == CHUNK and TQ_PROMPT == NB_SAMPLE * CHUNK

    n_c = bsz + dec_b
    c_rows = -(-n_c // SUBLANES) * SUBLANES
    c_all = jnp.concatenate([c_prompt, c_sample, jnp.zeros((c_rows - n_c, D_MODEL), F32)], axis=0)
    mods = _ada(c_all, w_ada, b_ada).reshape(DEPTH, c_rows, 6, D_MODEL)
    mods = jnp.pad(mods, ((0, 0), (0, 0), (0, SUBLANES - 6), (0, 0)))

    rel_b = jnp.arange(TB_W) - (CHUNK - 1) - B_WINDOW
    bias_b, bias_c = _bias_tables(t5_bias, c_rel_bias, _t5_bucket(rel_b).astype(jnp.int32)[None, :])
    bias_b = bias_b.reshape(STACK, LK_B)
    bias_c = bias_c.reshape(DEPTH, STACK, LK_C)

    tri, lvl = _gla_constants(TQ_PROMPT)
    qb0 = 1552
    head = lambda h: slice(qb0 + h * HEAD_DIM, qb0 + (h + 1) * HEAD_DIM)
    ob0 = D_AV
    orow = lambda h: slice(ob0 + h * HEAD_DIM, ob0 + (h + 1) * HEAD_DIM)

    win = jnp.concatenate([w_in[:, :, 0:1536], w_in[:, :, head(0)], w_in[:, :, head(2)], w_in[:, :, head(1)],
                           w_in[:, :, head(3)], w_in[:, :, 1808:2832], w_in[:, :, 1536:1552],
                           jnp.zeros((DEPTH, D_MODEL, LANES - A_GATE_RANK), F32)], axis=2).astype(BF16)
    wa2 = jnp.concatenate([w_a2, jnp.zeros((DEPTH, LANES - A_GATE_RANK, D_AQK), F32)], axis=1).astype(BF16)
    wout = jnp.concatenate([w_out[:, 0:D_AV], w_out[:, orow(0)], w_out[:, orow(2)], w_out[:, orow(1)],
                            w_out[:, orow(3)], w_out[:, D_AV + B_HEADS * HEAD_DIM:]], axis=1).astype(BF16)
    wup = jnp.transpose(w_up.astype(BF16).reshape(DEPTH, D_MODEL, N_FF, FF_BLOCK), (0, 2, 1, 3))
    wdown = w_down.astype(BF16).reshape(DEPTH, N_FF, FF_BLOCK, D_MODEL)
    sink = jnp.repeat(b_sink, CHUNK, axis=1)[:, :, None]
    stacked = (norm_mix_g[:, None, :], norm_mlp_g[:, None, :], a_norm_g[:, None, :], win, wa2,
               b_a2[:, None, :], wout, wup, wdown, bias_c, sink)
    shared = (final_norm_g[None, :], bias_b, tri, lvl)

    x_p = x_prompt.reshape(seq, D_MODEL)
    x_s = x_sample.reshape(dec_b * dec_s, D_MODEL)
    outs_p, outs_s = [], []
    fmajor = lambda cache: jnp.transpose(cache, (0, 1, 3, 4, 2)).reshape(
        DEPTH, dec_b, cache.shape[3] * cache.shape[4], cache.shape[2])
    caches = (state_gla.reshape(DEPTH, dec_b, D_AQK, A_DV), fmajor(cache_b_k), fmajor(cache_b_v),
              fmajor(cache_c_k), fmajor(cache_c_v))
    for l in range(DEPTH):
        final = l == DEPTH - 1
        res_p = _run_layer(x_p, mods[l, 0:bsz], l, stacked, shared, None, prompt=True, final=final,
                           tq=TQ_PROMPT, nb=None)
        x_p = res_p[0]
        outs_p.append(res_p[1:])

        res_s = _run_layer(x_s, mods[l, bsz:bsz + dec_b], l, stacked, shared, caches, prompt=False,
                           final=final, tq=NB_SAMPLE * CHUNK, nb=NB_SAMPLE)
        x_s = res_s[0]
        outs_s.append(res_s[1:])

    sg_p = jnp.stack([o[0].reshape(bsz, A_HEADS, A_DK, A_DV) for o in outs_p])
    kb_p = jnp.stack([o[1].reshape(bsz, B_WINDOW, B_KV_HEADS, HEAD_DIM) for o in outs_p])
    vb_p = jnp.stack([o[2].reshape(bsz, B_WINDOW, B_KV_HEADS, HEAD_DIM) for o in outs_p])
    kc_p = jnp.stack([o[3].reshape(bsz, C_BAND, C_HEADS, HEAD_DIM) for o in outs_p])
    vc_p = jnp.stack([o[4].reshape(bsz, C_BAND, C_HEADS, HEAD_DIM) for o in outs_p])
    sg_s = jnp.stack([o[0].reshape(dec_b, A_HEADS, A_DK, A_DV) for o in outs_s])
    tmajor = lambda arrs, heads: jnp.transpose(
        jnp.stack(arrs).reshape(DEPTH, dec_b, heads, HEAD_DIM, -1), (0, 1, 4, 2, 3))
    kb_s = tmajor([o[1] for o in outs_s], B_KV_HEADS)
    vb_s = tmajor([o[2] for o in outs_s], B_KV_HEADS)
    kc_s = tmajor([o[3] for o in outs_s], C_HEADS)
    vc_s = tmajor([o[4] for o in outs_s], C_HEADS)
    return (x_p.reshape(bsz, seq, D_MODEL), x_s.reshape(dec_b, dec_s, D_MODEL),
            sg_p, kb_p, vb_p, kc_p, vc_p, sg_s, kb_s, vb_s, kc_s, vc_s)
```

```python
import functools

import numpy as np
import jax
import jax.numpy as jnp
from jax import lax
from jax.experimental import pallas as pl
from jax.experimental.pallas import tpu as pltpu

D_MODEL = 1024
DEPTH = 2
CHUNK = 64
HEAD_DIM = 64
A_HEADS = 4
A_DK = 64
A_DV = 128
A_GATE_RANK = 16
A_GATE_TAU = 16.0
B_HEADS = 4
B_KV_HEADS = 2
B_WINDOW = 128
C_HEADS = 4
C_BAND = 512
C_CLIP = 256
T5_BUCKETS = 32
T5_MAX_DIST = 128
D_FF = 4 * D_MODEL
NORM_EPS = 1e-6
NEG_INF = -1e30

LANES = 128
SUBLANES = 8
VMEM_LIMIT_BYTES = 60000 * 1024
VMEM_LIMIT_SAMPLE_BYTES = 62 * 1024 * 1024

OFF_QA = 0
OFF_KA = OFF_QA + A_HEADS * A_DK
OFF_VA = OFF_KA + A_HEADS * A_DK
OFF_GA = OFF_VA + A_HEADS * A_DV
OFF_QB = OFF_GA + A_HEADS * A_DV
OFF_KB = OFF_QB + B_HEADS * HEAD_DIM
OFF_VB = OFF_KB + B_KV_HEADS * HEAD_DIM
OFF_QC = OFF_VB + B_KV_HEADS * HEAD_DIM
OFF_KC = OFF_QC + C_HEADS * HEAD_DIM
OFF_VC = OFF_KC + C_HEADS * HEAD_DIM
OFF_RA = OFF_VC + C_HEADS * HEAD_DIM
P_IN = OFF_RA + LANES
D_AQK = A_HEADS * A_DK
D_AV = A_HEADS * A_DV
D_BKV = B_KV_HEADS * HEAD_DIM
D_C = C_HEADS * HEAD_DIM
D_MIX = D_AV + B_HEADS * HEAD_DIM + D_C
LK_B = B_WINDOW + CHUNK
LK_C = C_BAND + CHUNK
STACK = 4 * CHUNK
GLA_LEVELS = (1, 2, 4, 8, 16, 32)
FF_BLOCK = 1024
N_FF = D_FF // FF_BLOCK

BF16 = jnp.bfloat16
F32 = jnp.float32


def _dot(a, b):
    return jnp.dot(a, b, preferred_element_type=F32)


def _dot_nt(a, b):
    return lax.dot_general(a, b, (((1,), (1,)), ((), ())), preferred_element_type=F32)


def _dot_tn(a, b):
    return lax.dot_general(a, b, (((0,), (0,)), ((), ())), preferred_element_type=F32)


def _rms(x, g):
    return x * lax.rsqrt(jnp.mean(x * x, axis=-1, keepdims=True) + NORM_EPS) * g


def _split3(x):
    hi = x.astype(BF16)
    r1 = x - hi.astype(F32)
    mid = r1.astype(BF16)
    lo = (r1 - mid.astype(F32)).astype(BF16)
    return hi, mid, lo


def _head_masks(width, per_head):
    lane = lax.broadcasted_iota(jnp.int32, (1, width), 1)
    return [(lane >= h * per_head) & (lane < (h + 1) * per_head) for h in range(width // per_head)]


def _stack_masked(x, masks):
    return jnp.concatenate([jnp.where(m, x, 0.0) for m in masks], axis=0)


def _tile_rows(x, n):
    return jnp.concatenate([x] * n, axis=0)


def _scores(q_st, segments, bias_st, maskrow):
    scores, off = [], 0
    for k, v, transposed in segments:
        n = k.shape[1] if transposed else k.shape[0]
        s = (_dot(q_st, k) if transposed else _dot_nt(q_st, k)) + bias_st[:, off:off + n]
        if maskrow is not None:
            s = s + maskrow[:, off:off + n]
        scores.append(s)
        off += n
    return scores


def _softmax_pv(scores, segments, sink):
    m = functools.reduce(jnp.maximum, [jnp.max(s, axis=1, keepdims=True) for s in scores])
    if sink is not None:
        m = jnp.maximum(m, sink)
    es = [jnp.exp(s - m) for s in scores]
    l = functools.reduce(jnp.add, [jnp.sum(e, axis=1, keepdims=True) for e in es])
    if sink is not None:
        l = l + jnp.exp(sink - m)
    o = None
    for e, (k, v, transposed) in zip(es, segments):
        part = _dot_nt(e.astype(BF16), v) if transposed else _dot(e.astype(BF16), v)
        o = part if o is None else o + part
    return o / l


def _level_exponent(b, m, t):
    bcast = lambda i: jnp.broadcast_to(b[i:i + 1], (SUBLANES, b.shape[1]))
    low_half = lax.broadcasted_iota(jnp.int32, (SUBLANES, b.shape[1]), 0) < SUBLANES // 2
    parts = []
    for g in range(CHUNK // SUBLANES):
        r = g * SUBLANES
        rows = b[r:r + SUBLANES]
        if m >= SUBLANES:
            ref = b[(r // (2 * m)) * 2 * m + m - 1:(r // (2 * m)) * 2 * m + m]
            parts.append(rows - ref if (r // m) % 2 == 1 else ref - rows)
        elif m == 4:
            parts.append(rows - bcast(r + 3))
        else:
            parts.append(rows - jnp.where(low_half, bcast(r + 1), bcast(r + 5)))
    d = jnp.concatenate(parts, axis=0)
    return d if m >= SUBLANES else jnp.where((t & m) != 0, d, -d)


def _gla_levels(q, k, la, b, hm_a):
    t = lax.broadcasted_iota(jnp.int32, (CHUNK, D_AQK), 0)
    q_st = _stack_masked(q, hm_a)
    k16 = k.astype(BF16)
    prods = [_dot_nt(q_st.astype(BF16), k16)]
    for m in GLA_LEVELS:
        e = jnp.exp(la if m == 1 else _level_exponent(b, m, t))
        qt = (q_st * _tile_rows(e, A_HEADS)).astype(BF16)
        kt = k16 if m == 1 else (k * e).astype(BF16)
        prods.append(_dot_nt(qt, kt))
    return q_st, prods


def _gla_finish(q_st, prods, k, v, b, s, lvl):
    attn = jnp.where(lvl == 0, prods[0], 0.0)
    for li in range(1, len(prods)):
        attn = jnp.where(lvl == li, prods[li], attn)
    qhat = (q_st * _tile_rows(jnp.exp(b), A_HEADS)).astype(BF16)
    o_inter = _dot(qhat, s.astype(BF16))
    v16 = v.astype(BF16)
    o_intra = jnp.concatenate(
        [_dot(attn[h * CHUNK:(h + 1) * CHUNK].astype(BF16), v16[:, h * A_DV:(h + 1) * A_DV])
         for h in range(A_HEADS)], axis=0)
    b_last = b[CHUNK - 1:CHUNK]
    khat = (k * jnp.exp(b_last - b)).astype(BF16)
    upd = _dot_tn(khat, v16)
    decay = jnp.broadcast_to(jnp.exp(b_last), (A_DV, D_AQK)).T
    s_new = s * decay + jnp.concatenate(
        [upd[h * A_DK:(h + 1) * A_DK, h * A_DV:(h + 1) * A_DV] for h in range(A_HEADS)], axis=0)
    return o_intra + o_inter, s_new


def _layer_kernel(*refs, prompt, final, nch, tq, nt):
    (x_ref, mods_ref, modp_ref, gmix_ref, gmlp_ref, gfin_ref, anorm_ref, win_ref, wa2_ref, ba2_ref, wout_ref,
     wup_ref, wdown_ref, biasb_ref, biasc_ref, sink_ref, tri_ref, lvl_ref) = refs[:18]
    pos = 18
    if not prompt:
        st_ref, cbk_ref, cbv_ref, cck_ref, ccv_ref = refs[pos:pos + 5]
        pos += 5
    y_ref, so_ref, kbo_ref, vbo_ref, kco_ref, vco_ref = refs[pos:pos + 6]
    pos += 6
    (h_scr, h2_scr, proj_scr, la_scr, b_scr, mix_scr, x1_scr, acc_scr, st_scr,
     kbw, vbw, kcw, vcw) = refs[pos:]

    step = pl.program_id(0)

    def mod(ref, c, j):
        return ref[0 if prompt else c, j:j + 1, :]

    @pl.when(step == 0)
    def _():
        h2_scr[...] = jnp.zeros_like(h2_scr)
        x1_scr[...] = jnp.zeros_like(x1_scr)
        acc_scr[...] = jnp.zeros_like(acc_scr)
        if prompt:
            st_scr[...] = jnp.zeros_like(st_scr)
            kbw[...] = jnp.zeros_like(kbw)
            vbw[...] = jnp.zeros_like(vbw)
            kcw[...] = jnp.zeros_like(kcw)
            vcw[...] = jnp.zeros_like(vcw)

    def finish_previous_tile():
        gfin = gfin_ref[...]
        for c in range(nch):
            r = slice(c * CHUNK, (c + 1) * CHUNK)
            x2 = x1_scr[r, :] + mod(modp_ref, c, 5) * acc_scr[r, :]
            y_ref[r, :] = _rms(x2, gfin) if final else x2
        acc_scr[...] = jnp.zeros_like(acc_scr)

    @pl.when(step < nt)
    def _():
        gmix = gmix_ref[...]
        for c in range(nch):
            r = slice(c * CHUNK, (c + 1) * CHUNK)
            h = _rms(x_ref[r, :], gmix) * (1.0 + mod(mods_ref, c, 1)) + mod(mods_ref, c, 0)
            h_scr[r, :] = h.astype(BF16)
        hmix = h_scr[...]
        cut1, cut2 = OFF_GA, OFF_QC
        ra = _dot(hmix, win_ref[:, OFF_RA:OFF_RA + LANES])
        proj_scr[:, 0:cut1] = _dot(hmix, win_ref[:, 0:cut1])
        z = _dot(ra.astype(BF16), wa2_ref[...]) + ba2_ref[...]
        la = (jnp.minimum(z, 0.0) - jnp.log1p(jnp.exp(-jnp.abs(z)))) * (1.0 / A_GATE_TAU)
        la_scr[...] = la
        proj_scr[:, cut1:cut2] = _dot(hmix, win_ref[:, cut1:cut2])
        tri = tri_ref[...]
        hi, mid, lo = _split3(la)
        b_scr[...] = _dot(tri, hi) + _dot(tri, mid) + _dot(tri, lo)
        proj_scr[:, cut2:OFF_RA] = _dot(hmix, win_ref[:, cut2:OFF_RA])

        if prompt:
            kbw[B_WINDOW:B_WINDOW + tq, :] = proj_scr[:, OFF_KB:OFF_KB + D_BKV].astype(BF16)
            vbw[B_WINDOW:B_WINDOW + tq, :] = proj_scr[:, OFF_VB:OFF_VB + D_BKV].astype(BF16)
            kcw[C_BAND:C_BAND + tq, :] = proj_scr[:, OFF_KC:OFF_KC + D_C].astype(BF16)
            vcw[C_BAND:C_BAND + tq, :] = proj_scr[:, OFF_VC:OFF_VC + D_C].astype(BF16)

        hm_a = _head_masks(D_AQK, A_DK)
        hm_c = _head_masks(D_C, HEAD_DIM)
        hm_b = _head_masks(D_BKV, HEAD_DIM)
        anorm = anorm_ref[...]

        def chunk_body(c, carry):
            r0 = pl.multiple_of(c * CHUNK, CHUNK)
            rows = pl.ds(r0, CHUNK)

            if prompt:
                seg_b = [(kbw[pl.ds(r0, LK_B), :], vbw[pl.ds(r0, LK_B), :], False)]
                seg_c = [(kcw[pl.ds(r0, LK_C), :], vcw[pl.ds(r0, LK_C), :], False)]
                gidx = step * nch + c
                lane_b = lax.broadcasted_iota(jnp.int32, (1, LK_B), 1)
                lane_c = lax.broadcasted_iota(jnp.int32, (1, LK_C), 1)
                mask_b = jnp.where(lane_b >= B_WINDOW - gidx * CHUNK, 0.0, NEG_INF)
                mask_c = jnp.where(lane_c >= C_BAND - gidx * CHUNK, 0.0, NEG_INF)
            else:
                kb_new = proj_scr[rows, OFF_KB:OFF_KB + D_BKV]
                vb_new = proj_scr[rows, OFF_VB:OFF_VB + D_BKV]
                kc_new = proj_scr[rows, OFF_KC:OFF_KC + D_C]
                vc_new = proj_scr[rows, OFF_VC:OFF_VC + D_C]
                seg_b = [(cbk_ref[c].astype(BF16), cbv_ref[c].astype(BF16), True),
                         (kb_new.astype(BF16), vb_new.astype(BF16), False)]
                seg_c = [(cck_ref[c].astype(BF16), ccv_ref[c].astype(BF16), True),
                         (kc_new.astype(BF16), vc_new.astype(BF16), False)]
                mask_b = mask_c = None

            q = proj_scr[rows, OFF_QA:OFF_QA + D_AQK] * (A_DK ** -0.5)
            k = proj_scr[rows, OFF_KA:OFF_KA + D_AQK]
            b = b_scr[rows, :]
            q_st, prods = _gla_levels(q, k, la_scr[rows, :], b, hm_a)
            qc = proj_scr[rows, OFF_QC:OFF_QC + D_C] * (HEAD_DIM ** -0.5)
            sc_c = _scores(_stack_masked(qc, hm_c).astype(BF16), seg_c, biasc_ref[...], mask_c)
            qb = proj_scr[rows, OFF_QB:OFF_QB + B_HEADS * HEAD_DIM] * (HEAD_DIM ** -0.5)
            qb_a, qb_b = qb[:, 0:D_BKV], qb[:, D_BKV:2 * D_BKV]
            qb_st = jnp.concatenate([jnp.where(hm_b[0], qb_a, 0.0), jnp.where(hm_b[0], qb_b, 0.0),
                                     jnp.where(hm_b[1], qb_a, 0.0), jnp.where(hm_b[1], qb_b, 0.0)], axis=0)
            sc_b = _scores(qb_st.astype(BF16), seg_b, biasb_ref[...], mask_b)

            up = _dot(h2_scr[...], wup_ref[c])
            act = jnp.square(jnp.maximum(up, 0.0)).astype(BF16)

            v = proj_scr[rows, OFF_VA:OFF_VA + D_AV]
            s_old = st_scr[...] if prompt else st_ref[c]
            o_st, s_new = _gla_finish(q_st, prods, k, v, b, s_old, lvl_ref[...])
            if prompt:
                st_scr[...] = s_new
            else:
                so_ref[c] = s_new
            g = proj_scr[rows, OFF_GA:OFF_GA + D_AV]
            for h in range(A_HEADS):
                on = _rms(o_st[h * CHUNK:(h + 1) * CHUNK], anorm)
                gh = g[:, h * A_DV:(h + 1) * A_DV]
                mix_scr[rows, h * A_DV:(h + 1) * A_DV] = (on * (gh * jax.nn.sigmoid(gh))).astype(BF16)

            oc_st = _softmax_pv(sc_c, seg_c, None)
            oc = jnp.where(hm_c[0], oc_st[0:CHUNK], 0.0)
            for h in range(1, C_HEADS):
                oc = jnp.where(hm_c[h], oc_st[h * CHUNK:(h + 1) * CHUNK], oc)
            mix_scr[rows, D_AV + 2 * D_BKV:D_MIX] = oc.astype(BF16)

            ob_st = _softmax_pv(sc_b, seg_b, sink_ref[...])
            ob_a = jnp.where(hm_b[0], ob_st[0:CHUNK], ob_st[2 * CHUNK:3 * CHUNK])
            ob_b = jnp.where(hm_b[0], ob_st[CHUNK:2 * CHUNK], ob_st[3 * CHUNK:4 * CHUNK])
            mix_scr[rows, D_AV:D_AV + D_BKV] = ob_a.astype(BF16)
            mix_scr[rows, D_AV + D_BKV:D_AV + 2 * D_BKV] = ob_b.astype(BF16)

            acc_scr[...] += _dot(act, wdown_ref[c])

            if not prompt:
                for dst, src, new, width in ((kbo_ref, cbk_ref, kb_new, B_WINDOW), (vbo_ref, cbv_ref, vb_new, B_WINDOW),
                                             (kco_ref, cck_ref, kc_new, C_BAND), (vco_ref, ccv_ref, vc_new, C_BAND)):
                    dst[c, :, 0:width - CHUNK] = src[c, :, CHUNK:width]
                    dst[c, :, width - CHUNK:width] = new.T
            return carry

        lax.fori_loop(0, nch, chunk_body, 0)

        if prompt:
            kbw[0:B_WINDOW, :] = kbw[tq:tq + B_WINDOW, :]
            vbw[0:B_WINDOW, :] = vbw[tq:tq + B_WINDOW, :]
            kcw[0:C_BAND, :] = kcw[tq:tq + C_BAND, :]
            vcw[0:C_BAND, :] = vcw[tq:tq + C_BAND, :]

            so_ref[...] = st_scr[...]
            kbo_ref[...] = proj_scr[tq - B_WINDOW:tq, OFF_KB:OFF_KB + D_BKV]
            vbo_ref[...] = proj_scr[tq - B_WINDOW:tq, OFF_VB:OFF_VB + D_BKV]
            rb = kco_ref.shape[0]
            kco_ref[...] = proj_scr[tq - rb:tq, OFF_KC:OFF_KC + D_C]
            vco_ref[...] = proj_scr[tq - rb:tq, OFF_VC:OFF_VC + D_C]

        mixed = _dot(mix_scr[...], wout_ref[...])

        finish_previous_tile()

        gmlp = gmlp_ref[...]
        for c in range(nch):
            r = slice(c * CHUNK, (c + 1) * CHUNK)
            x1 = x_ref[r, :] + mod(mods_ref, c, 2) * mixed[r, :]
            x1_scr[r, :] = x1
            h2 = _rms(x1, gmlp) * (1.0 + mod(mods_ref, c, 4)) + mod(mods_ref, c, 3)
            h2_scr[r, :] = h2.astype(BF16)

    @pl.when(step == nt)
    def _():
        def mlp_block(c, carry):
            up = _dot(h2_scr[...], wup_ref[c])
            act = jnp.square(jnp.maximum(up, 0.0)).astype(BF16)
            acc_scr[...] += _dot(act, wdown_ref[c])
            return carry

        lax.fori_loop(0, nch, mlp_block, 0)
        finish_previous_tile()


def _layer_spec(arr, layer):
    nd = arr.ndim - 1
    return pl.BlockSpec((None,) + arr.shape[1:], lambda i, _l=layer, _nd=nd: (_l,) + (0,) * _nd,
                        pipeline_mode=pl.Buffered(1))


def _const_spec(arr):
    nd = arr.ndim
    return pl.BlockSpec(arr.shape, lambda i, _nd=nd: (0,) * _nd, pipeline_mode=pl.Buffered(1))


def _run_layer(x, mods, layer, stacked, shared, caches, *, prompt, final, tq, nb):
    rows = x.shape[0]
    nch = tq // CHUNK
    nt = rows // tq
    assert rows % tq == 0 and nch == N_FF
    (gmix, gmlp, anorm, win, wa2, ba2, wout, wup, wdown, biasc, sink) = stacked
    (gfin, biasb, tri, lvl) = shared
    cur = lambda i: jnp.minimum(i, nt - 1)
    prv = lambda i: jnp.maximum(i - 1, 0)

    nmod = 1 if prompt else nb
    in_specs = [pl.BlockSpec((tq, D_MODEL), lambda i: (cur(i), 0)),
                pl.BlockSpec((nmod, SUBLANES, D_MODEL), lambda i: (0 if prompt else cur(i), 0, 0)),
                pl.BlockSpec((nmod, SUBLANES, D_MODEL), lambda i: (0 if prompt else prv(i), 0, 0))]
    ls = functools.partial(_layer_spec, layer=layer)
    in_specs += [ls(gmix), ls(gmlp), _const_spec(gfin), ls(anorm), ls(win), ls(wa2), ls(ba2), ls(wout),
                 ls(wup), ls(wdown), _const_spec(biasb), ls(biasc), ls(sink), _const_spec(tri), _const_spec(lvl)]
    args = [x, mods, mods, gmix, gmlp, gfin, anorm, win, wa2, ba2, wout, wup, wdown, biasb, biasc, sink, tri, lvl]

    y_spec = pl.BlockSpec((tq, D_MODEL), lambda i: (prv(i), 0))
    if prompt:
        assert tq >= B_WINDOW and (tq % C_BAND == 0 or C_BAND % tq == 0)
        rb = min(tq, C_BAND)
        first = nt - C_BAND // rb
        out_shape = [jax.ShapeDtypeStruct((rows, D_MODEL), F32),
                     jax.ShapeDtypeStruct((D_AQK, A_DV), F32),
                     jax.ShapeDtypeStruct((B_WINDOW, D_BKV), F32),
                     jax.ShapeDtypeStruct((B_WINDOW, D_BKV), F32),
                     jax.ShapeDtypeStruct((C_BAND, D_C), F32),
                     jax.ShapeDtypeStruct((C_BAND, D_C), F32)]
        band = pl.BlockSpec((rb, D_C), lambda i: (jnp.maximum(cur(i) - first, 0), 0))
        out_specs = [y_spec,
                     pl.BlockSpec((D_AQK, A_DV), lambda i: (0, 0)),
                     pl.BlockSpec((B_WINDOW, D_BKV), lambda i: (0, 0)),
                     pl.BlockSpec((B_WINDOW, D_BKV), lambda i: (0, 0)),
                     band, band]
        kbw_rows, kcw_rows = B_WINDOW + tq, C_BAND + tq
    else:
        assert nb == nch
        nseq = rows // CHUNK
        st, cbk, cbv, cck, ccv = caches
        seq_in = lambda r, w: pl.BlockSpec((None, nb, r, w), lambda i, _l=layer: (_l, cur(i), 0, 0))
        seq_out = lambda r, w: pl.BlockSpec((nb, r, w), lambda i: (cur(i), 0, 0))
        in_specs += [seq_in(D_AQK, A_DV), seq_in(D_BKV, B_WINDOW), seq_in(D_BKV, B_WINDOW),
                     seq_in(D_C, C_BAND), seq_in(D_C, C_BAND)]
        args += [st, cbk, cbv, cck, ccv]
        out_shape = [jax.ShapeDtypeStruct((rows, D_MODEL), F32),
                     jax.ShapeDtypeStruct((nseq, D_AQK, A_DV), F32),
                     jax.ShapeDtypeStruct((nseq, D_BKV, B_WINDOW), F32),
                     jax.ShapeDtypeStruct((nseq, D_BKV, B_WINDOW), F32),
                     jax.ShapeDtypeStruct((nseq, D_C, C_BAND), F32),
                     jax.ShapeDtypeStruct((nseq, D_C, C_BAND), F32)]
        out_specs = [y_spec, seq_out(D_AQK, A_DV), seq_out(D_BKV, B_WINDOW), seq_out(D_BKV, B_WINDOW),
                     seq_out(D_C, C_BAND), seq_out(D_C, C_BAND)]
        kbw_rows, kcw_rows = SUBLANES * 2, SUBLANES * 2

    scratch = [pltpu.VMEM((tq, D_MODEL), BF16),
               pltpu.VMEM((tq, D_MODEL), BF16),
               pltpu.VMEM((tq, OFF_RA), F32),
               pltpu.VMEM((tq, D_AQK), F32),
               pltpu.VMEM((tq, D_AQK), F32),
               pltpu.VMEM((tq, D_MIX), BF16),
               pltpu.VMEM((tq, D_MODEL), F32),
               pltpu.VMEM((tq, D_MODEL), F32),
               pltpu.VMEM((D_AQK, A_DV), F32),
               pltpu.VMEM((kbw_rows, D_BKV), BF16), pltpu.VMEM((kbw_rows, D_BKV), BF16),
               pltpu.VMEM((kcw_rows, D_C), BF16), pltpu.VMEM((kcw_rows, D_C), BF16)]

    kern = functools.partial(_layer_kernel, prompt=prompt, final=final, nch=nch, tq=tq, nt=nt)
    return pl.pallas_call(
        kern,
        grid=(nt + 1,),
        in_specs=in_specs,
        out_specs=out_specs,
        out_shape=out_shape,
        scratch_shapes=scratch,
        compiler_params=pltpu.CompilerParams(
            dimension_semantics=("arbitrary",),
            vmem_limit_bytes=VMEM_LIMIT_BYTES if prompt else VMEM_LIMIT_SAMPLE_BYTES),
        name=("layer_prompt" if prompt else "layer_sample") + ("_final" if final else ""),
    )(*args)


ADA_BLOCK = 1536


def _ada_kernel(c_ref, w_ref, b_ref, o_ref):
    c = c_ref[...]
    sc = c * jax.nn.sigmoid(c)
    sc_hi = sc.astype(BF16)
    sc_lo = (sc - sc_hi.astype(F32)).astype(BF16)
    w = w_ref[0]
    w_hi = w.astype(BF16)
    w_lo = (w - w_hi.astype(F32)).astype(BF16)
    o_ref[0] = _dot(sc_hi, w_hi) + _dot(sc_lo, w_hi) + _dot(sc_hi, w_lo) + b_ref[0]


def _ada(c_all, w_ada, b_ada):
    rows = c_all.shape[0]
    n = 6 * D_MODEL
    return pl.pallas_call(
        _ada_kernel,
        grid=(DEPTH, n // ADA_BLOCK),
        in_specs=[pl.BlockSpec((rows, D_MODEL), lambda l, j: (0, 0)),
                  pl.BlockSpec((1, D_MODEL, ADA_BLOCK), lambda l, j: (l, 0, j)),
                  pl.BlockSpec((1, 1, ADA_BLOCK), lambda l, j: (l, 0, j))],
        out_specs=pl.BlockSpec((1, rows, ADA_BLOCK), lambda l, j: (l, 0, j)),
        out_shape=jax.ShapeDtypeStruct((DEPTH, rows, n), F32),
        compiler_params=pltpu.CompilerParams(dimension_semantics=("arbitrary", "arbitrary"),
                                             vmem_limit_bytes=VMEM_LIMIT_BYTES),
        name="adaln",
    )(c_all, w_ada, b_ada.reshape(DEPTH, 1, n))


TB_W = 2 * LANES
TC_W = 5 * LANES
TC_FLAT = TC_W - (CHUNK + C_CLIP + 1)


def _shear(f, width):
    tiled = jnp.broadcast_to(f, (CHUNK, width))
    return pltpu.roll(tiled, width - (CHUNK - 1), 1, stride=1, stride_axis=0)


def _bias_kernel(t5_ref, crel_ref, bucket_ref, ob_ref, oc_ref):
    bucket = bucket_ref[...]
    for h in range(B_HEADS):
        def b_body(i, acc, h=h):
            return jnp.where(bucket == i, t5_ref[h, i], acc)
        f = lax.fori_loop(0, T5_BUCKETS, b_body, jnp.zeros(bucket.shape, F32))
        ob_ref[h] = _shear(f, TB_W)[:, 0:LK_B]
    lane = lax.broadcasted_iota(jnp.int32, (1, TC_W), 1)
    for l in range(DEPTH):
        for h in range(C_HEADS):
            row = crel_ref[l, h:h + 1, :]
            f = jnp.where(lane < TC_FLAT, row[:, 0:1], pltpu.roll(row, TC_FLAT, 1))
            oc_ref[l, h] = _shear(f, TC_W)[:, 0:LK_C]


def _bias_tables(t5_bias, c_rel_bias, bucket):
    smem = pl.BlockSpec(memory_space=pltpu.SMEM)
    vmem = pl.BlockSpec(memory_space=pltpu.VMEM)
    crel = jnp.pad(jnp.swapaxes(c_rel_bias, 1, 2),
                   ((0, 0), (0, SUBLANES - C_HEADS), (0, TC_W - (2 * C_CLIP + 1))))
    return pl.pallas_call(
        _bias_kernel,
        in_specs=[smem, vmem, vmem],
        out_specs=[vmem, vmem],
        out_shape=[jax.ShapeDtypeStruct((B_HEADS, CHUNK, LK_B), F32),
                   jax.ShapeDtypeStruct((DEPTH, C_HEADS, CHUNK, LK_C), F32)],
        name="bias_tables",
    )(t5_bias.T, crel, bucket)


def _t5_bucket(rel):
    half = T5_BUCKETS // 2
    max_exact = half // 2
    steps = half - max_exact
    thresholds = [int(np.ceil(max_exact * (T5_MAX_DIST / max_exact) ** (k / steps) - 1e-9)) for k in range(1, steps)]
    n = jnp.abs(rel)
    large = jnp.minimum(max_exact + sum((n >= t).astype(jnp.int32) for t in thresholds), half - 1)
    return jnp.where(rel > 0, half, 0) + jnp.where(n < max_exact, n, large)


def _gla_constants(tq):
    r = np.arange(tq)
    tri = ((r[:, None] // CHUNK == r[None, :] // CHUNK) & (r[None, :] <= r[:, None])).astype(np.float32)
    t = np.arange(CHUNK)[:, None]
    s = np.arange(CHUNK)[None, :]
    lvl = np.full((CHUNK, CHUNK), -1, np.int32)
    lvl[t == s] = 0
    for li, m in enumerate(GLA_LEVELS):
        sel = ((t // m) % 2 == 1) & ((s // m) == (t // m) - 1)
        lvl[np.broadcast_to(sel, lvl.shape)] = li + 1
    return jnp.asarray(tri, BF16), jnp.asarray(np.tile(lvl, (A_HEADS, 1)))


TQ_PROMPT = 256
NB_SAMPLE = 4


def kernel(x_prompt, x_sample, c_prompt, c_sample, state_gla, cache_b_k, cache_b_v, cache_c_k, cache_c_v,
           w_ada, b_ada, norm_mix_g, norm_mlp_g, w_in, w_a2, b_a2, a_norm_g, b_sink, t5_bias, c_rel_bias,
           w_out, w_up, w_down, final_norm_g):
    bsz, seq, _ = x_prompt.shape
    dec_b, dec_s, _ = x_sample.shape
    assert bsz == 1 and dec_s == CHUNK and TQ_PROMPT == NB_SAMPLE * CHUNK

    n_c = bsz + dec_b
    c_rows = -(-n_c // SUBLANES) * SUBLANES
    c_all = jnp.concatenate([c_prompt, c_sample, jnp.zeros((c_rows - n_c, D_MODEL), F32)], axis=0)
    mods = _ada(c_all, w_ada, b_ada).reshape(DEPTH, c_rows, 6, D_MODEL)
    mods = jnp.pad(mods, ((0, 0), (0, 0), (0, SUBLANES - 6), (0, 0)))

    rel_b = jnp.arange(TB_W) - (CHUNK - 1) - B_WINDOW
    bias_b, bias_c = _bias_tables(t5_bias, c_rel_bias, _t5_bucket(rel_b).astype(jnp.int32)[None, :])
    bias_b = bias_b.reshape(STACK, LK_B)
    bias_c = bias_c.reshape(DEPTH, STACK, LK_C)

    tri, lvl = _gla_constants(TQ_PROMPT)
    ra0 = 2 * D_AQK + 2 * D_AV
    qb0 = ra0 + A_GATE_RANK
    kb0 = qb0 + B_HEADS * HEAD_DIM
    d_in = kb0 + 2 * D_BKV + 3 * D_C
    head = lambda h: slice(qb0 + h * HEAD_DIM, qb0 + (h + 1) * HEAD_DIM)
    ob0 = D_AV
    orow = lambda h: slice(ob0 + h * HEAD_DIM, ob0 + (h + 1) * HEAD_DIM)

    win = jnp.concatenate([w_in[:, :, 0:ra0], w_in[:, :, head(0)], w_in[:, :, head(2)], w_in[:, :, head(1)],
                           w_in[:, :, head(3)], w_in[:, :, kb0:d_in], w_in[:, :, ra0:qb0],
                           jnp.zeros((DEPTH, D_MODEL, LANES - A_GATE_RANK), F32)], axis=2).astype(BF16)
    wa2 = jnp.concatenate([w_a2, jnp.zeros((DEPTH, LANES - A_GATE_RANK, D_AQK), F32)], axis=1).astype(BF16)
    wout = jnp.concatenate([w_out[:, 0:D_AV], w_out[:, orow(0)], w_out[:, orow(2)], w_out[:, orow(1)],
                            w_out[:, orow(3)], w_out[:, D_AV + B_HEADS * HEAD_DIM:]], axis=1).astype(BF16)
    wup = jnp.transpose(w_up.astype(BF16).reshape(DEPTH, D_MODEL, N_FF, FF_BLOCK), (0, 2, 1, 3))
    wdown = w_down.astype(BF16).reshape(DEPTH, N_FF, FF_BLOCK, D_MODEL)
    sink = jnp.repeat(b_sink, CHUNK, axis=1)[:, :, None]
    stacked = (norm_mix_g[:, None, :], norm_mlp_g[:, None, :], a_norm_g[:, None, :], win, wa2,
               b_a2[:, None, :], wout, wup, wdown, bias_c, sink)
    shared = (final_norm_g[None, :], bias_b, tri, lvl)

    x_p = x_prompt.reshape(seq, D_MODEL)
    x_s = x_sample.reshape(dec_b * dec_s, D_MODEL)
    outs_p, outs_s = [], []
    fmajor = lambda cache: jnp.transpose(cache, (0, 1, 3, 4, 2)).reshape(
        DEPTH, dec_b, cache.shape[3] * cache.shape[4], cache.shape[2])
    caches = (state_gla.reshape(DEPTH, dec_b, D_AQK, A_DV), fmajor(cache_b_k), fmajor(cache_b_v),
              fmajor(cache_c_k), fmajor(cache_c_v))
    for l in range(DEPTH):
        final = l == DEPTH - 1
        res_p = _run_layer(x_p, mods[l, 0:bsz], l, stacked, shared, None, prompt=True, final=final,
                           tq=TQ_PROMPT, nb=None)
        x_p = res_p[0]
        outs_p.append(res_p[1:])

        res_s = _run_layer(x_s, mods[l, bsz:bsz + dec_b], l, stacked, shared, caches, prompt=False,
                           final=final, tq=NB_SAMPLE * CHUNK, nb=NB_SAMPLE)
        x_s = res_s[0]
        outs_s.append(res_s[1:])

    sg_p = jnp.stack([o[0].reshape(bsz, A_HEADS, A_DK, A_DV) for o in outs_p])
    kb_p = jnp.stack([o[1].reshape(bsz, B_WINDOW, B_KV_HEADS, HEAD_DIM) for o in outs_p])
    vb_p = jnp.stack([o[2].reshape(bsz, B_WINDOW, B_KV_HEADS, HEAD_DIM) for o in outs_p])
    kc_p = jnp.stack([o[3].reshape(bsz, C_BAND, C_HEADS, HEAD_DIM) for o in outs_p])
    vc_p = jnp.stack([o[4].reshape(bsz, C_BAND, C_HEADS, HEAD_DIM) for o in outs_p])
    sg_s = jnp.stack([o[0].reshape(dec_b, A_HEADS, A_DK, A_DV) for o in outs_s])
    tmajor = lambda arrs, heads: jnp.transpose(
        jnp.stack(arrs).reshape(DEPTH, dec_b, heads, HEAD_DIM, -1), (0, 1, 4, 2, 3))
    kb_s = tmajor([o[1] for o in outs_s], B_KV_HEADS)
    vb_s = tmajor([o[2] for o in outs_s], B_KV_HEADS)
    kc_s = tmajor([o[3] for o in outs_s], C_HEADS)
    vc_s = tmajor([o[4] for o in outs_s], C_HEADS)
    return (x_p.reshape(bsz, seq, D_MODEL), x_s.reshape(dec_b, dec_s, D_MODEL),
            sg_p, kb_p, vb_p, kc_p, vc_p, sg_s, kb_s, vb_s, kc_s, vc_s)
```

```python
import functools

import numpy as np
import jax
import jax.numpy as jnp
from jax import lax
from jax.experimental import pallas as pl
from jax.experimental.pallas import tpu as pltpu

D_MODEL = 1024
DEPTH = 2
CHUNK = 64
HEAD_DIM = 64
A_HEADS = 4
A_DK = 64
A_DV = 128
A_GATE_RANK = 16
A_GATE_TAU = 16.0
B_HEADS = 4
B_KV_HEADS = 2
B_WINDOW = 128
C_HEADS = 4
C_BAND = 512
C_CLIP = 256
T5_BUCKETS = 32
T5_MAX_DIST = 128
D_FF = 4 * D_MODEL
NORM_EPS = 1e-6
NEG_INF = -1e30

LANES = 128
SUBLANES = 8
VMEM_LIMIT_BYTES = 60000 * 1024
VMEM_LIMIT_SAMPLE_BYTES = 62 * 1024 * 1024

OFF_QA = 0
OFF_KA = OFF_QA + A_HEADS * A_DK
OFF_VA = OFF_KA + A_HEADS * A_DK
OFF_GA = OFF_VA + A_HEADS * A_DV
OFF_QB = OFF_GA + A_HEADS * A_DV
OFF_KB = OFF_QB + B_HEADS * HEAD_DIM
OFF_VB = OFF_KB + B_KV_HEADS * HEAD_DIM
OFF_QC = OFF_VB + B_KV_HEADS * HEAD_DIM
OFF_KC = OFF_QC + C_HEADS * HEAD_DIM
OFF_VC = OFF_KC + C_HEADS * HEAD_DIM
OFF_RA = OFF_VC + C_HEADS * HEAD_DIM
P_IN = OFF_RA + LANES
D_AQK = A_HEADS * A_DK
D_AV = A_HEADS * A_DV
D_BKV = B_KV_HEADS * HEAD_DIM
D_C = C_HEADS * HEAD_DIM
D_MIX = D_AV + B_HEADS * HEAD_DIM + D_C
LK_B = B_WINDOW + CHUNK
LK_C = C_BAND + CHUNK
STACK = 4 * CHUNK
GLA_LEVELS = (1, 2, 4, 8, 16, 32)
FF_BLOCK = 1024
N_FF = D_FF // FF_BLOCK

BF16 = jnp.bfloat16
F32 = jnp.float32


def _dot(a, b):
    return jnp.dot(a, b, preferred_element_type=F32)


def _dot_nt(a, b):
    return lax.dot_general(a, b, (((1,), (1,)), ((), ())), preferred_element_type=F32)


def _dot_tn(a, b):
    return lax.dot_general(a, b, (((0,), (0,)), ((), ())), preferred_element_type=F32)


def _rms(x, g):
    return x * lax.rsqrt(jnp.mean(x * x, axis=-1, keepdims=True) + NORM_EPS) * g


def _split3(x):
    hi = x.astype(BF16)
    r1 = x - hi.astype(F32)
    mid = r1.astype(BF16)
    lo = (r1 - mid.astype(F32)).astype(BF16)
    return hi, mid, lo


def _head_masks(width, per_head):
    lane = lax.broadcasted_iota(jnp.int32, (1, width), 1)
    return [(lane >= h * per_head) & (lane < (h + 1) * per_head) for h in range(width // per_head)]


def _stack_masked(x, masks):
    return jnp.concatenate([jnp.where(m, x, 0.0) for m in masks], axis=0)


def _tile_rows(x, n):
    return jnp.concatenate([x] * n, axis=0)


def _scores(q_st, segments, bias_st, maskrow):
    scores, off = [], 0
    for k, v, transposed in segments:
        n = k.shape[1] if transposed else k.shape[0]
        s = (_dot(q_st, k) if transposed else _dot_nt(q_st, k)) + bias_st[:, off:off + n]
        if maskrow is not None:
            s = s + maskrow[:, off:off + n]
        scores.append(s)
        off += n
    return scores


def _softmax_pv(scores, segments, sink):
    m = functools.reduce(jnp.maximum, [jnp.max(s, axis=1, keepdims=True) for s in scores])
    if sink is not None:
        m = jnp.maximum(m, sink)
    es = [jnp.exp(s - m) for s in scores]
    l = functools.reduce(jnp.add, [jnp.sum(e, axis=1, keepdims=True) for e in es])
    if sink is not None:
        l = l + jnp.exp(sink - m)
    o = None
    for e, (k, v, transposed) in zip(es, segments):
        part = _dot_nt(e.astype(BF16), v) if transposed else _dot(e.astype(BF16), v)
        o = part if o is None else o + part
    return o / l


def _level_exponent(b, m, t):
    bcast = lambda i: jnp.broadcast_to(b[i:i + 1], (SUBLANES, b.shape[1]))
    low_half = lax.broadcasted_iota(jnp.int32, (SUBLANES, b.shape[1]), 0) < SUBLANES // 2
    parts = []
    for g in range(CHUNK // SUBLANES):
        r = g * SUBLANES
        rows = b[r:r + SUBLANES]
        if m >= SUBLANES:
            ref = b[(r // (2 * m)) * 2 * m + m - 1:(r // (2 * m)) * 2 * m + m]
            parts.append(rows - ref if (r // m) % 2 == 1 else ref - rows)
        elif m == 4:
            parts.append(rows - bcast(r + 3))
        else:
            parts.append(rows - jnp.where(low_half, bcast(r + 1), bcast(r + 5)))
    d = jnp.concatenate(parts, axis=0)
    return d if m >= SUBLANES else jnp.where((t & m) != 0, d, -d)


def _odd_block_rows(x, m, groups):
    return jnp.concatenate([x[g * CHUNK + r:g * CHUNK + r + SUBLANES] for g in range(groups)
                            for r in range(0, CHUNK, SUBLANES) if (r // m) % 2 == 1], axis=0)


def _spread_odd_block_rows(y, m, groups):
    zero = jnp.zeros((SUBLANES, y.shape[1]), y.dtype)
    parts, i = [], 0
    for g in range(groups):
        for r in range(0, CHUNK, SUBLANES):
            if (r // m) % 2 == 1:
                parts.append(y[i:i + SUBLANES])
                i += SUBLANES
            else:
                parts.append(zero)
    return jnp.concatenate(parts, axis=0)


def _gla_levels(q, k, la, b, hm_a):
    t = lax.broadcasted_iota(jnp.int32, (CHUNK, D_AQK), 0)
    q_st = _stack_masked(q, hm_a)
    k16 = k.astype(BF16)
    prods = [_dot_nt(q_st.astype(BF16), k16)]
    for m in GLA_LEVELS:
        e = jnp.exp(la if m == 1 else _level_exponent(b, m, t))
        kt = k16 if m == 1 else (k * e).astype(BF16)
        if m >= SUBLANES:
            qt = _odd_block_rows(q_st, m, A_HEADS) * _tile_rows(_odd_block_rows(e, m, 1), A_HEADS)
            prods.append(_spread_odd_block_rows(_dot_nt(qt.astype(BF16), kt), m, A_HEADS))
        else:
            qt = (q_st * _tile_rows(e, A_HEADS)).astype(BF16)
            prods.append(_dot_nt(qt, kt))
    return q_st, prods


def _gla_finish(q_st, prods, k, v, b, s, lvl, hm_a):
    attn = jnp.where(lvl == 0, prods[0], 0.0)
    for li in range(1, len(prods)):
        attn = jnp.where(lvl == li, prods[li], attn)
    qhat = (q_st * _tile_rows(jnp.exp(b), A_HEADS)).astype(BF16)
    o_inter = _dot(qhat, s.astype(BF16))
    v16 = v.astype(BF16)
    o_intra = jnp.concatenate(
        [_dot(attn[h * CHUNK:(h + 1) * CHUNK].astype(BF16), v16[:, h * A_DV:(h + 1) * A_DV])
         for h in range(A_HEADS)], axis=0)
    b_last = b[CHUNK - 1:CHUNK]
    khat = _stack_masked(k * jnp.exp(b_last - b), hm_a).astype(BF16)
    v_st = jnp.concatenate([v16[:, h * A_DV:(h + 1) * A_DV] for h in range(A_HEADS)], axis=0)
    decay = jnp.broadcast_to(jnp.exp(b_last), (A_DV, D_AQK)).T
    s_new = s * decay + _dot_tn(khat, v_st)
    return o_intra + o_inter, s_new


def _layer_kernel(*refs, prompt, final, nch, tq, nt):
    (x_ref, mods_ref, modp_ref, gmix_ref, gmlp_ref, gfin_ref, anorm_ref, win_ref, wa2_ref, ba2_ref, wout_ref,
     wup_ref, wdown_ref, biasb_ref, biasc_ref, sink_ref, tri_ref, lvl_ref) = refs[:18]
    pos = 18
    if not prompt:
        st_ref, cbk_ref, cbv_ref, cck_ref, ccv_ref = refs[pos:pos + 5]
        pos += 5
    y_ref, so_ref, kbo_ref, vbo_ref, kco_ref, vco_ref = refs[pos:pos + 6]
    pos += 6
    (h_scr, h2_scr, proj_scr, la_scr, b_scr, mix_scr, x1_scr, acc_scr, st_scr,
     kbw, vbw, kcw, vcw) = refs[pos:]

    step = pl.program_id(0)

    def mod(ref, c, j):
        return ref[0 if prompt else c, j:j + 1, :]

    @pl.when(step == 0)
    def _():
        h2_scr[...] = jnp.zeros_like(h2_scr)
        x1_scr[...] = jnp.zeros_like(x1_scr)
        acc_scr[...] = jnp.zeros_like(acc_scr)
        if prompt:
            st_scr[...] = jnp.zeros_like(st_scr)
            kbw[...] = jnp.zeros_like(kbw)
            vbw[...] = jnp.zeros_like(vbw)
            kcw[...] = jnp.zeros_like(kcw)
            vcw[...] = jnp.zeros_like(vcw)

    def finish_previous_tile():
        gfin = gfin_ref[...]
        for c in range(nch):
            r = slice(c * CHUNK, (c + 1) * CHUNK)
            x2 = x1_scr[r, :] + mod(modp_ref, c, 5) * acc_scr[r, :]
            y_ref[r, :] = _rms(x2, gfin) if final else x2
        acc_scr[...] = jnp.zeros_like(acc_scr)

    @pl.when(step < nt)
    def _():
        gmix = gmix_ref[...]
        for c in range(nch):
            r = slice(c * CHUNK, (c + 1) * CHUNK)
            h = _rms(x_ref[r, :], gmix) * (1.0 + mod(mods_ref, c, 1)) + mod(mods_ref, c, 0)
            h_scr[r, :] = h.astype(BF16)
        hmix = h_scr[...]
        cut1, cut2 = OFF_GA, OFF_QC
        ra = _dot(hmix, win_ref[:, OFF_RA:OFF_RA + LANES])
        proj_scr[:, 0:cut1] = _dot(hmix, win_ref[:, 0:cut1])
        z = _dot(ra.astype(BF16), wa2_ref[...]) + ba2_ref[...]
        la = (jnp.minimum(z, 0.0) - jnp.log1p(jnp.exp(-jnp.abs(z)))) * (1.0 / A_GATE_TAU)
        la_scr[...] = la
        proj_scr[:, cut1:cut2] = _dot(hmix, win_ref[:, cut1:cut2])
        tri = tri_ref[...]
        hi, mid, lo = _split3(la)
        b_scr[...] = _dot(tri, hi) + _dot(tri, mid) + _dot(tri, lo)
        proj_scr[:, cut2:OFF_RA] = _dot(hmix, win_ref[:, cut2:OFF_RA])

        if prompt:
            kbw[B_WINDOW:B_WINDOW + tq, :] = proj_scr[:, OFF_KB:OFF_KB + D_BKV].astype(BF16)
            vbw[B_WINDOW:B_WINDOW + tq, :] = proj_scr[:, OFF_VB:OFF_VB + D_BKV].astype(BF16)
            kcw[C_BAND:C_BAND + tq, :] = proj_scr[:, OFF_KC:OFF_KC + D_C].astype(BF16)
            vcw[C_BAND:C_BAND + tq, :] = proj_scr[:, OFF_VC:OFF_VC + D_C].astype(BF16)

        hm_a = _head_masks(D_AQK, A_DK)
        hm_c = _head_masks(D_C, HEAD_DIM)
        hm_b = _head_masks(D_BKV, HEAD_DIM)
        anorm = anorm_ref[...]

        def chunk_body(c, carry):
            r0 = pl.multiple_of(c * CHUNK, CHUNK)
            rows = pl.ds(r0, CHUNK)

            if prompt:
                seg_b = [(kbw[pl.ds(r0, LK_B), :], vbw[pl.ds(r0, LK_B), :], False)]
                seg_c = [(kcw[pl.ds(r0, LK_C), :], vcw[pl.ds(r0, LK_C), :], False)]
                gidx = step * nch + c
                lane_b = lax.broadcasted_iota(jnp.int32, (1, LK_B), 1)
                lane_c = lax.broadcasted_iota(jnp.int32, (1, LK_C), 1)
                mask_b = jnp.where(lane_b >= B_WINDOW - gidx * CHUNK, 0.0, NEG_INF)
                mask_c = jnp.where(lane_c >= C_BAND - gidx * CHUNK, 0.0, NEG_INF)
            else:
                kb_new = proj_scr[rows, OFF_KB:OFF_KB + D_BKV]
                vb_new = proj_scr[rows, OFF_VB:OFF_VB + D_BKV]
                kc_new = proj_scr[rows, OFF_KC:OFF_KC + D_C]
                vc_new = proj_scr[rows, OFF_VC:OFF_VC + D_C]
                seg_b = [(cbk_ref[c].astype(BF16), cbv_ref[c].astype(BF16), True),
                         (kb_new.astype(BF16), vb_new.astype(BF16), False)]
                seg_c = [(cck_ref[c].astype(BF16), ccv_ref[c].astype(BF16), True),
                         (kc_new.astype(BF16), vc_new.astype(BF16), False)]
                mask_b = mask_c = None

            q = proj_scr[rows, OFF_QA:OFF_QA + D_AQK] * (A_DK ** -0.5)
            k = proj_scr[rows, OFF_KA:OFF_KA + D_AQK]
            b = b_scr[rows, :]
            q_st, prods = _gla_levels(q, k, la_scr[rows, :], b, hm_a)
            qc = proj_scr[rows, OFF_QC:OFF_QC + D_C] * (HEAD_DIM ** -0.5)
            sc_c = _scores(_stack_masked(qc, hm_c).astype(BF16), seg_c, biasc_ref[...], mask_c)
            qb = proj_scr[rows, OFF_QB:OFF_QB + B_HEADS * HEAD_DIM] * (HEAD_DIM ** -0.5)
            qb_a, qb_b = qb[:, 0:D_BKV], qb[:, D_BKV:2 * D_BKV]
            qb_st = jnp.concatenate([jnp.where(hm_b[0], qb_a, 0.0), jnp.where(hm_b[0], qb_b, 0.0),
                                     jnp.where(hm_b[1], qb_a, 0.0), jnp.where(hm_b[1], qb_b, 0.0)], axis=0)
            sc_b = _scores(qb_st.astype(BF16), seg_b, biasb_ref[...], mask_b)

            up = _dot(h2_scr[...], wup_ref[c])
            act = jnp.square(jnp.maximum(up, 0.0)).astype(BF16)

            v = proj_scr[rows, OFF_VA:OFF_VA + D_AV]
            s_old = st_scr[...] if prompt else st_ref[c]
            o_st, s_new = _gla_finish(q_st, prods, k, v, b, s_old, lvl_ref[...], hm_a)
            if prompt:
                st_scr[...] = s_new
            else:
                so_ref[c] = s_new
            g = proj_scr[rows, OFF_GA:OFF_GA + D_AV]
            for h in range(A_HEADS):
                on = _rms(o_st[h * CHUNK:(h + 1) * CHUNK], anorm)
                gh = g[:, h * A_DV:(h + 1) * A_DV]
                mix_scr[rows, h * A_DV:(h + 1) * A_DV] = (on * (gh * jax.nn.sigmoid(gh))).astype(BF16)

            oc_st = _softmax_pv(sc_c, seg_c, None)
            oc = jnp.where(hm_c[0], oc_st[0:CHUNK], 0.0)
            for h in range(1, C_HEADS):
                oc = jnp.where(hm_c[h], oc_st[h * CHUNK:(h + 1) * CHUNK], oc)
            mix_scr[rows, D_AV + 2 * D_BKV:D_MIX] = oc.astype(BF16)

            ob_st = _softmax_pv(sc_b, seg_b, sink_ref[...])
            ob_a = jnp.where(hm_b[0], ob_st[0:CHUNK], ob_st[2 * CHUNK:3 * CHUNK])
            ob_b = jnp.where(hm_b[0], ob_st[CHUNK:2 * CHUNK], ob_st[3 * CHUNK:4 * CHUNK])
            mix_scr[rows, D_AV:D_AV + D_BKV] = ob_a.astype(BF16)
            mix_scr[rows, D_AV + D_BKV:D_AV + 2 * D_BKV] = ob_b.astype(BF16)

            acc_scr[...] += _dot(act, wdown_ref[c])

            if not prompt:
                for dst, src, new, width in ((kbo_ref, cbk_ref, kb_new, B_WINDOW), (vbo_ref, cbv_ref, vb_new, B_WINDOW),
                                             (kco_ref, cck_ref, kc_new, C_BAND), (vco_ref, ccv_ref, vc_new, C_BAND)):
                    dst[c, :, 0:width - CHUNK] = src[c, :, CHUNK:width]
                    dst[c, :, width - CHUNK:width] = new.T
            return carry

        lax.fori_loop(0, nch, chunk_body, 0)

        if prompt:
            kbw[0:B_WINDOW, :] = kbw[tq:tq + B_WINDOW, :]
            vbw[0:B_WINDOW, :] = vbw[tq:tq + B_WINDOW, :]
            kcw[0:C_BAND, :] = kcw[tq:tq + C_BAND, :]
            vcw[0:C_BAND, :] = vcw[tq:tq + C_BAND, :]

            so_ref[...] = st_scr[...]
            kbo_ref[...] = proj_scr[tq - B_WINDOW:tq, OFF_KB:OFF_KB + D_BKV]
            vbo_ref[...] = proj_scr[tq - B_WINDOW:tq, OFF_VB:OFF_VB + D_BKV]
            rb = kco_ref.shape[0]
            kco_ref[...] = proj_scr[tq - rb:tq, OFF_KC:OFF_KC + D_C]
            vco_ref[...] = proj_scr[tq - rb:tq, OFF_VC:OFF_VC + D_C]

        mixed = _dot(mix_scr[...], wout_ref[...])

        finish_previous_tile()

        gmlp = gmlp_ref[...]
        for c in range(nch):
            r = slice(c * CHUNK, (c + 1) * CHUNK)
            x1 = x_ref[r, :] + mod(mods_ref, c, 2) * mixed[r, :]
            x1_scr[r, :] = x1
            h2 = _rms(x1, gmlp) * (1.0 + mod(mods_ref, c, 4)) + mod(mods_ref, c, 3)
            h2_scr[r, :] = h2.astype(BF16)

    @pl.when(step == nt)
    def _():
        def mlp_block(c, carry):
            up = _dot(h2_scr[...], wup_ref[c])
            act = jnp.square(jnp.maximum(up, 0.0)).astype(BF16)
            acc_scr[...] += _dot(act, wdown_ref[c])
            return carry

        lax.fori_loop(0, nch, mlp_block, 0)
        finish_previous_tile()


def _layer_spec(arr, layer):
    nd = arr.ndim - 1
    return pl.BlockSpec((None,) + arr.shape[1:], lambda i, _l=layer, _nd=nd: (_l,) + (0,) * _nd,
                        pipeline_mode=pl.Buffered(1))


def _const_spec(arr):
    nd = arr.ndim
    return pl.BlockSpec(arr.shape, lambda i, _nd=nd: (0,) * _nd, pipeline_mode=pl.Buffered(1))


def _run_layer(x, mods, layer, stacked, shared, caches, *, prompt, final, tq, nb):
    rows = x.shape[0]
    nch = tq // CHUNK
    nt = rows // tq
    assert rows % tq == 0 and nch == N_FF
    (gmix, gmlp, anorm, win, wa2, ba2, wout, wup, wdown, biasc, sink) = stacked
    (gfin, biasb, tri, lvl) = shared
    cur = lambda i: jnp.minimum(i, nt - 1)
    prv = lambda i: jnp.maximum(i - 1, 0)

    nmod = 1 if prompt else nb
    in_specs = [pl.BlockSpec((tq, D_MODEL), lambda i: (cur(i), 0)),
                pl.BlockSpec((nmod, SUBLANES, D_MODEL), lambda i: (0 if prompt else cur(i), 0, 0)),
                pl.BlockSpec((nmod, SUBLANES, D_MODEL), lambda i: (0 if prompt else prv(i), 0, 0))]
    ls = functools.partial(_layer_spec, layer=layer)
    in_specs += [ls(gmix), ls(gmlp), _const_spec(gfin), ls(anorm), ls(win), ls(wa2), ls(ba2), ls(wout),
                 ls(wup), ls(wdown), _const_spec(biasb), ls(biasc), ls(sink), _const_spec(tri), _const_spec(lvl)]
    args = [x, mods, mods, gmix, gmlp, gfin, anorm, win, wa2, ba2, wout, wup, wdown, biasb, biasc, sink, tri, lvl]

    y_spec = pl.BlockSpec((tq, D_MODEL), lambda i: (prv(i), 0))
    if prompt:
        assert tq >= B_WINDOW and (tq % C_BAND == 0 or C_BAND % tq == 0)
        rb = min(tq, C_BAND)
        first = nt - C_BAND // rb
        out_shape = [jax.ShapeDtypeStruct((rows, D_MODEL), F32),
                     jax.ShapeDtypeStruct((D_AQK, A_DV), F32),
                     jax.ShapeDtypeStruct((B_WINDOW, D_BKV), F32),
                     jax.ShapeDtypeStruct((B_WINDOW, D_BKV), F32),
                     jax.ShapeDtypeStruct((C_BAND, D_C), F32),
                     jax.ShapeDtypeStruct((C_BAND, D_C), F32)]
        band = pl.BlockSpec((rb, D_C), lambda i: (jnp.maximum(cur(i) - first, 0), 0))
        out_specs = [y_spec,
                     pl.BlockSpec((D_AQK, A_DV), lambda i: (0, 0)),
                     pl.BlockSpec((B_WINDOW, D_BKV), lambda i: (0, 0)),
                     pl.BlockSpec((B_WINDOW, D_BKV), lambda i: (0, 0)),
                     band, band]
        kbw_rows, kcw_rows = B_WINDOW + tq, C_BAND + tq
    else:
        assert nb == nch
        nseq = rows // CHUNK
        st, cbk, cbv, cck, ccv = caches
        seq_in = lambda r, w: pl.BlockSpec((None, nb, r, w), lambda i, _l=layer: (_l, cur(i), 0, 0))
        seq_out = lambda r, w: pl.BlockSpec((nb, r, w), lambda i: (cur(i), 0, 0))
        in_specs += [seq_in(D_AQK, A_DV), seq_in(D_BKV, B_WINDOW), seq_in(D_BKV, B_WINDOW),
                     seq_in(D_C, C_BAND), seq_in(D_C, C_BAND)]
        args += [st, cbk, cbv, cck, ccv]
        out_shape = [jax.ShapeDtypeStruct((rows, D_MODEL), F32),
                     jax.ShapeDtypeStruct((nseq, D_AQK, A_DV), F32),
                     jax.ShapeDtypeStruct((nseq, D_BKV, B_WINDOW), F32),
                     jax.ShapeDtypeStruct((nseq, D_BKV, B_WINDOW), F32),
                     jax.ShapeDtypeStruct((nseq, D_C, C_BAND), F32),
                     jax.ShapeDtypeStruct((nseq, D_C, C_BAND), F32)]
        out_specs = [y_spec, seq_out(D_AQK, A_DV), seq_out(D_BKV, B_WINDOW), seq_out(D_BKV, B_WINDOW),
                     seq_out(D_C, C_BAND), seq_out(D_C, C_BAND)]
        kbw_rows, kcw_rows = SUBLANES * 2, SUBLANES * 2

    scratch = [pltpu.VMEM((tq, D_MODEL), BF16),
               pltpu.VMEM((tq, D_MODEL), BF16),
               pltpu.VMEM((tq, OFF_RA), F32),
               pltpu.VMEM((tq, D_AQK), F32),
               pltpu.VMEM((tq, D_AQK), F32),
               pltpu.VMEM((tq, D_MIX), BF16),
               pltpu.VMEM((tq, D_MODEL), F32),
               pltpu.VMEM((tq, D_MODEL), F32),
               pltpu.VMEM((D_AQK, A_DV), F32),
               pltpu.VMEM((kbw_rows, D_BKV), BF16), pltpu.VMEM((kbw_rows, D_BKV), BF16),
               pltpu.VMEM((kcw_rows, D_C), BF16), pltpu.VMEM((kcw_rows, D_C), BF16)]

    kern = functools.partial(_layer_kernel, prompt=prompt, final=final, nch=nch, tq=tq, nt=nt)
    return pl.pallas_call(
        kern,
        grid=(nt + 1,),
        in_specs=in_specs,
        out_specs=out_specs,
        out_shape=out_shape,
        scratch_shapes=scratch,
        compiler_params=pltpu.CompilerParams(
            dimension_semantics=("arbitrary",),
            vmem_limit_bytes=VMEM_LIMIT_BYTES if prompt else VMEM_LIMIT_SAMPLE_BYTES),
        name=("layer_prompt" if prompt else "layer_sample") + ("_final" if final else ""),
    )(*args)


ADA_BLOCK = 1536


def _ada_kernel(c_ref, w_ref, b_ref, o_ref):
    c = c_ref[...]
    sc = c * jax.nn.sigmoid(c)
    sc_hi = sc.astype(BF16)
    sc_lo = (sc - sc_hi.astype(F32)).astype(BF16)
    w = w_ref[0]
    w_hi = w.astype(BF16)
    w_lo = (w - w_hi.astype(F32)).astype(BF16)
    o_ref[0] = _dot(sc_hi, w_hi) + _dot(sc_lo, w_hi) + _dot(sc_hi, w_lo) + b_ref[0]


def _ada(c_all, w_ada, b_ada):
    rows = c_all.shape[0]
    n = 6 * D_MODEL
    return pl.pallas_call(
        _ada_kernel,
        grid=(DEPTH, n // ADA_BLOCK),
        in_specs=[pl.BlockSpec((rows, D_MODEL), lambda l, j: (0, 0)),
                  pl.BlockSpec((1, D_MODEL, ADA_BLOCK), lambda l, j: (l, 0, j)),
                  pl.BlockSpec((1, 1, ADA_BLOCK), lambda l, j: (l, 0, j))],
        out_specs=pl.BlockSpec((1, rows, ADA_BLOCK), lambda l, j: (l, 0, j)),
        out_shape=jax.ShapeDtypeStruct((DEPTH, rows, n), F32),
        compiler_params=pltpu.CompilerParams(dimension_semantics=("arbitrary", "arbitrary"),
                                             vmem_limit_bytes=VMEM_LIMIT_BYTES),
        name="adaln",
    )(c_all, w_ada, b_ada.reshape(DEPTH, 1, n))


TB_W = 2 * LANES
TC_W = 5 * LANES
TC_FLAT = TC_W - (CHUNK + C_CLIP + 1)


def _shear(f, width):
    tiled = jnp.broadcast_to(f, (CHUNK, width))
    return pltpu.roll(tiled, width - (CHUNK - 1), 1, stride=1, stride_axis=0)


def _bias_kernel(t5_ref, crel_ref, bucket_ref, ob_ref, oc_ref):
    bucket = bucket_ref[...]
    for h in range(B_HEADS):
        def b_body(i, acc, h=h):
            return jnp.where(bucket == i, t5_ref[h, i], acc)
        f = lax.fori_loop(0, T5_BUCKETS, b_body, jnp.zeros(bucket.shape, F32))
        ob_ref[h] = _shear(f, TB_W)[:, 0:LK_B]
    lane = lax.broadcasted_iota(jnp.int32, (1, TC_W), 1)
    for l in range(DEPTH):
        for h in range(C_HEADS):
            row = crel_ref[l, h:h + 1, :]
            f = jnp.where(lane < TC_FLAT, row[:, 0:1], pltpu.roll(row, TC_FLAT, 1))
            oc_ref[l, h] = _shear(f, TC_W)[:, 0:LK_C]


def _bias_tables(t5_bias, c_rel_bias, bucket):
    smem = pl.BlockSpec(memory_space=pltpu.SMEM)
    vmem = pl.BlockSpec(memory_space=pltpu.VMEM)
    crel = jnp.pad(jnp.swapaxes(c_rel_bias, 1, 2),
                   ((0, 0), (0, SUBLANES - C_HEADS), (0, TC_W - (2 * C_CLIP + 1))))
    return pl.pallas_call(
        _bias_kernel,
        in_specs=[smem, vmem, vmem],
        out_specs=[vmem, vmem],
        out_shape=[jax.ShapeDtypeStruct((B_HEADS, CHUNK, LK_B), F32),
                   jax.ShapeDtypeStruct((DEPTH, C_HEADS, CHUNK, LK_C), F32)],
        name="bias_tables",
    )(t5_bias.T, crel, bucket)


def _t5_bucket(rel):
    half = T5_BUCKETS // 2
    max_exact = half // 2
    steps = half - max_exact
    thresholds = [int(np.ceil(max_exact * (T5_MAX_DIST / max_exact) ** (k / steps) - 1e-9)) for k in range(1, steps)]
    n = jnp.abs(rel)
    large = jnp.minimum(max_exact + sum((n >= t).astype(jnp.int32) for t in thresholds), half - 1)
    return jnp.where(rel > 0, half, 0) + jnp.where(n < max_exact, n, large)


def _gla_constants(tq):
    r = np.arange(tq)
    tri = ((r[:, None] // CHUNK == r[None, :] // CHUNK) & (r[None, :] <= r[:, None])).astype(np.float32)
    t = np.arange(CHUNK)[:, None]
    s = np.arange(CHUNK)[None, :]
    lvl = np.full((CHUNK, CHUNK), -1, np.int32)
    lvl[t == s] = 0
    for li, m in enumerate(GLA_LEVELS):
        sel = ((t // m) % 2 == 1) & ((s // m) == (t // m) - 1)
        lvl[np.broadcast_to(sel, lvl.shape)] = li + 1
    return jnp.asarray(tri, BF16), jnp.asarray(np.tile(lvl, (A_HEADS, 1)))


TQ_PROMPT = 256
NB_SAMPLE = 4


def kernel(x_prompt, x_sample, c_prompt, c_sample, state_gla, cache_b_k, cache_b_v, cache_c_k, cache_c_v,
           w_ada, b_ada, norm_mix_g, norm_mlp_g, w_in, w_a2, b_a2, a_norm_g, b_sink, t5_bias, c_rel_bias,
           w_out, w_up, w_down, final_norm_g):
    bsz, seq, _ = x_prompt.shape
    dec_b, dec_s, _ = x_sample.shape
    assert bsz == 1 and dec_s == CHUNK and TQ_PROMPT == NB_SAMPLE * CHUNK

    n_c = bsz + dec_b
    c_rows = -(-n_c // SUBLANES) * SUBLANES
    c_all = jnp.concatenate([c_prompt, c_sample, jnp.zeros((c_rows - n_c, D_MODEL), F32)], axis=0)
    mods = _ada(c_all, w_ada, b_ada).reshape(DEPTH, c_rows, 6, D_MODEL)
    mods = jnp.pad(mods, ((0, 0), (0, 0), (0, SUBLANES - 6), (0, 0)))

    rel_b = jnp.arange(TB_W) - (CHUNK - 1) - B_WINDOW
    bias_b, bias_c = _bias_tables(t5_bias, c_rel_bias, _t5_bucket(rel_b).astype(jnp.int32)[None, :])
    bias_b = bias_b.reshape(STACK, LK_B)
    bias_c = bias_c.reshape(DEPTH, STACK, LK_C)

    tri, lvl = _gla_constants(TQ_PROMPT)
    ra0 = 2 * D_AQK + 2 * D_AV
    qb0 = ra0 + A_GATE_RANK
    kb0 = qb0 + B_HEADS * HEAD_DIM
    d_in = kb0 + 2 * D_BKV + 3 * D_C
    head = lambda h: slice(qb0 + h * HEAD_DIM, qb0 + (h + 1) * HEAD_DIM)
    ob0 = D_AV
    orow = lambda h: slice(ob0 + h * HEAD_DIM, ob0 + (h + 1) * HEAD_DIM)

    win = jnp.concatenate([w_in[:, :, 0:ra0], w_in[:, :, head(0)], w_in[:, :, head(2)], w_in[:, :, head(1)],
                           w_in[:, :, head(3)], w_in[:, :, kb0:d_in], w_in[:, :, ra0:qb0],
                           jnp.zeros((DEPTH, D_MODEL, LANES - A_GATE_RANK), F32)], axis=2).astype(BF16)
    wa2 = jnp.concatenate([w_a2, jnp.zeros((DEPTH, LANES - A_GATE_RANK, D_AQK), F32)], axis=1).astype(BF16)
    wout = jnp.concatenate([w_out[:, 0:D_AV], w_out[:, orow(0)], w_out[:, orow(2)], w_out[:, orow(1)],
                            w_out[:, orow(3)], w_out[:, D_AV + B_HEADS * HEAD_DIM:]], axis=1).astype(BF16)
    wup = jnp.transpose(w_up.astype(BF16).reshape(DEPTH, D_MODEL, N_FF, FF_BLOCK), (0, 2, 1, 3))
    wdown = w_down.astype(BF16).reshape(DEPTH, N_FF, FF_BLOCK, D_MODEL)
    sink = jnp.repeat(b_sink, CHUNK, axis=1)[:, :, None]
    stacked = (norm_mix_g[:, None, :], norm_mlp_g[:, None, :], a_norm_g[:, None, :], win, wa2,
               b_a2[:, None, :], wout, wup, wdown, bias_c, sink)
    shared = (final_norm_g[None, :], bias_b, tri, lvl)

    x_p = x_prompt.reshape(seq, D_MODEL)
    x_s = x_sample.reshape(dec_b * dec_s, D_MODEL)
    outs_p, outs_s = [], []
    fmajor = lambda cache: jnp.transpose(cache, (0, 1, 3, 4, 2)).reshape(
        DEPTH, dec_b, cache.shape[3] * cache.shape[4], cache.shape[2])
    caches = (state_gla.reshape(DEPTH, dec_b, D_AQK, A_DV), fmajor(cache_b_k), fmajor(cache_b_v),
              fmajor(cache_c_k), fmajor(cache_c_v))
    for l in range(DEPTH):
        final = l == DEPTH - 1
        res_p = _run_layer(x_p, mods[l, 0:bsz], l, stacked, shared, None, prompt=True, final=final,
                           tq=TQ_PROMPT, nb=None)
        x_p = res_p[0]
        outs_p.append(res_p[1:])

        res_s = _run_layer(x_s, mods[l, bsz:bsz + dec_b], l, stacked, shared, caches, prompt=False,
                           final=final, tq=NB_SAMPLE * CHUNK, nb=NB_SAMPLE)
        x_s = res_s[0]
        outs_s.append(res_s[1:])

    sg_p = jnp.stack([o[0].reshape(bsz, A_HEADS, A_DK, A_DV) for o in outs_p])
    kb_p = jnp.stack([o[1].reshape(bsz, B_WINDOW, B_KV_HEADS, HEAD_DIM) for o in outs_p])
    vb_p = jnp.stack([o[2].reshape(bsz, B_WINDOW, B_KV_HEADS, HEAD_DIM) for o in outs_p])
    kc_p = jnp.stack([o[3].reshape(bsz, C_BAND, C_HEADS, HEAD_DIM) for o in outs_p])
    vc_p = jnp.stack([o[4].reshape(bsz, C_BAND, C_HEADS, HEAD_DIM) for o in outs_p])
    sg_s = jnp.stack([o[0].reshape(dec_b, A_HEADS, A_DK, A_DV) for o in outs_s])
    tmajor = lambda arrs, heads: jnp.transpose(
        jnp.stack(arrs).reshape(DEPTH, dec_b, heads, HEAD_DIM, -1), (0, 1, 4, 2, 3))
    kb_s = tmajor([o[1] for o in outs_s], B_KV_HEADS)
    vb_s = tmajor([o[2] for o in outs_s], B_KV_HEADS)
    kc_s = tmajor([o[3] for o in outs_s], C_HEADS)
    vc_s = tmajor([o[4] for o in outs_s], C_HEADS)
    return (x_p.reshape(bsz, seq, D_MODEL), x_s.reshape(dec_b, dec_s, D_MODEL),
            sg_p, kb_p, vb_p, kc_p, vc_p, sg_s, kb_s, vb_s, kc_s, vc_s)
```

```python
import functools

import numpy as np
import jax
import jax.numpy as jnp
from jax import lax
from jax.experimental import pallas as pl
from jax.experimental.pallas import tpu as pltpu

D_MODEL = 1024
DEPTH = 2
CHUNK = 64
HEAD_DIM = 64
A_HEADS = 4
A_DK = 64
A_DV = 128
A_GATE_RANK = 16
A_GATE_TAU = 16.0
B_HEADS = 4
B_KV_HEADS = 2
B_WINDOW = 128
C_HEADS = 4
C_BAND = 512
C_CLIP = 256
T5_BUCKETS = 32
T5_MAX_DIST = 128
D_FF = 4 * D_MODEL
NORM_EPS = 1e-6
NEG_INF = -1e30

LANES = 128
SUBLANES = 8
VMEM_LIMIT_BYTES = 60000 * 1024
VMEM_LIMIT_SAMPLE_BYTES = 62 * 1024 * 1024

OFF_QA = 0
OFF_KA = OFF_QA + A_HEADS * A_DK
OFF_VA = OFF_KA + A_HEADS * A_DK
OFF_GA = OFF_VA + A_HEADS * A_DV
OFF_QB = OFF_GA + A_HEADS * A_DV
OFF_KB = OFF_QB + B_HEADS * HEAD_DIM
OFF_VB = OFF_KB + B_KV_HEADS * HEAD_DIM
OFF_QC = OFF_VB + B_KV_HEADS * HEAD_DIM
OFF_KC = OFF_QC + C_HEADS * HEAD_DIM
OFF_VC = OFF_KC + C_HEADS * HEAD_DIM
OFF_RA = OFF_VC + C_HEADS * HEAD_DIM
P_IN = OFF_RA + LANES
D_AQK = A_HEADS * A_DK
D_AV = A_HEADS * A_DV
D_BKV = B_KV_HEADS * HEAD_DIM
D_C = C_HEADS * HEAD_DIM
D_MIX = D_AV + B_HEADS * HEAD_DIM + D_C
LK_B = B_WINDOW + CHUNK
LK_C = C_BAND + CHUNK
STACK = 4 * CHUNK
GLA_LEVELS = (1, 2, 4, 8, 16, 32)
FF_BLOCK = 1024
N_FF = D_FF // FF_BLOCK

BF16 = jnp.bfloat16
F32 = jnp.float32


def _dot(a, b):
    return jnp.dot(a, b, preferred_element_type=F32)


def _dot_nt(a, b):
    return lax.dot_general(a, b, (((1,), (1,)), ((), ())), preferred_element_type=F32)


def _dot_tn(a, b):
    return lax.dot_general(a, b, (((0,), (0,)), ((), ())), preferred_element_type=F32)


def _rms(x, g):
    return x * lax.rsqrt(jnp.mean(x * x, axis=-1, keepdims=True) + NORM_EPS) * g


def _split3(x):
    hi = x.astype(BF16)
    r1 = x - hi.astype(F32)
    mid = r1.astype(BF16)
    lo = (r1 - mid.astype(F32)).astype(BF16)
    return hi, mid, lo


def _head_masks(width, per_head):
    lane = lax.broadcasted_iota(jnp.int32, (1, width), 1)
    return [(lane >= h * per_head) & (lane < (h + 1) * per_head) for h in range(width // per_head)]


def _stack_masked(x, masks):
    return jnp.concatenate([jnp.where(m, x, 0.0) for m in masks], axis=0)


def _tile_rows(x, n):
    return jnp.concatenate([x] * n, axis=0)


def _scores(q_st, segments, bias_st, maskrow):
    scores, off = [], 0
    for k, v, transposed in segments:
        n = k.shape[1] if transposed else k.shape[0]
        s = (_dot(q_st, k) if transposed else _dot_nt(q_st, k)) + bias_st[:, off:off + n]
        if maskrow is not None:
            s = s + maskrow[:, off:off + n]
        scores.append(s)
        off += n
    return scores


def _softmax_pv(scores, segments, sink):
    m = functools.reduce(jnp.maximum, [jnp.max(s, axis=1, keepdims=True) for s in scores])
    if sink is not None:
        m = jnp.maximum(m, sink)
    es = [jnp.exp(s - m) for s in scores]
    l = functools.reduce(jnp.add, [jnp.sum(e, axis=1, keepdims=True) for e in es])
    if sink is not None:
        l = l + jnp.exp(sink - m)
    o = None
    for e, (k, v, transposed) in zip(es, segments):
        part = _dot_nt(e.astype(BF16), v) if transposed else _dot(e.astype(BF16), v)
        o = part if o is None else o + part
    return o / l


def _level_exponent(b, m, t):
    bcast = lambda i: jnp.broadcast_to(b[i:i + 1], (SUBLANES, b.shape[1]))
    low_half = lax.broadcasted_iota(jnp.int32, (SUBLANES, b.shape[1]), 0) < SUBLANES // 2
    parts = []
    for g in range(CHUNK // SUBLANES):
        r = g * SUBLANES
        rows = b[r:r + SUBLANES]
        if m >= SUBLANES:
            ref = b[(r // (2 * m)) * 2 * m + m - 1:(r // (2 * m)) * 2 * m + m]
            parts.append(rows - ref if (r // m) % 2 == 1 else ref - rows)
        elif m == 4:
            parts.append(rows - bcast(r + 3))
        else:
            parts.append(rows - jnp.where(low_half, bcast(r + 1), bcast(r + 5)))
    d = jnp.concatenate(parts, axis=0)
    return d if m >= SUBLANES else jnp.where((t & m) != 0, d, -d)


def _odd_block_rows(x, m, groups):
    return jnp.concatenate([x[g * CHUNK + r:g * CHUNK + r + SUBLANES] for g in range(groups)
                            for r in range(0, CHUNK, SUBLANES) if (r // m) % 2 == 1], axis=0)


def _spread_odd_block_rows(y, m, groups):
    zero = jnp.zeros((SUBLANES, y.shape[1]), y.dtype)
    parts, i = [], 0
    for g in range(groups):
        for r in range(0, CHUNK, SUBLANES):
            if (r // m) % 2 == 1:
                parts.append(y[i:i + SUBLANES])
                i += SUBLANES
            else:
                parts.append(zero)
    return jnp.concatenate(parts, axis=0)


def _gla_levels(q, k, la, b, hm_a):
    t = lax.broadcasted_iota(jnp.int32, (CHUNK, D_AQK), 0)
    q_st = _stack_masked(q, hm_a)
    k16 = k.astype(BF16)
    prods = [_dot_nt(q_st.astype(BF16), k16)]
    for m in GLA_LEVELS:
        e = jnp.exp(la if m == 1 else _level_exponent(b, m, t))
        kt = k16 if m == 1 else (k * e).astype(BF16)
        if m >= SUBLANES:
            qt = _odd_block_rows(q_st, m, A_HEADS) * _tile_rows(_odd_block_rows(e, m, 1), A_HEADS)
            prods.append(_spread_odd_block_rows(_dot_nt(qt.astype(BF16), kt), m, A_HEADS))
        else:
            qt = (q_st * _tile_rows(e, A_HEADS)).astype(BF16)
            prods.append(_dot_nt(qt, kt))
    return q_st, prods


def _gla_finish(q_st, prods, k, v, b, s, lvl):
    attn = jnp.where(lvl == 0, prods[0], 0.0)
    for li in range(1, len(prods)):
        attn = jnp.where(lvl == li, prods[li], attn)
    qhat = (q_st * _tile_rows(jnp.exp(b), A_HEADS)).astype(BF16)
    o_inter = _dot(qhat, s.astype(BF16))
    v16 = v.astype(BF16)
    o_intra = jnp.concatenate(
        [_dot(attn[h * CHUNK:(h + 1) * CHUNK].astype(BF16), v16[:, h * A_DV:(h + 1) * A_DV])
         for h in range(A_HEADS)], axis=0)
    b_last = b[CHUNK - 1:CHUNK]
    khat = (k * jnp.exp(b_last - b)).astype(BF16)
    upd = _dot_tn(khat, v16)
    decay = jnp.broadcast_to(jnp.exp(b_last), (A_DV, D_AQK)).T
    s_new = s * decay + jnp.concatenate(
        [upd[h * A_DK:(h + 1) * A_DK, h * A_DV:(h + 1) * A_DV] for h in range(A_HEADS)], axis=0)
    return o_intra + o_inter, s_new


def _layer_kernel(*refs, prompt, final, nch, tq, nt):
    (x_ref, mods_ref, modp_ref, gmix_ref, gmlp_ref, gfin_ref, anorm_ref, win_ref, wa2_ref, ba2_ref, wout_ref,
     wup_ref, wdown_ref, biasb_ref, biasc_ref, sink_ref, tri_ref, lvl_ref) = refs[:18]
    pos = 18
    if not prompt:
        st_ref, cbk_ref, cbv_ref, cck_ref, ccv_ref = refs[pos:pos + 5]
        pos += 5
    y_ref, so_ref, kbo_ref, vbo_ref, kco_ref, vco_ref = refs[pos:pos + 6]
    pos += 6
    (h_scr, h2_scr, proj_scr, la_scr, b_scr, mix_scr, x1_scr, acc_scr, st_scr,
     kbw, vbw, kcw, vcw) = refs[pos:]

    step = pl.program_id(0)

    def mod(ref, c, j):
        return ref[0 if prompt else c, j:j + 1, :]

    @pl.when(step == 0)
    def _():
        h2_scr[...] = jnp.zeros_like(h2_scr)
        x1_scr[...] = jnp.zeros_like(x1_scr)
        acc_scr[...] = jnp.zeros_like(acc_scr)
        if prompt:
            st_scr[...] = jnp.zeros_like(st_scr)
            kbw[...] = jnp.zeros_like(kbw)
            vbw[...] = jnp.zeros_like(vbw)
            kcw[...] = jnp.zeros_like(kcw)
            vcw[...] = jnp.zeros_like(vcw)

    def finish_previous_tile():
        gfin = gfin_ref[...]
        for c in range(nch):
            r = slice(c * CHUNK, (c + 1) * CHUNK)
            x2 = x1_scr[r, :] + mod(modp_ref, c, 5) * acc_scr[r, :]
            y_ref[r, :] = _rms(x2, gfin) if final else x2
        acc_scr[...] = jnp.zeros_like(acc_scr)

    @pl.when(step < nt)
    def _():
        gmix = gmix_ref[...]
        for c in range(nch):
            r = slice(c * CHUNK, (c + 1) * CHUNK)
            h = _rms(x_ref[r, :], gmix) * (1.0 + mod(mods_ref, c, 1)) + mod(mods_ref, c, 0)
            h_scr[r, :] = h.astype(BF16)
        hmix = h_scr[...]
        cut1, cut2 = OFF_GA, OFF_QC
        ra = _dot(hmix, win_ref[:, OFF_RA:OFF_RA + LANES])
        proj_scr[:, 0:cut1] = _dot(hmix, win_ref[:, 0:cut1])
        z = _dot(ra.astype(BF16), wa2_ref[...]) + ba2_ref[...]
        la = (jnp.minimum(z, 0.0) - jnp.log1p(jnp.exp(-jnp.abs(z)))) * (1.0 / A_GATE_TAU)
        la_scr[...] = la
        proj_scr[:, cut1:cut2] = _dot(hmix, win_ref[:, cut1:cut2])
        tri = tri_ref[...]
        hi, mid, lo = _split3(la)
        b_scr[...] = _dot(tri, hi) + _dot(tri, mid) + _dot(tri, lo)
        proj_scr[:, cut2:OFF_RA] = _dot(hmix, win_ref[:, cut2:OFF_RA])

        if prompt:
            kbw[B_WINDOW:B_WINDOW + tq, :] = proj_scr[:, OFF_KB:OFF_KB + D_BKV].astype(BF16)
            vbw[B_WINDOW:B_WINDOW + tq, :] = proj_scr[:, OFF_VB:OFF_VB + D_BKV].astype(BF16)
            kcw[C_BAND:C_BAND + tq, :] = proj_scr[:, OFF_KC:OFF_KC + D_C].astype(BF16)
            vcw[C_BAND:C_BAND + tq, :] = proj_scr[:, OFF_VC:OFF_VC + D_C].astype(BF16)

        hm_a = _head_masks(D_AQK, A_DK)
        hm_c = _head_masks(D_C, HEAD_DIM)
        hm_b = _head_masks(D_BKV, HEAD_DIM)
        anorm = anorm_ref[...]

        def chunk_body(c, carry):
            r0 = pl.multiple_of(c * CHUNK, CHUNK)
            rows = pl.ds(r0, CHUNK)

            if prompt:
                seg_b = [(kbw[pl.ds(r0, LK_B), :], vbw[pl.ds(r0, LK_B), :], False)]
                seg_c = [(kcw[pl.ds(r0, LK_C), :], vcw[pl.ds(r0, LK_C), :], False)]
                gidx = step * nch + c
                lane_b = lax.broadcasted_iota(jnp.int32, (1, LK_B), 1)
                lane_c = lax.broadcasted_iota(jnp.int32, (1, LK_C), 1)
                mask_b = jnp.where(lane_b >= B_WINDOW - gidx * CHUNK, 0.0, NEG_INF)
                mask_c = jnp.where(lane_c >= C_BAND - gidx * CHUNK, 0.0, NEG_INF)
            else:
                kb_new = proj_scr[rows, OFF_KB:OFF_KB + D_BKV]
                vb_new = proj_scr[rows, OFF_VB:OFF_VB + D_BKV]
                kc_new = proj_scr[rows, OFF_KC:OFF_KC + D_C]
                vc_new = proj_scr[rows, OFF_VC:OFF_VC + D_C]
                seg_b = [(cbk_ref[c].astype(BF16), cbv_ref[c].astype(BF16), True),
                         (kb_new.astype(BF16), vb_new.astype(BF16), False)]
                seg_c = [(cck_ref[c].astype(BF16), ccv_ref[c].astype(BF16), True),
                         (kc_new.astype(BF16), vc_new.astype(BF16), False)]
                mask_b = mask_c = None

            q = proj_scr[rows, OFF_QA:OFF_QA + D_AQK] * (A_DK ** -0.5)
            k = proj_scr[rows, OFF_KA:OFF_KA + D_AQK]
            b = b_scr[rows, :]
            q_st, prods = _gla_levels(q, k, la_scr[rows, :], b, hm_a)
            qc = proj_scr[rows, OFF_QC:OFF_QC + D_C] * (HEAD_DIM ** -0.5)
            sc_c = _scores(_stack_masked(qc, hm_c).astype(BF16), seg_c, biasc_ref[...], mask_c)
            qb = proj_scr[rows, OFF_QB:OFF_QB + B_HEADS * HEAD_DIM] * (HEAD_DIM ** -0.5)
            qb_a, qb_b = qb[:, 0:D_BKV], qb[:, D_BKV:2 * D_BKV]
            qb_st = jnp.concatenate([jnp.where(hm_b[0], qb_a, 0.0), jnp.where(hm_b[0], qb_b, 0.0),
                                     jnp.where(hm_b[1], qb_a, 0.0), jnp.where(hm_b[1], qb_b, 0.0)], axis=0)
            sc_b = _scores(qb_st.astype(BF16), seg_b, biasb_ref[...], mask_b)

            up = _dot(h2_scr[...], wup_ref[c])
            act = jnp.square(jnp.maximum(up, 0.0)).astype(BF16)

            v = proj_scr[rows, OFF_VA:OFF_VA + D_AV]
            s_old = st_scr[...] if prompt else st_ref[c]
            o_st, s_new = _gla_finish(q_st, prods, k, v, b, s_old, lvl_ref[...])
            if prompt:
                st_scr[...] = s_new
            else:
                so_ref[c] = s_new
            g = proj_scr[rows, OFF_GA:OFF_GA + D_AV]
            for h in range(A_HEADS):
                on = _rms(o_st[h * CHUNK:(h + 1) * CHUNK], anorm)
                gh = g[:, h * A_DV:(h + 1) * A_DV]
                mix_scr[rows, h * A_DV:(h + 1) * A_DV] = (on * (gh * jax.nn.sigmoid(gh))).astype(BF16)

            oc_st = _softmax_pv(sc_c, seg_c, None)
            oc = jnp.where(hm_c[0], oc_st[0:CHUNK], 0.0)
            for h in range(1, C_HEADS):
                oc = jnp.where(hm_c[h], oc_st[h * CHUNK:(h + 1) * CHUNK], oc)
            mix_scr[rows, D_AV + 2 * D_BKV:D_MIX] = oc.astype(BF16)

            ob_st = _softmax_pv(sc_b, seg_b, sink_ref[...])
            ob_a = jnp.where(hm_b[0], ob_st[0:CHUNK], ob_st[2 * CHUNK:3 * CHUNK])
            ob_b = jnp.where(hm_b[0], ob_st[CHUNK:2 * CHUNK], ob_st[3 * CHUNK:4 * CHUNK])
            mix_scr[rows, D_AV:D_AV + D_BKV] = ob_a.astype(BF16)
            mix_scr[rows, D_AV + D_BKV:D_AV + 2 * D_BKV] = ob_b.astype(BF16)

            acc_scr[...] += _dot(act, wdown_ref[c])

            if not prompt:
                for dst, src, new, width in ((kbo_ref, cbk_ref, kb_new, B_WINDOW), (vbo_ref, cbv_ref, vb_new, B_WINDOW),
                                             (kco_ref, cck_ref, kc_new, C_BAND), (vco_ref, ccv_ref, vc_new, C_BAND)):
                    dst[c, :, 0:width - CHUNK] = src[c, :, CHUNK:width]
                    dst[c, :, width - CHUNK:width] = new.T
            return carry

        lax.fori_loop(0, nch, chunk_body, 0)

        if prompt:
            kbw[0:B_WINDOW, :] = kbw[tq:tq + B_WINDOW, :]
            vbw[0:B_WINDOW, :] = vbw[tq:tq + B_WINDOW, :]
            kcw[0:C_BAND, :] = kcw[tq:tq + C_BAND, :]
            vcw[0:C_BAND, :] = vcw[tq:tq + C_BAND, :]

            so_ref[...] = st_scr[...]
            kbo_ref[...] = proj_scr[tq - B_WINDOW:tq, OFF_KB:OFF_KB + D_BKV]
            vbo_ref[...] = proj_scr[tq - B_WINDOW:tq, OFF_VB:OFF_VB + D_BKV]
            rb = kco_ref.shape[0]
            kco_ref[...] = proj_scr[tq - rb:tq, OFF_KC:OFF_KC + D_C]
            vco_ref[...] = proj_scr[tq - rb:tq, OFF_VC:OFF_VC + D_C]

        mixed = _dot(mix_scr[...], wout_ref[...])

        finish_previous_tile()

        gmlp = gmlp_ref[...]
        for c in range(nch):
            r = slice(c * CHUNK, (c + 1) * CHUNK)
            x1 = x_ref[r, :] + mod(mods_ref, c, 2) * mixed[r, :]
            x1_scr[r, :] = x1
            h2 = _rms(x1, gmlp) * (1.0 + mod(mods_ref, c, 4)) + mod(mods_ref, c, 3)
            h2_scr[r, :] = h2.astype(BF16)

    @pl.when(step == nt)
    def _():
        def mlp_block(c, carry):
            up = _dot(h2_scr[...], wup_ref[c])
            act = jnp.square(jnp.maximum(up, 0.0)).astype(BF16)
            acc_scr[...] += _dot(act, wdown_ref[c])
            return carry

        lax.fori_loop(0, nch, mlp_block, 0)
        finish_previous_tile()


def _layer_spec(arr, layer):
    nd = arr.ndim - 1
    return pl.BlockSpec((None,) + arr.shape[1:], lambda i, _l=layer, _nd=nd: (_l,) + (0,) * _nd,
                        pipeline_mode=pl.Buffered(1))


def _const_spec(arr):
    nd = arr.ndim
    return pl.BlockSpec(arr.shape, lambda i, _nd=nd: (0,) * _nd, pipeline_mode=pl.Buffered(1))


def _run_layer(x, mods, layer, stacked, shared, caches, *, prompt, final, tq, nb):
    rows = x.shape[0]
    nch = tq // CHUNK
    nt = rows // tq
    assert rows % tq == 0 and nch == N_FF
    (gmix, gmlp, anorm, win, wa2, ba2, wout, wup, wdown, biasc, sink) = stacked
    (gfin, biasb, tri, lvl) = shared
    cur = lambda i: jnp.minimum(i, nt - 1)
    prv = lambda i: jnp.maximum(i - 1, 0)

    nmod = 1 if prompt else nb
    in_specs = [pl.BlockSpec((tq, D_MODEL), lambda i: (cur(i), 0)),
                pl.BlockSpec((nmod, SUBLANES, D_MODEL), lambda i: (0 if prompt else cur(i), 0, 0)),
                pl.BlockSpec((nmod, SUBLANES, D_MODEL), lambda i: (0 if prompt else prv(i), 0, 0))]
    ls = functools.partial(_layer_spec, layer=layer)
    in_specs += [ls(gmix), ls(gmlp), _const_spec(gfin), ls(anorm), ls(win), ls(wa2), ls(ba2), ls(wout),
                 ls(wup), ls(wdown), _const_spec(biasb), ls(biasc), ls(sink), _const_spec(tri), _const_spec(lvl)]
    args = [x, mods, mods, gmix, gmlp, gfin, anorm, win, wa2, ba2, wout, wup, wdown, biasb, biasc, sink, tri, lvl]

    y_spec = pl.BlockSpec((tq, D_MODEL), lambda i: (prv(i), 0))
    if prompt:
        assert tq >= B_WINDOW and (tq % C_BAND == 0 or C_BAND % tq == 0)
        rb = min(tq, C_BAND)
        first = nt - C_BAND // rb
        out_shape = [jax.ShapeDtypeStruct((rows, D_MODEL), F32),
                     jax.ShapeDtypeStruct((D_AQK, A_DV), F32),
                     jax.ShapeDtypeStruct((B_WINDOW, D_BKV), F32),
                     jax.ShapeDtypeStruct((B_WINDOW, D_BKV), F32),
                     jax.ShapeDtypeStruct((C_BAND, D_C), F32),
                     jax.ShapeDtypeStruct((C_BAND, D_C), F32)]
        band = pl.BlockSpec((rb, D_C), lambda i: (jnp.maximum(cur(i) - first, 0), 0))
        out_specs = [y_spec,
                     pl.BlockSpec((D_AQK, A_DV), lambda i: (0, 0)),
                     pl.BlockSpec((B_WINDOW, D_BKV), lambda i: (0, 0)),
                     pl.BlockSpec((B_WINDOW, D_BKV), lambda i: (0, 0)),
                     band, band]
        kbw_rows, kcw_rows = B_WINDOW + tq, C_BAND + tq
    else:
        assert nb == nch
        nseq = rows // CHUNK
        st, cbk, cbv, cck, ccv = caches
        seq_in = lambda r, w: pl.BlockSpec((None, nb, r, w), lambda i, _l=layer: (_l, cur(i), 0, 0))
        seq_out = lambda r, w: pl.BlockSpec((nb, r, w), lambda i: (cur(i), 0, 0))
        in_specs += [seq_in(D_AQK, A_DV), seq_in(D_BKV, B_WINDOW), seq_in(D_BKV, B_WINDOW),
                     seq_in(D_C, C_BAND), seq_in(D_C, C_BAND)]
        args += [st, cbk, cbv, cck, ccv]
        out_shape = [jax.ShapeDtypeStruct((rows, D_MODEL), F32),
                     jax.ShapeDtypeStruct((nseq, D_AQK, A_DV), F32),
                     jax.ShapeDtypeStruct((nseq, D_BKV, B_WINDOW), F32),
                     jax.ShapeDtypeStruct((nseq, D_BKV, B_WINDOW), F32),
                     jax.ShapeDtypeStruct((nseq, D_C, C_BAND), F32),
                     jax.ShapeDtypeStruct((nseq, D_C, C_BAND), F32)]
        out_specs = [y_spec, seq_out(D_AQK, A_DV), seq_out(D_BKV, B_WINDOW), seq_out(D_BKV, B_WINDOW),
                     seq_out(D_C, C_BAND), seq_out(D_C, C_BAND)]
        kbw_rows, kcw_rows = SUBLANES * 2, SUBLANES * 2

    scratch = [pltpu.VMEM((tq, D_MODEL), BF16),
               pltpu.VMEM((tq, D_MODEL), BF16),
               pltpu.VMEM((tq, OFF_RA), F32),
               pltpu.VMEM((tq, D_AQK), F32),
               pltpu.VMEM((tq, D_AQK), F32),
               pltpu.VMEM((tq, D_MIX), BF16),
               pltpu.VMEM((tq, D_MODEL), F32),
               pltpu.VMEM((tq, D_MODEL), F32),
               pltpu.VMEM((D_AQK, A_DV), F32),
               pltpu.VMEM((kbw_rows, D_BKV), BF16), pltpu.VMEM((kbw_rows, D_BKV), BF16),
               pltpu.VMEM((kcw_rows, D_C), BF16), pltpu.VMEM((kcw_rows, D_C), BF16)]

    kern = functools.partial(_layer_kernel, prompt=prompt, final=final, nch=nch, tq=tq, nt=nt)
    return pl.pallas_call(
        kern,
        grid=(nt + 1,),
        in_specs=in_specs,
        out_specs=out_specs,
        out_shape=out_shape,
        scratch_shapes=scratch,
        compiler_params=pltpu.CompilerParams(
            dimension_semantics=("arbitrary",),
            vmem_limit_bytes=VMEM_LIMIT_BYTES if prompt else VMEM_LIMIT_SAMPLE_BYTES),
        name=("layer_prompt" if prompt else "layer_sample") + ("_final" if final else ""),
    )(*args)


ADA_BLOCK = 1536


def _ada_kernel(c_ref, w_ref, b_ref, o_ref):
    c = c_ref[...]
    sc = c * jax.nn.sigmoid(c)
    sc_hi = sc.astype(BF16)
    sc_lo = (sc - sc_hi.astype(F32)).astype(BF16)
    w = w_ref[0]
    w_hi = w.astype(BF16)
    w_lo = (w - w_hi.astype(F32)).astype(BF16)
    o_ref[0] = _dot(sc_hi, w_hi) + _dot(sc_lo, w_hi) + _dot(sc_hi, w_lo) + b_ref[0]


def _ada(c_all, w_ada, b_ada):
    rows = c_all.shape[0]
    n = 6 * D_MODEL
    return pl.pallas_call(
        _ada_kernel,
        grid=(DEPTH, n // ADA_BLOCK),
        in_specs=[pl.BlockSpec((rows, D_MODEL), lambda l, j: (0, 0)),
                  pl.BlockSpec((1, D_MODEL, ADA_BLOCK), lambda l, j: (l, 0, j)),
                  pl.BlockSpec((1, 1, ADA_BLOCK), lambda l, j: (l, 0, j))],
        out_specs=pl.BlockSpec((1, rows, ADA_BLOCK), lambda l, j: (l, 0, j)),
        out_shape=jax.ShapeDtypeStruct((DEPTH, rows, n), F32),
        compiler_params=pltpu.CompilerParams(dimension_semantics=("arbitrary", "arbitrary"),
                                             vmem_limit_bytes=VMEM_LIMIT_BYTES),
        name="adaln",
    )(c_all, w_ada, b_ada.reshape(DEPTH, 1, n))


TB_W = 2 * LANES
TC_W = 5 * LANES
TC_FLAT = TC_W - (CHUNK + C_CLIP + 1)


def _shear(f, width):
    tiled = jnp.broadcast_to(f, (CHUNK, width))
    return pltpu.roll(tiled, width - (CHUNK - 1), 1, stride=1, stride_axis=0)


def _bias_kernel(t5_ref, crel_ref, bucket_ref, ob_ref, oc_ref):
    bucket = bucket_ref[...]
    for h in range(B_HEADS):
        def b_body(i, acc, h=h):
            return jnp.where(bucket == i, t5_ref[h, i], acc)
        f = lax.fori_loop(0, T5_BUCKETS, b_body, jnp.zeros(bucket.shape, F32))
        ob_ref[h] = _shear(f, TB_W)[:, 0:LK_B]
    lane = lax.broadcasted_iota(jnp.int32, (1, TC_W), 1)
    for l in range(DEPTH):
        for h in range(C_HEADS):
            row = crel_ref[l, h:h + 1, :]
            f = jnp.where(lane < TC_FLAT, row[:, 0:1], pltpu.roll(row, TC_FLAT, 1))
            oc_ref[l, h] = _shear(f, TC_W)[:, 0:LK_C]


def _bias_tables(t5_bias, c_rel_bias, bucket):
    smem = pl.BlockSpec(memory_space=pltpu.SMEM)
    vmem = pl.BlockSpec(memory_space=pltpu.VMEM)
    crel = jnp.pad(jnp.swapaxes(c_rel_bias, 1, 2),
                   ((0, 0), (0, SUBLANES - C_HEADS), (0, TC_W - (2 * C_CLIP + 1))))
    return pl.pallas_call(
        _bias_kernel,
        in_specs=[smem, vmem, vmem],
        out_specs=[vmem, vmem],
        out_shape=[jax.ShapeDtypeStruct((B_HEADS, CHUNK, LK_B), F32),
                   jax.ShapeDtypeStruct((DEPTH, C_HEADS, CHUNK, LK_C), F32)],
        name="bias_tables",
    )(t5_bias.T, crel, bucket)


def _t5_bucket(rel):
    half = T5_BUCKETS // 2
    max_exact = half // 2
    steps = half - max_exact
    thresholds = [int(np.ceil(max_exact * (T5_MAX_DIST / max_exact) ** (k / steps) - 1e-9)) for k in range(1, steps)]
    n = jnp.abs(rel)
    large = jnp.minimum(max_exact + sum((n >= t).astype(jnp.int32) for t in thresholds), half - 1)
    return jnp.where(rel > 0, half, 0) + jnp.where(n < max_exact, n, large)


def _gla_constants(tq):
    r = np.arange(tq)
    tri = ((r[:, None] // CHUNK == r[None, :] // CHUNK) & (r[None, :] <= r[:, None])).astype(np.float32)
    t = np.arange(CHUNK)[:, None]
    s = np.arange(CHUNK)[None, :]
    lvl = np.full((CHUNK, CHUNK), -1, np.int32)
    lvl[t == s] = 0
    for li, m in enumerate(GLA_LEVELS):
        sel = ((t // m) % 2 == 1) & ((s // m) == (t // m) - 1)
        lvl[np.broadcast_to(sel, lvl.shape)] = li + 1
    return jnp.asarray(tri, BF16), jnp.asarray(np.tile(lvl, (A_HEADS, 1)))


SRC_RA = 2 * D_AQK + 2 * D_AV
SRC_QB = SRC_RA + A_GATE_RANK
SRC_KB = SRC_QB + B_HEADS * HEAD_DIM
D_IN = SRC_KB + 2 * D_BKV + 3 * D_C
PREP_ROWS = 256
QB_HEAD_ORDER = (0, 2, 1, 3)


def _w_in_kernel(w_ref, o_ref):
    def put(dst, src, width):
        o_ref[:, dst:dst + width] = w_ref[:, src:src + width].astype(BF16)

    put(OFF_QA, 0, SRC_RA)
    for i, h in enumerate(QB_HEAD_ORDER):
        put(OFF_QB + i * HEAD_DIM, SRC_QB + h * HEAD_DIM, HEAD_DIM)
    put(OFF_KB, SRC_KB, D_IN - SRC_KB)
    put(OFF_RA, SRC_RA, A_GATE_RANK)
    o_ref[:, OFF_RA + A_GATE_RANK:P_IN] = jnp.zeros((o_ref.shape[0], LANES - A_GATE_RANK), BF16)


def _prep_w_in(w_in):
    return pl.pallas_call(
        _w_in_kernel,
        grid=(DEPTH, D_MODEL // PREP_ROWS),
        in_specs=[pl.BlockSpec((None, PREP_ROWS, D_IN), lambda l, r: (l, r, 0))],
        out_specs=pl.BlockSpec((None, PREP_ROWS, P_IN), lambda l, r: (l, r, 0)),
        out_shape=jax.ShapeDtypeStruct((DEPTH, D_MODEL, P_IN), BF16),
        compiler_params=pltpu.CompilerParams(dimension_semantics=("arbitrary", "arbitrary")),
        name="prep_w_in",
    )(w_in)


def _cast_kernel(w_ref, o_ref):
    o_ref[...] = w_ref[...].astype(BF16)


def _prep_w_up(w_up):
    return pl.pallas_call(
        _cast_kernel,
        grid=(DEPTH, N_FF, D_MODEL // PREP_ROWS),
        in_specs=[pl.BlockSpec((None, PREP_ROWS, FF_BLOCK), lambda l, j, r: (l, r, j))],
        out_specs=pl.BlockSpec((None, None, PREP_ROWS, FF_BLOCK), lambda l, j, r: (l, j, r, 0)),
        out_shape=jax.ShapeDtypeStruct((DEPTH, N_FF, D_MODEL, FF_BLOCK), BF16),
        compiler_params=pltpu.CompilerParams(dimension_semantics=("arbitrary",) * 3),
        name="prep_w_up",
    )(w_up)


TQ_PROMPT = 256
NB_SAMPLE = 4


def kernel(x_prompt, x_sample, c_prompt, c_sample, state_gla, cache_b_k, cache_b_v, cache_c_k, cache_c_v,
           w_ada, b_ada, norm_mix_g, norm_mlp_g, w_in, w_a2, b_a2, a_norm_g, b_sink, t5_bias, c_rel_bias,
           w_out, w_up, w_down, final_norm_g):
    bsz, seq, _ = x_prompt.shape
    dec_b, dec_s, _ = x_sample.shape
    assert bsz == 1 and dec_s == CHUNK and TQ_PROMPT == NB_SAMPLE * CHUNK

    n_c = bsz + dec_b
    c_rows = -(-n_c // SUBLANES) * SUBLANES
    c_all = jnp.concatenate([c_prompt, c_sample, jnp.zeros((c_rows - n_c, D_MODEL), F32)], axis=0)
    mods = _ada(c_all, w_ada, b_ada).reshape(DEPTH, c_rows, 6, D_MODEL)
    mods = jnp.pad(mods, ((0, 0), (0, 0), (0, SUBLANES - 6), (0, 0)))

    rel_b = jnp.arange(TB_W) - (CHUNK - 1) - B_WINDOW
    bias_b, bias_c = _bias_tables(t5_bias, c_rel_bias, _t5_bucket(rel_b).astype(jnp.int32)[None, :])
    bias_b = bias_b.reshape(STACK, LK_B)
    bias_c = bias_c.reshape(DEPTH, STACK, LK_C)

    tri, lvl = _gla_constants(TQ_PROMPT)
    ob0 = D_AV
    orow = lambda h: slice(ob0 + h * HEAD_DIM, ob0 + (h + 1) * HEAD_DIM)

    win = _prep_w_in(w_in)
    wa2 = jnp.concatenate([w_a2, jnp.zeros((DEPTH, LANES - A_GATE_RANK, D_AQK), F32)], axis=1).astype(BF16)
    wout = jnp.concatenate([w_out[:, 0:D_AV], w_out[:, orow(0)], w_out[:, orow(2)], w_out[:, orow(1)],
                            w_out[:, orow(3)], w_out[:, D_AV + B_HEADS * HEAD_DIM:]], axis=1).astype(BF16)
    wup = _prep_w_up(w_up)
    wdown = w_down.astype(BF16).reshape(DEPTH, N_FF, FF_BLOCK, D_MODEL)
    sink = jnp.repeat(b_sink, CHUNK, axis=1)[:, :, None]
    stacked = (norm_mix_g[:, None, :], norm_mlp_g[:, None, :], a_norm_g[:, None, :], win, wa2,
               b_a2[:, None, :], wout, wup, wdown, bias_c, sink)
    shared = (final_norm_g[None, :], bias_b, tri, lvl)

    x_p = x_prompt.reshape(seq, D_MODEL)
    x_s = x_sample.reshape(dec_b * dec_s, D_MODEL)
    outs_p, outs_s = [], []
    fmajor = lambda cache: jnp.transpose(cache, (0, 1, 3, 4, 2)).reshape(
        DEPTH, dec_b, cache.shape[3] * cache.shape[4], cache.shape[2])
    caches = (state_gla.reshape(DEPTH, dec_b, D_AQK, A_DV), fmajor(cache_b_k), fmajor(cache_b_v),
              fmajor(cache_c_k), fmajor(cache_c_v))
    for l in range(DEPTH):
        final = l == DEPTH - 1
        res_p = _run_layer(x_p, mods[l, 0:bsz], l, stacked, shared, None, prompt=True, final=final,
                           tq=TQ_PROMPT, nb=None)
        x_p = res_p[0]
        outs_p.append(res_p[1:])

        res_s = _run_layer(x_s, mods[l, bsz:bsz + dec_b], l, stacked, shared, caches, prompt=False,
                           final=final, tq=NB_SAMPLE * CHUNK, nb=NB_SAMPLE)
        x_s = res_s[0]
        outs_s.append(res_s[1:])

    sg_p = jnp.stack([o[0].reshape(bsz, A_HEADS, A_DK, A_DV) for o in outs_p])
    kb_p = jnp.stack([o[1].reshape(bsz, B_WINDOW, B_KV_HEADS, HEAD_DIM) for o in outs_p])
    vb_p = jnp.stack([o[2].reshape(bsz, B_WINDOW, B_KV_HEADS, HEAD_DIM) for o in outs_p])
    kc_p = jnp.stack([o[3].reshape(bsz, C_BAND, C_HEADS, HEAD_DIM) for o in outs_p])
    vc_p = jnp.stack([o[4].reshape(bsz, C_BAND, C_HEADS, HEAD_DIM) for o in outs_p])
    sg_s = jnp.stack([o[0].reshape(dec_b, A_HEADS, A_DK, A_DV) for o in outs_s])
    tmajor = lambda arrs, heads: jnp.transpose(
        jnp.stack(arrs).reshape(DEPTH, dec_b, heads, HEAD_DIM, -1), (0, 1, 4, 2, 3))
    kb_s = tmajor([o[1] for o in outs_s], B_KV_HEADS)
    vb_s = tmajor([o[2] for o in outs_s], B_KV_HEADS)
    kc_s = tmajor([o[3] for o in outs_s], C_HEADS)
    vc_s = tmajor([o[4] for o in outs_s], C_HEADS)
    return (x_p.reshape(bsz, seq, D_MODEL), x_s.reshape(dec_b, dec_s, D_MODEL),
            sg_p, kb_p, vb_p, kc_p, vc_p, sg_s, kb_s, vb_s, kc_s, vc_s)
```

```python
import functools

import numpy as np
import jax
import jax.numpy as jnp
from jax import lax
from jax.experimental import pallas as pl
from jax.experimental.pallas import tpu as pltpu

D_MODEL = 1024
DEPTH = 2
CHUNK = 64
HEAD_DIM = 64
A_HEADS = 4
A_DK = 64
A_DV = 128
A_GATE_RANK = 16
A_GATE_TAU = 16.0
B_HEADS = 4
B_KV_HEADS = 2
B_WINDOW = 128
C_HEADS = 4
C_BAND = 512
C_CLIP = 256
T5_BUCKETS = 32
T5_MAX_DIST = 128
D_FF = 4 * D_MODEL
NORM_EPS = 1e-6
NEG_INF = -1e30

LANES = 128
SUBLANES = 8
VMEM_LIMIT_BYTES = 60000 * 1024
VMEM_LIMIT_SAMPLE_BYTES = 62 * 1024 * 1024

OFF_QA = 0
OFF_KA = OFF_QA + A_HEADS * A_DK
OFF_VA = OFF_KA + A_HEADS * A_DK
OFF_GA = OFF_VA + A_HEADS * A_DV
OFF_QB = OFF_GA + A_HEADS * A_DV
OFF_KB = OFF_QB + B_HEADS * HEAD_DIM
OFF_VB = OFF_KB + B_KV_HEADS * HEAD_DIM
OFF_QC = OFF_VB + B_KV_HEADS * HEAD_DIM
OFF_KC = OFF_QC + C_HEADS * HEAD_DIM
OFF_VC = OFF_KC + C_HEADS * HEAD_DIM
OFF_RA = OFF_VC + C_HEADS * HEAD_DIM
P_IN = OFF_RA + LANES
D_AQK = A_HEADS * A_DK
D_AV = A_HEADS * A_DV
D_BKV = B_KV_HEADS * HEAD_DIM
D_C = C_HEADS * HEAD_DIM
D_MIX = D_AV + B_HEADS * HEAD_DIM + D_C
LK_B = B_WINDOW + CHUNK
LK_C = C_BAND + CHUNK
STACK = 4 * CHUNK
GLA_LEVELS = (1, 2, 4, 8, 16, 32)
FF_BLOCK = 1024
N_FF = D_FF // FF_BLOCK

BF16 = jnp.bfloat16
F32 = jnp.float32


def _dot(a, b):
    return jnp.dot(a, b, preferred_element_type=F32)


def _dot_nt(a, b):
    return lax.dot_general(a, b, (((1,), (1,)), ((), ())), preferred_element_type=F32)


def _dot_tn(a, b):
    return lax.dot_general(a, b, (((0,), (0,)), ((), ())), preferred_element_type=F32)


def _rms(x, g):
    return x * lax.rsqrt(jnp.mean(x * x, axis=-1, keepdims=True) + NORM_EPS) * g


def _split3(x):
    hi = x.astype(BF16)
    r1 = x - hi.astype(F32)
    mid = r1.astype(BF16)
    lo = (r1 - mid.astype(F32)).astype(BF16)
    return hi, mid, lo


def _head_masks(width, per_head):
    lane = lax.broadcasted_iota(jnp.int32, (1, width), 1)
    return [(lane >= h * per_head) & (lane < (h + 1) * per_head) for h in range(width // per_head)]


def _stack_masked(x, masks):
    return jnp.concatenate([jnp.where(m, x, 0.0) for m in masks], axis=0)


def _tile_rows(x, n):
    return jnp.concatenate([x] * n, axis=0)


def _scores(q_st, segments, bias_st, maskrow):
    scores, off = [], 0
    for k, v, transposed in segments:
        n = k.shape[1] if transposed else k.shape[0]
        s = (_dot(q_st, k) if transposed else _dot_nt(q_st, k)) + bias_st[:, off:off + n]
        if maskrow is not None:
            s = s + maskrow[:, off:off + n]
        scores.append(s)
        off += n
    return scores


def _softmax_pv(scores, segments, sink):
    m = functools.reduce(jnp.maximum, [jnp.max(s, axis=1, keepdims=True) for s in scores])
    if sink is not None:
        m = jnp.maximum(m, sink)
    es = [jnp.exp(s - m) for s in scores]
    l = functools.reduce(jnp.add, [jnp.sum(e, axis=1, keepdims=True) for e in es])
    if sink is not None:
        l = l + jnp.exp(sink - m)
    o = None
    for e, (k, v, transposed) in zip(es, segments):
        part = _dot_nt(e.astype(BF16), v) if transposed else _dot(e.astype(BF16), v)
        o = part if o is None else o + part
    return o / l


def _level_exponent(b, m, t):
    bcast = lambda i: jnp.broadcast_to(b[i:i + 1], (SUBLANES, b.shape[1]))
    low_half = lax.broadcasted_iota(jnp.int32, (SUBLANES, b.shape[1]), 0) < SUBLANES // 2
    parts = []
    for g in range(CHUNK // SUBLANES):
        r = g * SUBLANES
        rows = b[r:r + SUBLANES]
        if m >= SUBLANES:
            ref = b[(r // (2 * m)) * 2 * m + m - 1:(r // (2 * m)) * 2 * m + m]
            parts.append(rows - ref if (r // m) % 2 == 1 else ref - rows)
        elif m == 4:
            parts.append(rows - bcast(r + 3))
        else:
            parts.append(rows - jnp.where(low_half, bcast(r + 1), bcast(r + 5)))
    d = jnp.concatenate(parts, axis=0)
    return d if m >= SUBLANES else jnp.where((t & m) != 0, d, -d)


def _odd_block_rows(x, m, groups):
    return jnp.concatenate([x[g * CHUNK + r:g * CHUNK + r + SUBLANES] for g in range(groups)
                            for r in range(0, CHUNK, SUBLANES) if (r // m) % 2 == 1], axis=0)


def _spread_odd_block_rows(y, m, groups):
    zero = jnp.zeros((SUBLANES, y.shape[1]), y.dtype)
    parts, i = [], 0
    for g in range(groups):
        for r in range(0, CHUNK, SUBLANES):
            if (r // m) % 2 == 1:
                parts.append(y[i:i + SUBLANES])
                i += SUBLANES
            else:
                parts.append(zero)
    return jnp.concatenate(parts, axis=0)


def _gla_levels(q, k, la, b, hm_a):
    t = lax.broadcasted_iota(jnp.int32, (CHUNK, D_AQK), 0)
    q_st = _stack_masked(q, hm_a)
    k16 = k.astype(BF16)
    prods = [_dot_nt(q_st.astype(BF16), k16)]
    for m in GLA_LEVELS:
        e = jnp.exp(la if m == 1 else _level_exponent(b, m, t))
        kt = k16 if m == 1 else (k * e).astype(BF16)
        if m >= SUBLANES:
            qt = _odd_block_rows(q_st, m, A_HEADS) * _tile_rows(_odd_block_rows(e, m, 1), A_HEADS)
            prods.append(_spread_odd_block_rows(_dot_nt(qt.astype(BF16), kt), m, A_HEADS))
        else:
            qt = (q_st * _tile_rows(e, A_HEADS)).astype(BF16)
            prods.append(_dot_nt(qt, kt))
    return q_st, prods


def _gla_finish(q_st, prods, k, v, b, s, lvl):
    attn = jnp.where(lvl == 0, prods[0], 0.0)
    for li in range(1, len(prods)):
        attn = jnp.where(lvl == li, prods[li], attn)
    qhat = (q_st * _tile_rows(jnp.exp(b), A_HEADS)).astype(BF16)
    o_inter = _dot(qhat, s.astype(BF16))
    v16 = v.astype(BF16)
    o_intra = jnp.concatenate(
        [_dot(attn[h * CHUNK:(h + 1) * CHUNK].astype(BF16), v16[:, h * A_DV:(h + 1) * A_DV])
         for h in range(A_HEADS)], axis=0)
    b_last = b[CHUNK - 1:CHUNK]
    khat = (k * jnp.exp(b_last - b)).astype(BF16)
    upd = _dot_tn(khat, v16)
    decay = jnp.broadcast_to(jnp.exp(b_last), (A_DV, D_AQK)).T
    s_new = s * decay + jnp.concatenate(
        [upd[h * A_DK:(h + 1) * A_DK, h * A_DV:(h + 1) * A_DV] for h in range(A_HEADS)], axis=0)
    return o_intra + o_inter, s_new


def _layer_kernel(*refs, prompt, final, nch, tq, nt):
    (x_ref, mods_ref, modp_ref, gmix_ref, gmlp_ref, gfin_ref, anorm_ref, win_ref, wa2_ref, ba2_ref, wout_ref,
     wup_ref, wdown_ref, biasb_ref, biasc_ref, sink_ref, tri_ref, lvl_ref) = refs[:18]
    pos = 18
    if not prompt:
        st_ref, cbk_ref, cbv_ref, cck_ref, ccv_ref = refs[pos:pos + 5]
        pos += 5
    y_ref, so_ref, kbo_ref, vbo_ref, kco_ref, vco_ref = refs[pos:pos + 6]
    pos += 6
    (h_scr, h2_scr, proj_scr, la_scr, b_scr, mix_scr, x1_scr, acc_scr, st_scr,
     kbw, vbw, kcw, vcw) = refs[pos:]

    step = pl.program_id(0)

    def mod(ref, c, j):
        return ref[0 if prompt else c, j:j + 1, :]

    @pl.when(step == 0)
    def _():
        h2_scr[...] = jnp.zeros_like(h2_scr)
        x1_scr[...] = jnp.zeros_like(x1_scr)
        acc_scr[...] = jnp.zeros_like(acc_scr)
        if prompt:
            st_scr[...] = jnp.zeros_like(st_scr)
            kbw[...] = jnp.zeros_like(kbw)
            vbw[...] = jnp.zeros_like(vbw)
            kcw[...] = jnp.zeros_like(kcw)
            vcw[...] = jnp.zeros_like(vcw)

    def finish_previous_tile():
        gfin = gfin_ref[...]
        for c in range(nch):
            r = slice(c * CHUNK, (c + 1) * CHUNK)
            x2 = x1_scr[r, :] + mod(modp_ref, c, 5) * acc_scr[r, :]
            y_ref[r, :] = _rms(x2, gfin) if final else x2
        acc_scr[...] = jnp.zeros_like(acc_scr)

    @pl.when(step < nt)
    def _():
        gmix = gmix_ref[...]
        for c in range(nch):
            r = slice(c * CHUNK, (c + 1) * CHUNK)
            h = _rms(x_ref[r, :], gmix * (1.0 + mod(mods_ref, c, 1))) + mod(mods_ref, c, 0)
            h_scr[r, :] = h.astype(BF16)
        hmix = h_scr[...]
        cut1, cut2 = OFF_GA, OFF_QC
        ra = _dot(hmix, win_ref[:, OFF_RA:OFF_RA + LANES])
        proj_scr[:, 0:cut1] = _dot(hmix, win_ref[:, 0:cut1])
        z = _dot(ra.astype(BF16), wa2_ref[...]) + ba2_ref[...]
        la = (jnp.minimum(z, 0.0) - jnp.log1p(jnp.exp(-jnp.abs(z)))) * (1.0 / A_GATE_TAU)
        la_scr[...] = la
        proj_scr[:, cut1:cut2] = _dot(hmix, win_ref[:, cut1:cut2])
        tri = tri_ref[...]
        hi, mid, lo = _split3(la)
        b_scr[...] = _dot(tri, hi) + _dot(tri, mid) + _dot(tri, lo)
        proj_scr[:, cut2:OFF_RA] = _dot(hmix, win_ref[:, cut2:OFF_RA])

        if prompt:
            kbw[B_WINDOW:B_WINDOW + tq, :] = proj_scr[:, OFF_KB:OFF_KB + D_BKV].astype(BF16)
            vbw[B_WINDOW:B_WINDOW + tq, :] = proj_scr[:, OFF_VB:OFF_VB + D_BKV].astype(BF16)
            kcw[C_BAND:C_BAND + tq, :] = proj_scr[:, OFF_KC:OFF_KC + D_C].astype(BF16)
            vcw[C_BAND:C_BAND + tq, :] = proj_scr[:, OFF_VC:OFF_VC + D_C].astype(BF16)

        hm_a = _head_masks(D_AQK, A_DK)
        hm_c = _head_masks(D_C, HEAD_DIM)
        hm_b = _head_masks(D_BKV, HEAD_DIM)
        anorm = anorm_ref[...]

        def chunk_body(c, carry):
            r0 = pl.multiple_of(c * CHUNK, CHUNK)
            rows = pl.ds(r0, CHUNK)

            if prompt:
                seg_b = [(kbw[pl.ds(r0, LK_B), :], vbw[pl.ds(r0, LK_B), :], False)]
                seg_c = [(kcw[pl.ds(r0, LK_C), :], vcw[pl.ds(r0, LK_C), :], False)]
                gidx = step * nch + c
                lane_b = lax.broadcasted_iota(jnp.int32, (1, LK_B), 1)
                lane_c = lax.broadcasted_iota(jnp.int32, (1, LK_C), 1)
                mask_b = jnp.where(lane_b >= B_WINDOW - gidx * CHUNK, 0.0, NEG_INF)
                mask_c = jnp.where(lane_c >= C_BAND - gidx * CHUNK, 0.0, NEG_INF)
            else:
                kb_new = proj_scr[rows, OFF_KB:OFF_KB + D_BKV]
                vb_new = proj_scr[rows, OFF_VB:OFF_VB + D_BKV]
                kc_new = proj_scr[rows, OFF_KC:OFF_KC + D_C]
                vc_new = proj_scr[rows, OFF_VC:OFF_VC + D_C]
                seg_b = [(cbk_ref[c].astype(BF16), cbv_ref[c].astype(BF16), True),
                         (kb_new.astype(BF16), vb_new.astype(BF16), False)]
                seg_c = [(cck_ref[c].astype(BF16), ccv_ref[c].astype(BF16), True),
                         (kc_new.astype(BF16), vc_new.astype(BF16), False)]
                mask_b = mask_c = None

            q = proj_scr[rows, OFF_QA:OFF_QA + D_AQK] * (A_DK ** -0.5)
            k = proj_scr[rows, OFF_KA:OFF_KA + D_AQK]
            b = b_scr[rows, :]
            q_st, prods = _gla_levels(q, k, la_scr[rows, :], b, hm_a)
            qc = proj_scr[rows, OFF_QC:OFF_QC + D_C] * (HEAD_DIM ** -0.5)
            sc_c = _scores(_stack_masked(qc, hm_c).astype(BF16), seg_c, biasc_ref[...], mask_c)
            qb = proj_scr[rows, OFF_QB:OFF_QB + B_HEADS * HEAD_DIM] * (HEAD_DIM ** -0.5)
            qb_a, qb_b = qb[:, 0:D_BKV], qb[:, D_BKV:2 * D_BKV]
            qb_st = jnp.concatenate([jnp.where(hm_b[0], qb_a, 0.0), jnp.where(hm_b[0], qb_b, 0.0),
                                     jnp.where(hm_b[1], qb_a, 0.0), jnp.where(hm_b[1], qb_b, 0.0)], axis=0)
            sc_b = _scores(qb_st.astype(BF16), seg_b, biasb_ref[...], mask_b)

            up = _dot(h2_scr[...], wup_ref[c])
            act = jnp.square(jnp.maximum(up, 0.0)).astype(BF16)

            v = proj_scr[rows, OFF_VA:OFF_VA + D_AV]
            s_old = st_scr[...] if prompt else st_ref[c]
            o_st, s_new = _gla_finish(q_st, prods, k, v, b, s_old, lvl_ref[...])
            if prompt:
                st_scr[...] = s_new
            else:
                so_ref[c] = s_new
            g = proj_scr[rows, OFF_GA:OFF_GA + D_AV]
            for h in range(A_HEADS):
                on = _rms(o_st[h * CHUNK:(h + 1) * CHUNK], anorm)
                gh = g[:, h * A_DV:(h + 1) * A_DV]
                mix_scr[rows, h * A_DV:(h + 1) * A_DV] = (on * (gh * jax.nn.sigmoid(gh))).astype(BF16)

            oc_st = _softmax_pv(sc_c, seg_c, None)
            oc = jnp.where(hm_c[0], oc_st[0:CHUNK], 0.0)
            for h in range(1, C_HEADS):
                oc = jnp.where(hm_c[h], oc_st[h * CHUNK:(h + 1) * CHUNK], oc)
            mix_scr[rows, D_AV + 2 * D_BKV:D_MIX] = oc.astype(BF16)

            ob_st = _softmax_pv(sc_b, seg_b, sink_ref[...])
            ob_a = jnp.where(hm_b[0], ob_st[0:CHUNK], ob_st[2 * CHUNK:3 * CHUNK])
            ob_b = jnp.where(hm_b[0], ob_st[CHUNK:2 * CHUNK], ob_st[3 * CHUNK:4 * CHUNK])
            mix_scr[rows, D_AV:D_AV + D_BKV] = ob_a.astype(BF16)
            mix_scr[rows, D_AV + D_BKV:D_AV + 2 * D_BKV] = ob_b.astype(BF16)

            acc_scr[...] += _dot(act, wdown_ref[c])

            if not prompt:
                for dst, src, new, width in ((kbo_ref, cbk_ref, kb_new, B_WINDOW), (vbo_ref, cbv_ref, vb_new, B_WINDOW),
                                             (kco_ref, cck_ref, kc_new, C_BAND), (vco_ref, ccv_ref, vc_new, C_BAND)):
                    dst[c, :, 0:width - CHUNK] = src[c, :, CHUNK:width]
                    dst[c, :, width - CHUNK:width] = new.T
            return carry

        lax.fori_loop(0, nch, chunk_body, 0)

        if prompt:
            kbw[0:B_WINDOW, :] = kbw[tq:tq + B_WINDOW, :]
            vbw[0:B_WINDOW, :] = vbw[tq:tq + B_WINDOW, :]
            kcw[0:C_BAND, :] = kcw[tq:tq + C_BAND, :]
            vcw[0:C_BAND, :] = vcw[tq:tq + C_BAND, :]

            so_ref[...] = st_scr[...]
            kbo_ref[...] = proj_scr[tq - B_WINDOW:tq, OFF_KB:OFF_KB + D_BKV]
            vbo_ref[...] = proj_scr[tq - B_WINDOW:tq, OFF_VB:OFF_VB + D_BKV]
            rb = kco_ref.shape[0]
            kco_ref[...] = proj_scr[tq - rb:tq, OFF_KC:OFF_KC + D_C]
            vco_ref[...] = proj_scr[tq - rb:tq, OFF_VC:OFF_VC + D_C]

        mixed = _dot(mix_scr[...], wout_ref[...])

        finish_previous_tile()

        gmlp = gmlp_ref[...]
        for c in range(nch):
            r = slice(c * CHUNK, (c + 1) * CHUNK)
            x1 = x_ref[r, :] + mod(mods_ref, c, 2) * mixed[r, :]
            x1_scr[r, :] = x1
            h2 = _rms(x1, gmlp * (1.0 + mod(mods_ref, c, 4))) + mod(mods_ref, c, 3)
            h2_scr[r, :] = h2.astype(BF16)

    @pl.when(step == nt)
    def _():
        def mlp_block(c, carry):
            up = _dot(h2_scr[...], wup_ref[c])
            act = jnp.square(jnp.maximum(up, 0.0)).astype(BF16)
            acc_scr[...] += _dot(act, wdown_ref[c])
            return carry

        lax.fori_loop(0, nch, mlp_block, 0)
        finish_previous_tile()


def _layer_spec(arr, layer):
    nd = arr.ndim - 1
    return pl.BlockSpec((None,) + arr.shape[1:], lambda i, _l=layer, _nd=nd: (_l,) + (0,) * _nd,
                        pipeline_mode=pl.Buffered(1))


def _const_spec(arr):
    nd = arr.ndim
    return pl.BlockSpec(arr.shape, lambda i, _nd=nd: (0,) * _nd, pipeline_mode=pl.Buffered(1))


def _run_layer(x, mods, layer, stacked, shared, caches, *, prompt, final, tq, nb):
    rows = x.shape[0]
    nch = tq // CHUNK
    nt = rows // tq
    assert rows % tq == 0 and nch == N_FF
    (gmix, gmlp, anorm, win, wa2, ba2, wout, wup, wdown, biasc, sink) = stacked
    (gfin, biasb, tri, lvl) = shared
    cur = lambda i: jnp.minimum(i, nt - 1)
    prv = lambda i: jnp.maximum(i - 1, 0)

    nmod = 1 if prompt else nb
    in_specs = [pl.BlockSpec((tq, D_MODEL), lambda i: (cur(i), 0)),
                pl.BlockSpec((nmod, SUBLANES, D_MODEL), lambda i: (0 if prompt else cur(i), 0, 0)),
                pl.BlockSpec((nmod, SUBLANES, D_MODEL), lambda i: (0 if prompt else prv(i), 0, 0))]
    ls = functools.partial(_layer_spec, layer=layer)
    in_specs += [ls(gmix), ls(gmlp), _const_spec(gfin), ls(anorm), ls(win), ls(wa2), ls(ba2), ls(wout),
                 ls(wup), ls(wdown), _const_spec(biasb), ls(biasc), ls(sink), _const_spec(tri), _const_spec(lvl)]
    args = [x, mods, mods, gmix, gmlp, gfin, anorm, win, wa2, ba2, wout, wup, wdown, biasb, biasc, sink, tri, lvl]

    y_spec = pl.BlockSpec((tq, D_MODEL), lambda i: (prv(i), 0))
    if prompt:
        assert tq >= B_WINDOW and (tq % C_BAND == 0 or C_BAND % tq == 0)
        rb = min(tq, C_BAND)
        first = nt - C_BAND // rb
        out_shape = [jax.ShapeDtypeStruct((rows, D_MODEL), F32),
                     jax.ShapeDtypeStruct((D_AQK, A_DV), F32),
                     jax.ShapeDtypeStruct((B_WINDOW, D_BKV), F32),
                     jax.ShapeDtypeStruct((B_WINDOW, D_BKV), F32),
                     jax.ShapeDtypeStruct((C_BAND, D_C), F32),
                     jax.ShapeDtypeStruct((C_BAND, D_C), F32)]
        band = pl.BlockSpec((rb, D_C), lambda i: (jnp.maximum(cur(i) - first, 0), 0))
        out_specs = [y_spec,
                     pl.BlockSpec((D_AQK, A_DV), lambda i: (0, 0)),
                     pl.BlockSpec((B_WINDOW, D_BKV), lambda i: (0, 0)),
                     pl.BlockSpec((B_WINDOW, D_BKV), lambda i: (0, 0)),
                     band, band]
        kbw_rows, kcw_rows = B_WINDOW + tq, C_BAND + tq
    else:
        assert nb == nch
        nseq = rows // CHUNK
        st, cbk, cbv, cck, ccv = caches
        seq_in = lambda r, w: pl.BlockSpec((None, nb, r, w), lambda i, _l=layer: (_l, cur(i), 0, 0))
        seq_out = lambda r, w: pl.BlockSpec((nb, r, w), lambda i: (cur(i), 0, 0))
        in_specs += [seq_in(D_AQK, A_DV), seq_in(D_BKV, B_WINDOW), seq_in(D_BKV, B_WINDOW),
                     seq_in(D_C, C_BAND), seq_in(D_C, C_BAND)]
        args += [st, cbk, cbv, cck, ccv]
        out_shape = [jax.ShapeDtypeStruct((rows, D_MODEL), F32),
                     jax.ShapeDtypeStruct((nseq, D_AQK, A_DV), F32),
                     jax.ShapeDtypeStruct((nseq, D_BKV, B_WINDOW), F32),
                     jax.ShapeDtypeStruct((nseq, D_BKV, B_WINDOW), F32),
                     jax.ShapeDtypeStruct((nseq, D_C, C_BAND), F32),
                     jax.ShapeDtypeStruct((nseq, D_C, C_BAND), F32)]
        out_specs = [y_spec, seq_out(D_AQK, A_DV), seq_out(D_BKV, B_WINDOW), seq_out(D_BKV, B_WINDOW),
                     seq_out(D_C, C_BAND), seq_out(D_C, C_BAND)]
        kbw_rows, kcw_rows = SUBLANES * 2, SUBLANES * 2

    scratch = [pltpu.VMEM((tq, D_MODEL), BF16),
               pltpu.VMEM((tq, D_MODEL), BF16),
               pltpu.VMEM((tq, OFF_RA), F32),
               pltpu.VMEM((tq, D_AQK), F32),
               pltpu.VMEM((tq, D_AQK), F32),
               pltpu.VMEM((tq, D_MIX), BF16),
               pltpu.VMEM((tq, D_MODEL), F32),
               pltpu.VMEM((tq, D_MODEL), F32),
               pltpu.VMEM((D_AQK, A_DV), F32),
               pltpu.VMEM((kbw_rows, D_BKV), BF16), pltpu.VMEM((kbw_rows, D_BKV), BF16),
               pltpu.VMEM((kcw_rows, D_C), BF16), pltpu.VMEM((kcw_rows, D_C), BF16)]

    kern = functools.partial(_layer_kernel, prompt=prompt, final=final, nch=nch, tq=tq, nt=nt)
    return pl.pallas_call(
        kern,
        grid=(nt + 1,),
        in_specs=in_specs,
        out_specs=out_specs,
        out_shape=out_shape,
        scratch_shapes=scratch,
        compiler_params=pltpu.CompilerParams(
            dimension_semantics=("arbitrary",),
            vmem_limit_bytes=VMEM_LIMIT_BYTES if prompt else VMEM_LIMIT_SAMPLE_BYTES),
        name=("layer_prompt" if prompt else "layer_sample") + ("_final" if final else ""),
    )(*args)


ADA_BLOCK = 1536


def _ada_kernel(c_ref, w_ref, b_ref, o_ref):
    c = c_ref[...]
    sc = c * jax.nn.sigmoid(c)
    sc_hi = sc.astype(BF16)
    sc_lo = (sc - sc_hi.astype(F32)).astype(BF16)
    w = w_ref[0]
    w_hi = w.astype(BF16)
    w_lo = (w - w_hi.astype(F32)).astype(BF16)
    o_ref[0] = _dot(sc_hi, w_hi) + _dot(sc_lo, w_hi) + _dot(sc_hi, w_lo) + b_ref[0]


def _ada(c_all, w_ada, b_ada):
    rows = c_all.shape[0]
    n = 6 * D_MODEL
    return pl.pallas_call(
        _ada_kernel,
        grid=(DEPTH, n // ADA_BLOCK),
        in_specs=[pl.BlockSpec((rows, D_MODEL), lambda l, j: (0, 0)),
                  pl.BlockSpec((1, D_MODEL, ADA_BLOCK), lambda l, j: (l, 0, j)),
                  pl.BlockSpec((1, 1, ADA_BLOCK), lambda l, j: (l, 0, j))],
        out_specs=pl.BlockSpec((1, rows, ADA_BLOCK), lambda l, j: (l, 0, j)),
        out_shape=jax.ShapeDtypeStruct((DEPTH, rows, n), F32),
        compiler_params=pltpu.CompilerParams(dimension_semantics=("arbitrary", "arbitrary"),
                                             vmem_limit_bytes=VMEM_LIMIT_BYTES),
        name="adaln",
    )(c_all, w_ada, b_ada.reshape(DEPTH, 1, n))


TB_W = 2 * LANES
TC_W = 5 * LANES
TC_FLAT = TC_W - (CHUNK + C_CLIP + 1)


def _shear(f, width):
    tiled = jnp.broadcast_to(f, (CHUNK, width))
    return pltpu.roll(tiled, width - (CHUNK - 1), 1, stride=1, stride_axis=0)


def _bias_kernel(t5_ref, crel_ref, bucket_ref, ob_ref, oc_ref):
    bucket = bucket_ref[...]
    for h in range(B_HEADS):
        def b_body(i, acc, h=h):
            return jnp.where(bucket == i, t5_ref[h, i], acc)
        f = lax.fori_loop(0, T5_BUCKETS, b_body, jnp.zeros(bucket.shape, F32))
        ob_ref[h] = _shear(f, TB_W)[:, 0:LK_B]
    lane = lax.broadcasted_iota(jnp.int32, (1, TC_W), 1)
    for l in range(DEPTH):
        for h in range(C_HEADS):
            row = crel_ref[l, h:h + 1, :]
            f = jnp.where(lane < TC_FLAT, row[:, 0:1], pltpu.roll(row, TC_FLAT, 1))
            oc_ref[l, h] = _shear(f, TC_W)[:, 0:LK_C]


def _bias_tables(t5_bias, c_rel_bias, bucket):
    smem = pl.BlockSpec(memory_space=pltpu.SMEM)
    vmem = pl.BlockSpec(memory_space=pltpu.VMEM)
    crel = jnp.pad(jnp.swapaxes(c_rel_bias, 1, 2),
                   ((0, 0), (0, SUBLANES - C_HEADS), (0, TC_W - (2 * C_CLIP + 1))))
    return pl.pallas_call(
        _bias_kernel,
        in_specs=[smem, vmem, vmem],
        out_specs=[vmem, vmem],
        out_shape=[jax.ShapeDtypeStruct((B_HEADS, CHUNK, LK_B), F32),
                   jax.ShapeDtypeStruct((DEPTH, C_HEADS, CHUNK, LK_C), F32)],
        name="bias_tables",
    )(t5_bias.T, crel, bucket)


def _t5_bucket(rel):
    half = T5_BUCKETS // 2
    max_exact = half // 2
    steps = half - max_exact
    thresholds = [int(np.ceil(max_exact * (T5_MAX_DIST / max_exact) ** (k / steps) - 1e-9)) for k in range(1, steps)]
    n = jnp.abs(rel)
    large = jnp.minimum(max_exact + sum((n >= t).astype(jnp.int32) for t in thresholds), half - 1)
    return jnp.where(rel > 0, half, 0) + jnp.where(n < max_exact, n, large)


def _gla_constants(tq):
    r = np.arange(tq)
    tri = ((r[:, None] // CHUNK == r[None, :] // CHUNK) & (r[None, :] <= r[:, None])).astype(np.float32)
    t = np.arange(CHUNK)[:, None]
    s = np.arange(CHUNK)[None, :]
    lvl = np.full((CHUNK, CHUNK), -1, np.int32)
    lvl[t == s] = 0
    for li, m in enumerate(GLA_LEVELS):
        sel = ((t // m) % 2 == 1) & ((s // m) == (t // m) - 1)
        lvl[np.broadcast_to(sel, lvl.shape)] = li + 1
    return jnp.asarray(tri, BF16), jnp.asarray(np.tile(lvl, (A_HEADS, 1)))


SRC_RA = 2 * D_AQK + 2 * D_AV
SRC_QB = SRC_RA + A_GATE_RANK
SRC_KB = SRC_QB + B_HEADS * HEAD_DIM
D_IN = SRC_KB + 2 * D_BKV + 3 * D_C
PREP_ROWS_IN = 512
QB_HEAD_ORDER = (0, 2, 1, 3)


def _w_in_kernel(w_ref, o_ref):
    def put(dst, src, width):
        o_ref[:, dst:dst + width] = w_ref[:, src:src + width].astype(BF16)

    put(OFF_QA, 0, SRC_RA)
    for i, h in enumerate(QB_HEAD_ORDER):
        put(OFF_QB + i * HEAD_DIM, SRC_QB + h * HEAD_DIM, HEAD_DIM)
    put(OFF_KB, SRC_KB, D_IN - SRC_KB)
    put(OFF_RA, SRC_RA, A_GATE_RANK)
    o_ref[:, OFF_RA + A_GATE_RANK:P_IN] = jnp.zeros((o_ref.shape[0], LANES - A_GATE_RANK), BF16)


def _prep_w_in(w_in):
    return pl.pallas_call(
        _w_in_kernel,
        grid=(DEPTH, D_MODEL // PREP_ROWS_IN),
        in_specs=[pl.BlockSpec((None, PREP_ROWS_IN, D_IN), lambda l, r: (l, r, 0))],
        out_specs=pl.BlockSpec((None, PREP_ROWS_IN, P_IN), lambda l, r: (l, r, 0)),
        out_shape=jax.ShapeDtypeStruct((DEPTH, D_MODEL, P_IN), BF16),
        compiler_params=pltpu.CompilerParams(dimension_semantics=("arbitrary", "arbitrary"),
                                             vmem_limit_bytes=VMEM_LIMIT_BYTES),
        name="prep_w_in",
    )(w_in)


def _cast_kernel(w_ref, o_ref):
    o_ref[...] = w_ref[...].astype(BF16)


def _prep_w_up(w_up):
    return pl.pallas_call(
        _cast_kernel,
        grid=(DEPTH, N_FF),
        in_specs=[pl.BlockSpec((None, D_MODEL, FF_BLOCK), lambda l, j: (l, 0, j))],
        out_specs=pl.BlockSpec((None, None, D_MODEL, FF_BLOCK), lambda l, j: (l, j, 0, 0)),
        out_shape=jax.ShapeDtypeStruct((DEPTH, N_FF, D_MODEL, FF_BLOCK), BF16),
        compiler_params=pltpu.CompilerParams(dimension_semantics=("arbitrary", "arbitrary"),
                                             vmem_limit_bytes=VMEM_LIMIT_BYTES),
        name="prep_w_up",
    )(w_up)


TQ_PROMPT = 256
NB_SAMPLE = 4


def kernel(x_prompt, x_sample, c_prompt, c_sample, state_gla, cache_b_k, cache_b_v, cache_c_k, cache_c_v,
           w_ada, b_ada, norm_mix_g, norm_mlp_g, w_in, w_a2, b_a2, a_norm_g, b_sink, t5_bias, c_rel_bias,
           w_out, w_up, w_down, final_norm_g):
    bsz, seq, _ = x_prompt.shape
    dec_b, dec_s, _ = x_sample.shape
    assert bsz == 1 and dec_s == CHUNK and TQ_PROMPT == NB_SAMPLE * CHUNK

    n_c = bsz + dec_b
    c_rows = -(-n_c // SUBLANES) * SUBLANES
    c_all = jnp.concatenate([c_prompt, c_sample, jnp.zeros((c_rows - n_c, D_MODEL), F32)], axis=0)
    mods = _ada(c_all, w_ada, b_ada).reshape(DEPTH, c_rows, 6, D_MODEL)
    mods = jnp.pad(mods, ((0, 0), (0, 0), (0, SUBLANES - 6), (0, 0)))

    rel_b = jnp.arange(TB_W) - (CHUNK - 1) - B_WINDOW
    bias_b, bias_c = _bias_tables(t5_bias, c_rel_bias, _t5_bucket(rel_b).astype(jnp.int32)[None, :])
    bias_b = bias_b.reshape(STACK, LK_B)
    bias_c = bias_c.reshape(DEPTH, STACK, LK_C)

    tri, lvl = _gla_constants(TQ_PROMPT)
    ob0 = D_AV
    orow = lambda h: slice(ob0 + h * HEAD_DIM, ob0 + (h + 1) * HEAD_DIM)

    win = _prep_w_in(w_in)
    wa2 = jnp.concatenate([w_a2, jnp.zeros((DEPTH, LANES - A_GATE_RANK, D_AQK), F32)], axis=1).astype(BF16)
    wout = jnp.concatenate([w_out[:, 0:D_AV], w_out[:, orow(0)], w_out[:, orow(2)], w_out[:, orow(1)],
                            w_out[:, orow(3)], w_out[:, D_AV + B_HEADS * HEAD_DIM:]], axis=1).astype(BF16)
    wup = _prep_w_up(w_up)
    wdown = w_down.astype(BF16).reshape(DEPTH, N_FF, FF_BLOCK, D_MODEL)
    sink = jnp.repeat(b_sink, CHUNK, axis=1)[:, :, None]
    stacked = (norm_mix_g[:, None, :], norm_mlp_g[:, None, :], a_norm_g[:, None, :], win, wa2,
               b_a2[:, None, :], wout, wup, wdown, bias_c, sink)
    shared = (final_norm_g[None, :], bias_b, tri, lvl)

    x_p = x_prompt.reshape(seq, D_MODEL)
    x_s = x_sample.reshape(dec_b * dec_s, D_MODEL)
    outs_p, outs_s = [], []
    fmajor = lambda cache: jnp.transpose(cache, (0, 1, 3, 4, 2)).reshape(
        DEPTH, dec_b, cache.shape[3] * cache.shape[4], cache.shape[2])
    caches = (state_gla.reshape(DEPTH, dec_b, D_AQK, A_DV), fmajor(cache_b_k), fmajor(cache_b_v),
              fmajor(cache_c_k), fmajor(cache_c_v))
    for l in range(DEPTH):
        final = l == DEPTH - 1
        res_p = _run_layer(x_p, mods[l, 0:bsz], l, stacked, shared, None, prompt=True, final=final,
                           tq=TQ_PROMPT, nb=None)
        x_p = res_p[0]
        outs_p.append(res_p[1:])

        res_s = _run_layer(x_s, mods[l, bsz:bsz + dec_b], l, stacked, shared, caches, prompt=False,
                           final=final, tq=NB_SAMPLE * CHUNK, nb=NB_SAMPLE)
        x_s = res_s[0]
        outs_s.append(res_s[1:])

    sg_p = jnp.stack([o[0].reshape(bsz, A_HEADS, A_DK, A_DV) for o in outs_p])
    kb_p = jnp.stack([o[1].reshape(bsz, B_WINDOW, B_KV_HEADS, HEAD_DIM) for o in outs_p])
    vb_p = jnp.stack([o[2].reshape(bsz, B_WINDOW, B_KV_HEADS, HEAD_DIM) for o in outs_p])
    kc_p = jnp.stack([o[3].reshape(bsz, C_BAND, C_HEADS, HEAD_DIM) for o in outs_p])
    vc_p = jnp.stack([o[4].reshape(bsz, C_BAND, C_HEADS, HEAD_DIM) for o in outs_p])
    sg_s = jnp.stack([o[0].reshape(dec_b, A_HEADS, A_DK, A_DV) for o in outs_s])
    tmajor = lambda arrs, heads: jnp.transpose(
        jnp.stack(arrs).reshape(DEPTH, dec_b, heads, HEAD_DIM, -1), (0, 1, 4, 2, 3))
    kb_s = tmajor([o[1] for o in outs_s], B_KV_HEADS)
    vb_s = tmajor([o[2] for o in outs_s], B_KV_HEADS)
    kc_s = tmajor([o[3] for o in outs_s], C_HEADS)
    vc_s = tmajor([o[4] for o in outs_s], C_HEADS)
    return (x_p.reshape(bsz, seq, D_MODEL), x_s.reshape(dec_b, dec_s, D_MODEL),
            sg_p, kb_p, vb_p, kc_p, vc_p, sg_s, kb_s, vb_s, kc_s, vc_s)
```

```python
import functools

import numpy as np
import jax
import jax.numpy as jnp
from jax import lax
from jax.experimental import pallas as pl
from jax.experimental.pallas import tpu as pltpu

D_MODEL = 1024
DEPTH = 2
CHUNK = 64
HEAD_DIM = 64
A_HEADS = 4
A_DK = 64
A_DV = 128
A_GATE_RANK = 16
A_GATE_TAU = 16.0
B_HEADS = 4
B_KV_HEADS = 2
B_WINDOW = 128
C_HEADS = 4
C_BAND = 512
C_CLIP = 256
T5_BUCKETS = 32
T5_MAX_DIST = 128
D_FF = 4 * D_MODEL
NORM_EPS = 1e-6
NEG_INF = -1e30

LANES = 128
SUBLANES = 8
VMEM_LIMIT_BYTES = 60000 * 1024
VMEM_LIMIT_SAMPLE_BYTES = 62 * 1024 * 1024

OFF_QA = 0
OFF_KA = OFF_QA + A_HEADS * A_DK
OFF_VA = OFF_KA + A_HEADS * A_DK
OFF_GA = OFF_VA + A_HEADS * A_DV
OFF_QB = OFF_GA + A_HEADS * A_DV
OFF_KB = OFF_QB + B_HEADS * HEAD_DIM
OFF_VB = OFF_KB + B_KV_HEADS * HEAD_DIM
OFF_QC = OFF_VB + B_KV_HEADS * HEAD_DIM
OFF_KC = OFF_QC + C_HEADS * HEAD_DIM
OFF_VC = OFF_KC + C_HEADS * HEAD_DIM
OFF_RA = OFF_VC + C_HEADS * HEAD_DIM
P_IN = OFF_RA + LANES
D_AQK = A_HEADS * A_DK
D_AV = A_HEADS * A_DV
D_BKV = B_KV_HEADS * HEAD_DIM
D_C = C_HEADS * HEAD_DIM
D_MIX = D_AV + B_HEADS * HEAD_DIM + D_C
LK_B = B_WINDOW + CHUNK
LK_C = C_BAND + CHUNK
STACK = 4 * CHUNK
GLA_LEVELS = (1, 2, 4, 8, 16, 32)
FF_BLOCK = 1024
N_FF = D_FF // FF_BLOCK
N_SEQ_OUT = 5

BF16 = jnp.bfloat16
F32 = jnp.float32


def _dot(a, b):
    return jnp.dot(a, b, preferred_element_type=F32)


def _dot_nt(a, b):
    return lax.dot_general(a, b, (((1,), (1,)), ((), ())), preferred_element_type=F32)


def _dot_tn(a, b):
    return lax.dot_general(a, b, (((0,), (0,)), ((), ())), preferred_element_type=F32)


def _rms(x, g):
    return x * lax.rsqrt(jnp.mean(x * x, axis=-1, keepdims=True) + NORM_EPS) * g


def _split3(x):
    hi = x.astype(BF16)
    r1 = x - hi.astype(F32)
    mid = r1.astype(BF16)
    lo = (r1 - mid.astype(F32)).astype(BF16)
    return hi, mid, lo


def _head_masks(width, per_head):
    lane = lax.broadcasted_iota(jnp.int32, (1, width), 1)
    return [(lane >= h * per_head) & (lane < (h + 1) * per_head) for h in range(width // per_head)]


def _stack_masked(x, masks):
    return jnp.concatenate([jnp.where(m, x, 0.0) for m in masks], axis=0)


def _tile_rows(x, n):
    return jnp.concatenate([x] * n, axis=0)


def _scores(q_st, segments, bias_st, maskrow):
    scores, off = [], 0
    for k, v, transposed in segments:
        n = k.shape[1] if transposed else k.shape[0]
        s = (_dot(q_st, k) if transposed else _dot_nt(q_st, k)) + bias_st[:, off:off + n]
        if maskrow is not None:
            s = s + maskrow[:, off:off + n]
        scores.append(s)
        off += n
    return scores


def _softmax_pv(scores, segments, sink):
    m = functools.reduce(jnp.maximum, [jnp.max(s, axis=1, keepdims=True) for s in scores])
    if sink is not None:
        m = jnp.maximum(m, sink)
    es = [jnp.exp(s - m) for s in scores]
    l = functools.reduce(jnp.add, [jnp.sum(e, axis=1, keepdims=True) for e in es])
    if sink is not None:
        l = l + jnp.exp(sink - m)
    o = None
    for e, (k, v, transposed) in zip(es, segments):
        part = _dot_nt(e.astype(BF16), v) if transposed else _dot(e.astype(BF16), v)
        o = part if o is None else o + part
    return o / l


def _level_exponent(b, m, t):
    bcast = lambda i: jnp.broadcast_to(b[i:i + 1], (SUBLANES, b.shape[1]))
    low_half = lax.broadcasted_iota(jnp.int32, (SUBLANES, b.shape[1]), 0) < SUBLANES // 2
    parts = []
    for g in range(CHUNK // SUBLANES):
        r = g * SUBLANES
        rows = b[r:r + SUBLANES]
        if m >= SUBLANES:
            ref = b[(r // (2 * m)) * 2 * m + m - 1:(r // (2 * m)) * 2 * m + m]
            parts.append(rows - ref if (r // m) % 2 == 1 else ref - rows)
        elif m == 4:
            parts.append(rows - bcast(r + 3))
        else:
            parts.append(rows - jnp.where(low_half, bcast(r + 1), bcast(r + 5)))
    d = jnp.concatenate(parts, axis=0)
    return d if m >= SUBLANES else jnp.where((t & m) != 0, d, -d)


def _odd_block_rows(x, m, groups):
    return jnp.concatenate([x[g * CHUNK + r:g * CHUNK + r + SUBLANES] for g in range(groups)
                            for r in range(0, CHUNK, SUBLANES) if (r // m) % 2 == 1], axis=0)


def _spread_odd_block_rows(y, m, groups):
    zero = jnp.zeros((SUBLANES, y.shape[1]), y.dtype)
    parts, i = [], 0
    for g in range(groups):
        for r in range(0, CHUNK, SUBLANES):
            if (r // m) % 2 == 1:
                parts.append(y[i:i + SUBLANES])
                i += SUBLANES
            else:
                parts.append(zero)
    return jnp.concatenate(parts, axis=0)


def _gla_levels(q, k, la, b, hm_a):
    t = lax.broadcasted_iota(jnp.int32, (CHUNK, D_AQK), 0)
    q_st = _stack_masked(q, hm_a)
    k16 = k.astype(BF16)
    prods = [_dot_nt(q_st.astype(BF16), k16)]
    for m in GLA_LEVELS:
        e = jnp.exp(la if m == 1 else _level_exponent(b, m, t))
        kt = k16 if m == 1 else (k * e).astype(BF16)
        if m >= SUBLANES:
            qt = _odd_block_rows(q_st, m, A_HEADS) * _tile_rows(_odd_block_rows(e, m, 1), A_HEADS)
            prods.append(_spread_odd_block_rows(_dot_nt(qt.astype(BF16), kt), m, A_HEADS))
        else:
            qt = (q_st * _tile_rows(e, A_HEADS)).astype(BF16)
            prods.append(_dot_nt(qt, kt))
    return q_st, prods


def _gla_finish(q_st, prods, k, v, b, s, lvl):
    attn = jnp.where(lvl == 0, prods[0], 0.0)
    for li in range(1, len(prods)):
        attn = jnp.where(lvl == li, prods[li], attn)
    qhat = (q_st * _tile_rows(jnp.exp(b), A_HEADS)).astype(BF16)
    o_inter = _dot(qhat, s.astype(BF16))
    v16 = v.astype(BF16)
    o_intra = jnp.concatenate(
        [_dot(attn[h * CHUNK:(h + 1) * CHUNK].astype(BF16), v16[:, h * A_DV:(h + 1) * A_DV])
         for h in range(A_HEADS)], axis=0)
    b_last = b[CHUNK - 1:CHUNK]
    khat = (k * jnp.exp(b_last - b)).astype(BF16)
    upd = _dot_tn(khat, v16)
    decay = jnp.broadcast_to(jnp.exp(b_last), (A_DV, D_AQK)).T
    s_new = s * decay + jnp.concatenate(
        [upd[h * A_DK:(h + 1) * A_DK, h * A_DV:(h + 1) * A_DV] for h in range(A_HEADS)], axis=0)
    return o_intra + o_inter, s_new


def _layer_kernel(*refs, prompt, final, nch, tq, nt):
    (x_ref, mods_ref, modp_ref, gmix_ref, gmlp_ref, gfin_ref, anorm_ref, win_ref, wa2_ref, ba2_ref, wout_ref,
     wup_ref, wdown_ref, biasb_ref, biasc_ref, sink_ref, tri_ref, lvl_ref) = refs[:18]
    pos = 18
    if not prompt:
        st_ref, cbk_ref, cbv_ref, cck_ref, ccv_ref = refs[pos:pos + 5]
        pos += 5 + N_SEQ_OUT
    y_ref, so_ref, kbo_ref, vbo_ref, kco_ref, vco_ref = refs[pos:pos + 6]
    pos += 6
    (h_scr, h2_scr, proj_scr, la_scr, b_scr, mix_scr, x1_scr, acc_scr, st_scr,
     kbw, vbw, kcw, vcw) = refs[pos:]

    step = pl.program_id(0)

    def mod(ref, c, j):
        return ref[0 if prompt else c, j:j + 1, :]

    @pl.when(step == 0)
    def _():
        h2_scr[...] = jnp.zeros_like(h2_scr)
        x1_scr[...] = jnp.zeros_like(x1_scr)
        acc_scr[...] = jnp.zeros_like(acc_scr)
        if prompt:
            st_scr[...] = jnp.zeros_like(st_scr)
            kbw[...] = jnp.zeros_like(kbw)
            vbw[...] = jnp.zeros_like(vbw)
            kcw[...] = jnp.zeros_like(kcw)
            vcw[...] = jnp.zeros_like(vcw)

    def finish_previous_tile():
        gfin = gfin_ref[...]
        for c in range(nch):
            r = slice(c * CHUNK, (c + 1) * CHUNK)
            x2 = x1_scr[r, :] + mod(modp_ref, c, 5) * acc_scr[r, :]
            y_ref[r, :] = _rms(x2, gfin) if final else x2
        acc_scr[...] = jnp.zeros_like(acc_scr)

    @pl.when(step < nt)
    def _():
        gmix = gmix_ref[...]
        for c in range(nch):
            r = slice(c * CHUNK, (c + 1) * CHUNK)
            h = _rms(x_ref[r, :], gmix * (1.0 + mod(mods_ref, c, 1))) + mod(mods_ref, c, 0)
            h_scr[r, :] = h.astype(BF16)
        hmix = h_scr[...]
        cut1, cut2 = OFF_GA, OFF_QC
        ra = _dot(hmix, win_ref[:, OFF_RA:OFF_RA + LANES])
        proj_scr[:, 0:cut1] = _dot(hmix, win_ref[:, 0:cut1])
        z = _dot(ra.astype(BF16), wa2_ref[...]) + ba2_ref[...]
        la = (jnp.minimum(z, 0.0) - jnp.log1p(jnp.exp(-jnp.abs(z)))) * (1.0 / A_GATE_TAU)
        la_scr[...] = la
        proj_scr[:, cut1:cut2] = _dot(hmix, win_ref[:, cut1:cut2])
        tri = tri_ref[...]
        hi, mid, lo = _split3(la)
        b_scr[...] = _dot(tri, hi) + _dot(tri, mid) + _dot(tri, lo)
        proj_scr[:, cut2:OFF_RA] = _dot(hmix, win_ref[:, cut2:OFF_RA])

        if prompt:
            kbw[B_WINDOW:B_WINDOW + tq, :] = proj_scr[:, OFF_KB:OFF_KB + D_BKV].astype(BF16)
            vbw[B_WINDOW:B_WINDOW + tq, :] = proj_scr[:, OFF_VB:OFF_VB + D_BKV].astype(BF16)
            kcw[C_BAND:C_BAND + tq, :] = proj_scr[:, OFF_KC:OFF_KC + D_C].astype(BF16)
            vcw[C_BAND:C_BAND + tq, :] = proj_scr[:, OFF_VC:OFF_VC + D_C].astype(BF16)

        hm_a = _head_masks(D_AQK, A_DK)
        hm_c = _head_masks(D_C, HEAD_DIM)
        hm_b = _head_masks(D_BKV, HEAD_DIM)
        anorm = anorm_ref[...]

        def chunk_body(c, carry):
            r0 = pl.multiple_of(c * CHUNK, CHUNK)
            rows = pl.ds(r0, CHUNK)

            if prompt:
                seg_b = [(kbw[pl.ds(r0, LK_B), :], vbw[pl.ds(r0, LK_B), :], False)]
                seg_c = [(kcw[pl.ds(r0, LK_C), :], vcw[pl.ds(r0, LK_C), :], False)]
                gidx = step * nch + c
                lane_b = lax.broadcasted_iota(jnp.int32, (1, LK_B), 1)
                lane_c = lax.broadcasted_iota(jnp.int32, (1, LK_C), 1)
                mask_b = jnp.where(lane_b >= B_WINDOW - gidx * CHUNK, 0.0, NEG_INF)
                mask_c = jnp.where(lane_c >= C_BAND - gidx * CHUNK, 0.0, NEG_INF)
            else:
                kb_new = proj_scr[rows, OFF_KB:OFF_KB + D_BKV]
                vb_new = proj_scr[rows, OFF_VB:OFF_VB + D_BKV]
                kc_new = proj_scr[rows, OFF_KC:OFF_KC + D_C]
                vc_new = proj_scr[rows, OFF_VC:OFF_VC + D_C]
                seg_b = [(cbk_ref[c].astype(BF16), cbv_ref[c].astype(BF16), True),
                         (kb_new.astype(BF16), vb_new.astype(BF16), False)]
                seg_c = [(cck_ref[c].astype(BF16), ccv_ref[c].astype(BF16), True),
                         (kc_new.astype(BF16), vc_new.astype(BF16), False)]
                mask_b = mask_c = None

            q = proj_scr[rows, OFF_QA:OFF_QA + D_AQK] * (A_DK ** -0.5)
            k = proj_scr[rows, OFF_KA:OFF_KA + D_AQK]
            b = b_scr[rows, :]
            q_st, prods = _gla_levels(q, k, la_scr[rows, :], b, hm_a)
            qc = proj_scr[rows, OFF_QC:OFF_QC + D_C] * (HEAD_DIM ** -0.5)
            sc_c = _scores(_stack_masked(qc, hm_c).astype(BF16), seg_c, biasc_ref[...], mask_c)
            qb = proj_scr[rows, OFF_QB:OFF_QB + B_HEADS * HEAD_DIM] * (HEAD_DIM ** -0.5)
            qb_a, qb_b = qb[:, 0:D_BKV], qb[:, D_BKV:2 * D_BKV]
            qb_st = jnp.concatenate([jnp.where(hm_b[0], qb_a, 0.0), jnp.where(hm_b[0], qb_b, 0.0),
                                     jnp.where(hm_b[1], qb_a, 0.0), jnp.where(hm_b[1], qb_b, 0.0)], axis=0)
            sc_b = _scores(qb_st.astype(BF16), seg_b, biasb_ref[...], mask_b)

            up = _dot(h2_scr[...], wup_ref[c])
            act = jnp.square(jnp.maximum(up, 0.0)).astype(BF16)

            v = proj_scr[rows, OFF_VA:OFF_VA + D_AV]
            s_old = st_scr[...] if prompt else st_ref[c]
            o_st, s_new = _gla_finish(q_st, prods, k, v, b, s_old, lvl_ref[...])
            if prompt:
                st_scr[...] = s_new
            else:
                so_ref[c] = s_new
            g = proj_scr[rows, OFF_GA:OFF_GA + D_AV]
            for h in range(A_HEADS):
                on = _rms(o_st[h * CHUNK:(h + 1) * CHUNK], anorm)
                gh = g[:, h * A_DV:(h + 1) * A_DV]
                mix_scr[rows, h * A_DV:(h + 1) * A_DV] = (on * (gh * jax.nn.sigmoid(gh))).astype(BF16)

            oc_st = _softmax_pv(sc_c, seg_c, None)
            oc = jnp.where(hm_c[0], oc_st[0:CHUNK], 0.0)
            for h in range(1, C_HEADS):
                oc = jnp.where(hm_c[h], oc_st[h * CHUNK:(h + 1) * CHUNK], oc)
            mix_scr[rows, D_AV + 2 * D_BKV:D_MIX] = oc.astype(BF16)

            ob_st = _softmax_pv(sc_b, seg_b, sink_ref[...])
            ob_a = jnp.where(hm_b[0], ob_st[0:CHUNK], ob_st[2 * CHUNK:3 * CHUNK])
            ob_b = jnp.where(hm_b[0], ob_st[CHUNK:2 * CHUNK], ob_st[3 * CHUNK:4 * CHUNK])
            mix_scr[rows, D_AV:D_AV + D_BKV] = ob_a.astype(BF16)
            mix_scr[rows, D_AV + D_BKV:D_AV + 2 * D_BKV] = ob_b.astype(BF16)

            acc_scr[...] += _dot(act, wdown_ref[c])

            if not prompt:
                for dst, src, new, width in ((kbo_ref, cbk_ref, kb_new, B_WINDOW), (vbo_ref, cbv_ref, vb_new, B_WINDOW),
                                             (kco_ref, cck_ref, kc_new, C_BAND), (vco_ref, ccv_ref, vc_new, C_BAND)):
                    dst[c, :, 0:width - CHUNK] = src[c, :, CHUNK:width]
                    dst[c, :, width - CHUNK:width] = new.T
            return carry

        lax.fori_loop(0, nch, chunk_body, 0)

        if prompt:
            kbw[0:B_WINDOW, :] = kbw[tq:tq + B_WINDOW, :]
            vbw[0:B_WINDOW, :] = vbw[tq:tq + B_WINDOW, :]
            kcw[0:C_BAND, :] = kcw[tq:tq + C_BAND, :]
            vcw[0:C_BAND, :] = vcw[tq:tq + C_BAND, :]

            so_ref[...] = st_scr[...]
            kbo_ref[...] = proj_scr[tq - B_WINDOW:tq, OFF_KB:OFF_KB + D_BKV]
            vbo_ref[...] = proj_scr[tq - B_WINDOW:tq, OFF_VB:OFF_VB + D_BKV]
            rb = kco_ref.shape[0]
            kco_ref[...] = proj_scr[tq - rb:tq, OFF_KC:OFF_KC + D_C]
            vco_ref[...] = proj_scr[tq - rb:tq, OFF_VC:OFF_VC + D_C]

        mixed = _dot(mix_scr[...], wout_ref[...])

        finish_previous_tile()

        gmlp = gmlp_ref[...]
        for c in range(nch):
            r = slice(c * CHUNK, (c + 1) * CHUNK)
            x1 = x_ref[r, :] + mod(mods_ref, c, 2) * mixed[r, :]
            x1_scr[r, :] = x1
            h2 = _rms(x1, gmlp * (1.0 + mod(mods_ref, c, 4))) + mod(mods_ref, c, 3)
            h2_scr[r, :] = h2.astype(BF16)

    @pl.when(step == nt)
    def _():
        def mlp_block(c, carry):
            up = _dot(h2_scr[...], wup_ref[c])
            act = jnp.square(jnp.maximum(up, 0.0)).astype(BF16)
            acc_scr[...] += _dot(act, wdown_ref[c])
            return carry

        lax.fori_loop(0, nch, mlp_block, 0)
        finish_previous_tile()


def _layer_spec(arr, layer):
    nd = arr.ndim - 1
    return pl.BlockSpec((None,) + arr.shape[1:], lambda i, _l=layer, _nd=nd: (_l,) + (0,) * _nd,
                        pipeline_mode=pl.Buffered(1))


def _const_spec(arr):
    nd = arr.ndim
    return pl.BlockSpec(arr.shape, lambda i, _nd=nd: (0,) * _nd, pipeline_mode=pl.Buffered(1))


def _run_layer(x, mods, layer, stacked, shared, caches, carried, *, prompt, final, tq, nb):
    rows = x.shape[0]
    nch = tq // CHUNK
    nt = rows // tq
    assert rows % tq == 0 and nch == N_FF
    (gmix, gmlp, anorm, win, wa2, ba2, wout, wup, wdown, biasc, sink) = stacked
    (gfin, biasb, tri, lvl) = shared
    cur = lambda i: jnp.minimum(i, nt - 1)
    prv = lambda i: jnp.maximum(i - 1, 0)

    nmod = 1 if prompt else nb
    in_specs = [pl.BlockSpec((tq, D_MODEL), lambda i: (cur(i), 0)),
                pl.BlockSpec((nmod, SUBLANES, D_MODEL), lambda i: (0 if prompt else cur(i), 0, 0)),
                pl.BlockSpec((nmod, SUBLANES, D_MODEL), lambda i: (0 if prompt else prv(i), 0, 0))]
    ls = functools.partial(_layer_spec, layer=layer)
    in_specs += [ls(gmix), ls(gmlp), _const_spec(gfin), ls(anorm), ls(win), ls(wa2), ls(ba2), ls(wout),
                 ls(wup), ls(wdown), _const_spec(biasb), ls(biasc), ls(sink), _const_spec(tri), _const_spec(lvl)]
    args = [x, mods, mods, gmix, gmlp, gfin, anorm, win, wa2, ba2, wout, wup, wdown, biasb, biasc, sink, tri, lvl]

    y_spec = pl.BlockSpec((tq, D_MODEL), lambda i: (prv(i), 0))
    if prompt:
        assert tq >= B_WINDOW and (tq % C_BAND == 0 or C_BAND % tq == 0)
        rb = min(tq, C_BAND)
        first = nt - C_BAND // rb
        out_shape = [jax.ShapeDtypeStruct((rows, D_MODEL), F32),
                     jax.ShapeDtypeStruct((D_AQK, A_DV), F32),
                     jax.ShapeDtypeStruct((B_WINDOW, D_BKV), F32),
                     jax.ShapeDtypeStruct((B_WINDOW, D_BKV), F32),
                     jax.ShapeDtypeStruct((C_BAND, D_C), F32),
                     jax.ShapeDtypeStruct((C_BAND, D_C), F32)]
        band = pl.BlockSpec((rb, D_C), lambda i: (jnp.maximum(cur(i) - first, 0), 0))
        out_specs = [y_spec,
                     pl.BlockSpec((D_AQK, A_DV), lambda i: (0, 0)),
                     pl.BlockSpec((B_WINDOW, D_BKV), lambda i: (0, 0)),
                     pl.BlockSpec((B_WINDOW, D_BKV), lambda i: (0, 0)),
                     band, band]
        kbw_rows, kcw_rows = B_WINDOW + tq, C_BAND + tq
        aliases = {}
    else:
        assert nb == nch
        nseq = rows // CHUNK
        st, cbk, cbv, cck, ccv = caches
        assert len(carried) == N_SEQ_OUT
        seq = lambda r, w: pl.BlockSpec((None, nb, r, w), lambda i, _l=layer: (_l, cur(i), 0, 0))
        seq_dims = [(D_AQK, A_DV), (D_BKV, B_WINDOW), (D_BKV, B_WINDOW), (D_C, C_BAND), (D_C, C_BAND)]
        in_specs += [seq(r, w) for r, w in seq_dims]
        aliases = {len(in_specs) + j: 1 + j for j in range(N_SEQ_OUT)}
        in_specs += [pl.BlockSpec(memory_space=pl.ANY)] * N_SEQ_OUT
        args += [st, cbk, cbv, cck, ccv, *carried]
        out_shape = [jax.ShapeDtypeStruct((rows, D_MODEL), F32)] + [
            jax.ShapeDtypeStruct((DEPTH, nseq, r, w), F32) for r, w in seq_dims]
        out_specs = [y_spec] + [seq(r, w) for r, w in seq_dims]
        kbw_rows, kcw_rows = SUBLANES * 2, SUBLANES * 2

    scratch = [pltpu.VMEM((tq, D_MODEL), BF16),
               pltpu.VMEM((tq, D_MODEL), BF16),
               pltpu.VMEM((tq, OFF_RA), F32),
               pltpu.VMEM((tq, D_AQK), F32),
               pltpu.VMEM((tq, D_AQK), F32),
               pltpu.VMEM((tq, D_MIX), BF16),
               pltpu.VMEM((tq, D_MODEL), F32),
               pltpu.VMEM((tq, D_MODEL), F32),
               pltpu.VMEM((D_AQK, A_DV), F32),
               pltpu.VMEM((kbw_rows, D_BKV), BF16), pltpu.VMEM((kbw_rows, D_BKV), BF16),
               pltpu.VMEM((kcw_rows, D_C), BF16), pltpu.VMEM((kcw_rows, D_C), BF16)]

    kern = functools.partial(_layer_kernel, prompt=prompt, final=final, nch=nch, tq=tq, nt=nt)
    return pl.pallas_call(
        kern,
        grid=(nt + 1,),
        in_specs=in_specs,
        out_specs=out_specs,
        out_shape=out_shape,
        scratch_shapes=scratch,
        input_output_aliases=aliases,
        compiler_params=pltpu.CompilerParams(
            dimension_semantics=("arbitrary",),
            vmem_limit_bytes=VMEM_LIMIT_BYTES if prompt else VMEM_LIMIT_SAMPLE_BYTES),
        name=("layer_prompt" if prompt else "layer_sample") + ("_final" if final else ""),
    )(*args)


ADA_BLOCK = 1536


def _ada_kernel(c_ref, w_ref, b_ref, o_ref):
    c = c_ref[...]
    sc = c * jax.nn.sigmoid(c)
    sc_hi = sc.astype(BF16)
    sc_lo = (sc - sc_hi.astype(F32)).astype(BF16)
    w = w_ref[0]
    w_hi = w.astype(BF16)
    w_lo = (w - w_hi.astype(F32)).astype(BF16)
    o_ref[0] = _dot(sc_hi, w_hi) + _dot(sc_lo, w_hi) + _dot(sc_hi, w_lo) + b_ref[0]


def _ada(c_all, w_ada, b_ada):
    rows = c_all.shape[0]
    n = 6 * D_MODEL
    return pl.pallas_call(
        _ada_kernel,
        grid=(DEPTH, n // ADA_BLOCK),
        in_specs=[pl.BlockSpec((rows, D_MODEL), lambda l, j: (0, 0)),
                  pl.BlockSpec((1, D_MODEL, ADA_BLOCK), lambda l, j: (l, 0, j)),
                  pl.BlockSpec((1, 1, ADA_BLOCK), lambda l, j: (l, 0, j))],
        out_specs=pl.BlockSpec((1, rows, ADA_BLOCK), lambda l, j: (l, 0, j)),
        out_shape=jax.ShapeDtypeStruct((DEPTH, rows, n), F32),
        compiler_params=pltpu.CompilerParams(dimension_semantics=("arbitrary", "arbitrary"),
                                             vmem_limit_bytes=VMEM_LIMIT_BYTES),
        name="adaln",
    )(c_all, w_ada, b_ada.reshape(DEPTH, 1, n))


TB_W = 2 * LANES
TC_W = 5 * LANES
TC_FLAT = TC_W - (CHUNK + C_CLIP + 1)


def _shear(f, width):
    tiled = jnp.broadcast_to(f, (CHUNK, width))
    return pltpu.roll(tiled, width - (CHUNK - 1), 1, stride=1, stride_axis=0)


def _bias_kernel(t5_ref, crel_ref, bucket_ref, ob_ref, oc_ref):
    bucket = bucket_ref[...]
    for h in range(B_HEADS):
        def b_body(i, acc, h=h):
            return jnp.where(bucket == i, t5_ref[h, i], acc)
        f = lax.fori_loop(0, T5_BUCKETS, b_body, jnp.zeros(bucket.shape, F32))
        ob_ref[h] = _shear(f, TB_W)[:, 0:LK_B]
    lane = lax.broadcasted_iota(jnp.int32, (1, TC_W), 1)
    for l in range(DEPTH):
        for h in range(C_HEADS):
            row = crel_ref[l, h:h + 1, :]
            f = jnp.where(lane < TC_FLAT, row[:, 0:1], pltpu.roll(row, TC_FLAT, 1))
            oc_ref[l, h] = _shear(f, TC_W)[:, 0:LK_C]


def _bias_tables(t5_bias, c_rel_bias, bucket):
    smem = pl.BlockSpec(memory_space=pltpu.SMEM)
    vmem = pl.BlockSpec(memory_space=pltpu.VMEM)
    crel = jnp.pad(jnp.swapaxes(c_rel_bias, 1, 2),
                   ((0, 0), (0, SUBLANES - C_HEADS), (0, TC_W - (2 * C_CLIP + 1))))
    return pl.pallas_call(
        _bias_kernel,
        in_specs=[smem, vmem, vmem],
        out_specs=[vmem, vmem],
        out_shape=[jax.ShapeDtypeStruct((B_HEADS, CHUNK, LK_B), F32),
                   jax.ShapeDtypeStruct((DEPTH, C_HEADS, CHUNK, LK_C), F32)],
        name="bias_tables",
    )(t5_bias.T, crel, bucket)


def _t5_bucket(rel):
    half = T5_BUCKETS // 2
    max_exact = half // 2
    steps = half - max_exact
    thresholds = [int(np.ceil(max_exact * (T5_MAX_DIST / max_exact) ** (k / steps) - 1e-9)) for k in range(1, steps)]
    n = jnp.abs(rel)
    large = jnp.minimum(max_exact + sum((n >= t).astype(jnp.int32) for t in thresholds), half - 1)
    return jnp.where(rel > 0, half, 0) + jnp.where(n < max_exact, n, large)


def _gla_constants(tq):
    r = np.arange(tq)
    tri = ((r[:, None] // CHUNK == r[None, :] // CHUNK) & (r[None, :] <= r[:, None])).astype(np.float32)
    t = np.arange(CHUNK)[:, None]
    s = np.arange(CHUNK)[None, :]
    lvl = np.full((CHUNK, CHUNK), -1, np.int32)
    lvl[t == s] = 0
    for li, m in enumerate(GLA_LEVELS):
        sel = ((t // m) % 2 == 1) & ((s // m) == (t // m) - 1)
        lvl[np.broadcast_to(sel, lvl.shape)] = li + 1
    return jnp.asarray(tri, BF16), jnp.asarray(np.tile(lvl, (A_HEADS, 1)))


SRC_RA = 2 * D_AQK + 2 * D_AV
SRC_QB = SRC_RA + A_GATE_RANK
SRC_KB = SRC_QB + B_HEADS * HEAD_DIM
D_IN = SRC_KB + 2 * D_BKV + 3 * D_C
PREP_ROWS_IN = 512
QB_HEAD_ORDER = (0, 2, 1, 3)


def _w_in_kernel(w_ref, o_ref):
    def put(dst, src, width):
        o_ref[:, dst:dst + width] = w_ref[:, src:src + width].astype(BF16)

    put(OFF_QA, 0, SRC_RA)
    for i, h in enumerate(QB_HEAD_ORDER):
        put(OFF_QB + i * HEAD_DIM, SRC_QB + h * HEAD_DIM, HEAD_DIM)
    put(OFF_KB, SRC_KB, D_IN - SRC_KB)
    put(OFF_RA, SRC_RA, A_GATE_RANK)
    o_ref[:, OFF_RA + A_GATE_RANK:P_IN] = jnp.zeros((o_ref.shape[0], LANES - A_GATE_RANK), BF16)


def _prep_w_in(w_in):
    return pl.pallas_call(
        _w_in_kernel,
        grid=(DEPTH, D_MODEL // PREP_ROWS_IN),
        in_specs=[pl.BlockSpec((None, PREP_ROWS_IN, D_IN), lambda l, r: (l, r, 0))],
        out_specs=pl.BlockSpec((None, PREP_ROWS_IN, P_IN), lambda l, r: (l, r, 0)),
        out_shape=jax.ShapeDtypeStruct((DEPTH, D_MODEL, P_IN), BF16),
        compiler_params=pltpu.CompilerParams(dimension_semantics=("arbitrary", "arbitrary"),
                                             vmem_limit_bytes=VMEM_LIMIT_BYTES),
        name="prep_w_in",
    )(w_in)


def _cast_kernel(w_ref, o_ref):
    o_ref[...] = w_ref[...].astype(BF16)


def _prep_w_up(w_up):
    return pl.pallas_call(
        _cast_kernel,
        grid=(DEPTH, N_FF),
        in_specs=[pl.BlockSpec((None, D_MODEL, FF_BLOCK), lambda l, j: (l, 0, j))],
        out_specs=pl.BlockSpec((None, None, D_MODEL, FF_BLOCK), lambda l, j: (l, j, 0, 0)),
        out_shape=jax.ShapeDtypeStruct((DEPTH, N_FF, D_MODEL, FF_BLOCK), BF16),
        compiler_params=pltpu.CompilerParams(dimension_semantics=("arbitrary", "arbitrary"),
                                             vmem_limit_bytes=VMEM_LIMIT_BYTES),
        name="prep_w_up",
    )(w_up)


TQ_PROMPT = 256
NB_SAMPLE = 4


def kernel(x_prompt, x_sample, c_prompt, c_sample, state_gla, cache_b_k, cache_b_v, cache_c_k, cache_c_v,
           w_ada, b_ada, norm_mix_g, norm_mlp_g, w_in, w_a2, b_a2, a_norm_g, b_sink, t5_bias, c_rel_bias,
           w_out, w_up, w_down, final_norm_g):
    bsz, seq, _ = x_prompt.shape
    dec_b, dec_s, _ = x_sample.shape
    assert bsz == 1 and dec_s == CHUNK and TQ_PROMPT == NB_SAMPLE * CHUNK

    n_c = bsz + dec_b
    c_rows = -(-n_c // SUBLANES) * SUBLANES
    c_all = jnp.concatenate([c_prompt, c_sample, jnp.zeros((c_rows - n_c, D_MODEL), F32)], axis=0)
    mods = _ada(c_all, w_ada, b_ada).reshape(DEPTH, c_rows, 6, D_MODEL)
    mods = jnp.pad(mods, ((0, 0), (0, 0), (0, SUBLANES - 6), (0, 0)))

    rel_b = jnp.arange(TB_W) - (CHUNK - 1) - B_WINDOW
    bias_b, bias_c = _bias_tables(t5_bias, c_rel_bias, _t5_bucket(rel_b).astype(jnp.int32)[None, :])
    bias_b = bias_b.reshape(STACK, LK_B)
    bias_c = bias_c.reshape(DEPTH, STACK, LK_C)

    tri, lvl = _gla_constants(TQ_PROMPT)
    ob0 = D_AV
    orow = lambda h: slice(ob0 + h * HEAD_DIM, ob0 + (h + 1) * HEAD_DIM)

    win = _prep_w_in(w_in)
    wa2 = jnp.concatenate([w_a2, jnp.zeros((DEPTH, LANES - A_GATE_RANK, D_AQK), F32)], axis=1).astype(BF16)
    wout = jnp.concatenate([w_out[:, 0:D_AV], w_out[:, orow(0)], w_out[:, orow(2)], w_out[:, orow(1)],
                            w_out[:, orow(3)], w_out[:, D_AV + B_HEADS * HEAD_DIM:]], axis=1).astype(BF16)
    wup = _prep_w_up(w_up)
    wdown = w_down.astype(BF16).reshape(DEPTH, N_FF, FF_BLOCK, D_MODEL)
    sink = jnp.repeat(b_sink, CHUNK, axis=1)[:, :, None]
    stacked = (norm_mix_g[:, None, :], norm_mlp_g[:, None, :], a_norm_g[:, None, :], win, wa2,
               b_a2[:, None, :], wout, wup, wdown, bias_c, sink)
    shared = (final_norm_g[None, :], bias_b, tri, lvl)

    x_p = x_prompt.reshape(seq, D_MODEL)
    x_s = x_sample.reshape(dec_b * dec_s, D_MODEL)
    outs_p = []
    fmajor = lambda cache: jnp.transpose(cache, (0, 1, 3, 4, 2)).reshape(
        DEPTH, dec_b, cache.shape[3] * cache.shape[4], cache.shape[2])
    caches = (state_gla.reshape(DEPTH, dec_b, D_AQK, A_DV), fmajor(cache_b_k), fmajor(cache_b_v),
              fmajor(cache_c_k), fmajor(cache_c_v))
    outs_s = tuple(jnp.zeros(c.shape, F32) for c in caches)
    for l in range(DEPTH):
        final = l == DEPTH - 1
        res_p = _run_layer(x_p, mods[l, 0:bsz], l, stacked, shared, None, None, prompt=True, final=final,
                           tq=TQ_PROMPT, nb=None)
        x_p = res_p[0]
        outs_p.append(res_p[1:])

        res_s = _run_layer(x_s, mods[l, bsz:bsz + dec_b], l, stacked, shared, caches, outs_s, prompt=False,
                           final=final, tq=NB_SAMPLE * CHUNK, nb=NB_SAMPLE)
        x_s = res_s[0]
        outs_s = tuple(res_s[1:])

    sg_p = jnp.stack([o[0].reshape(bsz, A_HEADS, A_DK, A_DV) for o in outs_p])
    kb_p = jnp.stack([o[1].reshape(bsz, B_WINDOW, B_KV_HEADS, HEAD_DIM) for o in outs_p])
    vb_p = jnp.stack([o[2].reshape(bsz, B_WINDOW, B_KV_HEADS, HEAD_DIM) for o in outs_p])
    kc_p = jnp.stack([o[3].reshape(bsz, C_BAND, C_HEADS, HEAD_DIM) for o in outs_p])
    vc_p = jnp.stack([o[4].reshape(bsz, C_BAND, C_HEADS, HEAD_DIM) for o in outs_p])
    sg_s = outs_s[0].reshape(DEPTH, dec_b, A_HEADS, A_DK, A_DV)
    tmajor = lambda arr, heads: jnp.transpose(arr.reshape(DEPTH, dec_b, heads, HEAD_DIM, -1), (0, 1, 4, 2, 3))
    kb_s = tmajor(outs_s[1], B_KV_HEADS)
    vb_s = tmajor(outs_s[2], B_KV_HEADS)
    kc_s = tmajor(outs_s[3], C_HEADS)
    vc_s = tmajor(outs_s[4], C_HEADS)
    return (x_p.reshape(bsz, seq, D_MODEL), x_s.reshape(dec_b, dec_s, D_MODEL),
            sg_p, kb_p, vb_p, kc_p, vc_p, sg_s, kb_s, vb_s, kc_s, vc_s)
```

```python
import functools

import numpy as np
import jax
import jax.numpy as jnp
from jax import lax
from jax.experimental import pallas as pl
from jax.experimental.pallas import tpu as pltpu

D_MODEL = 1024
DEPTH = 2
CHUNK = 64
HEAD_DIM = 64
A_HEADS = 4
A_DK = 64
A_DV = 128
A_GATE_RANK = 16
A_GATE_TAU = 16.0
B_HEADS = 4
B_KV_HEADS = 2
B_WINDOW = 128
C_HEADS = 4
C_BAND = 512
C_CLIP = 256
T5_BUCKETS = 32
T5_MAX_DIST = 128
D_FF = 4 * D_MODEL
NORM_EPS = 1e-6
NEG_INF = -1e30

LANES = 128
SUBLANES = 8
VMEM_LIMIT_BYTES = 60000 * 1024
VMEM_LIMIT_SAMPLE_BYTES = 62 * 1024 * 1024

OFF_QA = 0
OFF_KA = OFF_QA + A_HEADS * A_DK
OFF_VA = OFF_KA + A_HEADS * A_DK
OFF_GA = OFF_VA + A_HEADS * A_DV
OFF_QB = OFF_GA + A_HEADS * A_DV
OFF_KB = OFF_QB + B_HEADS * HEAD_DIM
OFF_VB = OFF_KB + B_KV_HEADS * HEAD_DIM
OFF_QC = OFF_VB + B_KV_HEADS * HEAD_DIM
OFF_KC = OFF_QC + C_HEADS * HEAD_DIM
OFF_VC = OFF_KC + C_HEADS * HEAD_DIM
OFF_RA = OFF_VC + C_HEADS * HEAD_DIM
P_IN = OFF_RA + LANES
D_AQK = A_HEADS * A_DK
D_AV = A_HEADS * A_DV
D_BKV = B_KV_HEADS * HEAD_DIM
D_C = C_HEADS * HEAD_DIM
D_MIX = D_AV + B_HEADS * HEAD_DIM + D_C
LK_B = B_WINDOW + CHUNK
LK_C = C_BAND + CHUNK
STACK = 4 * CHUNK
GLA_LEVELS = (1, 2, 4, 8, 16, 32)
FF_BLOCK = 1024
N_FF = D_FF // FF_BLOCK
N_SEQ_OUT = 5

BF16 = jnp.bfloat16
F32 = jnp.float32


def _dot(a, b):
    return jnp.dot(a, b, preferred_element_type=F32)


def _dot_nt(a, b):
    return lax.dot_general(a, b, (((1,), (1,)), ((), ())), preferred_element_type=F32)


def _dot_tn(a, b):
    return lax.dot_general(a, b, (((0,), (0,)), ((), ())), preferred_element_type=F32)


def _rms(x, g):
    return x * lax.rsqrt(jnp.mean(x * x, axis=-1, keepdims=True) + NORM_EPS) * g


def _split3(x):
    hi = x.astype(BF16)
    r1 = x - hi.astype(F32)
    mid = r1.astype(BF16)
    lo = (r1 - mid.astype(F32)).astype(BF16)
    return hi, mid, lo


def _head_masks(width, per_head):
    lane = lax.broadcasted_iota(jnp.int32, (1, width), 1)
    return [(lane >= h * per_head) & (lane < (h + 1) * per_head) for h in range(width // per_head)]


def _stack_masked(x, masks):
    return jnp.concatenate([jnp.where(m, x, 0.0) for m in masks], axis=0)


def _tile_rows(x, n):
    return jnp.concatenate([x] * n, axis=0)


def _scores(q_st, segments, bias_st, maskrow):
    scores, off = [], 0
    for k, v, transposed in segments:
        n = k.shape[1] if transposed else k.shape[0]
        s = (_dot(q_st, k) if transposed else _dot_nt(q_st, k)) + bias_st[:, off:off + n]
        if maskrow is not None:
            s = s + maskrow[:, off:off + n]
        scores.append(s)
        off += n
    return scores


def _softmax_pv(scores, segments, sink):
    m = functools.reduce(jnp.maximum, [jnp.max(s, axis=1, keepdims=True) for s in scores])
    if sink is not None:
        m = jnp.maximum(m, sink)
    es = [jnp.exp(s - m) for s in scores]
    l = functools.reduce(jnp.add, [jnp.sum(e, axis=1, keepdims=True) for e in es])
    if sink is not None:
        l = l + jnp.exp(sink - m)
    o = None
    for e, (k, v, transposed) in zip(es, segments):
        part = _dot_nt(e.astype(BF16), v) if transposed else _dot(e.astype(BF16), v)
        o = part if o is None else o + part
    return o / l


def _level_exponent(b, m, t):
    bcast = lambda i: jnp.broadcast_to(b[i:i + 1], (SUBLANES, b.shape[1]))
    low_half = lax.broadcasted_iota(jnp.int32, (SUBLANES, b.shape[1]), 0) < SUBLANES // 2
    parts = []
    for g in range(CHUNK // SUBLANES):
        r = g * SUBLANES
        rows = b[r:r + SUBLANES]
        if m >= SUBLANES:
            ref = b[(r // (2 * m)) * 2 * m + m - 1:(r // (2 * m)) * 2 * m + m]
            parts.append(rows - ref if (r // m) % 2 == 1 else ref - rows)
        elif m == 4:
            parts.append(rows - bcast(r + 3))
        else:
            parts.append(rows - jnp.where(low_half, bcast(r + 1), bcast(r + 5)))
    d = jnp.concatenate(parts, axis=0)
    return d if m >= SUBLANES else jnp.where((t & m) != 0, d, -d)


def _odd_block_rows(x, m, groups):
    return jnp.concatenate([x[g * CHUNK + r:g * CHUNK + r + SUBLANES] for g in range(groups)
                            for r in range(0, CHUNK, SUBLANES) if (r // m) % 2 == 1], axis=0)


def _spread_odd_block_rows(y, m, groups):
    zero = jnp.zeros((SUBLANES, y.shape[1]), y.dtype)
    parts, i = [], 0
    for g in range(groups):
        for r in range(0, CHUNK, SUBLANES):
            if (r // m) % 2 == 1:
                parts.append(y[i:i + SUBLANES])
                i += SUBLANES
            else:
                parts.append(zero)
    return jnp.concatenate(parts, axis=0)


def _gla_levels(q, k, la, b, hm_a):
    t = lax.broadcasted_iota(jnp.int32, (CHUNK, D_AQK), 0)
    q_st = _stack_masked(q, hm_a)
    k16 = k.astype(BF16)
    prods = [_dot_nt(q_st.astype(BF16), k16)]
    for m in GLA_LEVELS:
        e = jnp.exp(la if m == 1 else _level_exponent(b, m, t))
        kt = k16 if m == 1 else (k * e).astype(BF16)
        if m >= SUBLANES:
            qt = _odd_block_rows(q_st, m, A_HEADS) * _tile_rows(_odd_block_rows(e, m, 1), A_HEADS)
            prods.append(_spread_odd_block_rows(_dot_nt(qt.astype(BF16), kt), m, A_HEADS))
        else:
            qt = (q_st * _tile_rows(e, A_HEADS)).astype(BF16)
            prods.append(_dot_nt(qt, kt))
    return q_st, prods


def _gla_finish(q_st, prods, k, v, b, s, lvl):
    attn = jnp.where(lvl == 0, prods[0], 0.0)
    for li in range(1, len(prods)):
        attn = jnp.where(lvl == li, prods[li], attn)
    qhat = (q_st * _tile_rows(jnp.exp(b), A_HEADS)).astype(BF16)
    o_inter = _dot(qhat, s.astype(BF16))
    v16 = v.astype(BF16)
    o_intra = jnp.concatenate(
        [_dot(attn[h * CHUNK:(h + 1) * CHUNK].astype(BF16), v16[:, h * A_DV:(h + 1) * A_DV])
         for h in range(A_HEADS)], axis=0)
    b_last = b[CHUNK - 1:CHUNK]
    khat = (k * jnp.exp(b_last - b)).astype(BF16)
    upd = _dot_tn(khat, v16)
    decay = jnp.broadcast_to(jnp.exp(b_last), (A_DV, D_AQK)).T
    s_new = s * decay + jnp.concatenate(
        [upd[h * A_DK:(h + 1) * A_DK, h * A_DV:(h + 1) * A_DV] for h in range(A_HEADS)], axis=0)
    return o_intra + o_inter, s_new


def _layer_kernel(*refs, prompt, final, nch, tq, nt):
    (x_ref, mods_ref, modp_ref, gmix_ref, gmlp_ref, gfin_ref, anorm_ref, win_ref, wa2_ref, ba2_ref, wout_ref,
     wup_ref, wdown_ref, biasb_ref, biasc_ref, sink_ref, tri_ref, lvl_ref) = refs[:18]
    pos = 18
    if not prompt:
        st_ref, cbk_ref, cbv_ref, cck_ref, ccv_ref = refs[pos:pos + 5]
        pos += 5 + N_SEQ_OUT
    y_ref, so_ref, kbo_ref, vbo_ref, kco_ref, vco_ref = refs[pos:pos + 6]
    pos += 6
    (h_scr, h2_scr, proj_scr, la_scr, b_scr, mix_scr, x1_scr, acc_scr, st_scr,
     kbw, vbw, kcw, vcw) = refs[pos:]

    step = pl.program_id(0)

    def mod(ref, c, j):
        return ref[0 if prompt else c, j:j + 1, :]

    @pl.when(step == 0)
    def _():
        x1_scr[...] = jnp.zeros_like(x1_scr)
        acc_scr[...] = jnp.zeros_like(acc_scr)
        if prompt:
            st_scr[...] = jnp.zeros_like(st_scr)
            kbw[...] = jnp.zeros_like(kbw)
            vbw[...] = jnp.zeros_like(vbw)
            kcw[...] = jnp.zeros_like(kcw)
            vcw[...] = jnp.zeros_like(vcw)

    def finish_previous_tile():
        gfin = gfin_ref[...]
        for c in range(nch):
            r = slice(c * CHUNK, (c + 1) * CHUNK)
            x2 = x1_scr[r, :] + mod(modp_ref, c, 5) * acc_scr[r, :]
            y_ref[r, :] = _rms(x2, gfin) if final else x2
        acc_scr[...] = jnp.zeros_like(acc_scr)

    @pl.when(step < nt)
    def _():
        gmix = gmix_ref[...]
        for c in range(nch):
            r = slice(c * CHUNK, (c + 1) * CHUNK)
            h = _rms(x_ref[r, :], gmix * (1.0 + mod(mods_ref, c, 1))) + mod(mods_ref, c, 0)
            h_scr[r, :] = h.astype(BF16)
        hmix = h_scr[...]
        cut1, cut2 = OFF_GA, OFF_QC
        ra = _dot(hmix, win_ref[:, OFF_RA:OFF_RA + LANES])
        proj_scr[:, 0:cut1] = _dot(hmix, win_ref[:, 0:cut1])
        z = _dot(ra.astype(BF16), wa2_ref[...]) + ba2_ref[...]
        la = (jnp.minimum(z, 0.0) - jnp.log1p(jnp.exp(-jnp.abs(z)))) * (1.0 / A_GATE_TAU)
        la_scr[...] = la
        proj_scr[:, cut1:cut2] = _dot(hmix, win_ref[:, cut1:cut2])
        tri = tri_ref[...]
        hi, mid, lo = _split3(la)
        b_scr[...] = _dot(tri, hi) + _dot(tri, mid) + _dot(tri, lo)
        proj_scr[:, cut2:OFF_RA] = _dot(hmix, win_ref[:, cut2:OFF_RA])

        if prompt:
            kbw[B_WINDOW:B_WINDOW + tq, :] = proj_scr[:, OFF_KB:OFF_KB + D_BKV].astype(BF16)
            vbw[B_WINDOW:B_WINDOW + tq, :] = proj_scr[:, OFF_VB:OFF_VB + D_BKV].astype(BF16)
            kcw[C_BAND:C_BAND + tq, :] = proj_scr[:, OFF_KC:OFF_KC + D_C].astype(BF16)
            vcw[C_BAND:C_BAND + tq, :] = proj_scr[:, OFF_VC:OFF_VC + D_C].astype(BF16)

        hm_a = _head_masks(D_AQK, A_DK)
        hm_c = _head_masks(D_C, HEAD_DIM)
        hm_b = _head_masks(D_BKV, HEAD_DIM)
        anorm = anorm_ref[...]

        def chunk_body(c, carry, with_mlp):
            r0 = pl.multiple_of(c * CHUNK, CHUNK)
            rows = pl.ds(r0, CHUNK)

            if prompt:
                seg_b = [(kbw[pl.ds(r0, LK_B), :], vbw[pl.ds(r0, LK_B), :], False)]
                seg_c = [(kcw[pl.ds(r0, LK_C), :], vcw[pl.ds(r0, LK_C), :], False)]
                gidx = step * nch + c
                lane_b = lax.broadcasted_iota(jnp.int32, (1, LK_B), 1)
                lane_c = lax.broadcasted_iota(jnp.int32, (1, LK_C), 1)
                mask_b = jnp.where(lane_b >= B_WINDOW - gidx * CHUNK, 0.0, NEG_INF)
                mask_c = jnp.where(lane_c >= C_BAND - gidx * CHUNK, 0.0, NEG_INF)
            else:
                kb_new = proj_scr[rows, OFF_KB:OFF_KB + D_BKV]
                vb_new = proj_scr[rows, OFF_VB:OFF_VB + D_BKV]
                kc_new = proj_scr[rows, OFF_KC:OFF_KC + D_C]
                vc_new = proj_scr[rows, OFF_VC:OFF_VC + D_C]
                seg_b = [(cbk_ref[c].astype(BF16), cbv_ref[c].astype(BF16), True),
                         (kb_new.astype(BF16), vb_new.astype(BF16), False)]
                seg_c = [(cck_ref[c].astype(BF16), ccv_ref[c].astype(BF16), True),
                         (kc_new.astype(BF16), vc_new.astype(BF16), False)]
                mask_b = mask_c = None

            q = proj_scr[rows, OFF_QA:OFF_QA + D_AQK] * (A_DK ** -0.5)
            k = proj_scr[rows, OFF_KA:OFF_KA + D_AQK]
            b = b_scr[rows, :]
            q_st, prods = _gla_levels(q, k, la_scr[rows, :], b, hm_a)
            qc = proj_scr[rows, OFF_QC:OFF_QC + D_C] * (HEAD_DIM ** -0.5)
            sc_c = _scores(_stack_masked(qc, hm_c).astype(BF16), seg_c, biasc_ref[...], mask_c)
            qb = proj_scr[rows, OFF_QB:OFF_QB + B_HEADS * HEAD_DIM] * (HEAD_DIM ** -0.5)
            qb_a, qb_b = qb[:, 0:D_BKV], qb[:, D_BKV:2 * D_BKV]
            qb_st = jnp.concatenate([jnp.where(hm_b[0], qb_a, 0.0), jnp.where(hm_b[0], qb_b, 0.0),
                                     jnp.where(hm_b[1], qb_a, 0.0), jnp.where(hm_b[1], qb_b, 0.0)], axis=0)
            sc_b = _scores(qb_st.astype(BF16), seg_b, biasb_ref[...], mask_b)

            if with_mlp:
                up = _dot(h2_scr[...], wup_ref[c])
                act = jnp.square(jnp.maximum(up, 0.0)).astype(BF16)

            v = proj_scr[rows, OFF_VA:OFF_VA + D_AV]
            s_old = st_scr[...] if prompt else st_ref[c]
            o_st, s_new = _gla_finish(q_st, prods, k, v, b, s_old, lvl_ref[...])
            if prompt:
                st_scr[...] = s_new
            else:
                so_ref[c] = s_new
            g = proj_scr[rows, OFF_GA:OFF_GA + D_AV]
            for h in range(A_HEADS):
                on = _rms(o_st[h * CHUNK:(h + 1) * CHUNK], anorm)
                gh = g[:, h * A_DV:(h + 1) * A_DV]
                mix_scr[rows, h * A_DV:(h + 1) * A_DV] = (on * (gh * jax.nn.sigmoid(gh))).astype(BF16)

            oc_st = _softmax_pv(sc_c, seg_c, None)
            oc = jnp.where(hm_c[0], oc_st[0:CHUNK], 0.0)
            for h in range(1, C_HEADS):
                oc = jnp.where(hm_c[h], oc_st[h * CHUNK:(h + 1) * CHUNK], oc)
            mix_scr[rows, D_AV + 2 * D_BKV:D_MIX] = oc.astype(BF16)

            ob_st = _softmax_pv(sc_b, seg_b, sink_ref[...])
            ob_a = jnp.where(hm_b[0], ob_st[0:CHUNK], ob_st[2 * CHUNK:3 * CHUNK])
            ob_b = jnp.where(hm_b[0], ob_st[CHUNK:2 * CHUNK], ob_st[3 * CHUNK:4 * CHUNK])
            mix_scr[rows, D_AV:D_AV + D_BKV] = ob_a.astype(BF16)
            mix_scr[rows, D_AV + D_BKV:D_AV + 2 * D_BKV] = ob_b.astype(BF16)

            if with_mlp:
                acc_scr[...] += _dot(act, wdown_ref[c])

            if not prompt:
                for dst, src, new, width in ((kbo_ref, cbk_ref, kb_new, B_WINDOW), (vbo_ref, cbv_ref, vb_new, B_WINDOW),
                                             (kco_ref, cck_ref, kc_new, C_BAND), (vco_ref, ccv_ref, vc_new, C_BAND)):
                    dst[c, :, 0:width - CHUNK] = src[c, :, CHUNK:width]
                    dst[c, :, width - CHUNK:width] = new.T
            return carry

        @pl.when(step == 0)
        def _():
            lax.fori_loop(0, nch, functools.partial(chunk_body, with_mlp=False), 0)

        @pl.when(step > 0)
        def _():
            lax.fori_loop(0, nch, functools.partial(chunk_body, with_mlp=True), 0)

        if prompt:
            kbw[0:B_WINDOW, :] = kbw[tq:tq + B_WINDOW, :]
            vbw[0:B_WINDOW, :] = vbw[tq:tq + B_WINDOW, :]
            kcw[0:C_BAND, :] = kcw[tq:tq + C_BAND, :]
            vcw[0:C_BAND, :] = vcw[tq:tq + C_BAND, :]

            so_ref[...] = st_scr[...]
            kbo_ref[...] = proj_scr[tq - B_WINDOW:tq, OFF_KB:OFF_KB + D_BKV]
            vbo_ref[...] = proj_scr[tq - B_WINDOW:tq, OFF_VB:OFF_VB + D_BKV]
            rb = kco_ref.shape[0]
            kco_ref[...] = proj_scr[tq - rb:tq, OFF_KC:OFF_KC + D_C]
            vco_ref[...] = proj_scr[tq - rb:tq, OFF_VC:OFF_VC + D_C]

        mixed = _dot(mix_scr[...], wout_ref[...])

        finish_previous_tile()

        gmlp = gmlp_ref[...]
        for c in range(nch):
            r = slice(c * CHUNK, (c + 1) * CHUNK)
            x1 = x_ref[r, :] + mod(mods_ref, c, 2) * mixed[r, :]
            x1_scr[r, :] = x1
            h2 = _rms(x1, gmlp * (1.0 + mod(mods_ref, c, 4))) + mod(mods_ref, c, 3)
            h2_scr[r, :] = h2.astype(BF16)

    @pl.when(step == nt)
    def _():
        def mlp_block(c, carry):
            up = _dot(h2_scr[...], wup_ref[c])
            act = jnp.square(jnp.maximum(up, 0.0)).astype(BF16)
            acc_scr[...] += _dot(act, wdown_ref[c])
            return carry

        lax.fori_loop(0, nch, mlp_block, 0)
        finish_previous_tile()


def _layer_spec(arr, layer):
    nd = arr.ndim - 1
    return pl.BlockSpec((None,) + arr.shape[1:], lambda i, _l=layer, _nd=nd: (_l,) + (0,) * _nd,
                        pipeline_mode=pl.Buffered(1))


def _const_spec(arr):
    nd = arr.ndim
    return pl.BlockSpec(arr.shape, lambda i, _nd=nd: (0,) * _nd, pipeline_mode=pl.Buffered(1))


def _run_layer(x, mods, layer, stacked, shared, caches, carried, *, prompt, final, tq, nb):
    rows = x.shape[0]
    nch = tq // CHUNK
    nt = rows // tq
    assert rows % tq == 0 and nch == N_FF
    (gmix, gmlp, anorm, win, wa2, ba2, wout, wup, wdown, biasc, sink) = stacked
    (gfin, biasb, tri, lvl) = shared
    cur = lambda i: jnp.minimum(i, nt - 1)
    prv = lambda i: jnp.maximum(i - 1, 0)

    nmod = 1 if prompt else nb
    in_specs = [pl.BlockSpec((tq, D_MODEL), lambda i: (cur(i), 0)),
                pl.BlockSpec((nmod, SUBLANES, D_MODEL), lambda i: (0 if prompt else cur(i), 0, 0)),
                pl.BlockSpec((nmod, SUBLANES, D_MODEL), lambda i: (0 if prompt else prv(i), 0, 0))]
    ls = functools.partial(_layer_spec, layer=layer)
    in_specs += [ls(gmix), ls(gmlp), _const_spec(gfin), ls(anorm), ls(win), ls(wa2), ls(ba2), ls(wout),
                 ls(wup), ls(wdown), _const_spec(biasb), ls(biasc), ls(sink), _const_spec(tri), _const_spec(lvl)]
    args = [x, mods, mods, gmix, gmlp, gfin, anorm, win, wa2, ba2, wout, wup, wdown, biasb, biasc, sink, tri, lvl]

    y_spec = pl.BlockSpec((tq, D_MODEL), lambda i: (prv(i), 0))
    if prompt:
        assert tq >= B_WINDOW and (tq % C_BAND == 0 or C_BAND % tq == 0)
        rb = min(tq, C_BAND)
        first = nt - C_BAND // rb
        out_shape = [jax.ShapeDtypeStruct((rows, D_MODEL), F32),
                     jax.ShapeDtypeStruct((D_AQK, A_DV), F32),
                     jax.ShapeDtypeStruct((B_WINDOW, D_BKV), F32),
                     jax.ShapeDtypeStruct((B_WINDOW, D_BKV), F32),
                     jax.ShapeDtypeStruct((C_BAND, D_C), F32),
                     jax.ShapeDtypeStruct((C_BAND, D_C), F32)]
        band = pl.BlockSpec((rb, D_C), lambda i: (jnp.maximum(cur(i) - first, 0), 0))
        out_specs = [y_spec,
                     pl.BlockSpec((D_AQK, A_DV), lambda i: (0, 0)),
                     pl.BlockSpec((B_WINDOW, D_BKV), lambda i: (0, 0)),
                     pl.BlockSpec((B_WINDOW, D_BKV), lambda i: (0, 0)),
                     band, band]
        kbw_rows, kcw_rows = B_WINDOW + tq, C_BAND + tq
        aliases = {}
    else:
        assert nb == nch
        nseq = rows // CHUNK
        st, cbk, cbv, cck, ccv = caches
        assert len(carried) == N_SEQ_OUT
        seq = lambda r, w: pl.BlockSpec((None, nb, r, w), lambda i, _l=layer: (_l, cur(i), 0, 0))
        seq_dims = [(D_AQK, A_DV), (D_BKV, B_WINDOW), (D_BKV, B_WINDOW), (D_C, C_BAND), (D_C, C_BAND)]
        in_specs += [seq(r, w) for r, w in seq_dims]
        aliases = {len(in_specs) + j: 1 + j for j in range(N_SEQ_OUT)}
        in_specs += [pl.BlockSpec(memory_space=pl.ANY)] * N_SEQ_OUT
        args += [st, cbk, cbv, cck, ccv, *carried]
        out_shape = [jax.ShapeDtypeStruct((rows, D_MODEL), F32)] + [
            jax.ShapeDtypeStruct((DEPTH, nseq, r, w), F32) for r, w in seq_dims]
        out_specs = [y_spec] + [seq(r, w) for r, w in seq_dims]
        kbw_rows, kcw_rows = SUBLANES * 2, SUBLANES * 2

    scratch = [pltpu.VMEM((tq, D_MODEL), BF16),
               pltpu.VMEM((tq, D_MODEL), BF16),
               pltpu.VMEM((tq, OFF_RA), F32),
               pltpu.VMEM((tq, D_AQK), F32),
               pltpu.VMEM((tq, D_AQK), F32),
               pltpu.VMEM((tq, D_MIX), BF16),
               pltpu.VMEM((tq, D_MODEL), F32),
               pltpu.VMEM((tq, D_MODEL), F32),
               pltpu.VMEM((D_AQK, A_DV), F32),
               pltpu.VMEM((kbw_rows, D_BKV), BF16), pltpu.VMEM((kbw_rows, D_BKV), BF16),
               pltpu.VMEM((kcw_rows, D_C), BF16), pltpu.VMEM((kcw_rows, D_C), BF16)]

    kern = functools.partial(_layer_kernel, prompt=prompt, final=final, nch=nch, tq=tq, nt=nt)
    return pl.pallas_call(
        kern,
        grid=(nt + 1,),
        in_specs=in_specs,
        out_specs=out_specs,
        out_shape=out_shape,
        scratch_shapes=scratch,
        input_output_aliases=aliases,
        compiler_params=pltpu.CompilerParams(
            dimension_semantics=("arbitrary",),
            vmem_limit_bytes=VMEM_LIMIT_BYTES if prompt else VMEM_LIMIT_SAMPLE_BYTES),
        name=("layer_prompt" if prompt else "layer_sample") + ("_final" if final else ""),
    )(*args)


ADA_BLOCK = 1536


def _ada_kernel(c_ref, w_ref, b_ref, o_ref):
    c = c_ref[...]
    sc = c * jax.nn.sigmoid(c)
    sc_hi = sc.astype(BF16)
    sc_lo = (sc - sc_hi.astype(F32)).astype(BF16)
    w = w_ref[0]
    w_hi = w.astype(BF16)
    w_lo = (w - w_hi.astype(F32)).astype(BF16)
    o_ref[0] = _dot(sc_hi, w_hi) + _dot(sc_lo, w_hi) + _dot(sc_hi, w_lo) + b_ref[0]


def _ada(c_all, w_ada, b_ada):
    rows = c_all.shape[0]
    n = 6 * D_MODEL
    return pl.pallas_call(
        _ada_kernel,
        grid=(DEPTH, n // ADA_BLOCK),
        in_specs=[pl.BlockSpec((rows, D_MODEL), lambda l, j: (0, 0)),
                  pl.BlockSpec((1, D_MODEL, ADA_BLOCK), lambda l, j: (l, 0, j)),
                  pl.BlockSpec((1, 1, ADA_BLOCK), lambda l, j: (l, 0, j))],
        out_specs=pl.BlockSpec((1, rows, ADA_BLOCK), lambda l, j: (l, 0, j)),
        out_shape=jax.ShapeDtypeStruct((DEPTH, rows, n), F32),
        compiler_params=pltpu.CompilerParams(dimension_semantics=("arbitrary", "arbitrary"),
                                             vmem_limit_bytes=VMEM_LIMIT_BYTES),
        name="adaln",
    )(c_all, w_ada, b_ada.reshape(DEPTH, 1, n))


TB_W = 2 * LANES
TC_W = 5 * LANES
TC_FLAT = TC_W - (CHUNK + C_CLIP + 1)


def _shear(f, width):
    tiled = jnp.broadcast_to(f, (CHUNK, width))
    return pltpu.roll(tiled, width - (CHUNK - 1), 1, stride=1, stride_axis=0)


def _bias_kernel(t5_ref, crel_ref, bucket_ref, ob_ref, oc_ref):
    bucket = bucket_ref[...]
    for h in range(B_HEADS):
        def b_body(i, acc, h=h):
            return jnp.where(bucket == i, t5_ref[h, i], acc)
        f = lax.fori_loop(0, T5_BUCKETS, b_body, jnp.zeros(bucket.shape, F32))
        ob_ref[h] = _shear(f, TB_W)[:, 0:LK_B]
    lane = lax.broadcasted_iota(jnp.int32, (1, TC_W), 1)
    for l in range(DEPTH):
        for h in range(C_HEADS):
            row = crel_ref[l, h:h + 1, :]
            f = jnp.where(lane < TC_FLAT, row[:, 0:1], pltpu.roll(row, TC_FLAT, 1))
            oc_ref[l, h] = _shear(f, TC_W)[:, 0:LK_C]


def _bias_tables(t5_bias, c_rel_bias, bucket):
    smem = pl.BlockSpec(memory_space=pltpu.SMEM)
    vmem = pl.BlockSpec(memory_space=pltpu.VMEM)
    crel = jnp.pad(jnp.swapaxes(c_rel_bias, 1, 2),
                   ((0, 0), (0, SUBLANES - C_HEADS), (0, TC_W - (2 * C_CLIP + 1))))
    return pl.pallas_call(
        _bias_kernel,
        in_specs=[smem, vmem, vmem],
        out_specs=[vmem, vmem],
        out_shape=[jax.ShapeDtypeStruct((B_HEADS, CHUNK, LK_B), F32),
                   jax.ShapeDtypeStruct((DEPTH, C_HEADS, CHUNK, LK_C), F32)],
        name="bias_tables",
    )(t5_bias.T, crel, bucket)


def _t5_bucket(rel):
    half = T5_BUCKETS // 2
    max_exact = half // 2
    steps = half - max_exact
    thresholds = [int(np.ceil(max_exact * (T5_MAX_DIST / max_exact) ** (k / steps) - 1e-9)) for k in range(1, steps)]
    n = jnp.abs(rel)
    large = jnp.minimum(max_exact + sum((n >= t).astype(jnp.int32) for t in thresholds), half - 1)
    return jnp.where(rel > 0, half, 0) + jnp.where(n < max_exact, n, large)


def _gla_constants(tq):
    r = np.arange(tq)
    tri = ((r[:, None] // CHUNK == r[None, :] // CHUNK) & (r[None, :] <= r[:, None])).astype(np.float32)
    t = np.arange(CHUNK)[:, None]
    s = np.arange(CHUNK)[None, :]
    lvl = np.full((CHUNK, CHUNK), -1, np.int32)
    lvl[t == s] = 0
    for li, m in enumerate(GLA_LEVELS):
        sel = ((t // m) % 2 == 1) & ((s // m) == (t // m) - 1)
        lvl[np.broadcast_to(sel, lvl.shape)] = li + 1
    return jnp.asarray(tri, BF16), jnp.asarray(np.tile(lvl, (A_HEADS, 1)))


SRC_RA = 2 * D_AQK + 2 * D_AV
SRC_QB = SRC_RA + A_GATE_RANK
SRC_KB = SRC_QB + B_HEADS * HEAD_DIM
D_IN = SRC_KB + 2 * D_BKV + 3 * D_C
QB_HEAD_ORDER = (0, 2, 1, 3)


def _w_in_segments():
    segs = [(OFF_QA, 0, SRC_RA)]
    segs += [(OFF_QB + i * HEAD_DIM, SRC_QB + h * HEAD_DIM, HEAD_DIM) for i, h in enumerate(QB_HEAD_ORDER)]
    segs += [(OFF_KB, SRC_KB, D_IN - SRC_KB), (OFF_RA, SRC_RA, A_GATE_RANK)]
    return segs


def _w_in_kernel(w_ref, o_ref):
    segs = _w_in_segments()
    for blk in range(P_IN // LANES):
        lo, hi = blk * LANES, (blk + 1) * LANES
        rows, col = [], lo
        while col < hi:
            hit = [(d, s, w) for d, s, w in segs if d <= col < d + w]
            if hit:
                d, s, w = hit[0]
                n = min(hi, d + w) - col
                rows.append(w_ref[s + col - d:s + col - d + n, :])
            else:
                n = hi - col
                rows.append(jnp.zeros((n, D_MODEL), F32))
            col += n
        o_ref[:, lo:hi] = jnp.concatenate(rows, axis=0).T.astype(BF16)


def _prep_w_in(w_in):
    w_fm = jnp.swapaxes(w_in, 1, 2)
    return pl.pallas_call(
        _w_in_kernel,
        grid=(DEPTH,),
        in_specs=[pl.BlockSpec((None, D_IN, D_MODEL), lambda l: (l, 0, 0))],
        out_specs=pl.BlockSpec((None, D_MODEL, P_IN), lambda l: (l, 0, 0)),
        out_shape=jax.ShapeDtypeStruct((DEPTH, D_MODEL, P_IN), BF16),
        compiler_params=pltpu.CompilerParams(dimension_semantics=("arbitrary",),
                                             vmem_limit_bytes=VMEM_LIMIT_BYTES),
        name="prep_w_in",
    )(w_fm)


def _cast_kernel(w_ref, o_ref):
    o_ref[...] = w_ref[...].astype(BF16)


def _prep_w_up(w_up):
    return pl.pallas_call(
        _cast_kernel,
        grid=(DEPTH, N_FF),
        in_specs=[pl.BlockSpec((None, D_MODEL, FF_BLOCK), lambda l, j: (l, 0, j))],
        out_specs=pl.BlockSpec((None, None, D_MODEL, FF_BLOCK), lambda l, j: (l, j, 0, 0)),
        out_shape=jax.ShapeDtypeStruct((DEPTH, N_FF, D_MODEL, FF_BLOCK), BF16),
        compiler_params=pltpu.CompilerParams(dimension_semantics=("arbitrary", "arbitrary"),
                                             vmem_limit_bytes=VMEM_LIMIT_BYTES),
        name="prep_w_up",
    )(w_up)


TQ_PROMPT = 256
NB_SAMPLE = 4


def kernel(x_prompt, x_sample, c_prompt, c_sample, state_gla, cache_b_k, cache_b_v, cache_c_k, cache_c_v,
           w_ada, b_ada, norm_mix_g, norm_mlp_g, w_in, w_a2, b_a2, a_norm_g, b_sink, t5_bias, c_rel_bias,
           w_out, w_up, w_down, final_norm_g):
    bsz, seq, _ = x_prompt.shape
    dec_b, dec_s, _ = x_sample.shape
    assert bsz == 1 and dec_s == CHUNK and TQ_PROMPT == NB_SAMPLE * CHUNK

    n_c = bsz + dec_b
    c_rows = -(-n_c // SUBLANES) * SUBLANES
    c_all = jnp.concatenate([c_prompt, c_sample, jnp.zeros((c_rows - n_c, D_MODEL), F32)], axis=0)
    mods = _ada(c_all, w_ada, b_ada).reshape(DEPTH, c_rows, 6, D_MODEL)
    mods = jnp.pad(mods, ((0, 0), (0, 0), (0, SUBLANES - 6), (0, 0)))

    rel_b = jnp.arange(TB_W) - (CHUNK - 1) - B_WINDOW
    bias_b, bias_c = _bias_tables(t5_bias, c_rel_bias, _t5_bucket(rel_b).astype(jnp.int32)[None, :])
    bias_b = bias_b.reshape(STACK, LK_B)
    bias_c = bias_c.reshape(DEPTH, STACK, LK_C)

    tri, lvl = _gla_constants(TQ_PROMPT)
    ob0 = D_AV
    orow = lambda h: slice(ob0 + h * HEAD_DIM, ob0 + (h + 1) * HEAD_DIM)

    win = _prep_w_in(w_in)
    wa2 = jnp.concatenate([w_a2, jnp.zeros((DEPTH, LANES - A_GATE_RANK, D_AQK), F32)], axis=1).astype(BF16)
    wout = jnp.concatenate([w_out[:, 0:D_AV], w_out[:, orow(0)], w_out[:, orow(2)], w_out[:, orow(1)],
                            w_out[:, orow(3)], w_out[:, D_AV + B_HEADS * HEAD_DIM:]], axis=1).astype(BF16)
    wup = _prep_w_up(w_up)
    wdown = w_down.astype(BF16).reshape(DEPTH, N_FF, FF_BLOCK, D_MODEL)
    sink = jnp.repeat(b_sink, CHUNK, axis=1)[:, :, None]
    stacked = (norm_mix_g[:, None, :], norm_mlp_g[:, None, :], a_norm_g[:, None, :], win, wa2,
               b_a2[:, None, :], wout, wup, wdown, bias_c, sink)
    shared = (final_norm_g[None, :], bias_b, tri, lvl)

    x_p = x_prompt.reshape(seq, D_MODEL)
    x_s = x_sample.reshape(dec_b * dec_s, D_MODEL)
    outs_p = []
    fmajor = lambda cache: jnp.transpose(cache, (0, 1, 3, 4, 2)).reshape(
        DEPTH, dec_b, cache.shape[3] * cache.shape[4], cache.shape[2])
    caches = (state_gla.reshape(DEPTH, dec_b, D_AQK, A_DV), fmajor(cache_b_k), fmajor(cache_b_v),
              fmajor(cache_c_k), fmajor(cache_c_v))
    outs_s = tuple(jnp.zeros(c.shape, F32) for c in caches)
    for l in range(DEPTH):
        final = l == DEPTH - 1
        res_p = _run_layer(x_p, mods[l, 0:bsz], l, stacked, shared, None, None, prompt=True, final=final,
                           tq=TQ_PROMPT, nb=None)
        x_p = res_p[0]
        outs_p.append(res_p[1:])

        res_s = _run_layer(x_s, mods[l, bsz:bsz + dec_b], l, stacked, shared, caches, outs_s, prompt=False,
                           final=final, tq=NB_SAMPLE * CHUNK, nb=NB_SAMPLE)
        x_s = res_s[0]
        outs_s = tuple(res_s[1:])

    sg_p = jnp.stack([o[0].reshape(bsz, A_HEADS, A_DK, A_DV) for o in outs_p])
    kb_p = jnp.stack([o[1].reshape(bsz, B_WINDOW, B_KV_HEADS, HEAD_DIM) for o in outs_p])
    vb_p = jnp.stack([o[2].reshape(bsz, B_WINDOW, B_KV_HEADS, HEAD_DIM) for o in outs_p])
    kc_p = jnp.stack([o[3].reshape(bsz, C_BAND, C_HEADS, HEAD_DIM) for o in outs_p])
    vc_p = jnp.stack([o[4].reshape(bsz, C_BAND, C_HEADS, HEAD_DIM) for o in outs_p])
    sg_s = outs_s[0].reshape(DEPTH, dec_b, A_HEADS, A_DK, A_DV)
    tmajor = lambda arr, heads: jnp.transpose(arr.reshape(DEPTH, dec_b, heads, HEAD_DIM, -1), (0, 1, 4, 2, 3))
    kb_s = tmajor(outs_s[1], B_KV_HEADS)
    vb_s = tmajor(outs_s[2], B_KV_HEADS)
    kc_s = tmajor(outs_s[3], C_HEADS)
    vc_s = tmajor(outs_s[4], C_HEADS)
    return (x_p.reshape(bsz, seq, D_MODEL), x_s.reshape(dec_b, dec_s, D_MODEL),
            sg_p, kb_p, vb_p, kc_p, vc_p, sg_s, kb_s, vb_s, kc_s, vc_s)
```

```python
import functools

import numpy as np
import jax
import jax.numpy as jnp
from jax import lax
from jax.experimental import pallas as pl
from jax.experimental.pallas import tpu as pltpu

D_MODEL = 1024
DEPTH = 2
CHUNK = 64
HEAD_DIM = 64
A_HEADS = 4
A_DK = 64
A_DV = 128
A_GATE_RANK = 16
A_GATE_TAU = 16.0
B_HEADS = 4
B_KV_HEADS = 2
B_WINDOW = 128
C_HEADS = 4
C_BAND = 512
C_CLIP = 256
T5_BUCKETS = 32
T5_MAX_DIST = 128
D_FF = 4 * D_MODEL
NORM_EPS = 1e-6
NEG_INF = -1e30

LANES = 128
SUBLANES = 8
VMEM_LIMIT_BYTES = 60000 * 1024
VMEM_LIMIT_SAMPLE_BYTES = 62 * 1024 * 1024

OFF_QA = 0
OFF_KA = OFF_QA + A_HEADS * A_DK
OFF_VA = OFF_KA + A_HEADS * A_DK
OFF_GA = OFF_VA + A_HEADS * A_DV
OFF_QB = OFF_GA + A_HEADS * A_DV
OFF_KB = OFF_QB + B_HEADS * HEAD_DIM
OFF_VB = OFF_KB + B_KV_HEADS * HEAD_DIM
OFF_QC = OFF_VB + B_KV_HEADS * HEAD_DIM
OFF_KC = OFF_QC + C_HEADS * HEAD_DIM
OFF_VC = OFF_KC + C_HEADS * HEAD_DIM
OFF_RA = OFF_VC + C_HEADS * HEAD_DIM
P_IN = OFF_RA + LANES
D_AQK = A_HEADS * A_DK
D_AV = A_HEADS * A_DV
D_BKV = B_KV_HEADS * HEAD_DIM
D_C = C_HEADS * HEAD_DIM
D_MIX = D_AV + B_HEADS * HEAD_DIM + D_C
LK_B = B_WINDOW + CHUNK
LK_C = C_BAND + CHUNK
STACK = 4 * CHUNK
GLA_LEVELS = (1, 2, 4, 8, 16, 32)
FF_BLOCK = 1024
N_FF = D_FF // FF_BLOCK
N_SEQ_OUT = 5

BF16 = jnp.bfloat16
F32 = jnp.float32


def _dot(a, b):
    return jnp.dot(a, b, preferred_element_type=F32)


def _dot_nt(a, b):
    return lax.dot_general(a, b, (((1,), (1,)), ((), ())), preferred_element_type=F32)


def _dot_tn(a, b):
    return lax.dot_general(a, b, (((0,), (0,)), ((), ())), preferred_element_type=F32)


def _rms(x, g):
    return x * lax.rsqrt(jnp.mean(x * x, axis=-1, keepdims=True) + NORM_EPS) * g


def _split3(x):
    hi = x.astype(BF16)
    r1 = x - hi.astype(F32)
    mid = r1.astype(BF16)
    lo = (r1 - mid.astype(F32)).astype(BF16)
    return hi, mid, lo


def _head_masks(width, per_head):
    lane = lax.broadcasted_iota(jnp.int32, (1, width), 1)
    return [(lane >= h * per_head) & (lane < (h + 1) * per_head) for h in range(width // per_head)]


def _stack_masked(x, masks):
    return jnp.concatenate([jnp.where(m, x, 0.0) for m in masks], axis=0)


def _tile_rows(x, n):
    return jnp.concatenate([x] * n, axis=0)


def _scores(q_st, segments, bias_st, maskrow):
    scores, off = [], 0
    for k, v, transposed in segments:
        n = k.shape[1] if transposed else k.shape[0]
        s = (_dot(q_st, k) if transposed else _dot_nt(q_st, k)) + bias_st[:, off:off + n]
        if maskrow is not None:
            s = s + maskrow[:, off:off + n]
        scores.append(s)
        off += n
    return scores


def _softmax_pv(scores, segments, sink):
    m = functools.reduce(jnp.maximum, [jnp.max(s, axis=1, keepdims=True) for s in scores])
    if sink is not None:
        m = jnp.maximum(m, sink)
    es = [jnp.exp(s - m) for s in scores]
    l = functools.reduce(jnp.add, [jnp.sum(e, axis=1, keepdims=True) for e in es])
    if sink is not None:
        l = l + jnp.exp(sink - m)
    o = None
    for e, (k, v, transposed) in zip(es, segments):
        part = _dot_nt(e.astype(BF16), v) if transposed else _dot(e.astype(BF16), v)
        o = part if o is None else o + part
    return o / l


def _level_exponent(b, m, t):
    bcast = lambda i: jnp.broadcast_to(b[i:i + 1], (SUBLANES, b.shape[1]))
    low_half = lax.broadcasted_iota(jnp.int32, (SUBLANES, b.shape[1]), 0) < SUBLANES // 2
    parts = []
    for g in range(CHUNK // SUBLANES):
        r = g * SUBLANES
        rows = b[r:r + SUBLANES]
        if m >= SUBLANES:
            ref = b[(r // (2 * m)) * 2 * m + m - 1:(r // (2 * m)) * 2 * m + m]
            parts.append(rows - ref if (r // m) % 2 == 1 else ref - rows)
        elif m == 4:
            parts.append(rows - bcast(r + 3))
        else:
            parts.append(rows - jnp.where(low_half, bcast(r + 1), bcast(r + 5)))
    d = jnp.concatenate(parts, axis=0)
    return d if m >= SUBLANES else jnp.where((t & m) != 0, d, -d)


def _odd_block_rows(x, m, groups):
    return jnp.concatenate([x[g * CHUNK + r:g * CHUNK + r + SUBLANES] for g in range(groups)
                            for r in range(0, CHUNK, SUBLANES) if (r // m) % 2 == 1], axis=0)


def _spread_odd_block_rows(y, m, groups):
    zero = jnp.zeros((SUBLANES, y.shape[1]), y.dtype)
    parts, i = [], 0
    for g in range(groups):
        for r in range(0, CHUNK, SUBLANES):
            if (r // m) % 2 == 1:
                parts.append(y[i:i + SUBLANES])
                i += SUBLANES
            else:
                parts.append(zero)
    return jnp.concatenate(parts, axis=0)


def _gla_levels(q, k, la, b, hm_a):
    t = lax.broadcasted_iota(jnp.int32, (CHUNK, D_AQK), 0)
    q_st = _stack_masked(q, hm_a)
    k16 = k.astype(BF16)
    prods = [_dot_nt(q_st.astype(BF16), k16)]
    for m in GLA_LEVELS:
        e = jnp.exp(la if m == 1 else _level_exponent(b, m, t))
        kt = k16 if m == 1 else (k * e).astype(BF16)
        if m >= SUBLANES:
            qt = _odd_block_rows(q_st, m, A_HEADS) * _tile_rows(_odd_block_rows(e, m, 1), A_HEADS)
            prods.append(_spread_odd_block_rows(_dot_nt(qt.astype(BF16), kt), m, A_HEADS))
        else:
            qt = (q_st * _tile_rows(e, A_HEADS)).astype(BF16)
            prods.append(_dot_nt(qt, kt))
    return q_st, prods


def _gla_finish(q_st, prods, k, v, b, s, lvl):
    attn = jnp.where(lvl == 0, prods[0], 0.0)
    for li in range(1, len(prods)):
        attn = jnp.where(lvl == li, prods[li], attn)
    qhat = (q_st * _tile_rows(jnp.exp(b), A_HEADS)).astype(BF16)
    o_inter = _dot(qhat, s.astype(BF16))
    v16 = v.astype(BF16)
    o_intra = jnp.concatenate(
        [_dot(attn[h * CHUNK:(h + 1) * CHUNK].astype(BF16), v16[:, h * A_DV:(h + 1) * A_DV])
         for h in range(A_HEADS)], axis=0)
    b_last = b[CHUNK - 1:CHUNK]
    khat = (k * jnp.exp(b_last - b)).astype(BF16)
    upd = _dot_tn(khat, v16)
    decay = jnp.broadcast_to(jnp.exp(b_last), (A_DV, D_AQK)).T
    s_new = s * decay + jnp.concatenate(
        [upd[h * A_DK:(h + 1) * A_DK, h * A_DV:(h + 1) * A_DV] for h in range(A_HEADS)], axis=0)
    return o_intra + o_inter, s_new


def _layer_kernel(*refs, prompt, final, nch, tq, nt):
    (x_ref, mods_ref, modp_ref, gmix_ref, gmlp_ref, gfin_ref, anorm_ref, win_ref, wa2_ref, ba2_ref, wout_ref,
     wup_ref, wdown_ref, biasb_ref, biasc_ref, sink_ref, tri_ref, lvl_ref) = refs[:18]
    pos = 18
    if not prompt:
        st_ref, cbk_ref, cbv_ref, cck_ref, ccv_ref = refs[pos:pos + 5]
        pos += 5 + N_SEQ_OUT
    y_ref, so_ref, kbo_ref, vbo_ref, kco_ref, vco_ref = refs[pos:pos + 6]
    pos += 6
    (h_scr, h2_scr, proj_scr, la_scr, b_scr, mix_scr, x1_scr, acc_scr, st_scr,
     kbw, vbw, kcw, vcw) = refs[pos:]

    step = pl.program_id(0)

    def mod(ref, c, j):
        return ref[0 if prompt else c, j:j + 1, :]

    @pl.when(step == 0)
    def _():
        x1_scr[...] = jnp.zeros_like(x1_scr)
        acc_scr[...] = jnp.zeros_like(acc_scr)
        if prompt:
            st_scr[...] = jnp.zeros_like(st_scr)
            kbw[...] = jnp.zeros_like(kbw)
            vbw[...] = jnp.zeros_like(vbw)
            kcw[...] = jnp.zeros_like(kcw)
            vcw[...] = jnp.zeros_like(vcw)

    def finish_previous_tile():
        gfin = gfin_ref[...]
        for c in range(nch):
            r = slice(c * CHUNK, (c + 1) * CHUNK)
            x2 = x1_scr[r, :] + mod(modp_ref, c, 5) * acc_scr[r, :]
            y_ref[r, :] = _rms(x2, gfin) if final else x2
        acc_scr[...] = jnp.zeros_like(acc_scr)

    @pl.when(step < nt)
    def _():
        gmix = gmix_ref[...]
        for c in range(nch):
            r = slice(c * CHUNK, (c + 1) * CHUNK)
            h = _rms(x_ref[r, :], gmix * (1.0 + mod(mods_ref, c, 1))) + mod(mods_ref, c, 0)
            h_scr[r, :] = h.astype(BF16)
        hmix = h_scr[...]
        cut1, cut2 = OFF_GA, OFF_QC
        ra = _dot(hmix, win_ref[:, OFF_RA:OFF_RA + LANES])
        proj_scr[:, 0:cut1] = _dot(hmix, win_ref[:, 0:cut1])
        z = _dot(ra.astype(BF16), wa2_ref[...]) + ba2_ref[...]
        la = (jnp.minimum(z, 0.0) - jnp.log1p(jnp.exp(-jnp.abs(z)))) * (1.0 / A_GATE_TAU)
        la_scr[...] = la
        proj_scr[:, cut1:cut2] = _dot(hmix, win_ref[:, cut1:cut2])
        tri = tri_ref[...]
        hi, mid, lo = _split3(la)
        b_scr[...] = _dot(tri, hi) + _dot(tri, mid) + _dot(tri, lo)
        proj_scr[:, cut2:OFF_RA] = _dot(hmix, win_ref[:, cut2:OFF_RA])

        if prompt:
            kbw[B_WINDOW:B_WINDOW + tq, :] = proj_scr[:, OFF_KB:OFF_KB + D_BKV].astype(BF16)
            vbw[B_WINDOW:B_WINDOW + tq, :] = proj_scr[:, OFF_VB:OFF_VB + D_BKV].astype(BF16)
            kcw[C_BAND:C_BAND + tq, :] = proj_scr[:, OFF_KC:OFF_KC + D_C].astype(BF16)
            vcw[C_BAND:C_BAND + tq, :] = proj_scr[:, OFF_VC:OFF_VC + D_C].astype(BF16)

        hm_a = _head_masks(D_AQK, A_DK)
        hm_c = _head_masks(D_C, HEAD_DIM)
        hm_b = _head_masks(D_BKV, HEAD_DIM)
        anorm = anorm_ref[...]

        def chunk_body(c, carry, with_mlp):
            r0 = pl.multiple_of(c * CHUNK, CHUNK)
            rows = pl.ds(r0, CHUNK)

            if prompt:
                seg_b = [(kbw[pl.ds(r0, LK_B), :], vbw[pl.ds(r0, LK_B), :], False)]
                seg_c = [(kcw[pl.ds(r0, LK_C), :], vcw[pl.ds(r0, LK_C), :], False)]
                gidx = step * nch + c
                lane_b = lax.broadcasted_iota(jnp.int32, (1, LK_B), 1)
                lane_c = lax.broadcasted_iota(jnp.int32, (1, LK_C), 1)
                mask_b = jnp.where(lane_b >= B_WINDOW - gidx * CHUNK, 0.0, NEG_INF)
                mask_c = jnp.where(lane_c >= C_BAND - gidx * CHUNK, 0.0, NEG_INF)
            else:
                kb_new = proj_scr[rows, OFF_KB:OFF_KB + D_BKV]
                vb_new = proj_scr[rows, OFF_VB:OFF_VB + D_BKV]
                kc_new = proj_scr[rows, OFF_KC:OFF_KC + D_C]
                vc_new = proj_scr[rows, OFF_VC:OFF_VC + D_C]
                seg_b = [(cbk_ref[c].astype(BF16), cbv_ref[c].astype(BF16), True),
                         (kb_new.astype(BF16), vb_new.astype(BF16), False)]
                seg_c = [(cck_ref[c].astype(BF16), ccv_ref[c].astype(BF16), True),
                         (kc_new.astype(BF16), vc_new.astype(BF16), False)]
                mask_b = mask_c = None

            q = proj_scr[rows, OFF_QA:OFF_QA + D_AQK] * (A_DK ** -0.5)
            k = proj_scr[rows, OFF_KA:OFF_KA + D_AQK]
            b = b_scr[rows, :]
            q_st, prods = _gla_levels(q, k, la_scr[rows, :], b, hm_a)
            qc = proj_scr[rows, OFF_QC:OFF_QC + D_C] * (HEAD_DIM ** -0.5)
            sc_c = _scores(_stack_masked(qc, hm_c).astype(BF16), seg_c, biasc_ref[...], mask_c)
            qb = proj_scr[rows, OFF_QB:OFF_QB + B_HEADS * HEAD_DIM] * (HEAD_DIM ** -0.5)
            qb_a, qb_b = qb[:, 0:D_BKV], qb[:, D_BKV:2 * D_BKV]
            qb_st = jnp.concatenate([jnp.where(hm_b[0], qb_a, 0.0), jnp.where(hm_b[0], qb_b, 0.0),
                                     jnp.where(hm_b[1], qb_a, 0.0), jnp.where(hm_b[1], qb_b, 0.0)], axis=0)
            sc_b = _scores(qb_st.astype(BF16), seg_b, biasb_ref[...], mask_b)

            if with_mlp:
                up = _dot(h2_scr[...], wup_ref[c])
                act = jnp.square(jnp.maximum(up, 0.0)).astype(BF16)

            v = proj_scr[rows, OFF_VA:OFF_VA + D_AV]
            s_old = st_scr[...] if prompt else st_ref[c]
            o_st, s_new = _gla_finish(q_st, prods, k, v, b, s_old, lvl_ref[...])
            if prompt:
                st_scr[...] = s_new
            else:
                so_ref[c] = s_new
            g = proj_scr[rows, OFF_GA:OFF_GA + D_AV]
            for h in range(A_HEADS):
                on = _rms(o_st[h * CHUNK:(h + 1) * CHUNK], anorm)
                gh = g[:, h * A_DV:(h + 1) * A_DV]
                mix_scr[rows, h * A_DV:(h + 1) * A_DV] = (on * (gh * jax.nn.sigmoid(gh))).astype(BF16)

            oc_st = _softmax_pv(sc_c, seg_c, None)
            oc = jnp.where(hm_c[0], oc_st[0:CHUNK], 0.0)
            for h in range(1, C_HEADS):
                oc = jnp.where(hm_c[h], oc_st[h * CHUNK:(h + 1) * CHUNK], oc)
            mix_scr[rows, D_AV + 2 * D_BKV:D_MIX] = oc.astype(BF16)

            ob_st = _softmax_pv(sc_b, seg_b, sink_ref[...])
            ob_a = jnp.where(hm_b[0], ob_st[0:CHUNK], ob_st[2 * CHUNK:3 * CHUNK])
            ob_b = jnp.where(hm_b[0], ob_st[CHUNK:2 * CHUNK], ob_st[3 * CHUNK:4 * CHUNK])
            mix_scr[rows, D_AV:D_AV + D_BKV] = ob_a.astype(BF16)
            mix_scr[rows, D_AV + D_BKV:D_AV + 2 * D_BKV] = ob_b.astype(BF16)

            if with_mlp:
                acc_scr[...] += _dot(act, wdown_ref[c])

            if not prompt:
                for dst, src, new, width in ((kbo_ref, cbk_ref, kb_new, B_WINDOW), (vbo_ref, cbv_ref, vb_new, B_WINDOW),
                                             (kco_ref, cck_ref, kc_new, C_BAND), (vco_ref, ccv_ref, vc_new, C_BAND)):
                    dst[c, :, 0:width - CHUNK] = src[c, :, CHUNK:width]
                    dst[c, :, width - CHUNK:width] = new.T
            return carry

        @pl.when(step == 0)
        def _():
            lax.fori_loop(0, nch, functools.partial(chunk_body, with_mlp=False), 0)

        @pl.when(step > 0)
        def _():
            lax.fori_loop(0, nch, functools.partial(chunk_body, with_mlp=True), 0, unroll=2)

        if prompt:
            kbw[0:B_WINDOW, :] = kbw[tq:tq + B_WINDOW, :]
            vbw[0:B_WINDOW, :] = vbw[tq:tq + B_WINDOW, :]
            kcw[0:C_BAND, :] = kcw[tq:tq + C_BAND, :]
            vcw[0:C_BAND, :] = vcw[tq:tq + C_BAND, :]

            so_ref[...] = st_scr[...]
            kbo_ref[...] = proj_scr[tq - B_WINDOW:tq, OFF_KB:OFF_KB + D_BKV]
            vbo_ref[...] = proj_scr[tq - B_WINDOW:tq, OFF_VB:OFF_VB + D_BKV]
            rb = kco_ref.shape[0]
            kco_ref[...] = proj_scr[tq - rb:tq, OFF_KC:OFF_KC + D_C]
            vco_ref[...] = proj_scr[tq - rb:tq, OFF_VC:OFF_VC + D_C]

        mixed = _dot(mix_scr[...], wout_ref[...])

        finish_previous_tile()

        gmlp = gmlp_ref[...]
        for c in range(nch):
            r = slice(c * CHUNK, (c + 1) * CHUNK)
            x1 = x_ref[r, :] + mod(mods_ref, c, 2) * mixed[r, :]
            x1_scr[r, :] = x1
            h2 = _rms(x1, gmlp * (1.0 + mod(mods_ref, c, 4))) + mod(mods_ref, c, 3)
            h2_scr[r, :] = h2.astype(BF16)

    @pl.when(step == nt)
    def _():
        def mlp_block(c, carry):
            up = _dot(h2_scr[...], wup_ref[c])
            act = jnp.square(jnp.maximum(up, 0.0)).astype(BF16)
            acc_scr[...] += _dot(act, wdown_ref[c])
            return carry

        lax.fori_loop(0, nch, mlp_block, 0)
        finish_previous_tile()


def _layer_spec(arr, layer):
    nd = arr.ndim - 1
    return pl.BlockSpec((None,) + arr.shape[1:], lambda i, _l=layer, _nd=nd: (_l,) + (0,) * _nd,
                        pipeline_mode=pl.Buffered(1))


def _const_spec(arr):
    nd = arr.ndim
    return pl.BlockSpec(arr.shape, lambda i, _nd=nd: (0,) * _nd, pipeline_mode=pl.Buffered(1))


def _run_layer(x, mods, layer, stacked, shared, caches, carried, *, prompt, final, tq, nb):
    rows = x.shape[0]
    nch = tq // CHUNK
    nt = rows // tq
    assert rows % tq == 0 and nch == N_FF
    (gmix, gmlp, anorm, win, wa2, ba2, wout, wup, wdown, biasc, sink) = stacked
    (gfin, biasb, tri, lvl) = shared
    cur = lambda i: jnp.minimum(i, nt - 1)
    prv = lambda i: jnp.maximum(i - 1, 0)

    nmod = 1 if prompt else nb
    in_specs = [pl.BlockSpec((tq, D_MODEL), lambda i: (cur(i), 0)),
                pl.BlockSpec((nmod, SUBLANES, D_MODEL), lambda i: (0 if prompt else cur(i), 0, 0)),
                pl.BlockSpec((nmod, SUBLANES, D_MODEL), lambda i: (0 if prompt else prv(i), 0, 0))]
    ls = functools.partial(_layer_spec, layer=layer)
    in_specs += [ls(gmix), ls(gmlp), _const_spec(gfin), ls(anorm), ls(win), ls(wa2), ls(ba2), ls(wout),
                 ls(wup), ls(wdown), _const_spec(biasb), ls(biasc), ls(sink), _const_spec(tri), _const_spec(lvl)]
    args = [x, mods, mods, gmix, gmlp, gfin, anorm, win, wa2, ba2, wout, wup, wdown, biasb, biasc, sink, tri, lvl]

    y_spec = pl.BlockSpec((tq, D_MODEL), lambda i: (prv(i), 0))
    if prompt:
        assert tq >= B_WINDOW and (tq % C_BAND == 0 or C_BAND % tq == 0)
        rb = min(tq, C_BAND)
        first = nt - C_BAND // rb
        out_shape = [jax.ShapeDtypeStruct((rows, D_MODEL), F32),
                     jax.ShapeDtypeStruct((D_AQK, A_DV), F32),
                     jax.ShapeDtypeStruct((B_WINDOW, D_BKV), F32),
                     jax.ShapeDtypeStruct((B_WINDOW, D_BKV), F32),
                     jax.ShapeDtypeStruct((C_BAND, D_C), F32),
                     jax.ShapeDtypeStruct((C_BAND, D_C), F32)]
        band = pl.BlockSpec((rb, D_C), lambda i: (jnp.maximum(cur(i) - first, 0), 0))
        out_specs = [y_spec,
                     pl.BlockSpec((D_AQK, A_DV), lambda i: (0, 0)),
                     pl.BlockSpec((B_WINDOW, D_BKV), lambda i: (0, 0)),
                     pl.BlockSpec((B_WINDOW, D_BKV), lambda i: (0, 0)),
                     band, band]
        kbw_rows, kcw_rows = B_WINDOW + tq, C_BAND + tq
        aliases = {}
    else:
        assert nb == nch
        nseq = rows // CHUNK
        st, cbk, cbv, cck, ccv = caches
        assert len(carried) == N_SEQ_OUT
        seq = lambda r, w: pl.BlockSpec((None, nb, r, w), lambda i, _l=layer: (_l, cur(i), 0, 0))
        seq_dims = [(D_AQK, A_DV), (D_BKV, B_WINDOW), (D_BKV, B_WINDOW), (D_C, C_BAND), (D_C, C_BAND)]
        in_specs += [seq(r, w) for r, w in seq_dims]
        aliases = {len(in_specs) + j: 1 + j for j in range(N_SEQ_OUT)}
        in_specs += [pl.BlockSpec(memory_space=pl.ANY)] * N_SEQ_OUT
        args += [st, cbk, cbv, cck, ccv, *carried]
        out_shape = [jax.ShapeDtypeStruct((rows, D_MODEL), F32)] + [
            jax.ShapeDtypeStruct((DEPTH, nseq, r, w), F32) for r, w in seq_dims]
        out_specs = [y_spec] + [seq(r, w) for r, w in seq_dims]
        kbw_rows, kcw_rows = SUBLANES * 2, SUBLANES * 2

    scratch = [pltpu.VMEM((tq, D_MODEL), BF16),
               pltpu.VMEM((tq, D_MODEL), BF16),
               pltpu.VMEM((tq, OFF_RA), F32),
               pltpu.VMEM((tq, D_AQK), F32),
               pltpu.VMEM((tq, D_AQK), F32),
               pltpu.VMEM((tq, D_MIX), BF16),
               pltpu.VMEM((tq, D_MODEL), F32),
               pltpu.VMEM((tq, D_MODEL), F32),
               pltpu.VMEM((D_AQK, A_DV), F32),
               pltpu.VMEM((kbw_rows, D_BKV), BF16), pltpu.VMEM((kbw_rows, D_BKV), BF16),
               pltpu.VMEM((kcw_rows, D_C), BF16), pltpu.VMEM((kcw_rows, D_C), BF16)]

    kern = functools.partial(_layer_kernel, prompt=prompt, final=final, nch=nch, tq=tq, nt=nt)
    return pl.pallas_call(
        kern,
        grid=(nt + 1,),
        in_specs=in_specs,
        out_specs=out_specs,
        out_shape=out_shape,
        scratch_shapes=scratch,
        input_output_aliases=aliases,
        compiler_params=pltpu.CompilerParams(
            dimension_semantics=("arbitrary",),
            vmem_limit_bytes=VMEM_LIMIT_BYTES if prompt else VMEM_LIMIT_SAMPLE_BYTES),
        name=("layer_prompt" if prompt else "layer_sample") + ("_final" if final else ""),
    )(*args)


ADA_BLOCK = 3072


def _ada_kernel(c_ref, w_ref, b_ref, o_ref):
    c = c_ref[...]
    sc = c * jax.nn.sigmoid(c)
    sc_hi = sc.astype(BF16)
    sc_lo = (sc - sc_hi.astype(F32)).astype(BF16)
    w = w_ref[0]
    w_hi = w.astype(BF16)
    w_lo = (w - w_hi.astype(F32)).astype(BF16)
    o_ref[0] = _dot(sc_hi, w_hi) + _dot(sc_lo, w_hi) + _dot(sc_hi, w_lo) + b_ref[0]


def _ada(c_all, w_ada, b_ada):
    rows = c_all.shape[0]
    n = 6 * D_MODEL
    return pl.pallas_call(
        _ada_kernel,
        grid=(DEPTH, n // ADA_BLOCK),
        in_specs=[pl.BlockSpec((rows, D_MODEL), lambda l, j: (0, 0)),
                  pl.BlockSpec((1, D_MODEL, ADA_BLOCK), lambda l, j: (l, 0, j)),
                  pl.BlockSpec((1, 1, ADA_BLOCK), lambda l, j: (l, 0, j))],
        out_specs=pl.BlockSpec((1, rows, ADA_BLOCK), lambda l, j: (l, 0, j)),
        out_shape=jax.ShapeDtypeStruct((DEPTH, rows, n), F32),
        compiler_params=pltpu.CompilerParams(dimension_semantics=("arbitrary", "arbitrary"),
                                             vmem_limit_bytes=VMEM_LIMIT_BYTES),
        name="adaln",
    )(c_all, w_ada, b_ada.reshape(DEPTH, 1, n))


TB_W = 2 * LANES
TC_W = 5 * LANES
TC_FLAT = TC_W - (CHUNK + C_CLIP + 1)


def _shear(f, width):
    tiled = jnp.broadcast_to(f, (CHUNK, width))
    return pltpu.roll(tiled, width - (CHUNK - 1), 1, stride=1, stride_axis=0)


def _bias_kernel(t5_ref, crel_ref, bucket_ref, ob_ref, oc_ref):
    bucket = bucket_ref[...]
    for h in range(B_HEADS):
        def b_body(i, acc, h=h):
            return jnp.where(bucket == i, t5_ref[h, i], acc)
        f = lax.fori_loop(0, T5_BUCKETS, b_body, jnp.zeros(bucket.shape, F32))
        ob_ref[h] = _shear(f, TB_W)[:, 0:LK_B]
    lane = lax.broadcasted_iota(jnp.int32, (1, TC_W), 1)
    for l in range(DEPTH):
        for h in range(C_HEADS):
            row = crel_ref[l, h:h + 1, :]
            f = jnp.where(lane < TC_FLAT, row[:, 0:1], pltpu.roll(row, TC_FLAT, 1))
            oc_ref[l, h] = _shear(f, TC_W)[:, 0:LK_C]


def _bias_tables(t5_bias, c_rel_bias, bucket):
    smem = pl.BlockSpec(memory_space=pltpu.SMEM)
    vmem = pl.BlockSpec(memory_space=pltpu.VMEM)
    crel = jnp.pad(jnp.swapaxes(c_rel_bias, 1, 2),
                   ((0, 0), (0, SUBLANES - C_HEADS), (0, TC_W - (2 * C_CLIP + 1))))
    return pl.pallas_call(
        _bias_kernel,
        in_specs=[smem, vmem, vmem],
        out_specs=[vmem, vmem],
        out_shape=[jax.ShapeDtypeStruct((B_HEADS, CHUNK, LK_B), F32),
                   jax.ShapeDtypeStruct((DEPTH, C_HEADS, CHUNK, LK_C), F32)],
        name="bias_tables",
    )(t5_bias.T, crel, bucket)


def _t5_bucket(rel):
    half = T5_BUCKETS // 2
    max_exact = half // 2
    steps = half - max_exact
    thresholds = [int(np.ceil(max_exact * (T5_MAX_DIST / max_exact) ** (k / steps) - 1e-9)) for k in range(1, steps)]
    n = jnp.abs(rel)
    large = jnp.minimum(max_exact + sum((n >= t).astype(jnp.int32) for t in thresholds), half - 1)
    return jnp.where(rel > 0, half, 0) + jnp.where(n < max_exact, n, large)


def _gla_constants(tq):
    r = np.arange(tq)
    tri = ((r[:, None] // CHUNK == r[None, :] // CHUNK) & (r[None, :] <= r[:, None])).astype(np.float32)
    t = np.arange(CHUNK)[:, None]
    s = np.arange(CHUNK)[None, :]
    lvl = np.full((CHUNK, CHUNK), -1, np.int32)
    lvl[t == s] = 0
    for li, m in enumerate(GLA_LEVELS):
        sel = ((t // m) % 2 == 1) & ((s // m) == (t // m) - 1)
        lvl[np.broadcast_to(sel, lvl.shape)] = li + 1
    return jnp.asarray(tri, BF16), jnp.asarray(np.tile(lvl, (A_HEADS, 1)))


SRC_RA = 2 * D_AQK + 2 * D_AV
SRC_QB = SRC_RA + A_GATE_RANK
SRC_KB = SRC_QB + B_HEADS * HEAD_DIM
D_IN = SRC_KB + 2 * D_BKV + 3 * D_C
QB_HEAD_ORDER = (0, 2, 1, 3)


def _w_in_segments():
    segs = [(OFF_QA, 0, SRC_RA)]
    segs += [(OFF_QB + i * HEAD_DIM, SRC_QB + h * HEAD_DIM, HEAD_DIM) for i, h in enumerate(QB_HEAD_ORDER)]
    segs += [(OFF_KB, SRC_KB, D_IN - SRC_KB), (OFF_RA, SRC_RA, A_GATE_RANK)]
    return segs


def _w_in_kernel(w_ref, o_ref):
    segs = _w_in_segments()
    for blk in range(P_IN // LANES):
        lo, hi = blk * LANES, (blk + 1) * LANES
        rows, col = [], lo
        while col < hi:
            hit = [(d, s, w) for d, s, w in segs if d <= col < d + w]
            if hit:
                d, s, w = hit[0]
                n = min(hi, d + w) - col
                rows.append(w_ref[s + col - d:s + col - d + n, :])
            else:
                n = hi - col
                rows.append(jnp.zeros((n, D_MODEL), F32))
            col += n
        o_ref[:, lo:hi] = jnp.concatenate(rows, axis=0).T.astype(BF16)


def _prep_w_in(w_in):
    w_fm = jnp.swapaxes(w_in, 1, 2)
    return pl.pallas_call(
        _w_in_kernel,
        grid=(DEPTH,),
        in_specs=[pl.BlockSpec((None, D_IN, D_MODEL), lambda l: (l, 0, 0))],
        out_specs=pl.BlockSpec((None, D_MODEL, P_IN), lambda l: (l, 0, 0)),
        out_shape=jax.ShapeDtypeStruct((DEPTH, D_MODEL, P_IN), BF16),
        compiler_params=pltpu.CompilerParams(dimension_semantics=("arbitrary",),
                                             vmem_limit_bytes=VMEM_LIMIT_BYTES),
        name="prep_w_in",
    )(w_fm)


def _cast_kernel(w_ref, o_ref):
    o_ref[...] = w_ref[...].astype(BF16)


def _prep_w_up(w_up):
    return pl.pallas_call(
        _cast_kernel,
        grid=(DEPTH, N_FF),
        in_specs=[pl.BlockSpec((None, D_MODEL, FF_BLOCK), lambda l, j: (l, 0, j))],
        out_specs=pl.BlockSpec((None, None, D_MODEL, FF_BLOCK), lambda l, j: (l, j, 0, 0)),
        out_shape=jax.ShapeDtypeStruct((DEPTH, N_FF, D_MODEL, FF_BLOCK), BF16),
        compiler_params=pltpu.CompilerParams(dimension_semantics=("arbitrary", "arbitrary"),
                                             vmem_limit_bytes=VMEM_LIMIT_BYTES),
        name="prep_w_up",
    )(w_up)


def _prep_w_down(w_down):
    return pl.pallas_call(
        _cast_kernel,
        grid=(DEPTH, N_FF),
        in_specs=[pl.BlockSpec((None, FF_BLOCK, D_MODEL), lambda l, j: (l, j, 0))],
        out_specs=pl.BlockSpec((None, None, FF_BLOCK, D_MODEL), lambda l, j: (l, j, 0, 0)),
        out_shape=jax.ShapeDtypeStruct((DEPTH, N_FF, FF_BLOCK, D_MODEL), BF16),
        compiler_params=pltpu.CompilerParams(dimension_semantics=("arbitrary", "arbitrary"),
                                             vmem_limit_bytes=VMEM_LIMIT_BYTES),
        name="prep_w_down",
    )(w_down)


TQ_PROMPT = 256
NB_SAMPLE = 4


def kernel(x_prompt, x_sample, c_prompt, c_sample, state_gla, cache_b_k, cache_b_v, cache_c_k, cache_c_v,
           w_ada, b_ada, norm_mix_g, norm_mlp_g, w_in, w_a2, b_a2, a_norm_g, b_sink, t5_bias, c_rel_bias,
           w_out, w_up, w_down, final_norm_g):
    bsz, seq, _ = x_prompt.shape
    dec_b, dec_s, _ = x_sample.shape
    assert bsz == 1 and dec_s == CHUNK and TQ_PROMPT == NB_SAMPLE * CHUNK

    n_c = bsz + dec_b
    c_rows = -(-n_c // SUBLANES) * SUBLANES
    c_all = jnp.concatenate([c_prompt, c_sample, jnp.zeros((c_rows - n_c, D_MODEL), F32)], axis=0)
    mods = _ada(c_all, w_ada, b_ada).reshape(DEPTH, c_rows, 6, D_MODEL)
    mods = jnp.pad(mods, ((0, 0), (0, 0), (0, SUBLANES - 6), (0, 0)))

    rel_b = jnp.arange(TB_W) - (CHUNK - 1) - B_WINDOW
    bias_b, bias_c = _bias_tables(t5_bias, c_rel_bias, _t5_bucket(rel_b).astype(jnp.int32)[None, :])
    bias_b = bias_b.reshape(STACK, LK_B)
    bias_c = bias_c.reshape(DEPTH, STACK, LK_C)

    tri, lvl = _gla_constants(TQ_PROMPT)
    ob0 = D_AV
    orow = lambda h: slice(ob0 + h * HEAD_DIM, ob0 + (h + 1) * HEAD_DIM)

    win = _prep_w_in(w_in)
    wa2 = jnp.concatenate([w_a2, jnp.zeros((DEPTH, LANES - A_GATE_RANK, D_AQK), F32)], axis=1).astype(BF16)
    wout = jnp.concatenate([w_out[:, 0:D_AV], w_out[:, orow(0)], w_out[:, orow(2)], w_out[:, orow(1)],
                            w_out[:, orow(3)], w_out[:, D_AV + B_HEADS * HEAD_DIM:]], axis=1).astype(BF16)
    wup = _prep_w_up(w_up)
    wdown = _prep_w_down(w_down)
    sink = jnp.repeat(b_sink, CHUNK, axis=1)[:, :, None]
    stacked = (norm_mix_g[:, None, :], norm_mlp_g[:, None, :], a_norm_g[:, None, :], win, wa2,
               b_a2[:, None, :], wout, wup, wdown, bias_c, sink)
    shared = (final_norm_g[None, :], bias_b, tri, lvl)

    x_p = x_prompt.reshape(seq, D_MODEL)
    x_s = x_sample.reshape(dec_b * dec_s, D_MODEL)
    outs_p = []
    fmajor = lambda cache: jnp.transpose(cache, (0, 1, 3, 4, 2)).reshape(
        DEPTH, dec_b, cache.shape[3] * cache.shape[4], cache.shape[2])
    caches = (state_gla.reshape(DEPTH, dec_b, D_AQK, A_DV), fmajor(cache_b_k), fmajor(cache_b_v),
              fmajor(cache_c_k), fmajor(cache_c_v))
    outs_s = tuple(jnp.zeros(c.shape, F32) for c in caches)
    for l in range(DEPTH):
        final = l == DEPTH - 1
        res_p = _run_layer(x_p, mods[l, 0:bsz], l, stacked, shared, None, None, prompt=True, final=final,
                           tq=TQ_PROMPT, nb=None)
        x_p = res_p[0]
        outs_p.append(res_p[1:])

        res_s = _run_layer(x_s, mods[l, bsz:bsz + dec_b], l, stacked, shared, caches, outs_s, prompt=False,
                           final=final, tq=NB_SAMPLE * CHUNK, nb=NB_SAMPLE)
        x_s = res_s[0]
        outs_s = tuple(res_s[1:])

    sg_p = jnp.stack([o[0].reshape(bsz, A_HEADS, A_DK, A_DV) for o in outs_p])
    kb_p = jnp.stack([o[1].reshape(bsz, B_WINDOW, B_KV_HEADS, HEAD_DIM) for o in outs_p])
    vb_p = jnp.stack([o[2].reshape(bsz, B_WINDOW, B_KV_HEADS, HEAD_DIM) for o in outs_p])
    kc_p = jnp.stack([o[3].reshape(bsz, C_BAND, C_HEADS, HEAD_DIM) for o in outs_p])
    vc_p = jnp.stack([o[4].reshape(bsz, C_BAND, C_HEADS, HEAD_DIM) for o in outs_p])
    sg_s = outs_s[0].reshape(DEPTH, dec_b, A_HEADS, A_DK, A_DV)
    tmajor = lambda arr, heads: jnp.transpose(arr.reshape(DEPTH, dec_b, heads, HEAD_DIM, -1), (0, 1, 4, 2, 3))
    kb_s = tmajor(outs_s[1], B_KV_HEADS)
    vb_s = tmajor(outs_s[2], B_KV_HEADS)
    kc_s = tmajor(outs_s[3], C_HEADS)
    vc_s = tmajor(outs_s[4], C_HEADS)
    return (x_p.reshape(bsz, seq, D_MODEL), x_s.reshape(dec_b, dec_s, D_MODEL),
            sg_p, kb_p, vb_p, kc_p, vc_p, sg_s, kb_s, vb_s, kc_s, vc_s)
```

```python
import functools

import numpy as np
import jax
import jax.numpy as jnp
from jax import lax
from jax.experimental import pallas as pl
from jax.experimental.pallas import tpu as pltpu

D_MODEL = 1024
DEPTH = 2
CHUNK = 64
HEAD_DIM = 64
A_HEADS = 4
A_DK = 64
A_DV = 128
A_GATE_RANK = 16
A_GATE_TAU = 16.0
B_HEADS = 4
B_KV_HEADS = 2
B_WINDOW = 128
C_HEADS = 4
C_BAND = 512
C_CLIP = 256
T5_BUCKETS = 32
T5_MAX_DIST = 128
D_FF = 4 * D_MODEL
NORM_EPS = 1e-6
NEG_INF = -1e30

LANES = 128
SUBLANES = 8
VMEM_LIMIT_BYTES = 60000 * 1024
VMEM_LIMIT_SAMPLE_BYTES = 62 * 1024 * 1024

OFF_QA = 0
OFF_KA = OFF_QA + A_HEADS * A_DK
OFF_VA = OFF_KA + A_HEADS * A_DK
OFF_GA = OFF_VA + A_HEADS * A_DV
OFF_QB = OFF_GA + A_HEADS * A_DV
OFF_KB = OFF_QB + B_HEADS * HEAD_DIM
OFF_VB = OFF_KB + B_KV_HEADS * HEAD_DIM
OFF_QC = OFF_VB + B_KV_HEADS * HEAD_DIM
OFF_KC = OFF_QC + C_HEADS * HEAD_DIM
OFF_VC = OFF_KC + C_HEADS * HEAD_DIM
OFF_RA = OFF_VC + C_HEADS * HEAD_DIM
P_IN = OFF_RA + LANES
D_AQK = A_HEADS * A_DK
D_AV = A_HEADS * A_DV
D_BKV = B_KV_HEADS * HEAD_DIM
D_C = C_HEADS * HEAD_DIM
D_MIX = D_AV + B_HEADS * HEAD_DIM + D_C
LK_B = B_WINDOW + CHUNK
LK_C = C_BAND + CHUNK
STACK = 4 * CHUNK
GLA_LEVELS = (1, 2, 4, 8, 16, 32)
FF_BLOCK = 1024
N_FF = D_FF // FF_BLOCK
N_SEQ_OUT = 5

BF16 = jnp.bfloat16
F32 = jnp.float32


def _dot(a, b):
    return jnp.dot(a, b, preferred_element_type=F32)


def _dot_nt(a, b):
    return lax.dot_general(a, b, (((1,), (1,)), ((), ())), preferred_element_type=F32)


def _dot_tn(a, b):
    return lax.dot_general(a, b, (((0,), (0,)), ((), ())), preferred_element_type=F32)


def _rms(x, g):
    return x * lax.rsqrt(jnp.mean(x * x, axis=-1, keepdims=True) + NORM_EPS) * g


def _split3(x):
    hi = x.astype(BF16)
    r1 = x - hi.astype(F32)
    mid = r1.astype(BF16)
    lo = (r1 - mid.astype(F32)).astype(BF16)
    return hi, mid, lo


def _head_masks(width, per_head):
    lane = lax.broadcasted_iota(jnp.int32, (1, width), 1)
    return [(lane >= h * per_head) & (lane < (h + 1) * per_head) for h in range(width // per_head)]


def _stack_masked(x, masks):
    return jnp.concatenate([jnp.where(m, x, 0.0) for m in masks], axis=0)


def _tile_rows(x, n):
    return jnp.concatenate([x] * n, axis=0)


def _scores(q_st, segments, bias_st, maskrow):
    scores, off = [], 0
    for k, v, transposed in segments:
        n = k.shape[1] if transposed else k.shape[0]
        s = (_dot(q_st, k) if transposed else _dot_nt(q_st, k)) + bias_st[:, off:off + n]
        if maskrow is not None:
            s = s + maskrow[:, off:off + n]
        scores.append(s)
        off += n
    return scores


def _softmax_pv(scores, segments, sink):
    m = functools.reduce(jnp.maximum, [jnp.max(s, axis=1, keepdims=True) for s in scores])
    if sink is not None:
        m = jnp.maximum(m, sink)
    es = [jnp.exp(s - m) for s in scores]
    l = functools.reduce(jnp.add, [jnp.sum(e, axis=1, keepdims=True) for e in es])
    if sink is not None:
        l = l + jnp.exp(sink - m)
    o = None
    for e, (k, v, transposed) in zip(es, segments):
        part = _dot_nt(e.astype(BF16), v) if transposed else _dot(e.astype(BF16), v)
        o = part if o is None else o + part
    return o / l


def _level_exponent(b, m, t):
    bcast = lambda i: jnp.broadcast_to(b[i:i + 1], (SUBLANES, b.shape[1]))
    low_half = lax.broadcasted_iota(jnp.int32, (SUBLANES, b.shape[1]), 0) < SUBLANES // 2
    parts = []
    for g in range(CHUNK // SUBLANES):
        r = g * SUBLANES
        rows = b[r:r + SUBLANES]
        if m >= SUBLANES:
            ref = b[(r // (2 * m)) * 2 * m + m - 1:(r // (2 * m)) * 2 * m + m]
            parts.append(rows - ref if (r // m) % 2 == 1 else ref - rows)
        elif m == 4:
            parts.append(rows - bcast(r + 3))
        else:
            parts.append(rows - jnp.where(low_half, bcast(r + 1), bcast(r + 5)))
    d = jnp.concatenate(parts, axis=0)
    return d if m >= SUBLANES else jnp.where((t & m) != 0, d, -d)


def _odd_block_rows(x, m, groups):
    return jnp.concatenate([x[g * CHUNK + r:g * CHUNK + r + SUBLANES] for g in range(groups)
                            for r in range(0, CHUNK, SUBLANES) if (r // m) % 2 == 1], axis=0)


def _spread_odd_block_rows(y, m, groups):
    zero = jnp.zeros((SUBLANES, y.shape[1]), y.dtype)
    parts, i = [], 0
    for g in range(groups):
        for r in range(0, CHUNK, SUBLANES):
            if (r // m) % 2 == 1:
                parts.append(y[i:i + SUBLANES])
                i += SUBLANES
            else:
                parts.append(zero)
    return jnp.concatenate(parts, axis=0)


def _gla_levels(q, k, la, b, hm_a):
    t = lax.broadcasted_iota(jnp.int32, (CHUNK, D_AQK), 0)
    q_st = _stack_masked(q, hm_a)
    k16 = k.astype(BF16)
    prods = [_dot_nt(q_st.astype(BF16), k16)]
    for m in GLA_LEVELS:
        e = jnp.exp(la if m == 1 else _level_exponent(b, m, t))
        kt = k16 if m == 1 else (k * e).astype(BF16)
        if m >= SUBLANES:
            qt = _odd_block_rows(q_st, m, A_HEADS) * _tile_rows(_odd_block_rows(e, m, 1), A_HEADS)
            prods.append(_spread_odd_block_rows(_dot_nt(qt.astype(BF16), kt), m, A_HEADS))
        else:
            qt = (q_st * _tile_rows(e, A_HEADS)).astype(BF16)
            prods.append(_dot_nt(qt, kt))
    return q_st, prods


def _gla_finish(q_st, prods, k, v, b, s, lvl):
    attn = jnp.where(lvl == 0, prods[0], 0.0)
    for li in range(1, len(prods)):
        attn = jnp.where(lvl == li, prods[li], attn)
    qhat = (q_st * _tile_rows(jnp.exp(b), A_HEADS)).astype(BF16)
    o_inter = _dot(qhat, s.astype(BF16))
    v16 = v.astype(BF16)
    o_intra = jnp.concatenate(
        [_dot(attn[h * CHUNK:(h + 1) * CHUNK].astype(BF16), v16[:, h * A_DV:(h + 1) * A_DV])
         for h in range(A_HEADS)], axis=0)
    b_last = b[CHUNK - 1:CHUNK]
    khat = (k * jnp.exp(b_last - b)).astype(BF16)
    upd = _dot_tn(khat, v16)
    decay = jnp.broadcast_to(jnp.exp(b_last), (A_DV, D_AQK)).T
    s_new = s * decay + jnp.concatenate(
        [upd[h * A_DK:(h + 1) * A_DK, h * A_DV:(h + 1) * A_DV] for h in range(A_HEADS)], axis=0)
    return o_intra + o_inter, s_new


def _layer_kernel(*refs, prompt, final, nch, tq, nt):
    (x_ref, mods_ref, modp_ref, gmix_ref, gmlp_ref, gfin_ref, anorm_ref, win_ref, wa2_ref, ba2_ref, wout_ref,
     wup_ref, wdown_ref, biasb_ref, biasc_ref, sink_ref, tri_ref, lvl_ref) = refs[:18]
    pos = 18
    if not prompt:
        st_ref, cbk_ref, cbv_ref, cck_ref, ccv_ref = refs[pos:pos + 5]
        pos += 5 + N_SEQ_OUT
    y_ref, so_ref, kbo_ref, vbo_ref, kco_ref, vco_ref = refs[pos:pos + 6]
    pos += 6
    (h_scr, h2_scr, proj_scr, la_scr, b_scr, mix_scr, x1_scr, acc_scr, st_scr,
     kbw, vbw, kcw, vcw) = refs[pos:]

    step = pl.program_id(0)

    def mod(ref, c, j):
        return ref[0 if prompt else c, j:j + 1, :]

    @pl.when(step == 0)
    def _():
        x1_scr[...] = jnp.zeros_like(x1_scr)
        acc_scr[...] = jnp.zeros_like(acc_scr)
        if prompt:
            st_scr[...] = jnp.zeros_like(st_scr)
            kbw[...] = jnp.zeros_like(kbw)
            vbw[...] = jnp.zeros_like(vbw)
            kcw[...] = jnp.zeros_like(kcw)
            vcw[...] = jnp.zeros_like(vcw)

    def finish_previous_tile():
        gfin = gfin_ref[...]
        for c in range(nch):
            r = slice(c * CHUNK, (c + 1) * CHUNK)
            x2 = x1_scr[r, :] + mod(modp_ref, c, 5) * acc_scr[r, :]
            y_ref[r, :] = _rms(x2, gfin) if final else x2
        acc_scr[...] = jnp.zeros_like(acc_scr)

    @pl.when(step < nt)
    def _():
        gmix = gmix_ref[...]
        for c in range(nch):
            r = slice(c * CHUNK, (c + 1) * CHUNK)
            h = _rms(x_ref[r, :], gmix * (1.0 + mod(mods_ref, c, 1))) + mod(mods_ref, c, 0)
            h_scr[r, :] = h.astype(BF16)
        hmix = h_scr[...]
        cut1, cut2 = OFF_GA, OFF_QC
        ra = _dot(hmix, win_ref[:, OFF_RA:OFF_RA + LANES])
        proj_scr[:, 0:cut1] = _dot(hmix, win_ref[:, 0:cut1])
        z = _dot(ra.astype(BF16), wa2_ref[...]) + ba2_ref[...]
        la = (jnp.minimum(z, 0.0) - jnp.log1p(jnp.exp(-jnp.abs(z)))) * (1.0 / A_GATE_TAU)
        la_scr[...] = la
        proj_scr[:, cut1:cut2] = _dot(hmix, win_ref[:, cut1:cut2])
        tri = tri_ref[...]
        hi, mid, lo = _split3(la)
        b_scr[...] = _dot(tri, hi) + _dot(tri, mid) + _dot(tri, lo)
        proj_scr[:, cut2:OFF_RA] = _dot(hmix, win_ref[:, cut2:OFF_RA])

        if prompt:
            kbw[B_WINDOW:B_WINDOW + tq, :] = proj_scr[:, OFF_KB:OFF_KB + D_BKV].astype(BF16)
            vbw[B_WINDOW:B_WINDOW + tq, :] = proj_scr[:, OFF_VB:OFF_VB + D_BKV].astype(BF16)
            kcw[C_BAND:C_BAND + tq, :] = proj_scr[:, OFF_KC:OFF_KC + D_C].astype(BF16)
            vcw[C_BAND:C_BAND + tq, :] = proj_scr[:, OFF_VC:OFF_VC + D_C].astype(BF16)

        hm_a = _head_masks(D_AQK, A_DK)
        hm_c = _head_masks(D_C, HEAD_DIM)
        hm_b = _head_masks(D_BKV, HEAD_DIM)
        anorm = anorm_ref[...]

        def chunk_body(c, carry, with_mlp):
            r0 = pl.multiple_of(c * CHUNK, CHUNK)
            rows = pl.ds(r0, CHUNK)

            if prompt:
                seg_b = [(kbw[pl.ds(r0, LK_B), :], vbw[pl.ds(r0, LK_B), :], False)]
                seg_c = [(kcw[pl.ds(r0, LK_C), :], vcw[pl.ds(r0, LK_C), :], False)]
                gidx = step * nch + c
                lane_b = lax.broadcasted_iota(jnp.int32, (1, LK_B), 1)
                lane_c = lax.broadcasted_iota(jnp.int32, (1, LK_C), 1)
                mask_b = jnp.where(lane_b >= B_WINDOW - gidx * CHUNK, 0.0, NEG_INF)
                mask_c = jnp.where(lane_c >= C_BAND - gidx * CHUNK, 0.0, NEG_INF)
            else:
                kb_new = proj_scr[rows, OFF_KB:OFF_KB + D_BKV]
                vb_new = proj_scr[rows, OFF_VB:OFF_VB + D_BKV]
                kc_new = proj_scr[rows, OFF_KC:OFF_KC + D_C]
                vc_new = proj_scr[rows, OFF_VC:OFF_VC + D_C]
                seg_b = [(cbk_ref[c].astype(BF16), cbv_ref[c].astype(BF16), True),
                         (kb_new.astype(BF16), vb_new.astype(BF16), False)]
                seg_c = [(cck_ref[c].astype(BF16), ccv_ref[c].astype(BF16), True),
                         (kc_new.astype(BF16), vc_new.astype(BF16), False)]
                mask_b = mask_c = None

            q = proj_scr[rows, OFF_QA:OFF_QA + D_AQK] * (A_DK ** -0.5)
            k = proj_scr[rows, OFF_KA:OFF_KA + D_AQK]
            b = b_scr[rows, :]
            q_st, prods = _gla_levels(q, k, la_scr[rows, :], b, hm_a)
            qc = proj_scr[rows, OFF_QC:OFF_QC + D_C] * (HEAD_DIM ** -0.5)
            sc_c = _scores(_stack_masked(qc, hm_c).astype(BF16), seg_c, biasc_ref[...], mask_c)
            qb = proj_scr[rows, OFF_QB:OFF_QB + B_HEADS * HEAD_DIM] * (HEAD_DIM ** -0.5)
            qb_a, qb_b = qb[:, 0:D_BKV], qb[:, D_BKV:2 * D_BKV]
            qb_st = jnp.concatenate([jnp.where(hm_b[0], qb_a, 0.0), jnp.where(hm_b[0], qb_b, 0.0),
                                     jnp.where(hm_b[1], qb_a, 0.0), jnp.where(hm_b[1], qb_b, 0.0)], axis=0)
            sc_b = _scores(qb_st.astype(BF16), seg_b, biasb_ref[...], mask_b)

            half = FF_BLOCK // 2
            if with_mlp:
                up_a = _dot(h2_scr[...], wup_ref[c, :, 0:half])
                act_a = jnp.square(jnp.maximum(up_a, 0.0)).astype(BF16)

            v = proj_scr[rows, OFF_VA:OFF_VA + D_AV]
            s_old = st_scr[...] if prompt else st_ref[c]
            o_st, s_new = _gla_finish(q_st, prods, k, v, b, s_old, lvl_ref[...])
            if prompt:
                st_scr[...] = s_new
            else:
                so_ref[c] = s_new
            g = proj_scr[rows, OFF_GA:OFF_GA + D_AV]
            for h in range(A_HEADS):
                on = _rms(o_st[h * CHUNK:(h + 1) * CHUNK], anorm)
                gh = g[:, h * A_DV:(h + 1) * A_DV]
                mix_scr[rows, h * A_DV:(h + 1) * A_DV] = (on * (gh * jax.nn.sigmoid(gh))).astype(BF16)

            if with_mlp:
                up_b = _dot(h2_scr[...], wup_ref[c, :, half:FF_BLOCK])
                act_b = jnp.square(jnp.maximum(up_b, 0.0)).astype(BF16)

            oc_st = _softmax_pv(sc_c, seg_c, None)
            oc = jnp.where(hm_c[0], oc_st[0:CHUNK], 0.0)
            for h in range(1, C_HEADS):
                oc = jnp.where(hm_c[h], oc_st[h * CHUNK:(h + 1) * CHUNK], oc)
            mix_scr[rows, D_AV + 2 * D_BKV:D_MIX] = oc.astype(BF16)

            ob_st = _softmax_pv(sc_b, seg_b, sink_ref[...])
            ob_a = jnp.where(hm_b[0], ob_st[0:CHUNK], ob_st[2 * CHUNK:3 * CHUNK])
            ob_b = jnp.where(hm_b[0], ob_st[CHUNK:2 * CHUNK], ob_st[3 * CHUNK:4 * CHUNK])
            mix_scr[rows, D_AV:D_AV + D_BKV] = ob_a.astype(BF16)
            mix_scr[rows, D_AV + D_BKV:D_AV + 2 * D_BKV] = ob_b.astype(BF16)

            if with_mlp:
                acc_scr[...] += _dot(act_a, wdown_ref[c, 0:half, :]) + _dot(act_b, wdown_ref[c, half:FF_BLOCK, :])

            if not prompt:
                for dst, src, new, width in ((kbo_ref, cbk_ref, kb_new, B_WINDOW), (vbo_ref, cbv_ref, vb_new, B_WINDOW),
                                             (kco_ref, cck_ref, kc_new, C_BAND), (vco_ref, ccv_ref, vc_new, C_BAND)):
                    dst[c, :, 0:width - CHUNK] = src[c, :, CHUNK:width]
                    dst[c, :, width - CHUNK:width] = new.T
            return carry

        @pl.when(step == 0)
        def _():
            lax.fori_loop(0, nch, functools.partial(chunk_body, with_mlp=False), 0)

        @pl.when(step > 0)
        def _():
            lax.fori_loop(0, nch, functools.partial(chunk_body, with_mlp=True), 0, unroll=2)

        if prompt:
            kbw[0:B_WINDOW, :] = kbw[tq:tq + B_WINDOW, :]
            vbw[0:B_WINDOW, :] = vbw[tq:tq + B_WINDOW, :]
            kcw[0:C_BAND, :] = kcw[tq:tq + C_BAND, :]
            vcw[0:C_BAND, :] = vcw[tq:tq + C_BAND, :]

            so_ref[...] = st_scr[...]
            kbo_ref[...] = proj_scr[tq - B_WINDOW:tq, OFF_KB:OFF_KB + D_BKV]
            vbo_ref[...] = proj_scr[tq - B_WINDOW:tq, OFF_VB:OFF_VB + D_BKV]
            rb = kco_ref.shape[0]
            kco_ref[...] = proj_scr[tq - rb:tq, OFF_KC:OFF_KC + D_C]
            vco_ref[...] = proj_scr[tq - rb:tq, OFF_VC:OFF_VC + D_C]

        mixed = _dot(mix_scr[...], wout_ref[...])

        finish_previous_tile()

        gmlp = gmlp_ref[...]
        for c in range(nch):
            r = slice(c * CHUNK, (c + 1) * CHUNK)
            x1 = x_ref[r, :] + mod(mods_ref, c, 2) * mixed[r, :]
            x1_scr[r, :] = x1
            h2 = _rms(x1, gmlp * (1.0 + mod(mods_ref, c, 4))) + mod(mods_ref, c, 3)
            h2_scr[r, :] = h2.astype(BF16)

    @pl.when(step == nt)
    def _():
        def mlp_block(c, carry):
            up = _dot(h2_scr[...], wup_ref[c])
            act = jnp.square(jnp.maximum(up, 0.0)).astype(BF16)
            acc_scr[...] += _dot(act, wdown_ref[c])
            return carry

        lax.fori_loop(0, nch, mlp_block, 0)
        finish_previous_tile()


def _layer_spec(arr, layer):
    nd = arr.ndim - 1
    return pl.BlockSpec((None,) + arr.shape[1:], lambda i, _l=layer, _nd=nd: (_l,) + (0,) * _nd,
                        pipeline_mode=pl.Buffered(1))


def _const_spec(arr):
    nd = arr.ndim
    return pl.BlockSpec(arr.shape, lambda i, _nd=nd: (0,) * _nd, pipeline_mode=pl.Buffered(1))


def _run_layer(x, mods, layer, stacked, shared, caches, carried, *, prompt, final, tq, nb):
    rows = x.shape[0]
    nch = tq // CHUNK
    nt = rows // tq
    assert rows % tq == 0 and nch == N_FF
    (gmix, gmlp, anorm, win, wa2, ba2, wout, wup, wdown, biasc, sink) = stacked
    (gfin, biasb, tri, lvl) = shared
    cur = lambda i: jnp.minimum(i, nt - 1)
    prv = lambda i: jnp.maximum(i - 1, 0)

    nmod = 1 if prompt else nb
    in_specs = [pl.BlockSpec((tq, D_MODEL), lambda i: (cur(i), 0)),
                pl.BlockSpec((nmod, SUBLANES, D_MODEL), lambda i: (0 if prompt else cur(i), 0, 0)),
                pl.BlockSpec((nmod, SUBLANES, D_MODEL), lambda i: (0 if prompt else prv(i), 0, 0))]
    ls = functools.partial(_layer_spec, layer=layer)
    in_specs += [ls(gmix), ls(gmlp), _const_spec(gfin), ls(anorm), ls(win), ls(wa2), ls(ba2), ls(wout),
                 ls(wup), ls(wdown), _const_spec(biasb), ls(biasc), ls(sink), _const_spec(tri), _const_spec(lvl)]
    args = [x, mods, mods, gmix, gmlp, gfin, anorm, win, wa2, ba2, wout, wup, wdown, biasb, biasc, sink, tri, lvl]

    y_spec = pl.BlockSpec((tq, D_MODEL), lambda i: (prv(i), 0))
    if prompt:
        assert tq >= B_WINDOW and (tq % C_BAND == 0 or C_BAND % tq == 0)
        rb = min(tq, C_BAND)
        first = nt - C_BAND // rb
        out_shape = [jax.ShapeDtypeStruct((rows, D_MODEL), F32),
                     jax.ShapeDtypeStruct((D_AQK, A_DV), F32),
                     jax.ShapeDtypeStruct((B_WINDOW, D_BKV), F32),
                     jax.ShapeDtypeStruct((B_WINDOW, D_BKV), F32),
                     jax.ShapeDtypeStruct((C_BAND, D_C), F32),
                     jax.ShapeDtypeStruct((C_BAND, D_C), F32)]
        band = pl.BlockSpec((rb, D_C), lambda i: (jnp.maximum(cur(i) - first, 0), 0))
        out_specs = [y_spec,
                     pl.BlockSpec((D_AQK, A_DV), lambda i: (0, 0)),
                     pl.BlockSpec((B_WINDOW, D_BKV), lambda i: (0, 0)),
                     pl.BlockSpec((B_WINDOW, D_BKV), lambda i: (0, 0)),
                     band, band]
        kbw_rows, kcw_rows = B_WINDOW + tq, C_BAND + tq
        aliases = {}
    else:
        assert nb == nch
        nseq = rows // CHUNK
        st, cbk, cbv, cck, ccv = caches
        assert len(carried) == N_SEQ_OUT
        seq = lambda r, w: pl.BlockSpec((None, nb, r, w), lambda i, _l=layer: (_l, cur(i), 0, 0))
        seq_dims = [(D_AQK, A_DV), (D_BKV, B_WINDOW), (D_BKV, B_WINDOW), (D_C, C_BAND), (D_C, C_BAND)]
        in_specs += [seq(r, w) for r, w in seq_dims]
        aliases = {len(in_specs) + j: 1 + j for j in range(N_SEQ_OUT)}
        in_specs += [pl.BlockSpec(memory_space=pl.ANY)] * N_SEQ_OUT
        args += [st, cbk, cbv, cck, ccv, *carried]
        out_shape = [jax.ShapeDtypeStruct((rows, D_MODEL), F32)] + [
            jax.ShapeDtypeStruct((DEPTH, nseq, r, w), F32) for r, w in seq_dims]
        out_specs = [y_spec] + [seq(r, w) for r, w in seq_dims]
        kbw_rows, kcw_rows = SUBLANES * 2, SUBLANES * 2

    scratch = [pltpu.VMEM((tq, D_MODEL), BF16),
               pltpu.VMEM((tq, D_MODEL), BF16),
               pltpu.VMEM((tq, OFF_RA), F32),
               pltpu.VMEM((tq, D_AQK), F32),
               pltpu.VMEM((tq, D_AQK), F32),
               pltpu.VMEM((tq, D_MIX), BF16),
               pltpu.VMEM((tq, D_MODEL), F32),
               pltpu.VMEM((tq, D_MODEL), F32),
               pltpu.VMEM((D_AQK, A_DV), F32),
               pltpu.VMEM((kbw_rows, D_BKV), BF16), pltpu.VMEM((kbw_rows, D_BKV), BF16),
               pltpu.VMEM((kcw_rows, D_C), BF16), pltpu.VMEM((kcw_rows, D_C), BF16)]

    kern = functools.partial(_layer_kernel, prompt=prompt, final=final, nch=nch, tq=tq, nt=nt)
    return pl.pallas_call(
        kern,
        grid=(nt + 1,),
        in_specs=in_specs,
        out_specs=out_specs,
        out_shape=out_shape,
        scratch_shapes=scratch,
        input_output_aliases=aliases,
        compiler_params=pltpu.CompilerParams(
            dimension_semantics=("arbitrary",),
            vmem_limit_bytes=VMEM_LIMIT_BYTES if prompt else VMEM_LIMIT_SAMPLE_BYTES),
        name=("layer_prompt" if prompt else "layer_sample") + ("_final" if final else ""),
    )(*args)


ADA_BLOCK = 1536


def _ada_kernel(c_ref, w_ref, b_ref, o_ref):
    c = c_ref[...]
    sc = c * jax.nn.sigmoid(c)
    sc_hi = sc.astype(BF16)
    sc_lo = (sc - sc_hi.astype(F32)).astype(BF16)
    w = w_ref[0]
    w_hi = w.astype(BF16)
    w_lo = (w - w_hi.astype(F32)).astype(BF16)
    o_ref[0] = _dot(sc_hi, w_hi) + _dot(sc_lo, w_hi) + _dot(sc_hi, w_lo) + b_ref[0]


def _ada(c_all, w_ada, b_ada):
    rows = c_all.shape[0]
    n = 6 * D_MODEL
    return pl.pallas_call(
        _ada_kernel,
        grid=(DEPTH, n // ADA_BLOCK),
        in_specs=[pl.BlockSpec((rows, D_MODEL), lambda l, j: (0, 0)),
                  pl.BlockSpec((1, D_MODEL, ADA_BLOCK), lambda l, j: (l, 0, j)),
                  pl.BlockSpec((1, 1, ADA_BLOCK), lambda l, j: (l, 0, j))],
        out_specs=pl.BlockSpec((1, rows, ADA_BLOCK), lambda l, j: (l, 0, j)),
        out_shape=jax.ShapeDtypeStruct((DEPTH, rows, n), F32),
        compiler_params=pltpu.CompilerParams(dimension_semantics=("arbitrary", "arbitrary"),
                                             vmem_limit_bytes=VMEM_LIMIT_BYTES),
        name="adaln",
    )(c_all, w_ada, b_ada.reshape(DEPTH, 1, n))


TB_W = 2 * LANES
TC_W = 5 * LANES
TC_FLAT = TC_W - (CHUNK + C_CLIP + 1)


def _shear(f, width):
    tiled = jnp.broadcast_to(f, (CHUNK, width))
    return pltpu.roll(tiled, width - (CHUNK - 1), 1, stride=1, stride_axis=0)


def _bias_kernel(t5_ref, crel_ref, bucket_ref, ob_ref, oc_ref):
    bucket = bucket_ref[...]
    for h in range(B_HEADS):
        def b_body(i, acc, h=h):
            return jnp.where(bucket == i, t5_ref[h, i], acc)
        f = lax.fori_loop(0, T5_BUCKETS, b_body, jnp.zeros(bucket.shape, F32))
        ob_ref[h] = _shear(f, TB_W)[:, 0:LK_B]
    lane = lax.broadcasted_iota(jnp.int32, (1, TC_W), 1)
    for l in range(DEPTH):
        for h in range(C_HEADS):
            row = crel_ref[l, h:h + 1, :]
            f = jnp.where(lane < TC_FLAT, row[:, 0:1], pltpu.roll(row, TC_FLAT, 1))
            oc_ref[l, h] = _shear(f, TC_W)[:, 0:LK_C]


def _bias_tables(t5_bias, c_rel_bias, bucket):
    smem = pl.BlockSpec(memory_space=pltpu.SMEM)
    vmem = pl.BlockSpec(memory_space=pltpu.VMEM)
    crel = jnp.pad(jnp.swapaxes(c_rel_bias, 1, 2),
                   ((0, 0), (0, SUBLANES - C_HEADS), (0, TC_W - (2 * C_CLIP + 1))))
    return pl.pallas_call(
        _bias_kernel,
        in_specs=[smem, vmem, vmem],
        out_specs=[vmem, vmem],
        out_shape=[jax.ShapeDtypeStruct((B_HEADS, CHUNK, LK_B), F32),
                   jax.ShapeDtypeStruct((DEPTH, C_HEADS, CHUNK, LK_C), F32)],
        name="bias_tables",
    )(t5_bias.T, crel, bucket)


def _t5_bucket(rel):
    half = T5_BUCKETS // 2
    max_exact = half // 2
    steps = half - max_exact
    thresholds = [int(np.ceil(max_exact * (T5_MAX_DIST / max_exact) ** (k / steps) - 1e-9)) for k in range(1, steps)]
    n = jnp.abs(rel)
    large = jnp.minimum(max_exact + sum((n >= t).astype(jnp.int32) for t in thresholds), half - 1)
    return jnp.where(rel > 0, half, 0) + jnp.where(n < max_exact, n, large)


def _gla_constants(tq):
    r = np.arange(tq)
    tri = ((r[:, None] // CHUNK == r[None, :] // CHUNK) & (r[None, :] <= r[:, None])).astype(np.float32)
    t = np.arange(CHUNK)[:, None]
    s = np.arange(CHUNK)[None, :]
    lvl = np.full((CHUNK, CHUNK), -1, np.int32)
    lvl[t == s] = 0
    for li, m in enumerate(GLA_LEVELS):
        sel = ((t // m) % 2 == 1) & ((s // m) == (t // m) - 1)
        lvl[np.broadcast_to(sel, lvl.shape)] = li + 1
    return jnp.asarray(tri, BF16), jnp.asarray(np.tile(lvl, (A_HEADS, 1)))


SRC_RA = 2 * D_AQK + 2 * D_AV
SRC_QB = SRC_RA + A_GATE_RANK
SRC_KB = SRC_QB + B_HEADS * HEAD_DIM
D_IN = SRC_KB + 2 * D_BKV + 3 * D_C
QB_HEAD_ORDER = (0, 2, 1, 3)


def _w_in_segments():
    segs = [(OFF_QA, 0, SRC_RA)]
    segs += [(OFF_QB + i * HEAD_DIM, SRC_QB + h * HEAD_DIM, HEAD_DIM) for i, h in enumerate(QB_HEAD_ORDER)]
    segs += [(OFF_KB, SRC_KB, D_IN - SRC_KB), (OFF_RA, SRC_RA, A_GATE_RANK)]
    return segs


def _w_in_kernel(w_ref, o_ref):
    segs = _w_in_segments()
    for blk in range(P_IN // LANES):
        lo, hi = blk * LANES, (blk + 1) * LANES
        rows, col = [], lo
        while col < hi:
            hit = [(d, s, w) for d, s, w in segs if d <= col < d + w]
            if hit:
                d, s, w = hit[0]
                n = min(hi, d + w) - col
                rows.append(w_ref[s + col - d:s + col - d + n, :])
            else:
                n = hi - col
                rows.append(jnp.zeros((n, D_MODEL), F32))
            col += n
        o_ref[:, lo:hi] = jnp.concatenate(rows, axis=0).T.astype(BF16)


def _prep_w_in(w_in):
    w_fm = jnp.swapaxes(w_in, 1, 2)
    return pl.pallas_call(
        _w_in_kernel,
        grid=(DEPTH,),
        in_specs=[pl.BlockSpec((None, D_IN, D_MODEL), lambda l: (l, 0, 0))],
        out_specs=pl.BlockSpec((None, D_MODEL, P_IN), lambda l: (l, 0, 0)),
        out_shape=jax.ShapeDtypeStruct((DEPTH, D_MODEL, P_IN), BF16),
        compiler_params=pltpu.CompilerParams(dimension_semantics=("arbitrary",),
                                             vmem_limit_bytes=VMEM_LIMIT_BYTES),
        name="prep_w_in",
    )(w_fm)


def _cast_kernel(w_ref, o_ref):
    o_ref[...] = w_ref[...].astype(BF16)


def _prep_w_up(w_up):
    return pl.pallas_call(
        _cast_kernel,
        grid=(DEPTH, N_FF),
        in_specs=[pl.BlockSpec((None, D_MODEL, FF_BLOCK), lambda l, j: (l, 0, j))],
        out_specs=pl.BlockSpec((None, None, D_MODEL, FF_BLOCK), lambda l, j: (l, j, 0, 0)),
        out_shape=jax.ShapeDtypeStruct((DEPTH, N_FF, D_MODEL, FF_BLOCK), BF16),
        compiler_params=pltpu.CompilerParams(dimension_semantics=("arbitrary", "arbitrary"),
                                             vmem_limit_bytes=VMEM_LIMIT_BYTES),
        name="prep_w_up",
    )(w_up)


TQ_PROMPT = 256
NB_SAMPLE = 4


def kernel(x_prompt, x_sample, c_prompt, c_sample, state_gla, cache_b_k, cache_b_v, cache_c_k, cache_c_v,
           w_ada, b_ada, norm_mix_g, norm_mlp_g, w_in, w_a2, b_a2, a_norm_g, b_sink, t5_bias, c_rel_bias,
           w_out, w_up, w_down, final_norm_g):
    bsz, seq, _ = x_prompt.shape
    dec_b, dec_s, _ = x_sample.shape
    assert bsz == 1 and dec_s == CHUNK and TQ_PROMPT == NB_SAMPLE * CHUNK

    n_c = bsz + dec_b
    c_rows = -(-n_c // SUBLANES) * SUBLANES
    c_all = jnp.concatenate([c_prompt, c_sample, jnp.zeros((c_rows - n_c, D_MODEL), F32)], axis=0)
    mods = _ada(c_all, w_ada, b_ada).reshape(DEPTH, c_rows, 6, D_MODEL)
    mods = jnp.pad(mods, ((0, 0), (0, 0), (0, SUBLANES - 6), (0, 0)))

    rel_b = jnp.arange(TB_W) - (CHUNK - 1) - B_WINDOW
    bias_b, bias_c = _bias_tables(t5_bias, c_rel_bias, _t5_bucket(rel_b).astype(jnp.int32)[None, :])
    bias_b = bias_b.reshape(STACK, LK_B)
    bias_c = bias_c.reshape(DEPTH, STACK, LK_C)

    tri, lvl = _gla_constants(TQ_PROMPT)
    ob0 = D_AV
    orow = lambda h: slice(ob0 + h * HEAD_DIM, ob0 + (h + 1) * HEAD_DIM)

    win = _prep_w_in(w_in)
    wa2 = jnp.concatenate([w_a2, jnp.zeros((DEPTH, LANES - A_GATE_RANK, D_AQK), F32)], axis=1).astype(BF16)
    wout = jnp.concatenate([w_out[:, 0:D_AV], w_out[:, orow(0)], w_out[:, orow(2)], w_out[:, orow(1)],
                            w_out[:, orow(3)], w_out[:, D_AV + B_HEADS * HEAD_DIM:]], axis=1).astype(BF16)
    wup = _prep_w_up(w_up)
    wdown = w_down.astype(BF16).reshape(DEPTH, N_FF, FF_BLOCK, D_MODEL)
    sink = jnp.repeat(b_sink, CHUNK, axis=1)[:, :, None]
    stacked = (norm_mix_g[:, None, :], norm_mlp_g[:, None, :], a_norm_g[:, None, :], win, wa2,
               b_a2[:, None, :], wout, wup, wdown, bias_c, sink)
    shared = (final_norm_g[None, :], bias_b, tri, lvl)

    x_p = x_prompt.reshape(seq, D_MODEL)
    x_s = x_sample.reshape(dec_b * dec_s, D_MODEL)
    outs_p = []
    fmajor = lambda cache: jnp.transpose(cache, (0, 1, 3, 4, 2)).reshape(
        DEPTH, dec_b, cache.shape[3] * cache.shape[4], cache.shape[2])
    caches = (state_gla.reshape(DEPTH, dec_b, D_AQK, A_DV), fmajor(cache_b_k), fmajor(cache_b_v),
              fmajor(cache_c_k), fmajor(cache_c_v))
    outs_s = tuple(jnp.zeros(c.shape, F32) for c in caches)
    for l in range(DEPTH):
        final = l == DEPTH - 1
        res_p = _run_layer(x_p, mods[l, 0:bsz], l, stacked, shared, None, None, prompt=True, final=final,
                           tq=TQ_PROMPT, nb=None)
        x_p = res_p[0]
        outs_p.append(res_p[1:])

        res_s = _run_layer(x_s, mods[l, bsz:bsz + dec_b], l, stacked, shared, caches, outs_s, prompt=False,
                           final=final, tq=NB_SAMPLE * CHUNK, nb=NB_SAMPLE)
        x_s = res_s[0]
        outs_s = tuple(res_s[1:])

    sg_p = jnp.stack([o[0].reshape(bsz, A_HEADS, A_DK, A_DV) for o in outs_p])
    kb_p = jnp.stack([o[1].reshape(bsz, B_WINDOW, B_KV_HEADS, HEAD_DIM) for o in outs_p])
    vb_p = jnp.stack([o[2].reshape(bsz, B_WINDOW, B_KV_HEADS, HEAD_DIM) for o in outs_p])
    kc_p = jnp.stack([o[3].reshape(bsz, C_BAND, C_HEADS, HEAD_DIM) for o in outs_p])
    vc_p = jnp.stack([o[4].reshape(bsz, C_BAND, C_HEADS, HEAD_DIM) for o in outs_p])
    sg_s = outs_s[0].reshape(DEPTH, dec_b, A_HEADS, A_DK, A_DV)
    tmajor = lambda arr, heads: jnp.transpose(arr.reshape(DEPTH, dec_b, heads, HEAD_DIM, -1), (0, 1, 4, 2, 3))
    kb_s = tmajor(outs_s[1], B_KV_HEADS)
    vb_s = tmajor(outs_s[2], B_KV_HEADS)
    kc_s = tmajor(outs_s[3], C_HEADS)
    vc_s = tmajor(outs_s[4], C_HEADS)
    return (x_p.reshape(bsz, seq, D_MODEL), x_s.reshape(dec_b, dec_s, D_MODEL),
            sg_p, kb_p, vb_p, kc_p, vc_p, sg_s, kb_s, vb_s, kc_s, vc_s)
```

```python
import functools

import numpy as np
import jax
import jax.numpy as jnp
from jax import lax
from jax.experimental import pallas as pl
from jax.experimental.pallas import tpu as pltpu

D_MODEL = 1024
DEPTH = 2
CHUNK = 64
HEAD_DIM = 64
A_HEADS = 4
A_DK = 64
A_DV = 128
A_GATE_RANK = 16
A_GATE_TAU = 16.0
B_HEADS = 4
B_KV_HEADS = 2
B_WINDOW = 128
C_HEADS = 4
C_BAND = 512
C_CLIP = 256
T5_BUCKETS = 32
T5_MAX_DIST = 128
D_FF = 4 * D_MODEL
NORM_EPS = 1e-6
NEG_INF = -1e30

LANES = 128
SUBLANES = 8
VMEM_LIMIT_BYTES = 60000 * 1024
VMEM_LIMIT_SAMPLE_BYTES = 62 * 1024 * 1024

OFF_QA = 0
OFF_KA = OFF_QA + A_HEADS * A_DK
OFF_VA = OFF_KA + A_HEADS * A_DK
OFF_GA = OFF_VA + A_HEADS * A_DV
OFF_QB = OFF_GA + A_HEADS * A_DV
OFF_KB = OFF_QB + B_HEADS * HEAD_DIM
OFF_VB = OFF_KB + B_KV_HEADS * HEAD_DIM
OFF_QC = OFF_VB + B_KV_HEADS * HEAD_DIM
OFF_KC = OFF_QC + C_HEADS * HEAD_DIM
OFF_VC = OFF_KC + C_HEADS * HEAD_DIM
OFF_RA = OFF_VC + C_HEADS * HEAD_DIM
P_IN = OFF_RA + LANES
D_AQK = A_HEADS * A_DK
D_AV = A_HEADS * A_DV
D_BKV = B_KV_HEADS * HEAD_DIM
D_C = C_HEADS * HEAD_DIM
D_MIX = D_AV + B_HEADS * HEAD_DIM + D_C
LK_B = B_WINDOW + CHUNK
LK_C = C_BAND + CHUNK
STACK = 4 * CHUNK
GLA_LEVELS = (1, 2, 4, 8, 16, 32)
FF_BLOCK = 1024
N_FF = D_FF // FF_BLOCK
N_SEQ_OUT = 5

BF16 = jnp.bfloat16
F32 = jnp.float32


def _dot(a, b):
    return jnp.dot(a, b, preferred_element_type=F32)


def _dot_nt(a, b):
    return lax.dot_general(a, b, (((1,), (1,)), ((), ())), preferred_element_type=F32)


def _dot_tn(a, b):
    return lax.dot_general(a, b, (((0,), (0,)), ((), ())), preferred_element_type=F32)


def _rms(x, g):
    return x * lax.rsqrt(jnp.mean(x * x, axis=-1, keepdims=True) + NORM_EPS) * g


def _split3(x):
    hi = x.astype(BF16)
    r1 = x - hi.astype(F32)
    mid = r1.astype(BF16)
    lo = (r1 - mid.astype(F32)).astype(BF16)
    return hi, mid, lo


def _head_masks(width, per_head):
    lane = lax.broadcasted_iota(jnp.int32, (1, width), 1)
    return [(lane >= h * per_head) & (lane < (h + 1) * per_head) for h in range(width // per_head)]


def _stack_masked(x, masks):
    return jnp.concatenate([jnp.where(m, x, 0.0) for m in masks], axis=0)


def _tile_rows(x, n):
    return jnp.concatenate([x] * n, axis=0)


def _scores(q_st, segments, bias_st, maskrow):
    scores, off = [], 0
    for k, v, transposed in segments:
        n = k.shape[1] if transposed else k.shape[0]
        s = (_dot(q_st, k) if transposed else _dot_nt(q_st, k)) + bias_st[:, off:off + n]
        if maskrow is not None:
            s = s + maskrow[:, off:off + n]
        scores.append(s)
        off += n
    return scores


def _softmax_pv(scores, segments, sink):
    m = functools.reduce(jnp.maximum, [jnp.max(s, axis=1, keepdims=True) for s in scores])
    if sink is not None:
        m = jnp.maximum(m, sink)
    es = [jnp.exp(s - m) for s in scores]
    l = functools.reduce(jnp.add, [jnp.sum(e, axis=1, keepdims=True) for e in es])
    if sink is not None:
        l = l + jnp.exp(sink - m)
    o = None
    for e, (k, v, transposed) in zip(es, segments):
        part = _dot_nt(e.astype(BF16), v) if transposed else _dot(e.astype(BF16), v)
        o = part if o is None else o + part
    return o / l


def _level_exponent(b, m, t):
    bcast = lambda i: jnp.broadcast_to(b[i:i + 1], (SUBLANES, b.shape[1]))
    low_half = lax.broadcasted_iota(jnp.int32, (SUBLANES, b.shape[1]), 0) < SUBLANES // 2
    parts = []
    for g in range(CHUNK // SUBLANES):
        r = g * SUBLANES
        rows = b[r:r + SUBLANES]
        if m >= SUBLANES:
            ref = b[(r // (2 * m)) * 2 * m + m - 1:(r // (2 * m)) * 2 * m + m]
            parts.append(rows - ref if (r // m) % 2 == 1 else ref - rows)
        elif m == 4:
            parts.append(rows - bcast(r + 3))
        else:
            parts.append(rows - jnp.where(low_half, bcast(r + 1), bcast(r + 5)))
    d = jnp.concatenate(parts, axis=0)
    return d if m >= SUBLANES else jnp.where((t & m) != 0, d, -d)


def _odd_block_rows(x, m, groups):
    return jnp.concatenate([x[g * CHUNK + r:g * CHUNK + r + SUBLANES] for g in range(groups)
                            for r in range(0, CHUNK, SUBLANES) if (r // m) % 2 == 1], axis=0)


def _spread_odd_block_rows(y, m, groups):
    zero = jnp.zeros((SUBLANES, y.shape[1]), y.dtype)
    parts, i = [], 0
    for g in range(groups):
        for r in range(0, CHUNK, SUBLANES):
            if (r // m) % 2 == 1:
                parts.append(y[i:i + SUBLANES])
                i += SUBLANES
            else:
                parts.append(zero)
    return jnp.concatenate(parts, axis=0)


def _gla_levels(q, k, la, b, hm_a):
    t = lax.broadcasted_iota(jnp.int32, (CHUNK, D_AQK), 0)
    q_st = _stack_masked(q, hm_a)
    k16 = k.astype(BF16)
    prods = [_dot_nt(q_st.astype(BF16), k16)]
    for m in GLA_LEVELS:
        e = jnp.exp(la if m == 1 else _level_exponent(b, m, t))
        kt = k16 if m == 1 else (k * e).astype(BF16)
        if m >= SUBLANES:
            qt = _odd_block_rows(q_st, m, A_HEADS) * _tile_rows(_odd_block_rows(e, m, 1), A_HEADS)
            prods.append(_spread_odd_block_rows(_dot_nt(qt.astype(BF16), kt), m, A_HEADS))
        else:
            qt = (q_st * _tile_rows(e, A_HEADS)).astype(BF16)
            prods.append(_dot_nt(qt, kt))
    return q_st, prods


def _gla_finish(q_st, prods, k, v, b, s, lvl):
    attn = jnp.where(lvl == 0, prods[0], 0.0)
    for li in range(1, len(prods)):
        attn = jnp.where(lvl == li, prods[li], attn)
    qhat = (q_st * _tile_rows(jnp.exp(b), A_HEADS)).astype(BF16)
    o_inter = _dot(qhat, s.astype(BF16))
    v16 = v.astype(BF16)
    o_intra = jnp.concatenate(
        [_dot(attn[h * CHUNK:(h + 1) * CHUNK].astype(BF16), v16[:, h * A_DV:(h + 1) * A_DV])
         for h in range(A_HEADS)], axis=0)
    b_last = b[CHUNK - 1:CHUNK]
    khat = (k * jnp.exp(b_last - b)).astype(BF16)
    upd = _dot_tn(khat, v16)
    decay = jnp.broadcast_to(jnp.exp(b_last), (A_DV, D_AQK)).T
    s_new = s * decay + jnp.concatenate(
        [upd[h * A_DK:(h + 1) * A_DK, h * A_DV:(h + 1) * A_DV] for h in range(A_HEADS)], axis=0)
    return o_intra + o_inter, s_new


def _layer_kernel(*refs, prompt, final, nch, tq, nt, layer):
    (x_ref, mods_ref, modp_ref, gmix_ref, gmlp_ref, gfin_ref, anorm_ref, win_ref, wa2_ref, ba2_ref, wout_hbm,
     wup_hbm, wdown_hbm, biasb_ref, biasc_ref, sink_ref, tri_ref, lvl_ref) = refs[:18]
    pos = 18
    if not prompt:
        st_ref, cbk_ref, cbv_ref, cck_ref, ccv_ref = refs[pos:pos + 5]
        pos += 5 + N_SEQ_OUT
    y_ref, so_ref, kbo_ref, vbo_ref, kco_ref, vco_ref = refs[pos:pos + 6]
    pos += 6
    (h_scr, h2_scr, proj_scr, la_scr, b_scr, mix_scr, x1_scr, acc_scr, st_scr,
     kbw, vbw, kcw, vcw, wout_ref, wup_ref, wdown_ref, wsem) = refs[pos:]

    step = pl.program_id(0)

    late_weights = [pltpu.make_async_copy(src.at[layer], dst, wsem.at[j]) for j, (src, dst) in
                    enumerate(((wout_hbm, wout_ref), (wup_hbm, wup_ref), (wdown_hbm, wdown_ref)))]

    def mod(ref, c, j):
        return ref[0 if prompt else c, j:j + 1, :]

    @pl.when(step == 0)
    def _():
        for cp in late_weights:
            cp.start()
        x1_scr[...] = jnp.zeros_like(x1_scr)
        acc_scr[...] = jnp.zeros_like(acc_scr)
        if prompt:
            st_scr[...] = jnp.zeros_like(st_scr)
            kbw[...] = jnp.zeros_like(kbw)
            vbw[...] = jnp.zeros_like(vbw)
            kcw[...] = jnp.zeros_like(kcw)
            vcw[...] = jnp.zeros_like(vcw)

    def finish_previous_tile():
        gfin = gfin_ref[...]
        for c in range(nch):
            r = slice(c * CHUNK, (c + 1) * CHUNK)
            x2 = x1_scr[r, :] + mod(modp_ref, c, 5) * acc_scr[r, :]
            y_ref[r, :] = _rms(x2, gfin) if final else x2
        acc_scr[...] = jnp.zeros_like(acc_scr)

    @pl.when(step < nt)
    def _():
        gmix = gmix_ref[...]
        for c in range(nch):
            r = slice(c * CHUNK, (c + 1) * CHUNK)
            h = _rms(x_ref[r, :], gmix * (1.0 + mod(mods_ref, c, 1))) + mod(mods_ref, c, 0)
            h_scr[r, :] = h.astype(BF16)
        hmix = h_scr[...]
        cut1, cut2 = OFF_GA, OFF_QC
        ra = _dot(hmix, win_ref[:, OFF_RA:OFF_RA + LANES])
        proj_scr[:, 0:cut1] = _dot(hmix, win_ref[:, 0:cut1])
        z = _dot(ra.astype(BF16), wa2_ref[...]) + ba2_ref[...]
        la = (jnp.minimum(z, 0.0) - jnp.log1p(jnp.exp(-jnp.abs(z)))) * (1.0 / A_GATE_TAU)
        la_scr[...] = la
        proj_scr[:, cut1:cut2] = _dot(hmix, win_ref[:, cut1:cut2])
        tri = tri_ref[...]
        hi, mid, lo = _split3(la)
        b_scr[...] = _dot(tri, hi) + _dot(tri, mid) + _dot(tri, lo)
        proj_scr[:, cut2:OFF_RA] = _dot(hmix, win_ref[:, cut2:OFF_RA])

        if prompt:
            kbw[B_WINDOW:B_WINDOW + tq, :] = proj_scr[:, OFF_KB:OFF_KB + D_BKV].astype(BF16)
            vbw[B_WINDOW:B_WINDOW + tq, :] = proj_scr[:, OFF_VB:OFF_VB + D_BKV].astype(BF16)
            kcw[C_BAND:C_BAND + tq, :] = proj_scr[:, OFF_KC:OFF_KC + D_C].astype(BF16)
            vcw[C_BAND:C_BAND + tq, :] = proj_scr[:, OFF_VC:OFF_VC + D_C].astype(BF16)

        hm_a = _head_masks(D_AQK, A_DK)
        hm_c = _head_masks(D_C, HEAD_DIM)
        hm_b = _head_masks(D_BKV, HEAD_DIM)
        anorm = anorm_ref[...]

        def chunk_body(c, carry, with_mlp):
            r0 = pl.multiple_of(c * CHUNK, CHUNK)
            rows = pl.ds(r0, CHUNK)

            if prompt:
                seg_b = [(kbw[pl.ds(r0, LK_B), :], vbw[pl.ds(r0, LK_B), :], False)]
                seg_c = [(kcw[pl.ds(r0, LK_C), :], vcw[pl.ds(r0, LK_C), :], False)]
                gidx = step * nch + c
                lane_b = lax.broadcasted_iota(jnp.int32, (1, LK_B), 1)
                lane_c = lax.broadcasted_iota(jnp.int32, (1, LK_C), 1)
                mask_b = jnp.where(lane_b >= B_WINDOW - gidx * CHUNK, 0.0, NEG_INF)
                mask_c = jnp.where(lane_c >= C_BAND - gidx * CHUNK, 0.0, NEG_INF)
            else:
                kb_new = proj_scr[rows, OFF_KB:OFF_KB + D_BKV]
                vb_new = proj_scr[rows, OFF_VB:OFF_VB + D_BKV]
                kc_new = proj_scr[rows, OFF_KC:OFF_KC + D_C]
                vc_new = proj_scr[rows, OFF_VC:OFF_VC + D_C]
                seg_b = [(cbk_ref[c].astype(BF16), cbv_ref[c].astype(BF16), True),
                         (kb_new.astype(BF16), vb_new.astype(BF16), False)]
                seg_c = [(cck_ref[c].astype(BF16), ccv_ref[c].astype(BF16), True),
                         (kc_new.astype(BF16), vc_new.astype(BF16), False)]
                mask_b = mask_c = None

            q = proj_scr[rows, OFF_QA:OFF_QA + D_AQK] * (A_DK ** -0.5)
            k = proj_scr[rows, OFF_KA:OFF_KA + D_AQK]
            b = b_scr[rows, :]
            q_st, prods = _gla_levels(q, k, la_scr[rows, :], b, hm_a)
            qc = proj_scr[rows, OFF_QC:OFF_QC + D_C] * (HEAD_DIM ** -0.5)
            sc_c = _scores(_stack_masked(qc, hm_c).astype(BF16), seg_c, biasc_ref[...], mask_c)
            qb = proj_scr[rows, OFF_QB:OFF_QB + B_HEADS * HEAD_DIM] * (HEAD_DIM ** -0.5)
            qb_a, qb_b = qb[:, 0:D_BKV], qb[:, D_BKV:2 * D_BKV]
            qb_st = jnp.concatenate([jnp.where(hm_b[0], qb_a, 0.0), jnp.where(hm_b[0], qb_b, 0.0),
                                     jnp.where(hm_b[1], qb_a, 0.0), jnp.where(hm_b[1], qb_b, 0.0)], axis=0)
            sc_b = _scores(qb_st.astype(BF16), seg_b, biasb_ref[...], mask_b)

            half = FF_BLOCK // 2
            if with_mlp:
                up_a = _dot(h2_scr[...], wup_ref[c, :, 0:half])
                act_a = jnp.square(jnp.maximum(up_a, 0.0)).astype(BF16)

            v = proj_scr[rows, OFF_VA:OFF_VA + D_AV]
            s_old = st_scr[...] if prompt else st_ref[c]
            o_st, s_new = _gla_finish(q_st, prods, k, v, b, s_old, lvl_ref[...])
            if prompt:
                st_scr[...] = s_new
            else:
                so_ref[c] = s_new
            g = proj_scr[rows, OFF_GA:OFF_GA + D_AV]
            for h in range(A_HEADS):
                on = _rms(o_st[h * CHUNK:(h + 1) * CHUNK], anorm)
                gh = g[:, h * A_DV:(h + 1) * A_DV]
                mix_scr[rows, h * A_DV:(h + 1) * A_DV] = (on * (gh * jax.nn.sigmoid(gh))).astype(BF16)

            if with_mlp:
                up_b = _dot(h2_scr[...], wup_ref[c, :, half:FF_BLOCK])
                act_b = jnp.square(jnp.maximum(up_b, 0.0)).astype(BF16)

            oc_st = _softmax_pv(sc_c, seg_c, None)
            oc = jnp.where(hm_c[0], oc_st[0:CHUNK], 0.0)
            for h in range(1, C_HEADS):
                oc = jnp.where(hm_c[h], oc_st[h * CHUNK:(h + 1) * CHUNK], oc)
            mix_scr[rows, D_AV + 2 * D_BKV:D_MIX] = oc.astype(BF16)

            ob_st = _softmax_pv(sc_b, seg_b, sink_ref[...])
            ob_a = jnp.where(hm_b[0], ob_st[0:CHUNK], ob_st[2 * CHUNK:3 * CHUNK])
            ob_b = jnp.where(hm_b[0], ob_st[CHUNK:2 * CHUNK], ob_st[3 * CHUNK:4 * CHUNK])
            mix_scr[rows, D_AV:D_AV + D_BKV] = ob_a.astype(BF16)
            mix_scr[rows, D_AV + D_BKV:D_AV + 2 * D_BKV] = ob_b.astype(BF16)

            if with_mlp:
                acc_scr[...] += _dot(act_a, wdown_ref[c, 0:half, :]) + _dot(act_b, wdown_ref[c, half:FF_BLOCK, :])

            if not prompt:
                for dst, src, new, width in ((kbo_ref, cbk_ref, kb_new, B_WINDOW), (vbo_ref, cbv_ref, vb_new, B_WINDOW),
                                             (kco_ref, cck_ref, kc_new, C_BAND), (vco_ref, ccv_ref, vc_new, C_BAND)):
                    dst[c, :, 0:width - CHUNK] = src[c, :, CHUNK:width]
                    dst[c, :, width - CHUNK:width] = new.T
            return carry

        @pl.when(step == 0)
        def _():
            lax.fori_loop(0, nch, functools.partial(chunk_body, with_mlp=False), 0)
            for cp in late_weights:
                cp.wait()

        @pl.when(step > 0)
        def _():
            lax.fori_loop(0, nch, functools.partial(chunk_body, with_mlp=True), 0, unroll=2)

        if prompt:
            kbw[0:B_WINDOW, :] = kbw[tq:tq + B_WINDOW, :]
            vbw[0:B_WINDOW, :] = vbw[tq:tq + B_WINDOW, :]
            kcw[0:C_BAND, :] = kcw[tq:tq + C_BAND, :]
            vcw[0:C_BAND, :] = vcw[tq:tq + C_BAND, :]

            so_ref[...] = st_scr[...]
            kbo_ref[...] = proj_scr[tq - B_WINDOW:tq, OFF_KB:OFF_KB + D_BKV]
            vbo_ref[...] = proj_scr[tq - B_WINDOW:tq, OFF_VB:OFF_VB + D_BKV]
            rb = kco_ref.shape[0]
            kco_ref[...] = proj_scr[tq - rb:tq, OFF_KC:OFF_KC + D_C]
            vco_ref[...] = proj_scr[tq - rb:tq, OFF_VC:OFF_VC + D_C]

        mixed = _dot(mix_scr[...], wout_ref[...])

        finish_previous_tile()

        gmlp = gmlp_ref[...]
        for c in range(nch):
            r = slice(c * CHUNK, (c + 1) * CHUNK)
            x1 = x_ref[r, :] + mod(mods_ref, c, 2) * mixed[r, :]
            x1_scr[r, :] = x1
            h2 = _rms(x1, gmlp * (1.0 + mod(mods_ref, c, 4))) + mod(mods_ref, c, 3)
            h2_scr[r, :] = h2.astype(BF16)

    @pl.when(step == nt)
    def _():
        def mlp_block(c, carry):
            up = _dot(h2_scr[...], wup_ref[c])
            act = jnp.square(jnp.maximum(up, 0.0)).astype(BF16)
            acc_scr[...] += _dot(act, wdown_ref[c])
            return carry

        lax.fori_loop(0, nch, mlp_block, 0)
        finish_previous_tile()


def _layer_spec(arr, layer):
    nd = arr.ndim - 1
    return pl.BlockSpec((None,) + arr.shape[1:], lambda i, _l=layer, _nd=nd: (_l,) + (0,) * _nd,
                        pipeline_mode=pl.Buffered(1))


def _const_spec(arr):
    nd = arr.ndim
    return pl.BlockSpec(arr.shape, lambda i, _nd=nd: (0,) * _nd, pipeline_mode=pl.Buffered(1))


def _run_layer(x, mods, layer, stacked, shared, caches, carried, *, prompt, final, tq, nb):
    rows = x.shape[0]
    nch = tq // CHUNK
    nt = rows // tq
    assert rows % tq == 0 and nch == N_FF
    (gmix, gmlp, anorm, win, wa2, ba2, wout, wup, wdown, biasc, sink) = stacked
    (gfin, biasb, tri, lvl) = shared
    cur = lambda i: jnp.minimum(i, nt - 1)
    prv = lambda i: jnp.maximum(i - 1, 0)

    nmod = 1 if prompt else nb
    in_specs = [pl.BlockSpec((tq, D_MODEL), lambda i: (cur(i), 0)),
                pl.BlockSpec((nmod, SUBLANES, D_MODEL), lambda i: (0 if prompt else cur(i), 0, 0)),
                pl.BlockSpec((nmod, SUBLANES, D_MODEL), lambda i: (0 if prompt else prv(i), 0, 0))]
    ls = functools.partial(_layer_spec, layer=layer)
    hbm = pl.BlockSpec(memory_space=pl.ANY)
    in_specs += [ls(gmix), ls(gmlp), _const_spec(gfin), ls(anorm), ls(win), ls(wa2), ls(ba2), hbm,
                 hbm, hbm, _const_spec(biasb), ls(biasc), ls(sink), _const_spec(tri), _const_spec(lvl)]
    args = [x, mods, mods, gmix, gmlp, gfin, anorm, win, wa2, ba2, wout, wup, wdown, biasb, biasc, sink, tri, lvl]

    y_spec = pl.BlockSpec((tq, D_MODEL), lambda i: (prv(i), 0))
    if prompt:
        assert tq >= B_WINDOW and (tq % C_BAND == 0 or C_BAND % tq == 0)
        rb = min(tq, C_BAND)
        first = nt - C_BAND // rb
        out_shape = [jax.ShapeDtypeStruct((rows, D_MODEL), F32),
                     jax.ShapeDtypeStruct((D_AQK, A_DV), F32),
                     jax.ShapeDtypeStruct((B_WINDOW, D_BKV), F32),
                     jax.ShapeDtypeStruct((B_WINDOW, D_BKV), F32),
                     jax.ShapeDtypeStruct((C_BAND, D_C), F32),
                     jax.ShapeDtypeStruct((C_BAND, D_C), F32)]
        band = pl.BlockSpec((rb, D_C), lambda i: (jnp.maximum(cur(i) - first, 0), 0))
        out_specs = [y_spec,
                     pl.BlockSpec((D_AQK, A_DV), lambda i: (0, 0)),
                     pl.BlockSpec((B_WINDOW, D_BKV), lambda i: (0, 0)),
                     pl.BlockSpec((B_WINDOW, D_BKV), lambda i: (0, 0)),
                     band, band]
        kbw_rows, kcw_rows = B_WINDOW + tq, C_BAND + tq
        aliases = {}
    else:
        assert nb == nch
        nseq = rows // CHUNK
        st, cbk, cbv, cck, ccv = caches
        assert len(carried) == N_SEQ_OUT
        seq = lambda r, w: pl.BlockSpec((None, nb, r, w), lambda i, _l=layer: (_l, cur(i), 0, 0))
        seq_dims = [(D_AQK, A_DV), (D_BKV, B_WINDOW), (D_BKV, B_WINDOW), (D_C, C_BAND), (D_C, C_BAND)]
        in_specs += [seq(r, w) for r, w in seq_dims]
        aliases = {len(in_specs) + j: 1 + j for j in range(N_SEQ_OUT)}
        in_specs += [pl.BlockSpec(memory_space=pl.ANY)] * N_SEQ_OUT
        args += [st, cbk, cbv, cck, ccv, *carried]
        out_shape = [jax.ShapeDtypeStruct((rows, D_MODEL), F32)] + [
            jax.ShapeDtypeStruct((DEPTH, nseq, r, w), F32) for r, w in seq_dims]
        out_specs = [y_spec] + [seq(r, w) for r, w in seq_dims]
        kbw_rows, kcw_rows = SUBLANES * 2, SUBLANES * 2

    scratch = [pltpu.VMEM((tq, D_MODEL), BF16),
               pltpu.VMEM((tq, D_MODEL), BF16),
               pltpu.VMEM((tq, OFF_RA), F32),
               pltpu.VMEM((tq, D_AQK), F32),
               pltpu.VMEM((tq, D_AQK), F32),
               pltpu.VMEM((tq, D_MIX), BF16),
               pltpu.VMEM((tq, D_MODEL), F32),
               pltpu.VMEM((tq, D_MODEL), F32),
               pltpu.VMEM((D_AQK, A_DV), F32),
               pltpu.VMEM((kbw_rows, D_BKV), BF16), pltpu.VMEM((kbw_rows, D_BKV), BF16),
               pltpu.VMEM((kcw_rows, D_C), BF16), pltpu.VMEM((kcw_rows, D_C), BF16),
               pltpu.VMEM(wout.shape[1:], BF16), pltpu.VMEM(wup.shape[1:], BF16), pltpu.VMEM(wdown.shape[1:], BF16),
               pltpu.SemaphoreType.DMA((3,))]

    kern = functools.partial(_layer_kernel, prompt=prompt, final=final, nch=nch, tq=tq, nt=nt, layer=layer)
    return pl.pallas_call(
        kern,
        grid=(nt + 1,),
        in_specs=in_specs,
        out_specs=out_specs,
        out_shape=out_shape,
        scratch_shapes=scratch,
        input_output_aliases=aliases,
        compiler_params=pltpu.CompilerParams(
            dimension_semantics=("arbitrary",),
            vmem_limit_bytes=VMEM_LIMIT_BYTES if prompt else VMEM_LIMIT_SAMPLE_BYTES),
        name=("layer_prompt" if prompt else "layer_sample") + ("_final" if final else ""),
    )(*args)


ADA_BLOCK = 1536


def _ada_kernel(c_ref, w_ref, b_ref, o_ref):
    c = c_ref[...]
    sc = c * jax.nn.sigmoid(c)
    sc_hi = sc.astype(BF16)
    sc_lo = (sc - sc_hi.astype(F32)).astype(BF16)
    w = w_ref[0]
    w_hi = w.astype(BF16)
    w_lo = (w - w_hi.astype(F32)).astype(BF16)
    o_ref[0] = _dot(sc_hi, w_hi) + _dot(sc_lo, w_hi) + _dot(sc_hi, w_lo) + b_ref[0]


def _ada(c_all, w_ada, b_ada):
    rows = c_all.shape[0]
    n = 6 * D_MODEL
    return pl.pallas_call(
        _ada_kernel,
        grid=(DEPTH, n // ADA_BLOCK),
        in_specs=[pl.BlockSpec((rows, D_MODEL), lambda l, j: (0, 0)),
                  pl.BlockSpec((1, D_MODEL, ADA_BLOCK), lambda l, j: (l, 0, j)),
                  pl.BlockSpec((1, 1, ADA_BLOCK), lambda l, j: (l, 0, j))],
        out_specs=pl.BlockSpec((1, rows, ADA_BLOCK), lambda l, j: (l, 0, j)),
        out_shape=jax.ShapeDtypeStruct((DEPTH, rows, n), F32),
        compiler_params=pltpu.CompilerParams(dimension_semantics=("arbitrary", "arbitrary"),
                                             vmem_limit_bytes=VMEM_LIMIT_BYTES),
        name="adaln",
    )(c_all, w_ada, b_ada.reshape(DEPTH, 1, n))


TB_W = 2 * LANES
TC_W = 5 * LANES
TC_FLAT = TC_W - (CHUNK + C_CLIP + 1)


def _shear(f, width):
    tiled = jnp.broadcast_to(f, (CHUNK, width))
    return pltpu.roll(tiled, width - (CHUNK - 1), 1, stride=1, stride_axis=0)


def _bias_kernel(t5_ref, crel_ref, bucket_ref, ob_ref, oc_ref):
    bucket = bucket_ref[...]
    for h in range(B_HEADS):
        def b_body(i, acc, h=h):
            return jnp.where(bucket == i, t5_ref[h, i], acc)
        f = lax.fori_loop(0, T5_BUCKETS, b_body, jnp.zeros(bucket.shape, F32))
        ob_ref[h] = _shear(f, TB_W)[:, 0:LK_B]
    lane = lax.broadcasted_iota(jnp.int32, (1, TC_W), 1)
    for l in range(DEPTH):
        for h in range(C_HEADS):
            row = crel_ref[l, h:h + 1, :]
            f = jnp.where(lane < TC_FLAT, row[:, 0:1], pltpu.roll(row, TC_FLAT, 1))
            oc_ref[l, h] = _shear(f, TC_W)[:, 0:LK_C]


def _bias_tables(t5_bias, c_rel_bias, bucket):
    smem = pl.BlockSpec(memory_space=pltpu.SMEM)
    vmem = pl.BlockSpec(memory_space=pltpu.VMEM)
    crel = jnp.pad(jnp.swapaxes(c_rel_bias, 1, 2),
                   ((0, 0), (0, SUBLANES - C_HEADS), (0, TC_W - (2 * C_CLIP + 1))))
    return pl.pallas_call(
        _bias_kernel,
        in_specs=[smem, vmem, vmem],
        out_specs=[vmem, vmem],
        out_shape=[jax.ShapeDtypeStruct((B_HEADS, CHUNK, LK_B), F32),
                   jax.ShapeDtypeStruct((DEPTH, C_HEADS, CHUNK, LK_C), F32)],
        name="bias_tables",
    )(t5_bias.T, crel, bucket)


def _t5_bucket(rel):
    half = T5_BUCKETS // 2
    max_exact = half // 2
    steps = half - max_exact
    thresholds = [int(np.ceil(max_exact * (T5_MAX_DIST / max_exact) ** (k / steps) - 1e-9)) for k in range(1, steps)]
    n = jnp.abs(rel)
    large = jnp.minimum(max_exact + sum((n >= t).astype(jnp.int32) for t in thresholds), half - 1)
    return jnp.where(rel > 0, half, 0) + jnp.where(n < max_exact, n, large)


def _gla_constants(tq):
    r = np.arange(tq)
    tri = ((r[:, None] // CHUNK == r[None, :] // CHUNK) & (r[None, :] <= r[:, None])).astype(np.float32)
    t = np.arange(CHUNK)[:, None]
    s = np.arange(CHUNK)[None, :]
    lvl = np.full((CHUNK, CHUNK), -1, np.int32)
    lvl[t == s] = 0
    for li, m in enumerate(GLA_LEVELS):
        sel = ((t // m) % 2 == 1) & ((s // m) == (t // m) - 1)
        lvl[np.broadcast_to(sel, lvl.shape)] = li + 1
    return jnp.asarray(tri, BF16), jnp.asarray(np.tile(lvl, (A_HEADS, 1)))


SRC_RA = 2 * D_AQK + 2 * D_AV
SRC_QB = SRC_RA + A_GATE_RANK
SRC_KB = SRC_QB + B_HEADS * HEAD_DIM
D_IN = SRC_KB + 2 * D_BKV + 3 * D_C
QB_HEAD_ORDER = (0, 2, 1, 3)


def _w_in_segments():
    segs = [(OFF_QA, 0, SRC_RA)]
    segs += [(OFF_QB + i * HEAD_DIM, SRC_QB + h * HEAD_DIM, HEAD_DIM) for i, h in enumerate(QB_HEAD_ORDER)]
    segs += [(OFF_KB, SRC_KB, D_IN - SRC_KB), (OFF_RA, SRC_RA, A_GATE_RANK)]
    return segs


def _w_in_kernel(w_ref, o_ref):
    segs = _w_in_segments()
    for blk in range(P_IN // LANES):
        lo, hi = blk * LANES, (blk + 1) * LANES
        rows, col = [], lo
        while col < hi:
            hit = [(d, s, w) for d, s, w in segs if d <= col < d + w]
            if hit:
                d, s, w = hit[0]
                n = min(hi, d + w) - col
                rows.append(w_ref[s + col - d:s + col - d + n, :])
            else:
                n = hi - col
                rows.append(jnp.zeros((n, D_MODEL), F32))
            col += n
        o_ref[:, lo:hi] = jnp.concatenate(rows, axis=0).T.astype(BF16)


def _prep_w_in(w_in):
    w_fm = jnp.swapaxes(w_in, 1, 2)
    return pl.pallas_call(
        _w_in_kernel,
        grid=(DEPTH,),
        in_specs=[pl.BlockSpec((None, D_IN, D_MODEL), lambda l: (l, 0, 0))],
        out_specs=pl.BlockSpec((None, D_MODEL, P_IN), lambda l: (l, 0, 0)),
        out_shape=jax.ShapeDtypeStruct((DEPTH, D_MODEL, P_IN), BF16),
        compiler_params=pltpu.CompilerParams(dimension_semantics=("arbitrary",),
                                             vmem_limit_bytes=VMEM_LIMIT_BYTES),
        name="prep_w_in",
    )(w_fm)


def _cast_kernel(w_ref, o_ref):
    o_ref[...] = w_ref[...].astype(BF16)


def _prep_w_up(w_up):
    return pl.pallas_call(
        _cast_kernel,
        grid=(DEPTH, N_FF),
        in_specs=[pl.BlockSpec((None, D_MODEL, FF_BLOCK), lambda l, j: (l, 0, j))],
        out_specs=pl.BlockSpec((None, None, D_MODEL, FF_BLOCK), lambda l, j: (l, j, 0, 0)),
        out_shape=jax.ShapeDtypeStruct((DEPTH, N_FF, D_MODEL, FF_BLOCK), BF16),
        compiler_params=pltpu.CompilerParams(dimension_semantics=("arbitrary", "arbitrary"),
                                             vmem_limit_bytes=VMEM_LIMIT_BYTES),
        name="prep_w_up",
    )(w_up)


TQ_PROMPT = 256
NB_SAMPLE = 4


def kernel(x_prompt, x_sample, c_prompt, c_sample, state_gla, cache_b_k, cache_b_v, cache_c_k, cache_c_v,
           w_ada, b_ada, norm_mix_g, norm_mlp_g, w_in, w_a2, b_a2, a_norm_g, b_sink, t5_bias, c_rel_bias,
           w_out, w_up, w_down, final_norm_g):
    bsz, seq, _ = x_prompt.shape
    dec_b, dec_s, _ = x_sample.shape
    assert bsz == 1 and dec_s == CHUNK and TQ_PROMPT == NB_SAMPLE * CHUNK

    n_c = bsz + dec_b
    c_rows = -(-n_c // SUBLANES) * SUBLANES
    c_all = jnp.concatenate([c_prompt, c_sample, jnp.zeros((c_rows - n_c, D_MODEL), F32)], axis=0)
    mods = _ada(c_all, w_ada, b_ada).reshape(DEPTH, c_rows, 6, D_MODEL)
    mods = jnp.pad(mods, ((0, 0), (0, 0), (0, SUBLANES - 6), (0, 0)))

    rel_b = jnp.arange(TB_W) - (CHUNK - 1) - B_WINDOW
    bias_b, bias_c = _bias_tables(t5_bias, c_rel_bias, _t5_bucket(rel_b).astype(jnp.int32)[None, :])
    bias_b = bias_b.reshape(STACK, LK_B)
    bias_c = bias_c.reshape(DEPTH, STACK, LK_C)

    tri, lvl = _gla_constants(TQ_PROMPT)
    ob0 = D_AV
    orow = lambda h: slice(ob0 + h * HEAD_DIM, ob0 + (h + 1) * HEAD_DIM)

    win = _prep_w_in(w_in)
    wa2 = jnp.concatenate([w_a2, jnp.zeros((DEPTH, LANES - A_GATE_RANK, D_AQK), F32)], axis=1).astype(BF16)
    wout = jnp.concatenate([w_out[:, 0:D_AV], w_out[:, orow(0)], w_out[:, orow(2)], w_out[:, orow(1)],
                            w_out[:, orow(3)], w_out[:, D_AV + B_HEADS * HEAD_DIM:]], axis=1).astype(BF16)
    wup = _prep_w_up(w_up)
    wdown = w_down.astype(BF16).reshape(DEPTH, N_FF, FF_BLOCK, D_MODEL)
    sink = jnp.repeat(b_sink, CHUNK, axis=1)[:, :, None]
    stacked = (norm_mix_g[:, None, :], norm_mlp_g[:, None, :], a_norm_g[:, None, :], win, wa2,
               b_a2[:, None, :], wout, wup, wdown, bias_c, sink)
    shared = (final_norm_g[None, :], bias_b, tri, lvl)

    x_p = x_prompt.reshape(seq, D_MODEL)
    x_s = x_sample.reshape(dec_b * dec_s, D_MODEL)
    outs_p = []
    fmajor = lambda cache: jnp.transpose(cache, (0, 1, 3, 4, 2)).reshape(
        DEPTH, dec_b, cache.shape[3] * cache.shape[4], cache.shape[2])
    caches = (state_gla.reshape(DEPTH, dec_b, D_AQK, A_DV), fmajor(cache_b_k), fmajor(cache_b_v),
              fmajor(cache_c_k), fmajor(cache_c_v))
    outs_s = tuple(jnp.zeros(c.shape, F32) for c in caches)
    for l in range(DEPTH):
        final = l == DEPTH - 1
        res_p = _run_layer(x_p, mods[l, 0:bsz], l, stacked, shared, None, None, prompt=True, final=final,
                           tq=TQ_PROMPT, nb=None)
        x_p = res_p[0]
        outs_p.append(res_p[1:])

        res_s = _run_layer(x_s, mods[l, bsz:bsz + dec_b], l, stacked, shared, caches, outs_s, prompt=False,
                           final=final, tq=NB_SAMPLE * CHUNK, nb=NB_SAMPLE)
        x_s = res_s[0]
        outs_s = tuple(res_s[1:])

    sg_p = jnp.stack([o[0].reshape(bsz, A_HEADS, A_DK, A_DV) for o in outs_p])
    kb_p = jnp.stack([o[1].reshape(bsz, B_WINDOW, B_KV_HEADS, HEAD_DIM) for o in outs_p])
    vb_p = jnp.stack([o[2].reshape(bsz, B_WINDOW, B_KV_HEADS, HEAD_DIM) for o in outs_p])
    kc_p = jnp.stack([o[3].reshape(bsz, C_BAND, C_HEADS, HEAD_DIM) for o in outs_p])
    vc_p = jnp.stack([o[4].reshape(bsz, C_BAND, C_HEADS, HEAD_DIM) for o in outs_p])
    sg_s = outs_s[0].reshape(DEPTH, dec_b, A_HEADS, A_DK, A_DV)
    tmajor = lambda arr, heads: jnp.transpose(arr.reshape(DEPTH, dec_b, heads, HEAD_DIM, -1), (0, 1, 4, 2, 3))
    kb_s = tmajor(outs_s[1], B_KV_HEADS)
    vb_s = tmajor(outs_s[2], B_KV_HEADS)
    kc_s = tmajor(outs_s[3], C_HEADS)
    vc_s = tmajor(outs_s[4], C_HEADS)
    return (x_p.reshape(bsz, seq, D_MODEL), x_s.reshape(dec_b, dec_s, D_MODEL),
            sg_p, kb_p, vb_p, kc_p, vc_p, sg_s, kb_s, vb_s, kc_s, vc_s)
```

```python
import functools

import numpy as np
import jax
import jax.numpy as jnp
from jax import lax
from jax.experimental import pallas as pl
from jax.experimental.pallas import tpu as pltpu

D_MODEL = 1024
DEPTH = 2
CHUNK = 64
HEAD_DIM = 64
A_HEADS = 4
A_DK = 64
A_DV = 128
A_GATE_RANK = 16
A_GATE_TAU = 16.0
B_HEADS = 4
B_KV_HEADS = 2
B_WINDOW = 128
C_HEADS = 4
C_BAND = 512
C_CLIP = 256
T5_BUCKETS = 32
T5_MAX_DIST = 128
D_FF = 4 * D_MODEL
NORM_EPS = 1e-6
NEG_INF = -1e30

LANES = 128
SUBLANES = 8
VMEM_LIMIT_BYTES = 60000 * 1024
VMEM_LIMIT_SAMPLE_BYTES = 62 * 1024 * 1024

OFF_QA = 0
OFF_KA = OFF_QA + A_HEADS * A_DK
OFF_VA = OFF_KA + A_HEADS * A_DK
OFF_GA = OFF_VA + A_HEADS * A_DV
OFF_QB = OFF_GA + A_HEADS * A_DV
OFF_KB = OFF_QB + B_HEADS * HEAD_DIM
OFF_VB = OFF_KB + B_KV_HEADS * HEAD_DIM
OFF_QC = OFF_VB + B_KV_HEADS * HEAD_DIM
OFF_KC = OFF_QC + C_HEADS * HEAD_DIM
OFF_VC = OFF_KC + C_HEADS * HEAD_DIM
OFF_RA = OFF_VC + C_HEADS * HEAD_DIM
P_IN = OFF_RA + LANES
D_AQK = A_HEADS * A_DK
D_AV = A_HEADS * A_DV
D_BKV = B_KV_HEADS * HEAD_DIM
D_C = C_HEADS * HEAD_DIM
D_MIX = D_AV + B_HEADS * HEAD_DIM + D_C
LK_B = B_WINDOW + CHUNK
LK_C = C_BAND + CHUNK
STACK = 4 * CHUNK
GLA_LEVELS = (1, 2, 4, 8, 16, 32)
FF_BLOCK = 1024
N_FF = D_FF // FF_BLOCK
N_SEQ_OUT = 5

BF16 = jnp.bfloat16
F32 = jnp.float32


def _dot(a, b):
    return jnp.dot(a, b, preferred_element_type=F32)


def _dot_nt(a, b):
    return lax.dot_general(a, b, (((1,), (1,)), ((), ())), preferred_element_type=F32)


def _dot_tn(a, b):
    return lax.dot_general(a, b, (((0,), (0,)), ((), ())), preferred_element_type=F32)


def _rms(x, g):
    return x * lax.rsqrt(jnp.mean(x * x, axis=-1, keepdims=True) + NORM_EPS) * g


def _split3(x):
    hi = x.astype(BF16)
    r1 = x - hi.astype(F32)
    mid = r1.astype(BF16)
    lo = (r1 - mid.astype(F32)).astype(BF16)
    return hi, mid, lo


def _head_masks(width, per_head):
    lane = lax.broadcasted_iota(jnp.int32, (1, width), 1)
    return [(lane >= h * per_head) & (lane < (h + 1) * per_head) for h in range(width // per_head)]


def _stack_masked(x, masks):
    return jnp.concatenate([jnp.where(m, x, 0.0) for m in masks], axis=0)


def _tile_rows(x, n):
    return jnp.concatenate([x] * n, axis=0)


def _scores(q_st, segments, bias_st, maskrow):
    scores, off = [], 0
    for k, v, transposed in segments:
        n = k.shape[1] if transposed else k.shape[0]
        s = (_dot(q_st, k) if transposed else _dot_nt(q_st, k)) + bias_st[:, off:off + n]
        if maskrow is not None:
            s = s + maskrow[:, off:off + n]
        scores.append(s)
        off += n
    return scores


def _softmax_pv(scores, segments, sink):
    m = functools.reduce(jnp.maximum, [jnp.max(s, axis=1, keepdims=True) for s in scores])
    if sink is not None:
        m = jnp.maximum(m, sink)
    es = [jnp.exp(s - m) for s in scores]
    l = functools.reduce(jnp.add, [jnp.sum(e, axis=1, keepdims=True) for e in es])
    if sink is not None:
        l = l + jnp.exp(sink - m)
    o = None
    for e, (k, v, transposed) in zip(es, segments):
        part = _dot_nt(e.astype(BF16), v) if transposed else _dot(e.astype(BF16), v)
        o = part if o is None else o + part
    return o / l


def _level_exponent(b, m, t):
    bcast = lambda i: jnp.broadcast_to(b[i:i + 1], (SUBLANES, b.shape[1]))
    low_half = lax.broadcasted_iota(jnp.int32, (SUBLANES, b.shape[1]), 0) < SUBLANES // 2
    parts = []
    for g in range(CHUNK // SUBLANES):
        r = g * SUBLANES
        rows = b[r:r + SUBLANES]
        if m >= SUBLANES:
            ref = b[(r // (2 * m)) * 2 * m + m - 1:(r // (2 * m)) * 2 * m + m]
            parts.append(rows - ref if (r // m) % 2 == 1 else ref - rows)
        elif m == 4:
            parts.append(rows - bcast(r + 3))
        else:
            parts.append(rows - jnp.where(low_half, bcast(r + 1), bcast(r + 5)))
    d = jnp.concatenate(parts, axis=0)
    return d if m >= SUBLANES else jnp.where((t & m) != 0, d, -d)


def _odd_block_rows(x, m, groups):
    return jnp.concatenate([x[g * CHUNK + r:g * CHUNK + r + SUBLANES] for g in range(groups)
                            for r in range(0, CHUNK, SUBLANES) if (r // m) % 2 == 1], axis=0)


def _spread_odd_block_rows(y, m, groups):
    zero = jnp.zeros((SUBLANES, y.shape[1]), y.dtype)
    parts, i = [], 0
    for g in range(groups):
        for r in range(0, CHUNK, SUBLANES):
            if (r // m) % 2 == 1:
                parts.append(y[i:i + SUBLANES])
                i += SUBLANES
            else:
                parts.append(zero)
    return jnp.concatenate(parts, axis=0)


def _gla_levels(q, k, la, b, hm_a):
    t = lax.broadcasted_iota(jnp.int32, (CHUNK, D_AQK), 0)
    q_st = _stack_masked(q, hm_a)
    k16 = k.astype(BF16)
    prods = [_dot_nt(q_st.astype(BF16), k16)]
    for m in GLA_LEVELS:
        e = jnp.exp(la if m == 1 else _level_exponent(b, m, t))
        kt = k16 if m == 1 else (k * e).astype(BF16)
        if m >= SUBLANES:
            qt = _odd_block_rows(q_st, m, A_HEADS) * _tile_rows(_odd_block_rows(e, m, 1), A_HEADS)
            prods.append(_spread_odd_block_rows(_dot_nt(qt.astype(BF16), kt), m, A_HEADS))
        else:
            qt = (q_st * _tile_rows(e, A_HEADS)).astype(BF16)
            prods.append(_dot_nt(qt, kt))
    return q_st, prods


def _gla_finish(q_st, prods, k, v, b, s, lvl):
    attn = jnp.where(lvl == 0, prods[0], 0.0)
    for li in range(1, len(prods)):
        attn = jnp.where(lvl == li, prods[li], attn)
    qhat = (q_st * _tile_rows(jnp.exp(b), A_HEADS)).astype(BF16)
    o_inter = _dot(qhat, s.astype(BF16))
    v16 = v.astype(BF16)
    o_intra = jnp.concatenate(
        [_dot(attn[h * CHUNK:(h + 1) * CHUNK].astype(BF16), v16[:, h * A_DV:(h + 1) * A_DV])
         for h in range(A_HEADS)], axis=0)
    b_last = b[CHUNK - 1:CHUNK]
    khat = (k * jnp.exp(b_last - b)).astype(BF16)
    upd = _dot_tn(khat, v16)
    decay = jnp.broadcast_to(jnp.exp(b_last), (A_DV, D_AQK)).T
    s_new = s * decay + jnp.concatenate(
        [upd[h * A_DK:(h + 1) * A_DK, h * A_DV:(h + 1) * A_DV] for h in range(A_HEADS)], axis=0)
    return o_intra + o_inter, s_new


def _layer_kernel(*refs, prompt, final, nch, tq, nt, layer):
    (x_ref, mods_ref, modp_ref, gmix_ref, gmlp_ref, gfin_ref, anorm_ref, win_hbm, wa2_ref, ba2_ref, wout_hbm,
     wup_hbm, wdown_hbm, biasb_ref, biasc_ref, sink_ref, tri_ref, lvl_ref) = refs[:18]
    pos = 18
    if not prompt:
        st_ref, cbk_ref, cbv_ref, cck_ref, ccv_ref = refs[pos:pos + 5]
        pos += 5 + N_SEQ_OUT
    y_ref, so_ref, kbo_ref, vbo_ref, kco_ref, vco_ref = refs[pos:pos + 6]
    pos += 6
    (h_scr, h2_scr, proj_scr, la_scr, b_scr, mix_scr, x1_scr, acc_scr, st_scr,
     kbw, vbw, kcw, vcw, win_ref, wout_ref, wup_ref, wdown_ref, wsem) = refs[pos:]

    step = pl.program_id(0)

    late_weights = [pltpu.make_async_copy(src.at[layer], dst, wsem.at[j]) for j, (src, dst) in
                    enumerate(((wout_hbm, wout_ref), (wup_hbm, wup_ref), (wdown_hbm, wdown_ref)))]
    win_groups = ((OFF_RA, OFF_RA + LANES), (0, OFF_GA), (OFF_GA, OFF_QC), (OFF_QC, OFF_RA))
    win_copies = [pltpu.make_async_copy(win_hbm.at[layer, :, a:b], win_ref.at[:, a:b], wsem.at[3 + j])
                  for j, (a, b) in enumerate(win_groups)]

    def mod(ref, c, j):
        return ref[0 if prompt else c, j:j + 1, :]

    @pl.when(step == 0)
    def _():
        for cp in win_copies + late_weights:
            cp.start()
        x1_scr[...] = jnp.zeros_like(x1_scr)
        acc_scr[...] = jnp.zeros_like(acc_scr)
        if prompt:
            st_scr[...] = jnp.zeros_like(st_scr)
            kbw[...] = jnp.zeros_like(kbw)
            vbw[...] = jnp.zeros_like(vbw)
            kcw[...] = jnp.zeros_like(kcw)
            vcw[...] = jnp.zeros_like(vcw)

    def finish_previous_tile():
        gfin = gfin_ref[...]
        for c in range(nch):
            r = slice(c * CHUNK, (c + 1) * CHUNK)
            x2 = x1_scr[r, :] + mod(modp_ref, c, 5) * acc_scr[r, :]
            y_ref[r, :] = _rms(x2, gfin) if final else x2
        acc_scr[...] = jnp.zeros_like(acc_scr)

    def mixer_input(first_step):
        def arrived(j):
            if first_step:
                win_copies[j].wait()

        gmix = gmix_ref[...]
        for c in range(nch):
            r = slice(c * CHUNK, (c + 1) * CHUNK)
            h = _rms(x_ref[r, :], gmix * (1.0 + mod(mods_ref, c, 1))) + mod(mods_ref, c, 0)
            h_scr[r, :] = h.astype(BF16)
        hmix = h_scr[...]
        cut1, cut2 = OFF_GA, OFF_QC
        arrived(0)
        ra = _dot(hmix, win_ref[:, OFF_RA:OFF_RA + LANES])
        arrived(1)
        proj_scr[:, 0:cut1] = _dot(hmix, win_ref[:, 0:cut1])
        z = _dot(ra.astype(BF16), wa2_ref[...]) + ba2_ref[...]
        la = (jnp.minimum(z, 0.0) - jnp.log1p(jnp.exp(-jnp.abs(z)))) * (1.0 / A_GATE_TAU)
        la_scr[...] = la
        arrived(2)
        proj_scr[:, cut1:cut2] = _dot(hmix, win_ref[:, cut1:cut2])
        tri = tri_ref[...]
        hi, mid, lo = _split3(la)
        b_scr[...] = _dot(tri, hi) + _dot(tri, mid) + _dot(tri, lo)
        arrived(3)
        proj_scr[:, cut2:OFF_RA] = _dot(hmix, win_ref[:, cut2:OFF_RA])

        if prompt:
            kbw[B_WINDOW:B_WINDOW + tq, :] = proj_scr[:, OFF_KB:OFF_KB + D_BKV].astype(BF16)
            vbw[B_WINDOW:B_WINDOW + tq, :] = proj_scr[:, OFF_VB:OFF_VB + D_BKV].astype(BF16)
            kcw[C_BAND:C_BAND + tq, :] = proj_scr[:, OFF_KC:OFF_KC + D_C].astype(BF16)
            vcw[C_BAND:C_BAND + tq, :] = proj_scr[:, OFF_VC:OFF_VC + D_C].astype(BF16)

    @pl.when(step < nt)
    def _():
        hm_a = _head_masks(D_AQK, A_DK)
        hm_c = _head_masks(D_C, HEAD_DIM)
        hm_b = _head_masks(D_BKV, HEAD_DIM)
        anorm = anorm_ref[...]

        def chunk_body(c, carry, with_mlp):
            r0 = pl.multiple_of(c * CHUNK, CHUNK)
            rows = pl.ds(r0, CHUNK)

            if prompt:
                seg_b = [(kbw[pl.ds(r0, LK_B), :], vbw[pl.ds(r0, LK_B), :], False)]
                seg_c = [(kcw[pl.ds(r0, LK_C), :], vcw[pl.ds(r0, LK_C), :], False)]
                gidx = step * nch + c
                lane_b = lax.broadcasted_iota(jnp.int32, (1, LK_B), 1)
                lane_c = lax.broadcasted_iota(jnp.int32, (1, LK_C), 1)
                mask_b = jnp.where(lane_b >= B_WINDOW - gidx * CHUNK, 0.0, NEG_INF)
                mask_c = jnp.where(lane_c >= C_BAND - gidx * CHUNK, 0.0, NEG_INF)
            else:
                kb_new = proj_scr[rows, OFF_KB:OFF_KB + D_BKV]
                vb_new = proj_scr[rows, OFF_VB:OFF_VB + D_BKV]
                kc_new = proj_scr[rows, OFF_KC:OFF_KC + D_C]
                vc_new = proj_scr[rows, OFF_VC:OFF_VC + D_C]
                seg_b = [(cbk_ref[c].astype(BF16), cbv_ref[c].astype(BF16), True),
                         (kb_new.astype(BF16), vb_new.astype(BF16), False)]
                seg_c = [(cck_ref[c].astype(BF16), ccv_ref[c].astype(BF16), True),
                         (kc_new.astype(BF16), vc_new.astype(BF16), False)]
                mask_b = mask_c = None

            q = proj_scr[rows, OFF_QA:OFF_QA + D_AQK] * (A_DK ** -0.5)
            k = proj_scr[rows, OFF_KA:OFF_KA + D_AQK]
            b = b_scr[rows, :]
            q_st, prods = _gla_levels(q, k, la_scr[rows, :], b, hm_a)
            qc = proj_scr[rows, OFF_QC:OFF_QC + D_C] * (HEAD_DIM ** -0.5)
            sc_c = _scores(_stack_masked(qc, hm_c).astype(BF16), seg_c, biasc_ref[...], mask_c)
            qb = proj_scr[rows, OFF_QB:OFF_QB + B_HEADS * HEAD_DIM] * (HEAD_DIM ** -0.5)
            qb_a, qb_b = qb[:, 0:D_BKV], qb[:, D_BKV:2 * D_BKV]
            qb_st = jnp.concatenate([jnp.where(hm_b[0], qb_a, 0.0), jnp.where(hm_b[0], qb_b, 0.0),
                                     jnp.where(hm_b[1], qb_a, 0.0), jnp.where(hm_b[1], qb_b, 0.0)], axis=0)
            sc_b = _scores(qb_st.astype(BF16), seg_b, biasb_ref[...], mask_b)

            half = FF_BLOCK // 2
            if with_mlp:
                up_a = _dot(h2_scr[...], wup_ref[c, :, 0:half])
                act_a = jnp.square(jnp.maximum(up_a, 0.0)).astype(BF16)

            v = proj_scr[rows, OFF_VA:OFF_VA + D_AV]
            s_old = st_scr[...] if prompt else st_ref[c]
            o_st, s_new = _gla_finish(q_st, prods, k, v, b, s_old, lvl_ref[...])
            if prompt:
                st_scr[...] = s_new
            else:
                so_ref[c] = s_new
            g = proj_scr[rows, OFF_GA:OFF_GA + D_AV]
            for h in range(A_HEADS):
                on = _rms(o_st[h * CHUNK:(h + 1) * CHUNK], anorm)
                gh = g[:, h * A_DV:(h + 1) * A_DV]
                mix_scr[rows, h * A_DV:(h + 1) * A_DV] = (on * (gh * jax.nn.sigmoid(gh))).astype(BF16)

            if with_mlp:
                up_b = _dot(h2_scr[...], wup_ref[c, :, half:FF_BLOCK])
                act_b = jnp.square(jnp.maximum(up_b, 0.0)).astype(BF16)

            oc_st = _softmax_pv(sc_c, seg_c, None)
            oc = jnp.where(hm_c[0], oc_st[0:CHUNK], 0.0)
            for h in range(1, C_HEADS):
                oc = jnp.where(hm_c[h], oc_st[h * CHUNK:(h + 1) * CHUNK], oc)
            mix_scr[rows, D_AV + 2 * D_BKV:D_MIX] = oc.astype(BF16)

            ob_st = _softmax_pv(sc_b, seg_b, sink_ref[...])
            ob_a = jnp.where(hm_b[0], ob_st[0:CHUNK], ob_st[2 * CHUNK:3 * CHUNK])
            ob_b = jnp.where(hm_b[0], ob_st[CHUNK:2 * CHUNK], ob_st[3 * CHUNK:4 * CHUNK])
            mix_scr[rows, D_AV:D_AV + D_BKV] = ob_a.astype(BF16)
            mix_scr[rows, D_AV + D_BKV:D_AV + 2 * D_BKV] = ob_b.astype(BF16)

            if with_mlp:
                acc_scr[...] += _dot(act_a, wdown_ref[c, 0:half, :]) + _dot(act_b, wdown_ref[c, half:FF_BLOCK, :])

            if not prompt:
                for dst, src, new, width in ((kbo_ref, cbk_ref, kb_new, B_WINDOW), (vbo_ref, cbv_ref, vb_new, B_WINDOW),
                                             (kco_ref, cck_ref, kc_new, C_BAND), (vco_ref, ccv_ref, vc_new, C_BAND)):
                    dst[c, :, 0:width - CHUNK] = src[c, :, CHUNK:width]
                    dst[c, :, width - CHUNK:width] = new.T
            return carry

        @pl.when(step == 0)
        def _():
            mixer_input(True)
            lax.fori_loop(0, nch, functools.partial(chunk_body, with_mlp=False), 0)
            for cp in late_weights:
                cp.wait()

        @pl.when(step > 0)
        def _():
            mixer_input(False)
            lax.fori_loop(0, nch, functools.partial(chunk_body, with_mlp=True), 0, unroll=2)

        if prompt:
            kbw[0:B_WINDOW, :] = kbw[tq:tq + B_WINDOW, :]
            vbw[0:B_WINDOW, :] = vbw[tq:tq + B_WINDOW, :]
            kcw[0:C_BAND, :] = kcw[tq:tq + C_BAND, :]
            vcw[0:C_BAND, :] = vcw[tq:tq + C_BAND, :]

            so_ref[...] = st_scr[...]
            kbo_ref[...] = proj_scr[tq - B_WINDOW:tq, OFF_KB:OFF_KB + D_BKV]
            vbo_ref[...] = proj_scr[tq - B_WINDOW:tq, OFF_VB:OFF_VB + D_BKV]
            rb = kco_ref.shape[0]
            kco_ref[...] = proj_scr[tq - rb:tq, OFF_KC:OFF_KC + D_C]
            vco_ref[...] = proj_scr[tq - rb:tq, OFF_VC:OFF_VC + D_C]

        mixed = _dot(mix_scr[...], wout_ref[...])

        finish_previous_tile()

        gmlp = gmlp_ref[...]
        for c in range(nch):
            r = slice(c * CHUNK, (c + 1) * CHUNK)
            x1 = x_ref[r, :] + mod(mods_ref, c, 2) * mixed[r, :]
            x1_scr[r, :] = x1
            h2 = _rms(x1, gmlp * (1.0 + mod(mods_ref, c, 4))) + mod(mods_ref, c, 3)
            h2_scr[r, :] = h2.astype(BF16)

    @pl.when(step == nt)
    def _():
        def mlp_block(c, carry):
            up = _dot(h2_scr[...], wup_ref[c])
            act = jnp.square(jnp.maximum(up, 0.0)).astype(BF16)
            acc_scr[...] += _dot(act, wdown_ref[c])
            return carry

        lax.fori_loop(0, nch, mlp_block, 0)
        finish_previous_tile()


def _layer_spec(arr, layer):
    nd = arr.ndim - 1
    return pl.BlockSpec((None,) + arr.shape[1:], lambda i, _l=layer, _nd=nd: (_l,) + (0,) * _nd,
                        pipeline_mode=pl.Buffered(1))


def _const_spec(arr):
    nd = arr.ndim
    return pl.BlockSpec(arr.shape, lambda i, _nd=nd: (0,) * _nd, pipeline_mode=pl.Buffered(1))


def _run_layer(x, mods, layer, stacked, shared, caches, carried, *, prompt, final, tq, nb):
    rows = x.shape[0]
    nch = tq // CHUNK
    nt = rows // tq
    assert rows % tq == 0 and nch == N_FF
    (gmix, gmlp, anorm, win, wa2, ba2, wout, wup, wdown, biasc, sink) = stacked
    (gfin, biasb, tri, lvl) = shared
    cur = lambda i: jnp.minimum(i, nt - 1)
    prv = lambda i: jnp.maximum(i - 1, 0)

    nmod = 1 if prompt else nb
    in_specs = [pl.BlockSpec((tq, D_MODEL), lambda i: (cur(i), 0)),
                pl.BlockSpec((nmod, SUBLANES, D_MODEL), lambda i: (0 if prompt else cur(i), 0, 0)),
                pl.BlockSpec((nmod, SUBLANES, D_MODEL), lambda i: (0 if prompt else prv(i), 0, 0))]
    ls = functools.partial(_layer_spec, layer=layer)
    hbm = pl.BlockSpec(memory_space=pl.ANY)
    in_specs += [ls(gmix), ls(gmlp), _const_spec(gfin), ls(anorm), hbm, ls(wa2), ls(ba2), hbm,
                 hbm, hbm, _const_spec(biasb), ls(biasc), ls(sink), _const_spec(tri), _const_spec(lvl)]
    args = [x, mods, mods, gmix, gmlp, gfin, anorm, win, wa2, ba2, wout, wup, wdown, biasb, biasc, sink, tri, lvl]

    y_spec = pl.BlockSpec((tq, D_MODEL), lambda i: (prv(i), 0))
    if prompt:
        assert tq >= B_WINDOW and (tq % C_BAND == 0 or C_BAND % tq == 0)
        rb = min(tq, C_BAND)
        first = nt - C_BAND // rb
        out_shape = [jax.ShapeDtypeStruct((rows, D_MODEL), F32),
                     jax.ShapeDtypeStruct((D_AQK, A_DV), F32),
                     jax.ShapeDtypeStruct((B_WINDOW, D_BKV), F32),
                     jax.ShapeDtypeStruct((B_WINDOW, D_BKV), F32),
                     jax.ShapeDtypeStruct((C_BAND, D_C), F32),
                     jax.ShapeDtypeStruct((C_BAND, D_C), F32)]
        band = pl.BlockSpec((rb, D_C), lambda i: (jnp.maximum(cur(i) - first, 0), 0))
        out_specs = [y_spec,
                     pl.BlockSpec((D_AQK, A_DV), lambda i: (0, 0)),
                     pl.BlockSpec((B_WINDOW, D_BKV), lambda i: (0, 0)),
                     pl.BlockSpec((B_WINDOW, D_BKV), lambda i: (0, 0)),
                     band, band]
        kbw_rows, kcw_rows = B_WINDOW + tq, C_BAND + tq
        aliases = {}
    else:
        assert nb == nch
        nseq = rows // CHUNK
        st, cbk, cbv, cck, ccv = caches
        assert len(carried) == N_SEQ_OUT
        seq = lambda r, w: pl.BlockSpec((None, nb, r, w), lambda i, _l=layer: (_l, cur(i), 0, 0))
        seq_dims = [(D_AQK, A_DV), (D_BKV, B_WINDOW), (D_BKV, B_WINDOW), (D_C, C_BAND), (D_C, C_BAND)]
        in_specs += [seq(r, w) for r, w in seq_dims]
        aliases = {len(in_specs) + j: 1 + j for j in range(N_SEQ_OUT)}
        in_specs += [pl.BlockSpec(memory_space=pl.ANY)] * N_SEQ_OUT
        args += [st, cbk, cbv, cck, ccv, *carried]
        out_shape = [jax.ShapeDtypeStruct((rows, D_MODEL), F32)] + [
            jax.ShapeDtypeStruct((DEPTH, nseq, r, w), F32) for r, w in seq_dims]
        out_specs = [y_spec] + [seq(r, w) for r, w in seq_dims]
        kbw_rows, kcw_rows = SUBLANES * 2, SUBLANES * 2

    scratch = [pltpu.VMEM((tq, D_MODEL), BF16),
               pltpu.VMEM((tq, D_MODEL), BF16),
               pltpu.VMEM((tq, OFF_RA), F32),
               pltpu.VMEM((tq, D_AQK), F32),
               pltpu.VMEM((tq, D_AQK), F32),
               pltpu.VMEM((tq, D_MIX), BF16),
               pltpu.VMEM((tq, D_MODEL), F32),
               pltpu.VMEM((tq, D_MODEL), F32),
               pltpu.VMEM((D_AQK, A_DV), F32),
               pltpu.VMEM((kbw_rows, D_BKV), BF16), pltpu.VMEM((kbw_rows, D_BKV), BF16),
               pltpu.VMEM((kcw_rows, D_C), BF16), pltpu.VMEM((kcw_rows, D_C), BF16),
               pltpu.VMEM(win.shape[1:], BF16),
               pltpu.VMEM(wout.shape[1:], BF16), pltpu.VMEM(wup.shape[1:], BF16), pltpu.VMEM(wdown.shape[1:], BF16),
               pltpu.SemaphoreType.DMA((7,))]

    kern = functools.partial(_layer_kernel, prompt=prompt, final=final, nch=nch, tq=tq, nt=nt, layer=layer)
    return pl.pallas_call(
        kern,
        grid=(nt + 1,),
        in_specs=in_specs,
        out_specs=out_specs,
        out_shape=out_shape,
        scratch_shapes=scratch,
        input_output_aliases=aliases,
        compiler_params=pltpu.CompilerParams(
            dimension_semantics=("arbitrary",),
            vmem_limit_bytes=VMEM_LIMIT_BYTES if prompt else VMEM_LIMIT_SAMPLE_BYTES),
        name=("layer_prompt" if prompt else "layer_sample") + ("_final" if final else ""),
    )(*args)


ADA_BLOCK = 1536


def _ada_kernel(c_ref, w_ref, b_ref, o_ref):
    c = c_ref[...]
    sc = c * jax.nn.sigmoid(c)
    sc_hi = sc.astype(BF16)
    sc_lo = (sc - sc_hi.astype(F32)).astype(BF16)
    w = w_ref[0]
    w_hi = w.astype(BF16)
    w_lo = (w - w_hi.astype(F32)).astype(BF16)
    o_ref[0] = _dot(sc_hi, w_hi) + _dot(sc_lo, w_hi) + _dot(sc_hi, w_lo) + b_ref[0]


def _ada(c_all, w_ada, b_ada):
    rows = c_all.shape[0]
    n = 6 * D_MODEL
    return pl.pallas_call(
        _ada_kernel,
        grid=(DEPTH, n // ADA_BLOCK),
        in_specs=[pl.BlockSpec((rows, D_MODEL), lambda l, j: (0, 0)),
                  pl.BlockSpec((1, D_MODEL, ADA_BLOCK), lambda l, j: (l, 0, j)),
                  pl.BlockSpec((1, 1, ADA_BLOCK), lambda l, j: (l, 0, j))],
        out_specs=pl.BlockSpec((1, rows, ADA_BLOCK), lambda l, j: (l, 0, j)),
        out_shape=jax.ShapeDtypeStruct((DEPTH, rows, n), F32),
        compiler_params=pltpu.CompilerParams(dimension_semantics=("arbitrary", "arbitrary"),
                                             vmem_limit_bytes=VMEM_LIMIT_BYTES),
        name="adaln",
    )(c_all, w_ada, b_ada.reshape(DEPTH, 1, n))


TB_W = 2 * LANES
TC_W = 5 * LANES
TC_FLAT = TC_W - (CHUNK + C_CLIP + 1)


def _shear(f, width):
    tiled = jnp.broadcast_to(f, (CHUNK, width))
    return pltpu.roll(tiled, width - (CHUNK - 1), 1, stride=1, stride_axis=0)


def _bias_kernel(t5_ref, crel_ref, bucket_ref, ob_ref, oc_ref):
    bucket = bucket_ref[...]
    for h in range(B_HEADS):
        def b_body(i, acc, h=h):
            return jnp.where(bucket == i, t5_ref[h, i], acc)
        f = lax.fori_loop(0, T5_BUCKETS, b_body, jnp.zeros(bucket.shape, F32))
        ob_ref[h] = _shear(f, TB_W)[:, 0:LK_B]
    lane = lax.broadcasted_iota(jnp.int32, (1, TC_W), 1)
    for l in range(DEPTH):
        for h in range(C_HEADS):
            row = crel_ref[l, h:h + 1, :]
            f = jnp.where(lane < TC_FLAT, row[:, 0:1], pltpu.roll(row, TC_FLAT, 1))
            oc_ref[l, h] = _shear(f, TC_W)[:, 0:LK_C]


def _bias_tables(t5_bias, c_rel_bias, bucket):
    smem = pl.BlockSpec(memory_space=pltpu.SMEM)
    vmem = pl.BlockSpec(memory_space=pltpu.VMEM)
    crel = jnp.pad(jnp.swapaxes(c_rel_bias, 1, 2),
                   ((0, 0), (0, SUBLANES - C_HEADS), (0, TC_W - (2 * C_CLIP + 1))))
    return pl.pallas_call(
        _bias_kernel,
        in_specs=[smem, vmem, vmem],
        out_specs=[vmem, vmem],
        out_shape=[jax.ShapeDtypeStruct((B_HEADS, CHUNK, LK_B), F32),
                   jax.ShapeDtypeStruct((DEPTH, C_HEADS, CHUNK, LK_C), F32)],
        name="bias_tables",
    )(t5_bias.T, crel, bucket)


def _t5_bucket(rel):
    half = T5_BUCKETS // 2
    max_exact = half // 2
    steps = half - max_exact
    thresholds = [int(np.ceil(max_exact * (T5_MAX_DIST / max_exact) ** (k / steps) - 1e-9)) for k in range(1, steps)]
    n = jnp.abs(rel)
    large = jnp.minimum(max_exact + sum((n >= t).astype(jnp.int32) for t in thresholds), half - 1)
    return jnp.where(rel > 0, half, 0) + jnp.where(n < max_exact, n, large)


def _gla_constants(tq):
    r = np.arange(tq)
    tri = ((r[:, None] // CHUNK == r[None, :] // CHUNK) & (r[None, :] <= r[:, None])).astype(np.float32)
    t = np.arange(CHUNK)[:, None]
    s = np.arange(CHUNK)[None, :]
    lvl = np.full((CHUNK, CHUNK), -1, np.int32)
    lvl[t == s] = 0
    for li, m in enumerate(GLA_LEVELS):
        sel = ((t // m) % 2 == 1) & ((s // m) == (t // m) - 1)
        lvl[np.broadcast_to(sel, lvl.shape)] = li + 1
    return jnp.asarray(tri, BF16), jnp.asarray(np.tile(lvl, (A_HEADS, 1)))


SRC_RA = 2 * D_AQK + 2 * D_AV
SRC_QB = SRC_RA + A_GATE_RANK
SRC_KB = SRC_QB + B_HEADS * HEAD_DIM
D_IN = SRC_KB + 2 * D_BKV + 3 * D_C
QB_HEAD_ORDER = (0, 2, 1, 3)


def _w_in_segments():
    segs = [(OFF_QA, 0, SRC_RA)]
    segs += [(OFF_QB + i * HEAD_DIM, SRC_QB + h * HEAD_DIM, HEAD_DIM) for i, h in enumerate(QB_HEAD_ORDER)]
    segs += [(OFF_KB, SRC_KB, D_IN - SRC_KB), (OFF_RA, SRC_RA, A_GATE_RANK)]
    return segs


def _w_in_kernel(w_ref, o_ref):
    segs = _w_in_segments()
    for blk in range(P_IN // LANES):
        lo, hi = blk * LANES, (blk + 1) * LANES
        rows, col = [], lo
        while col < hi:
            hit = [(d, s, w) for d, s, w in segs if d <= col < d + w]
            if hit:
                d, s, w = hit[0]
                n = min(hi, d + w) - col
                rows.append(w_ref[s + col - d:s + col - d + n, :])
            else:
                n = hi - col
                rows.append(jnp.zeros((n, D_MODEL), F32))
            col += n
        o_ref[:, lo:hi] = jnp.concatenate(rows, axis=0).T.astype(BF16)


def _prep_w_in(w_in):
    w_fm = jnp.swapaxes(w_in, 1, 2)
    return pl.pallas_call(
        _w_in_kernel,
        grid=(DEPTH,),
        in_specs=[pl.BlockSpec((None, D_IN, D_MODEL), lambda l: (l, 0, 0))],
        out_specs=pl.BlockSpec((None, D_MODEL, P_IN), lambda l: (l, 0, 0)),
        out_shape=jax.ShapeDtypeStruct((DEPTH, D_MODEL, P_IN), BF16),
        compiler_params=pltpu.CompilerParams(dimension_semantics=("arbitrary",),
                                             vmem_limit_bytes=VMEM_LIMIT_BYTES),
        name="prep_w_in",
    )(w_fm)


def _cast_kernel(w_ref, o_ref):
    o_ref[...] = w_ref[...].astype(BF16)


def _prep_w_up(w_up):
    return pl.pallas_call(
        _cast_kernel,
        grid=(DEPTH, N_FF),
        in_specs=[pl.BlockSpec((None, D_MODEL, FF_BLOCK), lambda l, j: (l, 0, j))],
        out_specs=pl.BlockSpec((None, None, D_MODEL, FF_BLOCK), lambda l, j: (l, j, 0, 0)),
        out_shape=jax.ShapeDtypeStruct((DEPTH, N_FF, D_MODEL, FF_BLOCK), BF16),
        compiler_params=pltpu.CompilerParams(dimension_semantics=("arbitrary", "arbitrary"),
                                             vmem_limit_bytes=VMEM_LIMIT_BYTES),
        name="prep_w_up",
    )(w_up)


TQ_PROMPT = 256
NB_SAMPLE = 4


def kernel(x_prompt, x_sample, c_prompt, c_sample, state_gla, cache_b_k, cache_b_v, cache_c_k, cache_c_v,
           w_ada, b_ada, norm_mix_g, norm_mlp_g, w_in, w_a2, b_a2, a_norm_g, b_sink, t5_bias, c_rel_bias,
           w_out, w_up, w_down, final_norm_g):
    bsz, seq, _ = x_prompt.shape
    dec_b, dec_s, _ = x_sample.shape
    assert bsz == 1 and dec_s == CHUNK and TQ_PROMPT == NB_SAMPLE * CHUNK

    n_c = bsz + dec_b
    c_rows = -(-n_c // SUBLANES) * SUBLANES
    c_all = jnp.concatenate([c_prompt, c_sample, jnp.zeros((c_rows - n_c, D_MODEL), F32)], axis=0)
    mods = _ada(c_all, w_ada, b_ada).reshape(DEPTH, c_rows, 6, D_MODEL)
    mods = jnp.pad(mods, ((0, 0), (0, 0), (0, SUBLANES - 6), (0, 0)))

    rel_b = jnp.arange(TB_W) - (CHUNK - 1) - B_WINDOW
    bias_b, bias_c = _bias_tables(t5_bias, c_rel_bias, _t5_bucket(rel_b).astype(jnp.int32)[None, :])
    bias_b = bias_b.reshape(STACK, LK_B)
    bias_c = bias_c.reshape(DEPTH, STACK, LK_C)

    tri, lvl = _gla_constants(TQ_PROMPT)
    ob0 = D_AV
    orow = lambda h: slice(ob0 + h * HEAD_DIM, ob0 + (h + 1) * HEAD_DIM)

    win = _prep_w_in(w_in)
    wa2 = jnp.concatenate([w_a2, jnp.zeros((DEPTH, LANES - A_GATE_RANK, D_AQK), F32)], axis=1).astype(BF16)
    wout = jnp.concatenate([w_out[:, 0:D_AV], w_out[:, orow(0)], w_out[:, orow(2)], w_out[:, orow(1)],
                            w_out[:, orow(3)], w_out[:, D_AV + B_HEADS * HEAD_DIM:]], axis=1).astype(BF16)
    wup = _prep_w_up(w_up)
    wdown = w_down.astype(BF16).reshape(DEPTH, N_FF, FF_BLOCK, D_MODEL)
    sink = jnp.repeat(b_sink, CHUNK, axis=1)[:, :, None]
    stacked = (norm_mix_g[:, None, :], norm_mlp_g[:, None, :], a_norm_g[:, None, :], win, wa2,
               b_a2[:, None, :], wout, wup, wdown, bias_c, sink)
    shared = (final_norm_g[None, :], bias_b, tri, lvl)

    x_p = x_prompt.reshape(seq, D_MODEL)
    x_s = x_sample.reshape(dec_b * dec_s, D_MODEL)
    outs_p = []
    fmajor = lambda cache: jnp.transpose(cache, (0, 1, 3, 4, 2)).reshape(
        DEPTH, dec_b, cache.shape[3] * cache.shape[4], cache.shape[2])
    caches = (state_gla.reshape(DEPTH, dec_b, D_AQK, A_DV), fmajor(cache_b_k), fmajor(cache_b_v),
              fmajor(cache_c_k), fmajor(cache_c_v))
    outs_s = tuple(jnp.zeros(c.shape, F32) for c in caches)
    for l in range(DEPTH):
        final = l == DEPTH - 1
        res_p = _run_layer(x_p, mods[l, 0:bsz], l, stacked, shared, None, None, prompt=True, final=final,
                           tq=TQ_PROMPT, nb=None)
        x_p = res_p[0]
        outs_p.append(res_p[1:])

        res_s = _run_layer(x_s, mods[l, bsz:bsz + dec_b], l, stacked, shared, caches, outs_s, prompt=False,
                           final=final, tq=NB_SAMPLE * CHUNK, nb=NB_SAMPLE)
        x_s = res_s[0]
        outs_s = tuple(res_s[1:])

    sg_p = jnp.stack([o[0].reshape(bsz, A_HEADS, A_DK, A_DV) for o in outs_p])
    kb_p = jnp.stack([o[1].reshape(bsz, B_WINDOW, B_KV_HEADS, HEAD_DIM) for o in outs_p])
    vb_p = jnp.stack([o[2].reshape(bsz, B_WINDOW, B_KV_HEADS, HEAD_DIM) for o in outs_p])
    kc_p = jnp.stack([o[3].reshape(bsz, C_BAND, C_HEADS, HEAD_DIM) for o in outs_p])
    vc_p = jnp.stack([o[4].reshape(bsz, C_BAND, C_HEADS, HEAD_DIM) for o in outs_p])
    sg_s = outs_s[0].reshape(DEPTH, dec_b, A_HEADS, A_DK, A_DV)
    tmajor = lambda arr, heads: jnp.transpose(arr.reshape(DEPTH, dec_b, heads, HEAD_DIM, -1), (0, 1, 4, 2, 3))
    kb_s = tmajor(outs_s[1], B_KV_HEADS)
    vb_s = tmajor(outs_s[2], B_KV_HEADS)
    kc_s = tmajor(outs_s[3], C_HEADS)
    vc_s = tmajor(outs_s[4], C_HEADS)
    return (x_p.reshape(bsz, seq, D_MODEL), x_s.reshape(dec_b, dec_s, D_MODEL),
            sg_p, kb_p, vb_p, kc_p, vc_p, sg_s, kb_s, vb_s, kc_s, vc_s)
```

```python
import functools

import numpy as np
import jax
import jax.numpy as jnp
from jax import lax
from jax.experimental import pallas as pl
from jax.experimental.pallas import tpu as pltpu

D_MODEL = 1024
DEPTH = 2
CHUNK = 64
HEAD_DIM = 64
A_HEADS = 4
A_DK = 64
A_DV = 128
A_GATE_RANK = 16
A_GATE_TAU = 16.0
B_HEADS = 4
B_KV_HEADS = 2
B_WINDOW = 128
C_HEADS = 4
C_BAND = 512
C_CLIP = 256
T5_BUCKETS = 32
T5_MAX_DIST = 128
D_FF = 4 * D_MODEL
NORM_EPS = 1e-6
NEG_INF = -1e30

LANES = 128
SUBLANES = 8
VMEM_LIMIT_BYTES = 60000 * 1024
VMEM_LIMIT_SAMPLE_BYTES = 62 * 1024 * 1024

OFF_QA = 0
OFF_KA = OFF_QA + A_HEADS * A_DK
OFF_VA = OFF_KA + A_HEADS * A_DK
OFF_GA = OFF_VA + A_HEADS * A_DV
OFF_QB = OFF_GA + A_HEADS * A_DV
OFF_KB = OFF_QB + B_HEADS * HEAD_DIM
OFF_VB = OFF_KB + B_KV_HEADS * HEAD_DIM
OFF_QC = OFF_VB + B_KV_HEADS * HEAD_DIM
OFF_KC = OFF_QC + C_HEADS * HEAD_DIM
OFF_VC = OFF_KC + C_HEADS * HEAD_DIM
OFF_RA = OFF_VC + C_HEADS * HEAD_DIM
P_IN = OFF_RA + LANES
D_AQK = A_HEADS * A_DK
D_AV = A_HEADS * A_DV
D_BKV = B_KV_HEADS * HEAD_DIM
D_C = C_HEADS * HEAD_DIM
D_MIX = D_AV + B_HEADS * HEAD_DIM + D_C
LK_B = B_WINDOW + CHUNK
LK_C = C_BAND + CHUNK
STACK = 4 * CHUNK
GLA_LEVELS = (1, 2, 4, 8, 16, 32)
FF_BLOCK = 1024
N_FF = D_FF // FF_BLOCK
N_SEQ_OUT = 5

BF16 = jnp.bfloat16
F32 = jnp.float32


def _dot(a, b):
    return jnp.dot(a, b, preferred_element_type=F32)


def _dot_nt(a, b):
    return lax.dot_general(a, b, (((1,), (1,)), ((), ())), preferred_element_type=F32)


def _dot_tn(a, b):
    return lax.dot_general(a, b, (((0,), (0,)), ((), ())), preferred_element_type=F32)


def _rms(x, g):
    return x * lax.rsqrt(jnp.mean(x * x, axis=-1, keepdims=True) + NORM_EPS) * g


def _split3(x):
    hi = x.astype(BF16)
    r1 = x - hi.astype(F32)
    mid = r1.astype(BF16)
    lo = (r1 - mid.astype(F32)).astype(BF16)
    return hi, mid, lo


def _head_masks(width, per_head):
    lane = lax.broadcasted_iota(jnp.int32, (1, width), 1)
    return [(lane >= h * per_head) & (lane < (h + 1) * per_head) for h in range(width // per_head)]


def _stack_masked(x, masks):
    return jnp.concatenate([jnp.where(m, x, 0.0) for m in masks], axis=0)


def _tile_rows(x, n):
    return jnp.concatenate([x] * n, axis=0)


def _scores(q_st, segments, bias_st, maskrow):
    scores, off = [], 0
    for k, v, transposed in segments:
        n = k.shape[1] if transposed else k.shape[0]
        s = (_dot(q_st, k) if transposed else _dot_nt(q_st, k)) + bias_st[:, off:off + n]
        if maskrow is not None:
            s = s + maskrow[:, off:off + n]
        scores.append(s)
        off += n
    return scores


def _softmax_pv(scores, segments, sink):
    m = functools.reduce(jnp.maximum, [jnp.max(s, axis=1, keepdims=True) for s in scores])
    if sink is not None:
        m = jnp.maximum(m, sink)
    es = [jnp.exp(s - m) for s in scores]
    l = functools.reduce(jnp.add, [jnp.sum(e, axis=1, keepdims=True) for e in es])
    if sink is not None:
        l = l + jnp.exp(sink - m)
    o = None
    for e, (k, v, transposed) in zip(es, segments):
        part = _dot_nt(e.astype(BF16), v) if transposed else _dot(e.astype(BF16), v)
        o = part if o is None else o + part
    return o / l


def _level_exponent(b, m, t):
    bcast = lambda i: jnp.broadcast_to(b[i:i + 1], (SUBLANES, b.shape[1]))
    low_half = lax.broadcasted_iota(jnp.int32, (SUBLANES, b.shape[1]), 0) < SUBLANES // 2
    parts = []
    for g in range(CHUNK // SUBLANES):
        r = g * SUBLANES
        rows = b[r:r + SUBLANES]
        if m >= SUBLANES:
            ref = b[(r // (2 * m)) * 2 * m + m - 1:(r // (2 * m)) * 2 * m + m]
            parts.append(rows - ref if (r // m) % 2 == 1 else ref - rows)
        elif m == 4:
            parts.append(rows - bcast(r + 3))
        else:
            parts.append(rows - jnp.where(low_half, bcast(r + 1), bcast(r + 5)))
    d = jnp.concatenate(parts, axis=0)
    return d if m >= SUBLANES else jnp.where((t & m) != 0, d, -d)


def _odd_block_rows(x, m, groups):
    return jnp.concatenate([x[g * CHUNK + r:g * CHUNK + r + SUBLANES] for g in range(groups)
                            for r in range(0, CHUNK, SUBLANES) if (r // m) % 2 == 1], axis=0)


def _spread_odd_block_rows(y, m, groups):
    zero = jnp.zeros((SUBLANES, y.shape[1]), y.dtype)
    parts, i = [], 0
    for g in range(groups):
        for r in range(0, CHUNK, SUBLANES):
            if (r // m) % 2 == 1:
                parts.append(y[i:i + SUBLANES])
                i += SUBLANES
            else:
                parts.append(zero)
    return jnp.concatenate(parts, axis=0)


def _gla_levels(q, k, la, b, hm_a):
    t = lax.broadcasted_iota(jnp.int32, (CHUNK, D_AQK), 0)
    q_st = _stack_masked(q, hm_a)
    k16 = k.astype(BF16)
    prods = [_dot_nt(q_st.astype(BF16), k16)]
    for m in GLA_LEVELS:
        e = jnp.exp(la if m == 1 else _level_exponent(b, m, t))
        kt = k16 if m == 1 else (k * e).astype(BF16)
        if m >= SUBLANES:
            qt = _odd_block_rows(q_st, m, A_HEADS) * _tile_rows(_odd_block_rows(e, m, 1), A_HEADS)
            prods.append(_spread_odd_block_rows(_dot_nt(qt.astype(BF16), kt), m, A_HEADS))
        else:
            qt = (q_st * _tile_rows(e, A_HEADS)).astype(BF16)
            prods.append(_dot_nt(qt, kt))
    return q_st, prods


def _gla_finish(q_st, prods, k, v, b, s, lvl):
    attn = jnp.where(lvl == 0, prods[0], 0.0)
    for li in range(1, len(prods)):
        attn = jnp.where(lvl == li, prods[li], attn)
    qhat = (q_st * _tile_rows(jnp.exp(b), A_HEADS)).astype(BF16)
    o_inter = _dot(qhat, s.astype(BF16))
    v16 = v.astype(BF16)
    o_intra = jnp.concatenate(
        [_dot(attn[h * CHUNK:(h + 1) * CHUNK].astype(BF16), v16[:, h * A_DV:(h + 1) * A_DV])
         for h in range(A_HEADS)], axis=0)
    b_last = b[CHUNK - 1:CHUNK]
    khat = (k * jnp.exp(b_last - b)).astype(BF16)
    upd = _dot_tn(khat, v16)
    decay = jnp.broadcast_to(jnp.exp(b_last), (A_DV, D_AQK)).T
    s_new = s * decay + jnp.concatenate(
        [upd[h * A_DK:(h + 1) * A_DK, h * A_DV:(h + 1) * A_DV] for h in range(A_HEADS)], axis=0)
    return o_intra + o_inter, s_new


def _layer_kernel(*refs, prompt, final, nch, tq, nt, layer):
    (x_ref, mods_ref, modp_ref, gmix_ref, gmlp_ref, gfin_ref, anorm_ref, win_ref, wa2_ref, ba2_ref, wout_hbm,
     wup_hbm, wdown_hbm, biasb_ref, biasc_ref, sink_ref, tri_ref, lvl_ref) = refs[:18]
    pos = 18
    if not prompt:
        st_ref, cbk_ref, cbv_ref, cck_ref, ccv_ref = refs[pos:pos + 5]
        pos += 5 + N_SEQ_OUT
    y_ref, so_ref, kbo_ref, vbo_ref, kco_ref, vco_ref = refs[pos:pos + 6]
    pos += 6
    (h_scr, h2_scr, proj_scr, la_scr, b_scr, mix_scr, x1_scr, acc_scr, st_scr,
     kbw, vbw, kcw, vcw, wout_ref, wup_ref, wdown_ref, wsem) = refs[pos:]

    step = pl.program_id(0)

    late_weights = [pltpu.make_async_copy(src.at[layer], dst, wsem.at[j]) for j, (src, dst) in
                    enumerate(((wout_hbm, wout_ref), (wup_hbm, wup_ref), (wdown_hbm, wdown_ref)))]

    def mod(ref, c, j):
        return ref[0 if prompt else c, j:j + 1, :]

    @pl.when(step == 0)
    def _():
        for cp in late_weights:
            cp.start()
        x1_scr[...] = jnp.zeros_like(x1_scr)
        acc_scr[...] = jnp.zeros_like(acc_scr)
        if prompt:
            st_scr[...] = jnp.zeros_like(st_scr)
            kbw[...] = jnp.zeros_like(kbw)
            vbw[...] = jnp.zeros_like(vbw)
            kcw[...] = jnp.zeros_like(kcw)
            vcw[...] = jnp.zeros_like(vcw)

    def finish_previous_tile():
        gfin = gfin_ref[...]
        for c in range(nch):
            r = slice(c * CHUNK, (c + 1) * CHUNK)
            x2 = x1_scr[r, :] + mod(modp_ref, c, 5) * acc_scr[r, :]
            y_ref[r, :] = _rms(x2, gfin) if final else x2
        acc_scr[...] = jnp.zeros_like(acc_scr)

    @pl.when(step < nt)
    def _():
        gmix = gmix_ref[...]
        for c in range(nch):
            r = slice(c * CHUNK, (c + 1) * CHUNK)
            h = _rms(x_ref[r, :], gmix * (1.0 + mod(mods_ref, c, 1))) + mod(mods_ref, c, 0)
            h_scr[r, :] = h.astype(BF16)
        hmix = h_scr[...]
        cut1, cut2 = OFF_GA, OFF_QC
        ra = _dot(hmix, win_ref[:, OFF_RA:OFF_RA + LANES])
        proj_scr[:, 0:cut1] = _dot(hmix, win_ref[:, 0:cut1])
        z = _dot(ra.astype(BF16), wa2_ref[...]) + ba2_ref[...]
        la = (jnp.minimum(z, 0.0) - jnp.log1p(jnp.exp(-jnp.abs(z)))) * (1.0 / A_GATE_TAU)
        la_scr[...] = la
        proj_scr[:, cut1:cut2] = _dot(hmix, win_ref[:, cut1:cut2])
        tri = tri_ref[...]
        hi, mid, lo = _split3(la)
        b_scr[...] = _dot(tri, hi) + _dot(tri, mid) + _dot(tri, lo)
        proj_scr[:, cut2:OFF_RA] = _dot(hmix, win_ref[:, cut2:OFF_RA])

        if prompt:
            kbw[B_WINDOW:B_WINDOW + tq, :] = proj_scr[:, OFF_KB:OFF_KB + D_BKV].astype(BF16)
            vbw[B_WINDOW:B_WINDOW + tq, :] = proj_scr[:, OFF_VB:OFF_VB + D_BKV].astype(BF16)
            kcw[C_BAND:C_BAND + tq, :] = proj_scr[:, OFF_KC:OFF_KC + D_C].astype(BF16)
            vcw[C_BAND:C_BAND + tq, :] = proj_scr[:, OFF_VC:OFF_VC + D_C].astype(BF16)

        hm_a = _head_masks(D_AQK, A_DK)
        hm_c = _head_masks(D_C, HEAD_DIM)
        hm_b = _head_masks(D_BKV, HEAD_DIM)
        anorm = anorm_ref[...]

        def chunk_body(c, carry, with_mlp):
            r0 = pl.multiple_of(c * CHUNK, CHUNK)
            rows = pl.ds(r0, CHUNK)

            if prompt:
                seg_b = [(kbw[pl.ds(r0, LK_B), :], vbw[pl.ds(r0, LK_B), :], False)]
                seg_c = [(kcw[pl.ds(r0, LK_C), :], vcw[pl.ds(r0, LK_C), :], False)]
                gidx = step * nch + c
                lane_b = lax.broadcasted_iota(jnp.int32, (1, LK_B), 1)
                lane_c = lax.broadcasted_iota(jnp.int32, (1, LK_C), 1)
                mask_b = jnp.where(lane_b >= B_WINDOW - gidx * CHUNK, 0.0, NEG_INF)
                mask_c = jnp.where(lane_c >= C_BAND - gidx * CHUNK, 0.0, NEG_INF)
            else:
                kb_new = proj_scr[rows, OFF_KB:OFF_KB + D_BKV]
                vb_new = proj_scr[rows, OFF_VB:OFF_VB + D_BKV]
                kc_new = proj_scr[rows, OFF_KC:OFF_KC + D_C]
                vc_new = proj_scr[rows, OFF_VC:OFF_VC + D_C]
                seg_b = [(cbk_ref[c].astype(BF16), cbv_ref[c].astype(BF16), True),
                         (kb_new.astype(BF16), vb_new.astype(BF16), False)]
                seg_c = [(cck_ref[c].astype(BF16), ccv_ref[c].astype(BF16), True),
                         (kc_new.astype(BF16), vc_new.astype(BF16), False)]
                mask_b = mask_c = None

            q = proj_scr[rows, OFF_QA:OFF_QA + D_AQK] * (A_DK ** -0.5)
            k = proj_scr[rows, OFF_KA:OFF_KA + D_AQK]
            b = b_scr[rows, :]
            q_st, prods = _gla_levels(q, k, la_scr[rows, :], b, hm_a)
            qc = proj_scr[rows, OFF_QC:OFF_QC + D_C] * (HEAD_DIM ** -0.5)
            sc_c = _scores(_stack_masked(qc, hm_c).astype(BF16), seg_c, biasc_ref[...], mask_c)
            qb = proj_scr[rows, OFF_QB:OFF_QB + B_HEADS * HEAD_DIM] * (HEAD_DIM ** -0.5)
            qb_a, qb_b = qb[:, 0:D_BKV], qb[:, D_BKV:2 * D_BKV]
            qb_st = jnp.concatenate([jnp.where(hm_b[0], qb_a, 0.0), jnp.where(hm_b[0], qb_b, 0.0),
                                     jnp.where(hm_b[1], qb_a, 0.0), jnp.where(hm_b[1], qb_b, 0.0)], axis=0)
            sc_b = _scores(qb_st.astype(BF16), seg_b, biasb_ref[...], mask_b)

            half = FF_BLOCK // 2
            if with_mlp:
                up_a = _dot(h2_scr[...], wup_ref[c, :, 0:half])
                act_a = jnp.square(jnp.maximum(up_a, 0.0)).astype(BF16)

            v = proj_scr[rows, OFF_VA:OFF_VA + D_AV]
            s_old = st_scr[...] if prompt else st_ref[c]
            o_st, s_new = _gla_finish(q_st, prods, k, v, b, s_old, lvl_ref[...])
            if prompt:
                st_scr[...] = s_new
            else:
                so_ref[c] = s_new
            g = proj_scr[rows, OFF_GA:OFF_GA + D_AV]
            for h in range(A_HEADS):
                on = _rms(o_st[h * CHUNK:(h + 1) * CHUNK], anorm)
                gh = g[:, h * A_DV:(h + 1) * A_DV]
                mix_scr[rows, h * A_DV:(h + 1) * A_DV] = (on * (gh * jax.nn.sigmoid(gh))).astype(BF16)

            if with_mlp:
                up_b = _dot(h2_scr[...], wup_ref[c, :, half:FF_BLOCK])
                act_b = jnp.square(jnp.maximum(up_b, 0.0)).astype(BF16)

            oc_st = _softmax_pv(sc_c, seg_c, None)
            oc = jnp.where(hm_c[0], oc_st[0:CHUNK], 0.0)
            for h in range(1, C_HEADS):
                oc = jnp.where(hm_c[h], oc_st[h * CHUNK:(h + 1) * CHUNK], oc)
            mix_scr[rows, D_AV + 2 * D_BKV:D_MIX] = oc.astype(BF16)

            ob_st = _softmax_pv(sc_b, seg_b, sink_ref[...])
            ob_a = jnp.where(hm_b[0], ob_st[0:CHUNK], ob_st[2 * CHUNK:3 * CHUNK])
            ob_b = jnp.where(hm_b[0], ob_st[CHUNK:2 * CHUNK], ob_st[3 * CHUNK:4 * CHUNK])
            mix_scr[rows, D_AV:D_AV + D_BKV] = ob_a.astype(BF16)
            mix_scr[rows, D_AV + D_BKV:D_AV + 2 * D_BKV] = ob_b.astype(BF16)

            if with_mlp:
                acc_scr[...] += _dot(act_a, wdown_ref[c, 0:half, :]) + _dot(act_b, wdown_ref[c, half:FF_BLOCK, :])

            if not prompt:
                for dst, src, new, width in ((kbo_ref, cbk_ref, kb_new, B_WINDOW), (vbo_ref, cbv_ref, vb_new, B_WINDOW),
                                             (kco_ref, cck_ref, kc_new, C_BAND), (vco_ref, ccv_ref, vc_new, C_BAND)):
                    dst[c, :, 0:width - CHUNK] = src[c, :, CHUNK:width]
                    dst[c, :, width - CHUNK:width] = new.T
            return carry

        @pl.when(step == 0)
        def _():
            lax.fori_loop(0, nch, functools.partial(chunk_body, with_mlp=False), 0)
            late_weights[0].wait()

        @pl.when(step > 0)
        def _():
            @pl.when(step == 1)
            def _():
                late_weights[1].wait()
                late_weights[2].wait()

            lax.fori_loop(0, nch, functools.partial(chunk_body, with_mlp=True), 0, unroll=2)

        if prompt:
            kbw[0:B_WINDOW, :] = kbw[tq:tq + B_WINDOW, :]
            vbw[0:B_WINDOW, :] = vbw[tq:tq + B_WINDOW, :]
            kcw[0:C_BAND, :] = kcw[tq:tq + C_BAND, :]
            vcw[0:C_BAND, :] = vcw[tq:tq + C_BAND, :]

            rb = kco_ref.shape[0]

            @pl.when(step >= nt - C_BAND // rb)
            def _():
                so_ref[...] = st_scr[...]
                kbo_ref[...] = proj_scr[tq - B_WINDOW:tq, OFF_KB:OFF_KB + D_BKV]
                vbo_ref[...] = proj_scr[tq - B_WINDOW:tq, OFF_VB:OFF_VB + D_BKV]
                kco_ref[...] = proj_scr[tq - rb:tq, OFF_KC:OFF_KC + D_C]
                vco_ref[...] = proj_scr[tq - rb:tq, OFF_VC:OFF_VC + D_C]

        mixed = _dot(mix_scr[...], wout_ref[...])

        finish_previous_tile()

        gmlp = gmlp_ref[...]
        for c in range(nch):
            r = slice(c * CHUNK, (c + 1) * CHUNK)
            x1 = x_ref[r, :] + mod(mods_ref, c, 2) * mixed[r, :]
            x1_scr[r, :] = x1
            h2 = _rms(x1, gmlp * (1.0 + mod(mods_ref, c, 4))) + mod(mods_ref, c, 3)
            h2_scr[r, :] = h2.astype(BF16)

    @pl.when(step == nt)
    def _():
        def mlp_block(c, carry):
            up = _dot(h2_scr[...], wup_ref[c])
            act = jnp.square(jnp.maximum(up, 0.0)).astype(BF16)
            acc_scr[...] += _dot(act, wdown_ref[c])
            return carry

        lax.fori_loop(0, nch, mlp_block, 0)
        finish_previous_tile()


def _layer_spec(arr, layer):
    nd = arr.ndim - 1
    return pl.BlockSpec((None,) + arr.shape[1:], lambda i, _l=layer, _nd=nd: (_l,) + (0,) * _nd,
                        pipeline_mode=pl.Buffered(1))


def _const_spec(arr):
    nd = arr.ndim
    return pl.BlockSpec(arr.shape, lambda i, _nd=nd: (0,) * _nd, pipeline_mode=pl.Buffered(1))


def _run_layer(x, mods, layer, stacked, shared, caches, carried, *, prompt, final, tq, nb):
    rows = x.shape[0]
    nch = tq // CHUNK
    nt = rows // tq
    assert rows % tq == 0 and nch == N_FF
    assert nt >= 2
    (gmix, gmlp, anorm, win, wa2, ba2, wout, wup, wdown, biasc, sink) = stacked
    (gfin, biasb, tri, lvl) = shared
    cur = lambda i: jnp.minimum(i, nt - 1)
    prv = lambda i: jnp.maximum(i - 1, 0)

    nmod = 1 if prompt else nb
    in_specs = [pl.BlockSpec((tq, D_MODEL), lambda i: (cur(i), 0)),
                pl.BlockSpec((nmod, SUBLANES, D_MODEL), lambda i: (0 if prompt else cur(i), 0, 0)),
                pl.BlockSpec((nmod, SUBLANES, D_MODEL), lambda i: (0 if prompt else prv(i), 0, 0))]
    ls = functools.partial(_layer_spec, layer=layer)
    hbm = pl.BlockSpec(memory_space=pl.ANY)
    in_specs += [ls(gmix), ls(gmlp), _const_spec(gfin), ls(anorm), ls(win), ls(wa2), ls(ba2), hbm,
                 hbm, hbm, _const_spec(biasb), ls(biasc), ls(sink), _const_spec(tri), _const_spec(lvl)]
    args = [x, mods, mods, gmix, gmlp, gfin, anorm, win, wa2, ba2, wout, wup, wdown, biasb, biasc, sink, tri, lvl]

    y_spec = pl.BlockSpec((tq, D_MODEL), lambda i: (prv(i), 0))
    if prompt:
        assert tq >= B_WINDOW and (tq % C_BAND == 0 or C_BAND % tq == 0)
        rb = min(tq, C_BAND)
        first = nt - C_BAND // rb
        out_shape = [jax.ShapeDtypeStruct((rows, D_MODEL), F32),
                     jax.ShapeDtypeStruct((D_AQK, A_DV), F32),
                     jax.ShapeDtypeStruct((B_WINDOW, D_BKV), F32),
                     jax.ShapeDtypeStruct((B_WINDOW, D_BKV), F32),
                     jax.ShapeDtypeStruct((C_BAND, D_C), F32),
                     jax.ShapeDtypeStruct((C_BAND, D_C), F32)]
        band = pl.BlockSpec((rb, D_C), lambda i: (jnp.maximum(cur(i) - first, 0), 0))
        out_specs = [y_spec,
                     pl.BlockSpec((D_AQK, A_DV), lambda i: (0, 0)),
                     pl.BlockSpec((B_WINDOW, D_BKV), lambda i: (0, 0)),
                     pl.BlockSpec((B_WINDOW, D_BKV), lambda i: (0, 0)),
                     band, band]
        kbw_rows, kcw_rows = B_WINDOW + tq, C_BAND + tq
        aliases = {}
    else:
        assert nb == nch
        nseq = rows // CHUNK
        st, cbk, cbv, cck, ccv = caches
        assert len(carried) == N_SEQ_OUT
        seq = lambda r, w: pl.BlockSpec((None, nb, r, w), lambda i, _l=layer: (_l, cur(i), 0, 0))
        seq_dims = [(D_AQK, A_DV), (D_BKV, B_WINDOW), (D_BKV, B_WINDOW), (D_C, C_BAND), (D_C, C_BAND)]
        in_specs += [seq(r, w) for r, w in seq_dims]
        aliases = {len(in_specs) + j: 1 + j for j in range(N_SEQ_OUT)}
        in_specs += [pl.BlockSpec(memory_space=pl.ANY)] * N_SEQ_OUT
        args += [st, cbk, cbv, cck, ccv, *carried]
        out_shape = [jax.ShapeDtypeStruct((rows, D_MODEL), F32)] + [
            jax.ShapeDtypeStruct((DEPTH, nseq, r, w), F32) for r, w in seq_dims]
        out_specs = [y_spec] + [seq(r, w) for r, w in seq_dims]
        kbw_rows, kcw_rows = SUBLANES * 2, SUBLANES * 2

    scratch = [pltpu.VMEM((tq, D_MODEL), BF16),
               pltpu.VMEM((tq, D_MODEL), BF16),
               pltpu.VMEM((tq, OFF_RA), F32),
               pltpu.VMEM((tq, D_AQK), F32),
               pltpu.VMEM((tq, D_AQK), F32),
               pltpu.VMEM((tq, D_MIX), BF16),
               pltpu.VMEM((tq, D_MODEL), F32),
               pltpu.VMEM((tq, D_MODEL), F32),
               pltpu.VMEM((D_AQK, A_DV), F32),
               pltpu.VMEM((kbw_rows, D_BKV), BF16), pltpu.VMEM((kbw_rows, D_BKV), BF16),
               pltpu.VMEM((kcw_rows, D_C), BF16), pltpu.VMEM((kcw_rows, D_C), BF16),
               pltpu.VMEM(wout.shape[1:], BF16), pltpu.VMEM(wup.shape[1:], BF16), pltpu.VMEM(wdown.shape[1:], BF16),
               pltpu.SemaphoreType.DMA((3,))]

    kern = functools.partial(_layer_kernel, prompt=prompt, final=final, nch=nch, tq=tq, nt=nt, layer=layer)
    return pl.pallas_call(
        kern,
        grid=(nt + 1,),
        in_specs=in_specs,
        out_specs=out_specs,
        out_shape=out_shape,
        scratch_shapes=scratch,
        input_output_aliases=aliases,
        compiler_params=pltpu.CompilerParams(
            dimension_semantics=("arbitrary",),
            vmem_limit_bytes=VMEM_LIMIT_BYTES if prompt else VMEM_LIMIT_SAMPLE_BYTES),
        name=("layer_prompt" if prompt else "layer_sample") + ("_final" if final else ""),
    )(*args)


ADA_BLOCK = 1536


def _ada_kernel(c_ref, w_ref, b_ref, o_ref):
    c = c_ref[...]
    sc = c * jax.nn.sigmoid(c)
    sc_hi = sc.astype(BF16)
    sc_lo = (sc - sc_hi.astype(F32)).astype(BF16)
    w = w_ref[0]
    w_hi = w.astype(BF16)
    w_lo = (w - w_hi.astype(F32)).astype(BF16)
    o_ref[0] = _dot(sc_hi, w_hi) + _dot(sc_lo, w_hi) + _dot(sc_hi, w_lo) + b_ref[0]


def _ada(c_all, w_ada, b_ada):
    rows = c_all.shape[0]
    n = 6 * D_MODEL
    return pl.pallas_call(
        _ada_kernel,
        grid=(DEPTH, n // ADA_BLOCK),
        in_specs=[pl.BlockSpec((rows, D_MODEL), lambda l, j: (0, 0)),
                  pl.BlockSpec((1, D_MODEL, ADA_BLOCK), lambda l, j: (l, 0, j)),
                  pl.BlockSpec((1, 1, ADA_BLOCK), lambda l, j: (l, 0, j))],
        out_specs=pl.BlockSpec((1, rows, ADA_BLOCK), lambda l, j: (l, 0, j)),
        out_shape=jax.ShapeDtypeStruct((DEPTH, rows, n), F32),
        compiler_params=pltpu.CompilerParams(dimension_semantics=("arbitrary", "arbitrary"),
                                             vmem_limit_bytes=VMEM_LIMIT_BYTES),
        name="adaln",
    )(c_all, w_ada, b_ada.reshape(DEPTH, 1, n))


TB_W = 2 * LANES
TC_W = 5 * LANES
TC_FLAT = TC_W - (CHUNK + C_CLIP + 1)


def _shear(f, width):
    tiled = jnp.broadcast_to(f, (CHUNK, width))
    return pltpu.roll(tiled, width - (CHUNK - 1), 1, stride=1, stride_axis=0)


def _bias_kernel(t5_ref, crel_ref, bucket_ref, ob_ref, oc_ref):
    bucket = bucket_ref[...]
    for h in range(B_HEADS):
        def b_body(i, acc, h=h):
            return jnp.where(bucket == i, t5_ref[h, i], acc)
        f = lax.fori_loop(0, T5_BUCKETS, b_body, jnp.zeros(bucket.shape, F32))
        ob_ref[h] = _shear(f, TB_W)[:, 0:LK_B]
    lane = lax.broadcasted_iota(jnp.int32, (1, TC_W), 1)
    for l in range(DEPTH):
        for h in range(C_HEADS):
            row = crel_ref[l, h:h + 1, :]
            f = jnp.where(lane < TC_FLAT, row[:, 0:1], pltpu.roll(row, TC_FLAT, 1))
            oc_ref[l, h] = _shear(f, TC_W)[:, 0:LK_C]


def _bias_tables(t5_bias, c_rel_bias, bucket):
    smem = pl.BlockSpec(memory_space=pltpu.SMEM)
    vmem = pl.BlockSpec(memory_space=pltpu.VMEM)
    crel = jnp.pad(jnp.swapaxes(c_rel_bias, 1, 2),
                   ((0, 0), (0, SUBLANES - C_HEADS), (0, TC_W - (2 * C_CLIP + 1))))
    return pl.pallas_call(
        _bias_kernel,
        in_specs=[smem, vmem, vmem],
        out_specs=[vmem, vmem],
        out_shape=[jax.ShapeDtypeStruct((B_HEADS, CHUNK, LK_B), F32),
                   jax.ShapeDtypeStruct((DEPTH, C_HEADS, CHUNK, LK_C), F32)],
        name="bias_tables",
    )(t5_bias.T, crel, bucket)


def _t5_bucket(rel):
    half = T5_BUCKETS // 2
    max_exact = half // 2
    steps = half - max_exact
    thresholds = [int(np.ceil(max_exact * (T5_MAX_DIST / max_exact) ** (k / steps) - 1e-9)) for k in range(1, steps)]
    n = jnp.abs(rel)
    large = jnp.minimum(max_exact + sum((n >= t).astype(jnp.int32) for t in thresholds), half - 1)
    return jnp.where(rel > 0, half, 0) + jnp.where(n < max_exact, n, large)


def _gla_constants(tq):
    r = np.arange(tq)
    tri = ((r[:, None] // CHUNK == r[None, :] // CHUNK) & (r[None, :] <= r[:, None])).astype(np.float32)
    t = np.arange(CHUNK)[:, None]
    s = np.arange(CHUNK)[None, :]
    lvl = np.full((CHUNK, CHUNK), -1, np.int32)
    lvl[t == s] = 0
    for li, m in enumerate(GLA_LEVELS):
        sel = ((t // m) % 2 == 1) & ((s // m) == (t // m) - 1)
        lvl[np.broadcast_to(sel, lvl.shape)] = li + 1
    return jnp.asarray(tri, BF16), jnp.asarray(np.tile(lvl, (A_HEADS, 1)))


SRC_RA = 2 * D_AQK + 2 * D_AV
SRC_QB = SRC_RA + A_GATE_RANK
SRC_KB = SRC_QB + B_HEADS * HEAD_DIM
D_IN = SRC_KB + 2 * D_BKV + 3 * D_C
QB_HEAD_ORDER = (0, 2, 1, 3)


def _w_in_segments():
    segs = [(OFF_QA, 0, SRC_RA)]
    segs += [(OFF_QB + i * HEAD_DIM, SRC_QB + h * HEAD_DIM, HEAD_DIM) for i, h in enumerate(QB_HEAD_ORDER)]
    segs += [(OFF_KB, SRC_KB, D_IN - SRC_KB), (OFF_RA, SRC_RA, A_GATE_RANK)]
    return segs


def _w_in_kernel(w_ref, o_ref):
    segs = _w_in_segments()
    for blk in range(P_IN // LANES):
        lo, hi = blk * LANES, (blk + 1) * LANES
        rows, col = [], lo
        while col < hi:
            hit = [(d, s, w) for d, s, w in segs if d <= col < d + w]
            if hit:
                d, s, w = hit[0]
                n = min(hi, d + w) - col
                rows.append(w_ref[s + col - d:s + col - d + n, :])
            else:
                n = hi - col
                rows.append(jnp.zeros((n, D_MODEL), F32))
            col += n
        o_ref[:, lo:hi] = jnp.concatenate(rows, axis=0).T.astype(BF16)


def _prep_w_in(w_in):
    w_fm = jnp.swapaxes(w_in, 1, 2)
    return pl.pallas_call(
        _w_in_kernel,
        grid=(DEPTH,),
        in_specs=[pl.BlockSpec((None, D_IN, D_MODEL), lambda l: (l, 0, 0))],
        out_specs=pl.BlockSpec((None, D_MODEL, P_IN), lambda l: (l, 0, 0)),
        out_shape=jax.ShapeDtypeStruct((DEPTH, D_MODEL, P_IN), BF16),
        compiler_params=pltpu.CompilerParams(dimension_semantics=("arbitrary",),
                                             vmem_limit_bytes=VMEM_LIMIT_BYTES),
        name="prep_w_in",
    )(w_fm)


def _cast_kernel(w_ref, o_ref):
    o_ref[...] = w_ref[...].astype(BF16)


def _prep_w_up(w_up):
    return pl.pallas_call(
        _cast_kernel,
        grid=(DEPTH, N_FF),
        in_specs=[pl.BlockSpec((None, D_MODEL, FF_BLOCK), lambda l, j: (l, 0, j))],
        out_specs=pl.BlockSpec((None, None, D_MODEL, FF_BLOCK), lambda l, j: (l, j, 0, 0)),
        out_shape=jax.ShapeDtypeStruct((DEPTH, N_FF, D_MODEL, FF_BLOCK), BF16),
        compiler_params=pltpu.CompilerParams(dimension_semantics=("arbitrary", "arbitrary"),
                                             vmem_limit_bytes=VMEM_LIMIT_BYTES),
        name="prep_w_up",
    )(w_up)


TQ_PROMPT = 256
NB_SAMPLE = 4


def kernel(x_prompt, x_sample, c_prompt, c_sample, state_gla, cache_b_k, cache_b_v, cache_c_k, cache_c_v,
           w_ada, b_ada, norm_mix_g, norm_mlp_g, w_in, w_a2, b_a2, a_norm_g, b_sink, t5_bias, c_rel_bias,
           w_out, w_up, w_down, final_norm_g):
    bsz, seq, _ = x_prompt.shape
    dec_b, dec_s, _ = x_sample.shape
    assert bsz == 1 and dec_s == CHUNK and TQ_PROMPT == NB_SAMPLE * CHUNK

    n_c = bsz + dec_b
    c_rows = -(-n_c // SUBLANES) * SUBLANES
    c_all = jnp.concatenate([c_prompt, c_sample, jnp.zeros((c_rows - n_c, D_MODEL), F32)], axis=0)
    mods = _ada(c_all, w_ada, b_ada).reshape(DEPTH, c_rows, 6, D_MODEL)
    mods = jnp.pad(mods, ((0, 0), (0, 0), (0, SUBLANES - 6), (0, 0)))

    rel_b = jnp.arange(TB_W) - (CHUNK - 1) - B_WINDOW
    bias_b, bias_c = _bias_tables(t5_bias, c_rel_bias, _t5_bucket(rel_b).astype(jnp.int32)[None, :])
    bias_b = bias_b.reshape(STACK, LK_B)
    bias_c = bias_c.reshape(DEPTH, STACK, LK_C)

    tri, lvl = _gla_constants(TQ_PROMPT)
    ob0 = D_AV
    orow = lambda h: slice(ob0 + h * HEAD_DIM, ob0 + (h + 1) * HEAD_DIM)

    win = _prep_w_in(w_in)
    wa2 = jnp.concatenate([w_a2, jnp.zeros((DEPTH, LANES - A_GATE_RANK, D_AQK), F32)], axis=1).astype(BF16)
    wout = jnp.concatenate([w_out[:, 0:D_AV], w_out[:, orow(0)], w_out[:, orow(2)], w_out[:, orow(1)],
                            w_out[:, orow(3)], w_out[:, D_AV + B_HEADS * HEAD_DIM:]], axis=1).astype(BF16)
    wup = _prep_w_up(w_up)
    wdown = w_down.astype(BF16).reshape(DEPTH, N_FF, FF_BLOCK, D_MODEL)
    sink = jnp.repeat(b_sink, CHUNK, axis=1)[:, :, None]
    stacked = (norm_mix_g[:, None, :], norm_mlp_g[:, None, :], a_norm_g[:, None, :], win, wa2,
               b_a2[:, None, :], wout, wup, wdown, bias_c, sink)
    shared = (final_norm_g[None, :], bias_b, tri, lvl)

    x_p = x_prompt.reshape(seq, D_MODEL)
    x_s = x_sample.reshape(dec_b * dec_s, D_MODEL)
    outs_p = []
    fmajor = lambda cache: jnp.transpose(cache, (0, 1, 3, 4, 2)).reshape(
        DEPTH, dec_b, cache.shape[3] * cache.shape[4], cache.shape[2])
    caches = (state_gla.reshape(DEPTH, dec_b, D_AQK, A_DV), fmajor(cache_b_k), fmajor(cache_b_v),
              fmajor(cache_c_k), fmajor(cache_c_v))
    outs_s = tuple(jnp.zeros(c.shape, F32) for c in caches)
    for l in range(DEPTH):
        final = l == DEPTH - 1
        res_p = _run_layer(x_p, mods[l, 0:bsz], l, stacked, shared, None, None, prompt=True, final=final,
                           tq=TQ_PROMPT, nb=None)
        x_p = res_p[0]
        outs_p.append(res_p[1:])

        res_s = _run_layer(x_s, mods[l, bsz:bsz + dec_b], l, stacked, shared, caches, outs_s, prompt=False,
                           final=final, tq=NB_SAMPLE * CHUNK, nb=NB_SAMPLE)
        x_s = res_s[0]
        outs_s = tuple(res_s[1:])

    sg_p = jnp.stack([o[0].reshape(bsz, A_HEADS, A_DK, A_DV) for o in outs_p])
    kb_p = jnp.stack([o[1].reshape(bsz, B_WINDOW, B_KV_HEADS, HEAD_DIM) for o in outs_p])
    vb_p = jnp.stack([o[2].reshape(bsz, B_WINDOW, B_KV_HEADS, HEAD_DIM) for o in outs_p])
    kc_p = jnp.stack([o[3].reshape(bsz, C_BAND, C_HEADS, HEAD_DIM) for o in outs_p])
    vc_p = jnp.stack([o[4].reshape(bsz, C_BAND, C_HEADS, HEAD_DIM) for o in outs_p])
    sg_s = outs_s[0].reshape(DEPTH, dec_b, A_HEADS, A_DK, A_DV)
    tmajor = lambda arr, heads: jnp.transpose(arr.reshape(DEPTH, dec_b, heads, HEAD_DIM, -1), (0, 1, 4, 2, 3))
    kb_s = tmajor(outs_s[1], B_KV_HEADS)
    vb_s = tmajor(outs_s[2], B_KV_HEADS)
    kc_s = tmajor(outs_s[3], C_HEADS)
    vc_s = tmajor(outs_s[4], C_HEADS)
    return (x_p.reshape(bsz, seq, D_MODEL), x_s.reshape(dec_b, dec_s, D_MODEL),
            sg_p, kb_p, vb_p, kc_p, vc_p, sg_s, kb_s, vb_s, kc_s, vc_s)
```

```python
import functools

import numpy as np
import jax
import jax.numpy as jnp
from jax import lax
from jax.experimental import pallas as pl
from jax.experimental.pallas import tpu as pltpu

D_MODEL = 1024
DEPTH = 2
CHUNK = 64
HEAD_DIM = 64
A_HEADS = 4
A_DK = 64
A_DV = 128
A_GATE_RANK = 16
A_GATE_TAU = 16.0
B_HEADS = 4
B_KV_HEADS = 2
B_WINDOW = 128
C_HEADS = 4
C_BAND = 512
C_CLIP = 256
T5_BUCKETS = 32
T5_MAX_DIST = 128
D_FF = 4 * D_MODEL
NORM_EPS = 1e-6
NEG_INF = -1e30

LANES = 128
SUBLANES = 8
VMEM_LIMIT_BYTES = 60000 * 1024
VMEM_LIMIT_SAMPLE_BYTES = 62 * 1024 * 1024

OFF_QA = 0
OFF_KA = OFF_QA + A_HEADS * A_DK
OFF_VA = OFF_KA + A_HEADS * A_DK
OFF_GA = OFF_VA + A_HEADS * A_DV
OFF_QB = OFF_GA + A_HEADS * A_DV
OFF_KB = OFF_QB + B_HEADS * HEAD_DIM
OFF_VB = OFF_KB + B_KV_HEADS * HEAD_DIM
OFF_QC = OFF_VB + B_KV_HEADS * HEAD_DIM
OFF_KC = OFF_QC + C_HEADS * HEAD_DIM
OFF_VC = OFF_KC + C_HEADS * HEAD_DIM
OFF_RA = OFF_VC + C_HEADS * HEAD_DIM
P_IN = OFF_RA + LANES
D_AQK = A_HEADS * A_DK
D_AV = A_HEADS * A_DV
D_BKV = B_KV_HEADS * HEAD_DIM
D_C = C_HEADS * HEAD_DIM
D_MIX = D_AV + B_HEADS * HEAD_DIM + D_C
LK_B = B_WINDOW + CHUNK
LK_C = C_BAND + CHUNK
STACK = 4 * CHUNK
GLA_LEVELS = (1, 2, 4, 8, 16, 32)
FF_BLOCK = 1024
N_FF = D_FF // FF_BLOCK
N_SEQ_OUT = 5

BF16 = jnp.bfloat16
F32 = jnp.float32


def _dot(a, b):
    return jnp.dot(a, b, preferred_element_type=F32)


def _dot_nt(a, b):
    return lax.dot_general(a, b, (((1,), (1,)), ((), ())), preferred_element_type=F32)


def _dot_tn(a, b):
    return lax.dot_general(a, b, (((0,), (0,)), ((), ())), preferred_element_type=F32)


def _rms(x, g):
    return x * lax.rsqrt(jnp.mean(x * x, axis=-1, keepdims=True) + NORM_EPS) * g


def _split3(x):
    hi = x.astype(BF16)
    r1 = x - hi.astype(F32)
    mid = r1.astype(BF16)
    lo = (r1 - mid.astype(F32)).astype(BF16)
    return hi, mid, lo


def _head_masks(width, per_head):
    lane = lax.broadcasted_iota(jnp.int32, (1, width), 1)
    return [(lane >= h * per_head) & (lane < (h + 1) * per_head) for h in range(width // per_head)]


def _stack_masked(x, masks):
    return jnp.concatenate([jnp.where(m, x, 0.0) for m in masks], axis=0)


def _tile_rows(x, n):
    return jnp.concatenate([x] * n, axis=0)


def _scores(q_st, segments, bias_st, maskrow):
    scores, off = [], 0
    for k, v, transposed in segments:
        n = k.shape[1] if transposed else k.shape[0]
        s = (_dot(q_st, k) if transposed else _dot_nt(q_st, k)) + bias_st[:, off:off + n]
        if maskrow is not None:
            s = s + maskrow[:, off:off + n]
        scores.append(s)
        off += n
    return scores


def _softmax_pv(scores, segments, sink):
    m = functools.reduce(jnp.maximum, [jnp.max(s, axis=1, keepdims=True) for s in scores])
    if sink is not None:
        m = jnp.maximum(m, sink)
    es = [jnp.exp(s - m) for s in scores]
    l = functools.reduce(jnp.add, [jnp.sum(e, axis=1, keepdims=True) for e in es])
    if sink is not None:
        l = l + jnp.exp(sink - m)
    o = None
    for e, (k, v, transposed) in zip(es, segments):
        part = _dot_nt(e.astype(BF16), v) if transposed else _dot(e.astype(BF16), v)
        o = part if o is None else o + part
    return o / l


def _level_exponent(b, m, t):
    bcast = lambda i: jnp.broadcast_to(b[i:i + 1], (SUBLANES, b.shape[1]))
    low_half = lax.broadcasted_iota(jnp.int32, (SUBLANES, b.shape[1]), 0) < SUBLANES // 2
    parts = []
    for g in range(CHUNK // SUBLANES):
        r = g * SUBLANES
        rows = b[r:r + SUBLANES]
        if m >= SUBLANES:
            ref = b[(r // (2 * m)) * 2 * m + m - 1:(r // (2 * m)) * 2 * m + m]
            parts.append(rows - ref if (r // m) % 2 == 1 else ref - rows)
        elif m == 4:
            parts.append(rows - bcast(r + 3))
        else:
            parts.append(rows - jnp.where(low_half, bcast(r + 1), bcast(r + 5)))
    d = jnp.concatenate(parts, axis=0)
    return d if m >= SUBLANES else jnp.where((t & m) != 0, d, -d)


def _odd_block_rows(x, m, groups):
    return jnp.concatenate([x[g * CHUNK + r:g * CHUNK + r + SUBLANES] for g in range(groups)
                            for r in range(0, CHUNK, SUBLANES) if (r // m) % 2 == 1], axis=0)


def _spread_odd_block_rows(y, m, groups):
    zero = jnp.zeros((SUBLANES, y.shape[1]), y.dtype)
    parts, i = [], 0
    for g in range(groups):
        for r in range(0, CHUNK, SUBLANES):
            if (r // m) % 2 == 1:
                parts.append(y[i:i + SUBLANES])
                i += SUBLANES
            else:
                parts.append(zero)
    return jnp.concatenate(parts, axis=0)


def _gla_levels(q, k, la, b, hm_a):
    t = lax.broadcasted_iota(jnp.int32, (CHUNK, D_AQK), 0)
    q_st = _stack_masked(q, hm_a)
    k16 = k.astype(BF16)
    prods = [_dot_nt(q_st.astype(BF16), k16)]
    for m in GLA_LEVELS:
        e = jnp.exp(la if m == 1 else _level_exponent(b, m, t))
        kt = k16 if m == 1 else (k * e).astype(BF16)
        if m >= SUBLANES:
            qt = _odd_block_rows(q_st, m, A_HEADS) * _tile_rows(_odd_block_rows(e, m, 1), A_HEADS)
            prods.append(_spread_odd_block_rows(_dot_nt(qt.astype(BF16), kt), m, A_HEADS))
        else:
            qt = (q_st * _tile_rows(e, A_HEADS)).astype(BF16)
            prods.append(_dot_nt(qt, kt))
    return q_st, prods


def _gla_finish(q_st, prods, k, v, b, s, lvl):
    attn = jnp.where(lvl == 0, prods[0], 0.0)
    for li in range(1, len(prods)):
        attn = jnp.where(lvl == li, prods[li], attn)
    qhat = (q_st * _tile_rows(jnp.exp(b), A_HEADS)).astype(BF16)
    o_inter = _dot(qhat, s.astype(BF16))
    v16 = v.astype(BF16)
    o_intra = jnp.concatenate(
        [_dot(attn[h * CHUNK:(h + 1) * CHUNK].astype(BF16), v16[:, h * A_DV:(h + 1) * A_DV])
         for h in range(A_HEADS)], axis=0)
    b_last = b[CHUNK - 1:CHUNK]
    khat = (k * jnp.exp(b_last - b)).astype(BF16)
    upd = _dot_tn(khat, v16)
    decay = jnp.broadcast_to(jnp.exp(b_last), (A_DV, D_AQK)).T
    s_new = s * decay + jnp.concatenate(
        [upd[h * A_DK:(h + 1) * A_DK, h * A_DV:(h + 1) * A_DV] for h in range(A_HEADS)], axis=0)
    return o_intra + o_inter, s_new


def _layer_kernel(*refs, prompt, final, nch, tq, nt, layer):
    (x_ref, mods_ref, modp_ref, gmix_ref, gmlp_ref, gfin_ref, anorm_ref, win_ref, wa2_ref, ba2_ref, wout_hbm,
     wup_hbm, wdown_hbm, biasb_ref, biasc_ref, sink_ref, tri_ref, lvl_ref) = refs[:18]
    pos = 18
    if not prompt:
        st_ref, cbk_ref, cbv_ref, cck_ref, ccv_ref = refs[pos:pos + 5]
        pos += 5 + N_SEQ_OUT
    y_ref, so_ref, kbo_ref, vbo_ref, kco_ref, vco_ref = refs[pos:pos + 6]
    pos += 6
    (h_scr, h2_scr, proj_scr, la_scr, b_scr, mix_scr, x1_scr, acc_scr, st_scr,
     kbw, vbw, kcw, vcw, wout_ref, wup_ref, wdown_ref, wsem) = refs[pos:]

    step = pl.program_id(0)

    late_weights = [pltpu.make_async_copy(src.at[layer], dst, wsem.at[j]) for j, (src, dst) in
                    enumerate(((wout_hbm, wout_ref), (wup_hbm, wup_ref), (wdown_hbm, wdown_ref)))]

    def mod(ref, c, j):
        return ref[0 if prompt else c, j:j + 1, :]

    @pl.when(step == 0)
    def _():
        late_weights[0].start()
        late_weights[1].start(priority=1)
        late_weights[2].start(priority=1)
        x1_scr[...] = jnp.zeros_like(x1_scr)
        acc_scr[...] = jnp.zeros_like(acc_scr)
        if prompt:
            st_scr[...] = jnp.zeros_like(st_scr)
            kbw[...] = jnp.zeros_like(kbw)
            vbw[...] = jnp.zeros_like(vbw)
            kcw[...] = jnp.zeros_like(kcw)
            vcw[...] = jnp.zeros_like(vcw)

    def finish_previous_tile():
        gfin = gfin_ref[...]
        for c in range(nch):
            r = slice(c * CHUNK, (c + 1) * CHUNK)
            x2 = x1_scr[r, :] + mod(modp_ref, c, 5) * acc_scr[r, :]
            y_ref[r, :] = _rms(x2, gfin) if final else x2
        acc_scr[...] = jnp.zeros_like(acc_scr)

    @pl.when(step < nt)
    def _():
        gmix = gmix_ref[...]
        for c in range(nch):
            r = slice(c * CHUNK, (c + 1) * CHUNK)
            h = _rms(x_ref[r, :], gmix * (1.0 + mod(mods_ref, c, 1))) + mod(mods_ref, c, 0)
            h_scr[r, :] = h.astype(BF16)
        hmix = h_scr[...]
        cut1, cut2 = OFF_GA, OFF_QC
        ra = _dot(hmix, win_ref[:, OFF_RA:OFF_RA + LANES])
        proj_scr[:, 0:cut1] = _dot(hmix, win_ref[:, 0:cut1])
        z = _dot(ra.astype(BF16), wa2_ref[...]) + ba2_ref[...]
        la = (jnp.minimum(z, 0.0) - jnp.log1p(jnp.exp(-jnp.abs(z)))) * (1.0 / A_GATE_TAU)
        la_scr[...] = la
        proj_scr[:, cut1:cut2] = _dot(hmix, win_ref[:, cut1:cut2])
        tri = tri_ref[...]
        hi, mid, lo = _split3(la)
        b_scr[...] = _dot(tri, hi) + _dot(tri, mid) + _dot(tri, lo)
        proj_scr[:, cut2:OFF_RA] = _dot(hmix, win_ref[:, cut2:OFF_RA])

        if prompt:
            kbw[B_WINDOW:B_WINDOW + tq, :] = proj_scr[:, OFF_KB:OFF_KB + D_BKV].astype(BF16)
            vbw[B_WINDOW:B_WINDOW + tq, :] = proj_scr[:, OFF_VB:OFF_VB + D_BKV].astype(BF16)
            kcw[C_BAND:C_BAND + tq, :] = proj_scr[:, OFF_KC:OFF_KC + D_C].astype(BF16)
            vcw[C_BAND:C_BAND + tq, :] = proj_scr[:, OFF_VC:OFF_VC + D_C].astype(BF16)

        hm_a = _head_masks(D_AQK, A_DK)
        hm_c = _head_masks(D_C, HEAD_DIM)
        hm_b = _head_masks(D_BKV, HEAD_DIM)
        anorm = anorm_ref[...]

        def chunk_body(c, carry, with_mlp):
            r0 = pl.multiple_of(c * CHUNK, CHUNK)
            rows = pl.ds(r0, CHUNK)

            if prompt:
                seg_b = [(kbw[pl.ds(r0, LK_B), :], vbw[pl.ds(r0, LK_B), :], False)]
                seg_c = [(kcw[pl.ds(r0, LK_C), :], vcw[pl.ds(r0, LK_C), :], False)]
                gidx = step * nch + c
                lane_b = lax.broadcasted_iota(jnp.int32, (1, LK_B), 1)
                lane_c = lax.broadcasted_iota(jnp.int32, (1, LK_C), 1)
                mask_b = jnp.where(lane_b >= B_WINDOW - gidx * CHUNK, 0.0, NEG_INF)
                mask_c = jnp.where(lane_c >= C_BAND - gidx * CHUNK, 0.0, NEG_INF)
            else:
                kb_new = proj_scr[rows, OFF_KB:OFF_KB + D_BKV]
                vb_new = proj_scr[rows, OFF_VB:OFF_VB + D_BKV]
                kc_new = proj_scr[rows, OFF_KC:OFF_KC + D_C]
                vc_new = proj_scr[rows, OFF_VC:OFF_VC + D_C]
                seg_b = [(cbk_ref[c].astype(BF16), cbv_ref[c].astype(BF16), True),
                         (kb_new.astype(BF16), vb_new.astype(BF16), False)]
                seg_c = [(cck_ref[c].astype(BF16), ccv_ref[c].astype(BF16), True),
                         (kc_new.astype(BF16), vc_new.astype(BF16), False)]
                mask_b = mask_c = None

            q = proj_scr[rows, OFF_QA:OFF_QA + D_AQK] * (A_DK ** -0.5)
            k = proj_scr[rows, OFF_KA:OFF_KA + D_AQK]
            b = b_scr[rows, :]
            q_st, prods = _gla_levels(q, k, la_scr[rows, :], b, hm_a)
            qc = proj_scr[rows, OFF_QC:OFF_QC + D_C] * (HEAD_DIM ** -0.5)
            sc_c = _scores(_stack_masked(qc, hm_c).astype(BF16), seg_c, biasc_ref[...], mask_c)
            qb = proj_scr[rows, OFF_QB:OFF_QB + B_HEADS * HEAD_DIM] * (HEAD_DIM ** -0.5)
            qb_a, qb_b = qb[:, 0:D_BKV], qb[:, D_BKV:2 * D_BKV]
            qb_st = jnp.concatenate([jnp.where(hm_b[0], qb_a, 0.0), jnp.where(hm_b[0], qb_b, 0.0),
                                     jnp.where(hm_b[1], qb_a, 0.0), jnp.where(hm_b[1], qb_b, 0.0)], axis=0)
            sc_b = _scores(qb_st.astype(BF16), seg_b, biasb_ref[...], mask_b)

            half = FF_BLOCK // 2
            if with_mlp:
                up_a = _dot(h2_scr[...], wup_ref[c, :, 0:half])
                act_a = jnp.square(jnp.maximum(up_a, 0.0)).astype(BF16)

            v = proj_scr[rows, OFF_VA:OFF_VA + D_AV]
            s_old = st_scr[...] if prompt else st_ref[c]
            o_st, s_new = _gla_finish(q_st, prods, k, v, b, s_old, lvl_ref[...])
            if prompt:
                st_scr[...] = s_new
            else:
                so_ref[c] = s_new
            g = proj_scr[rows, OFF_GA:OFF_GA + D_AV]
            for h in range(A_HEADS):
                on = _rms(o_st[h * CHUNK:(h + 1) * CHUNK], anorm)
                gh = g[:, h * A_DV:(h + 1) * A_DV]
                mix_scr[rows, h * A_DV:(h + 1) * A_DV] = (on * (gh * jax.nn.sigmoid(gh))).astype(BF16)

            if with_mlp:
                up_b = _dot(h2_scr[...], wup_ref[c, :, half:FF_BLOCK])
                act_b = jnp.square(jnp.maximum(up_b, 0.0)).astype(BF16)

            oc_st = _softmax_pv(sc_c, seg_c, None)
            oc = jnp.where(hm_c[0], oc_st[0:CHUNK], 0.0)
            for h in range(1, C_HEADS):
                oc = jnp.where(hm_c[h], oc_st[h * CHUNK:(h + 1) * CHUNK], oc)
            mix_scr[rows, D_AV + 2 * D_BKV:D_MIX] = oc.astype(BF16)

            ob_st = _softmax_pv(sc_b, seg_b, sink_ref[...])
            ob_a = jnp.where(hm_b[0], ob_st[0:CHUNK], ob_st[2 * CHUNK:3 * CHUNK])
            ob_b = jnp.where(hm_b[0], ob_st[CHUNK:2 * CHUNK], ob_st[3 * CHUNK:4 * CHUNK])
            mix_scr[rows, D_AV:D_AV + D_BKV] = ob_a.astype(BF16)
            mix_scr[rows, D_AV + D_BKV:D_AV + 2 * D_BKV] = ob_b.astype(BF16)

            if with_mlp:
                acc_scr[...] += _dot(act_a, wdown_ref[c, 0:half, :]) + _dot(act_b, wdown_ref[c, half:FF_BLOCK, :])

            if not prompt:
                for dst, src, new, width in ((kbo_ref, cbk_ref, kb_new, B_WINDOW), (vbo_ref, cbv_ref, vb_new, B_WINDOW),
                                             (kco_ref, cck_ref, kc_new, C_BAND), (vco_ref, ccv_ref, vc_new, C_BAND)):
                    dst[c, :, 0:width - CHUNK] = src[c, :, CHUNK:width]
                    dst[c, :, width - CHUNK:width] = new.T
            return carry

        @pl.when(step == 0)
        def _():
            lax.fori_loop(0, nch, functools.partial(chunk_body, with_mlp=False), 0)
            for cp in late_weights:
                cp.wait()

        @pl.when(step > 0)
        def _():
            lax.fori_loop(0, nch, functools.partial(chunk_body, with_mlp=True), 0, unroll=2)

        if prompt:
            kbw[0:B_WINDOW, :] = kbw[tq:tq + B_WINDOW, :]
            vbw[0:B_WINDOW, :] = vbw[tq:tq + B_WINDOW, :]
            kcw[0:C_BAND, :] = kcw[tq:tq + C_BAND, :]
            vcw[0:C_BAND, :] = vcw[tq:tq + C_BAND, :]

            so_ref[...] = st_scr[...]
            kbo_ref[...] = proj_scr[tq - B_WINDOW:tq, OFF_KB:OFF_KB + D_BKV]
            vbo_ref[...] = proj_scr[tq - B_WINDOW:tq, OFF_VB:OFF_VB + D_BKV]
            rb = kco_ref.shape[0]
            kco_ref[...] = proj_scr[tq - rb:tq, OFF_KC:OFF_KC + D_C]
            vco_ref[...] = proj_scr[tq - rb:tq, OFF_VC:OFF_VC + D_C]

        mixed = _dot(mix_scr[...], wout_ref[...])

        finish_previous_tile()

        gmlp = gmlp_ref[...]
        for c in range(nch):
            r = slice(c * CHUNK, (c + 1) * CHUNK)
            x1 = x_ref[r, :] + mod(mods_ref, c, 2) * mixed[r, :]
            x1_scr[r, :] = x1
            h2 = _rms(x1, gmlp * (1.0 + mod(mods_ref, c, 4))) + mod(mods_ref, c, 3)
            h2_scr[r, :] = h2.astype(BF16)

    @pl.when(step == nt)
    def _():
        def mlp_block(c, carry):
            up = _dot(h2_scr[...], wup_ref[c])
            act = jnp.square(jnp.maximum(up, 0.0)).astype(BF16)
            acc_scr[...] += _dot(act, wdown_ref[c])
            return carry

        lax.fori_loop(0, nch, mlp_block, 0)
        finish_previous_tile()


def _layer_spec(arr, layer):
    nd = arr.ndim - 1
    return pl.BlockSpec((None,) + arr.shape[1:], lambda i, _l=layer, _nd=nd: (_l,) + (0,) * _nd,
                        pipeline_mode=pl.Buffered(1))


def _const_spec(arr):
    nd = arr.ndim
    return pl.BlockSpec(arr.shape, lambda i, _nd=nd: (0,) * _nd, pipeline_mode=pl.Buffered(1))


def _run_layer(x, mods, layer, stacked, shared, caches, carried, *, prompt, final, tq, nb):
    rows = x.shape[0]
    nch = tq // CHUNK
    nt = rows // tq
    assert rows % tq == 0 and nch == N_FF
    (gmix, gmlp, anorm, win, wa2, ba2, wout, wup, wdown, biasc, sink) = stacked
    (gfin, biasb, tri, lvl) = shared
    cur = lambda i: jnp.minimum(i, nt - 1)
    prv = lambda i: jnp.maximum(i - 1, 0)

    nmod = 1 if prompt else nb
    in_specs = [pl.BlockSpec((tq, D_MODEL), lambda i: (cur(i), 0)),
                pl.BlockSpec((nmod, SUBLANES, D_MODEL), lambda i: (0 if prompt else cur(i), 0, 0)),
                pl.BlockSpec((nmod, SUBLANES, D_MODEL), lambda i: (0 if prompt else prv(i), 0, 0))]
    ls = functools.partial(_layer_spec, layer=layer)
    hbm = pl.BlockSpec(memory_space=pl.ANY)
    in_specs += [ls(gmix), ls(gmlp), _const_spec(gfin), ls(anorm), ls(win), ls(wa2), ls(ba2), hbm,
                 hbm, hbm, _const_spec(biasb), ls(biasc), ls(sink), _const_spec(tri), _const_spec(lvl)]
    args = [x, mods, mods, gmix, gmlp, gfin, anorm, win, wa2, ba2, wout, wup, wdown, biasb, biasc, sink, tri, lvl]

    y_spec = pl.BlockSpec((tq, D_MODEL), lambda i: (prv(i), 0))
    if prompt:
        assert tq >= B_WINDOW and (tq % C_BAND == 0 or C_BAND % tq == 0)
        rb = min(tq, C_BAND)
        first = nt - C_BAND // rb
        out_shape = [jax.ShapeDtypeStruct((rows, D_MODEL), F32),
                     jax.ShapeDtypeStruct((D_AQK, A_DV), F32),
                     jax.ShapeDtypeStruct((B_WINDOW, D_BKV), F32),
                     jax.ShapeDtypeStruct((B_WINDOW, D_BKV), F32),
                     jax.ShapeDtypeStruct((C_BAND, D_C), F32),
                     jax.ShapeDtypeStruct((C_BAND, D_C), F32)]
        band = pl.BlockSpec((rb, D_C), lambda i: (jnp.maximum(cur(i) - first, 0), 0))
        out_specs = [y_spec,
                     pl.BlockSpec((D_AQK, A_DV), lambda i: (0, 0)),
                     pl.BlockSpec((B_WINDOW, D_BKV), lambda i: (0, 0)),
                     pl.BlockSpec((B_WINDOW, D_BKV), lambda i: (0, 0)),
                     band, band]
        kbw_rows, kcw_rows = B_WINDOW + tq, C_BAND + tq
        aliases = {}
    else:
        assert nb == nch
        nseq = rows // CHUNK
        st, cbk, cbv, cck, ccv = caches
        assert len(carried) == N_SEQ_OUT
        seq = lambda r, w: pl.BlockSpec((None, nb, r, w), lambda i, _l=layer: (_l, cur(i), 0, 0))
        seq_dims = [(D_AQK, A_DV), (D_BKV, B_WINDOW), (D_BKV, B_WINDOW), (D_C, C_BAND), (D_C, C_BAND)]
        in_specs += [seq(r, w) for r, w in seq_dims]
        aliases = {len(in_specs) + j: 1 + j for j in range(N_SEQ_OUT)}
        in_specs += [pl.BlockSpec(memory_space=pl.ANY)] * N_SEQ_OUT
        args += [st, cbk, cbv, cck, ccv, *carried]
        out_shape = [jax.ShapeDtypeStruct((rows, D_MODEL), F32)] + [
            jax.ShapeDtypeStruct((DEPTH, nseq, r, w), F32) for r, w in seq_dims]
        out_specs = [y_spec] + [seq(r, w) for r, w in seq_dims]
        kbw_rows, kcw_rows = SUBLANES * 2, SUBLANES * 2

    scratch = [pltpu.VMEM((tq, D_MODEL), BF16),
               pltpu.VMEM((tq, D_MODEL), BF16),
               pltpu.VMEM((tq, OFF_RA), F32),
               pltpu.VMEM((tq, D_AQK), F32),
               pltpu.VMEM((tq, D_AQK), F32),
               pltpu.VMEM((tq, D_MIX), BF16),
               pltpu.VMEM((tq, D_MODEL), F32),
               pltpu.VMEM((tq, D_MODEL), F32),
               pltpu.VMEM((D_AQK, A_DV), F32),
               pltpu.VMEM((kbw_rows, D_BKV), BF16), pltpu.VMEM((kbw_rows, D_BKV), BF16),
               pltpu.VMEM((kcw_rows, D_C), BF16), pltpu.VMEM((kcw_rows, D_C), BF16),
               pltpu.VMEM(wout.shape[1:], BF16), pltpu.VMEM(wup.shape[1:], BF16), pltpu.VMEM(wdown.shape[1:], BF16),
               pltpu.SemaphoreType.DMA((3,))]

    kern = functools.partial(_layer_kernel, prompt=prompt, final=final, nch=nch, tq=tq, nt=nt, layer=layer)
    return pl.pallas_call(
        kern,
        grid=(nt + 1,),
        in_specs=in_specs,
        out_specs=out_specs,
        out_shape=out_shape,
        scratch_shapes=scratch,
        input_output_aliases=aliases,
        compiler_params=pltpu.CompilerParams(
            dimension_semantics=("arbitrary",),
            vmem_limit_bytes=VMEM_LIMIT_BYTES if prompt else VMEM_LIMIT_SAMPLE_BYTES),
        name=("layer_prompt" if prompt else "layer_sample") + ("_final" if final else ""),
    )(*args)


ADA_BLOCK = 1536


def _ada_kernel(c_ref, w_ref, b_ref, o_ref):
    c = c_ref[...]
    sc = c * jax.nn.sigmoid(c)
    sc_hi = sc.astype(BF16)
    sc_lo = (sc - sc_hi.astype(F32)).astype(BF16)
    w = w_ref[0]
    w_hi = w.astype(BF16)
    w_lo = (w - w_hi.astype(F32)).astype(BF16)
    o_ref[0] = _dot(sc_hi, w_hi) + _dot(sc_lo, w_hi) + _dot(sc_hi, w_lo) + b_ref[0]


def _ada(c_all, w_ada, b_ada):
    rows = c_all.shape[0]
    n = 6 * D_MODEL
    return pl.pallas_call(
        _ada_kernel,
        grid=(DEPTH, n // ADA_BLOCK),
        in_specs=[pl.BlockSpec((rows, D_MODEL), lambda l, j: (0, 0)),
                  pl.BlockSpec((1, D_MODEL, ADA_BLOCK), lambda l, j: (l, 0, j)),
                  pl.BlockSpec((1, 1, ADA_BLOCK), lambda l, j: (l, 0, j))],
        out_specs=pl.BlockSpec((1, rows, ADA_BLOCK), lambda l, j: (l, 0, j)),
        out_shape=jax.ShapeDtypeStruct((DEPTH, rows, n), F32),
        compiler_params=pltpu.CompilerParams(dimension_semantics=("arbitrary", "arbitrary"),
                                             vmem_limit_bytes=VMEM_LIMIT_BYTES),
        name="adaln",
    )(c_all, w_ada, b_ada.reshape(DEPTH, 1, n))


TB_W = 2 * LANES
TC_W = 5 * LANES
TC_FLAT = TC_W - (CHUNK + C_CLIP + 1)


def _shear(f, width):
    tiled = jnp.broadcast_to(f, (CHUNK, width))
    return pltpu.roll(tiled, width - (CHUNK - 1), 1, stride=1, stride_axis=0)


def _bias_kernel(t5_ref, crel_ref, bucket_ref, ob_ref, oc_ref):
    bucket = bucket_ref[...]
    for h in range(B_HEADS):
        def b_body(i, acc, h=h):
            return jnp.where(bucket == i, t5_ref[h, i], acc)
        f = lax.fori_loop(0, T5_BUCKETS, b_body, jnp.zeros(bucket.shape, F32))
        ob_ref[h] = _shear(f, TB_W)[:, 0:LK_B]
    lane = lax.broadcasted_iota(jnp.int32, (1, TC_W), 1)
    for l in range(DEPTH):
        for h in range(C_HEADS):
            row = crel_ref[l, h:h + 1, :]
            f = jnp.where(lane < TC_FLAT, row[:, 0:1], pltpu.roll(row, TC_FLAT, 1))
            oc_ref[l, h] = _shear(f, TC_W)[:, 0:LK_C]


def _bias_tables(t5_bias, c_rel_bias, bucket):
    smem = pl.BlockSpec(memory_space=pltpu.SMEM)
    vmem = pl.BlockSpec(memory_space=pltpu.VMEM)
    crel = jnp.pad(jnp.swapaxes(c_rel_bias, 1, 2),
                   ((0, 0), (0, SUBLANES - C_HEADS), (0, TC_W - (2 * C_CLIP + 1))))
    return pl.pallas_call(
        _bias_kernel,
        in_specs=[smem, vmem, vmem],
        out_specs=[vmem, vmem],
        out_shape=[jax.ShapeDtypeStruct((B_HEADS, CHUNK, LK_B), F32),
                   jax.ShapeDtypeStruct((DEPTH, C_HEADS, CHUNK, LK_C), F32)],
        name="bias_tables",
    )(t5_bias.T, crel, bucket)


def _t5_bucket(rel):
    half = T5_BUCKETS // 2
    max_exact = half // 2
    steps = half - max_exact
    thresholds = [int(np.ceil(max_exact * (T5_MAX_DIST / max_exact) ** (k / steps) - 1e-9)) for k in range(1, steps)]
    n = jnp.abs(rel)
    large = jnp.minimum(max_exact + sum((n >= t).astype(jnp.int32) for t in thresholds), half - 1)
    return jnp.where(rel > 0, half, 0) + jnp.where(n < max_exact, n, large)


def _gla_constants(tq):
    r = np.arange(tq)
    tri = ((r[:, None] // CHUNK == r[None, :] // CHUNK) & (r[None, :] <= r[:, None])).astype(np.float32)
    t = np.arange(CHUNK)[:, None]
    s = np.arange(CHUNK)[None, :]
    lvl = np.full((CHUNK, CHUNK), -1, np.int32)
    lvl[t == s] = 0
    for li, m in enumerate(GLA_LEVELS):
        sel = ((t // m) % 2 == 1) & ((s // m) == (t // m) - 1)
        lvl[np.broadcast_to(sel, lvl.shape)] = li + 1
    return jnp.asarray(tri, BF16), jnp.asarray(np.tile(lvl, (A_HEADS, 1)))


SRC_RA = 2 * D_AQK + 2 * D_AV
SRC_QB = SRC_RA + A_GATE_RANK
SRC_KB = SRC_QB + B_HEADS * HEAD_DIM
D_IN = SRC_KB + 2 * D_BKV + 3 * D_C
QB_HEAD_ORDER = (0, 2, 1, 3)


def _w_in_segments():
    segs = [(OFF_QA, 0, SRC_RA)]
    segs += [(OFF_QB + i * HEAD_DIM, SRC_QB + h * HEAD_DIM, HEAD_DIM) for i, h in enumerate(QB_HEAD_ORDER)]
    segs += [(OFF_KB, SRC_KB, D_IN - SRC_KB), (OFF_RA, SRC_RA, A_GATE_RANK)]
    return segs


def _w_in_kernel(w_ref, o_ref):
    segs = _w_in_segments()
    for blk in range(P_IN // LANES):
        lo, hi = blk * LANES, (blk + 1) * LANES
        rows, col = [], lo
        while col < hi:
            hit = [(d, s, w) for d, s, w in segs if d <= col < d + w]
            if hit:
                d, s, w = hit[0]
                n = min(hi, d + w) - col
                rows.append(w_ref[s + col - d:s + col - d + n, :])
            else:
                n = hi - col
                rows.append(jnp.zeros((n, D_MODEL), F32))
            col += n
        o_ref[:, lo:hi] = jnp.concatenate(rows, axis=0).T.astype(BF16)


def _prep_w_in(w_in):
    w_fm = jnp.swapaxes(w_in, 1, 2)
    return pl.pallas_call(
        _w_in_kernel,
        grid=(DEPTH,),
        in_specs=[pl.BlockSpec((None, D_IN, D_MODEL), lambda l: (l, 0, 0))],
        out_specs=pl.BlockSpec((None, D_MODEL, P_IN), lambda l: (l, 0, 0)),
        out_shape=jax.ShapeDtypeStruct((DEPTH, D_MODEL, P_IN), BF16),
        compiler_params=pltpu.CompilerParams(dimension_semantics=("arbitrary",),
                                             vmem_limit_bytes=VMEM_LIMIT_BYTES),
        name="prep_w_in",
    )(w_fm)


def _cast_kernel(w_ref, o_ref):
    o_ref[...] = w_ref[...].astype(BF16)


def _prep_w_up(w_up):
    return pl.pallas_call(
        _cast_kernel,
        grid=(DEPTH, N_FF),
        in_specs=[pl.BlockSpec((None, D_MODEL, FF_BLOCK), lambda l, j: (l, 0, j))],
        out_specs=pl.BlockSpec((None, None, D_MODEL, FF_BLOCK), lambda l, j: (l, j, 0, 0)),
        out_shape=jax.ShapeDtypeStruct((DEPTH, N_FF, D_MODEL, FF_BLOCK), BF16),
        compiler_params=pltpu.CompilerParams(dimension_semantics=("arbitrary", "arbitrary"),
                                             vmem_limit_bytes=VMEM_LIMIT_BYTES),
        name="prep_w_up",
    )(w_up)


TQ_PROMPT = 256
NB_SAMPLE = 4


def kernel(x_prompt, x_sample, c_prompt, c_sample, state_gla, cache_b_k, cache_b_v, cache_c_k, cache_c_v,
           w_ada, b_ada, norm_mix_g, norm_mlp_g, w_in, w_a2, b_a2, a_norm_g, b_sink, t5_bias, c_rel_bias,
           w_out, w_up, w_down, final_norm_g):
    bsz, seq, _ = x_prompt.shape
    dec_b, dec_s, _ = x_sample.shape
    assert bsz == 1 and dec_s == CHUNK and TQ_PROMPT == NB_SAMPLE * CHUNK

    n_c = bsz + dec_b
    c_rows = -(-n_c // SUBLANES) * SUBLANES
    c_all = jnp.concatenate([c_prompt, c_sample, jnp.zeros((c_rows - n_c, D_MODEL), F32)], axis=0)
    mods = _ada(c_all, w_ada, b_ada).reshape(DEPTH, c_rows, 6, D_MODEL)
    mods = jnp.pad(mods, ((0, 0), (0, 0), (0, SUBLANES - 6), (0, 0)))

    rel_b = jnp.arange(TB_W) - (CHUNK - 1) - B_WINDOW
    bias_b, bias_c = _bias_tables(t5_bias, c_rel_bias, _t5_bucket(rel_b).astype(jnp.int32)[None, :])
    bias_b = bias_b.reshape(STACK, LK_B)
    bias_c = bias_c.reshape(DEPTH, STACK, LK_C)

    tri, lvl = _gla_constants(TQ_PROMPT)
    ob0 = D_AV
    orow = lambda h: slice(ob0 + h * HEAD_DIM, ob0 + (h + 1) * HEAD_DIM)

    win = _prep_w_in(w_in)
    wa2 = jnp.concatenate([w_a2, jnp.zeros((DEPTH, LANES - A_GATE_RANK, D_AQK), F32)], axis=1).astype(BF16)
    wout = jnp.concatenate([w_out[:, 0:D_AV], w_out[:, orow(0)], w_out[:, orow(2)], w_out[:, orow(1)],
                            w_out[:, orow(3)], w_out[:, D_AV + B_HEADS * HEAD_DIM:]], axis=1).astype(BF16)
    wup = _prep_w_up(w_up)
    wdown = w_down.astype(BF16).reshape(DEPTH, N_FF, FF_BLOCK, D_MODEL)
    sink = jnp.repeat(b_sink, CHUNK, axis=1)[:, :, None]
    stacked = (norm_mix_g[:, None, :], norm_mlp_g[:, None, :], a_norm_g[:, None, :], win, wa2,
               b_a2[:, None, :], wout, wup, wdown, bias_c, sink)
    shared = (final_norm_g[None, :], bias_b, tri, lvl)

    x_p = x_prompt.reshape(seq, D_MODEL)
    x_s = x_sample.reshape(dec_b * dec_s, D_MODEL)
    outs_p = []
    fmajor = lambda cache: jnp.transpose(cache, (0, 1, 3, 4, 2)).reshape(
        DEPTH, dec_b, cache.shape[3] * cache.shape[4], cache.shape[2])
    caches = (state_gla.reshape(DEPTH, dec_b, D_AQK, A_DV), fmajor(cache_b_k), fmajor(cache_b_v),
              fmajor(cache_c_k), fmajor(cache_c_v))
    outs_s = tuple(jnp.zeros(c.shape, F32) for c in caches)
    for l in range(DEPTH):
        final = l == DEPTH - 1
        res_p = _run_layer(x_p, mods[l, 0:bsz], l, stacked, shared, None, None, prompt=True, final=final,
                           tq=TQ_PROMPT, nb=None)
        x_p = res_p[0]
        outs_p.append(res_p[1:])

        res_s = _run_layer(x_s, mods[l, bsz:bsz + dec_b], l, stacked, shared, caches, outs_s, prompt=False,
                           final=final, tq=NB_SAMPLE * CHUNK, nb=NB_SAMPLE)
        x_s = res_s[0]
        outs_s = tuple(res_s[1:])

    sg_p = jnp.stack([o[0].reshape(bsz, A_HEADS, A_DK, A_DV) for o in outs_p])
    kb_p = jnp.stack([o[1].reshape(bsz, B_WINDOW, B_KV_HEADS, HEAD_DIM) for o in outs_p])
    vb_p = jnp.stack([o[2].reshape(bsz, B_WINDOW, B_KV_HEADS, HEAD_DIM) for o in outs_p])
    kc_p = jnp.stack([o[3].reshape(bsz, C_BAND, C_HEADS, HEAD_DIM) for o in outs_p])
    vc_p = jnp.stack([o[4].reshape(bsz, C_BAND, C_HEADS, HEAD_DIM) for o in outs_p])
    sg_s = outs_s[0].reshape(DEPTH, dec_b, A_HEADS, A_DK, A_DV)
    tmajor = lambda arr, heads: jnp.transpose(arr.reshape(DEPTH, dec_b, heads, HEAD_DIM, -1), (0, 1, 4, 2, 3))
    kb_s = tmajor(outs_s[1], B_KV_HEADS)
    vb_s = tmajor(outs_s[2], B_KV_HEADS)
    kc_s = tmajor(outs_s[3], C_HEADS)
    vc_s = tmajor(outs_s[4], C_HEADS)
    return (x_p.reshape(bsz, seq, D_MODEL), x_s.reshape(dec_b, dec_s, D_MODEL),
            sg_p, kb_p, vb_p, kc_p, vc_p, sg_s, kb_s, vb_s, kc_s, vc_s)
```
